```python
import math
import jax, jax.numpy as jnp
from jax import lax
import numpy as np

D_MODEL = 1024
BATCH = 8
SEQ = 2048
DEPTH = 1

SSM_WIDTH = D_MODEL // 2
SSM_GROUP = 16
SSM_GROUPS = SSM_WIDTH // SSM_GROUP
SSM_STATE = 64
ATTN_HEADS = 8
ATTN_HEAD_DIM = 64
ATTN_WIDTH = ATTN_HEADS * ATTN_HEAD_DIM
Q_BLOCK = 128
LN_EPS = 1e-5
DEEPNORM_ALPHA = (2.0 * DEPTH) ** 0.25
DEEPNORM_BETA = (8.0 * DEPTH) ** -0.25
IN_SPLITS = (SSM_WIDTH, SSM_WIDTH, ATTN_WIDTH, ATTN_WIDTH, ATTN_WIDTH, ATTN_HEADS,
             ATTN_WIDTH, D_MODEL, D_MODEL)
IN_COLS = sum(IN_SPLITS)

kernel_name = "hybrid_s5_fox_gated_deepnorm"


def _layer_norm(x, g=None, b=None):
    xf = x.astype(jnp.float32)
    mu = jnp.mean(xf, axis=-1, keepdims=True)
    var = jnp.mean(jnp.square(xf - mu), axis=-1, keepdims=True)
    y = (xf - mu) * lax.rsqrt(var + LN_EPS)
    if g is not None:
        y = y * g.astype(jnp.float32) + b.astype(jnp.float32)
    return y


def _complex_linear_op(e1, e2):
    a1r, a1i, b1r, b1i = e1
    a2r, a2i, b2r, b2i = e2
    return (a2r * a1r - a2i * a1i,
            a2r * a1i + a2i * a1r,
            a2r * b1r - a2i * b1i + b2r,
            a2r * b1i + a2i * b1r + b2i)


def _s5_mixer(u, lam_re, lam_im, log_dt, b_re, b_im, c_re, c_im, d_skip, w_glu):
    bsz, s_len, _ = u.shape
    uf = u.astype(jnp.float32).reshape(bsz, s_len, SSM_GROUPS, SSM_GROUP)
    lr = lam_re.astype(jnp.float32)
    li = lam_im.astype(jnp.float32)
    dt = jnp.exp(log_dt.astype(jnp.float32))[:, None]
    mag = jnp.exp(lr * dt)
    abr = mag * jnp.cos(li * dt)
    abi = mag * jnp.sin(li * dt)
    den = lr * lr + li * li
    cr = ((abr - 1.0) * lr + abi * li) / den
    ci = (abi * lr - (abr - 1.0) * li) / den
    br = b_re.astype(jnp.float32)
    bi = b_im.astype(jnp.float32)
    bbr = cr[..., None] * br - ci[..., None] * bi
    bbi = cr[..., None] * bi + ci[..., None] * br
    bu_r = jnp.einsum('bsgh,gph->bsgp', uf, bbr)
    bu_i = jnp.einsum('bsgh,gph->bsgp', uf, bbi)
    a_r = jnp.broadcast_to(abr[None, None], (1, s_len, SSM_GROUPS, SSM_STATE))
    a_i = jnp.broadcast_to(abi[None, None], (1, s_len, SSM_GROUPS, SSM_STATE))
    _, _, h_r, h_i = lax.associative_scan(_complex_linear_op, (a_r, a_i, bu_r, bu_i), axis=1)
    y = (jnp.einsum('bsgp,ghp->bsgh', h_r, c_re.astype(jnp.float32))
         - jnp.einsum('bsgp,ghp->bsgh', h_i, c_im.astype(jnp.float32))
         + d_skip.astype(jnp.float32).reshape(SSM_GROUPS, SSM_GROUP) * uf)
    y = jax.nn.gelu(y.reshape(bsz, s_len, SSM_WIDTH))
    glu_a, glu_b = jnp.split(y @ w_glu.astype(jnp.float32), 2, axis=-1)
    return glu_a * jax.nn.sigmoid(glu_b)


def _forgetting_attention(q, k, v, log_f):
    s_len = q.shape[2]
    cum_f = jnp.cumsum(log_f, axis=-1)
    scale = ATTN_HEAD_DIM ** -0.5
    outs = []
    for blk in range(s_len // Q_BLOCK):
        q0 = blk * Q_BLOCK
        end = q0 + Q_BLOCK
        qb = q[:, :, q0:end]
        kb = k[:, :, :end]
        vb = v[:, :, :end]
        logits = (jnp.einsum('bhqd,bhkd->bhqk', qb, kb) * scale
                  + cum_f[:, :, q0:end, None] - cum_f[:, :, None, :end])
        causal = jnp.arange(end)[None, :] <= jnp.arange(q0, end)[:, None]
        logits = jnp.where(causal, logits, -jnp.inf)
        p = jax.nn.softmax(logits, axis=-1)
        outs.append(jnp.einsum('bhqk,bhkd->bhqd', p, vb))
    return jnp.concatenate(outs, axis=2)


def setup_inputs(seed: int = 0) -> dict:
    key = jax.random.key(seed)
    ks = jax.random.split(key, 20)
    f32 = jnp.float32
    L = DEPTH
    x = jax.random.normal(ks[0], (BATCH, SEQ, D_MODEL), f32)
    c = jax.random.normal(ks[1], (BATCH, D_MODEL), f32)
    w_ada = jax.random.normal(ks[2], (L, D_MODEL, 3 * D_MODEL), f32) * (0.3 * D_MODEL ** -0.5)
    b_ada = jax.random.normal(ks[3], (L, 3 * D_MODEL), f32) * 0.01
    w_in = jax.random.normal(ks[4], (L, D_MODEL, IN_COLS), f32) * D_MODEL ** -0.5
    b_f = jax.random.uniform(ks[5], (L, ATTN_HEADS), f32, 1.0, 4.0)
    n_idx = jnp.arange(SSM_STATE, dtype=f32)
    lam_re = -0.5 + 0.01 * jax.random.normal(ks[6], (L, SSM_GROUPS, SSM_STATE), f32)
    lam_im = math.pi * n_idx + 0.01 * jax.random.normal(ks[7], (L, SSM_GROUPS, SSM_STATE), f32)
    log_dt = jax.random.uniform(ks[8], (L, SSM_GROUPS), f32, math.log(1e-3), math.log(1e-1))
    b_scale = (2.0 * SSM_GROUP) ** -0.5
    ssm_b_re = jax.random.normal(ks[9], (L, SSM_GROUPS, SSM_STATE, SSM_GROUP), f32) * b_scale
    ssm_b_im = jax.random.normal(ks[10], (L, SSM_GROUPS, SSM_STATE, SSM_GROUP), f32) * b_scale
    c_scale = SSM_STATE ** -0.5
    ssm_c_re = jax.random.normal(ks[11], (L, SSM_GROUPS, SSM_GROUP, SSM_STATE), f32) * c_scale
    ssm_c_im = jax.random.normal(ks[12], (L, SSM_GROUPS, SSM_GROUP, SSM_STATE), f32) * c_scale
    ssm_d = jax.random.normal(ks[13], (L, SSM_WIDTH), f32)
    w_glu = jax.random.normal(ks[14], (L, SSM_WIDTH, 2 * SSM_WIDTH), f32) * SSM_WIDTH ** -0.5
    w_proj_ssm = jax.random.normal(ks[15], (L, SSM_WIDTH, D_MODEL), f32) * (SSM_WIDTH ** -0.5 * DEEPNORM_BETA)
    w_proj_attn = jax.random.normal(ks[16], (L, ATTN_WIDTH, D_MODEL), f32) * (ATTN_WIDTH ** -0.5 * DEEPNORM_BETA)
    w_out = jax.random.normal(ks[17], (L, D_MODEL, D_MODEL), f32) * (D_MODEL ** -0.5 * DEEPNORM_BETA)
    ln_g = 1.0 + 0.01 * jax.random.normal(ks[18], (L, D_MODEL), f32)
    ln_b = 0.01 * jax.random.normal(ks[19], (L, D_MODEL), f32)
    return {"x": x, "c": c, "w_ada": w_ada, "b_ada": b_ada, "w_in": w_in, "b_f": b_f,
            "lam_re": lam_re, "lam_im": lam_im, "log_dt": log_dt,
            "ssm_b_re": ssm_b_re, "ssm_b_im": ssm_b_im, "ssm_c_re": ssm_c_re, "ssm_c_im": ssm_c_im,
            "ssm_d": ssm_d, "w_glu": w_glu, "w_proj_ssm": w_proj_ssm, "w_proj_attn": w_proj_attn,
            "w_out": w_out, "ln_g": ln_g, "ln_b": ln_b}


def reference(x, c, w_ada, b_ada, w_in, b_f, lam_re, lam_im, log_dt, ssm_b_re, ssm_b_im,
              ssm_c_re, ssm_c_im, ssm_d, w_glu, w_proj_ssm, w_proj_attn, w_out, ln_g, ln_b):
    bsz, s_len, _ = x.shape
    split_idx = [int(v) for v in np.cumsum(IN_SPLITS)[:-1]]
    c_act = jax.nn.silu(c.astype(jnp.float32))
    h = x.astype(jnp.float32)
    for l in range(DEPTH):
        mod = c_act @ w_ada[l].astype(jnp.float32) + b_ada[l].astype(jnp.float32)
        shift, scale, gate = jnp.split(mod, 3, axis=-1)
        u = _layer_norm(h) * (1.0 + scale[:, None, :]) + shift[:, None, :]
        proj = u @ w_in[l].astype(jnp.float32)
        (x_ssm, z_ssm, q, k, v, f_logit, z_attn, g_ssm, g_attn) = jnp.split(proj, split_idx, axis=-1)
        y_ssm = _s5_mixer(x_ssm, lam_re[l], lam_im[l], log_dt[l], ssm_b_re[l], ssm_b_im[l],
                          ssm_c_re[l], ssm_c_im[l], ssm_d[l], w_glu[l])
        y_ssm = (y_ssm * jax.nn.silu(z_ssm)) @ w_proj_ssm[l].astype(jnp.float32)
        to_heads = lambda t: t.reshape(bsz, s_len, ATTN_HEADS, ATTN_HEAD_DIM).transpose(0, 2, 1, 3)
        log_f = jax.nn.log_sigmoid(f_logit + b_f[l].astype(jnp.float32)).transpose(0, 2, 1)
        o = _forgetting_attention(to_heads(q), to_heads(k), to_heads(v), log_f)
        o = o.transpose(0, 2, 1, 3).reshape(bsz, s_len, ATTN_WIDTH)
        y_attn = (o * jax.nn.silu(z_attn)) @ w_proj_attn[l].astype(jnp.float32)
        merged = jax.nn.sigmoid(g_ssm) * y_ssm + jax.nn.sigmoid(g_attn) * y_attn
        sub = (merged @ w_out[l].astype(jnp.float32)) * gate[:, None, :]
        h = _layer_norm(DEEPNORM_ALPHA * h + sub, ln_g[l], ln_b[l])
    return h.astype(x.dtype)
```

```python
import functools
import math

import jax
import jax.numpy as jnp
from jax import lax
from jax.experimental import pallas as pl
from jax.experimental.pallas import tpu as pltpu

D_MODEL = 1024
BATCH = 8
SEQ = 2048
SSM_WIDTH = 512
SSM_GROUP = 16
SSM_GROUPS = 32
SSM_STATE = 64
ATTN_HEADS = 8
ATTN_HEAD_DIM = 64
ATTN_WIDTH = 512
LN_EPS = 1e-5
DEEPNORM_ALPHA = 2.0 ** 0.25

F32 = jnp.float32
BF16 = jnp.bfloat16

LANES = 128
SUBLANES = 8
VMEM_LIMIT = 52 * 1024 * 1024

S5_TILES = 4
S5_TILE_GROUPS = SSM_GROUPS // S5_TILES
S5_HALF = S5_TILE_GROUPS * SSM_STATE
S5_TILE_LANES = 2 * S5_HALF
S5_TILE_IN = S5_TILE_GROUPS * SSM_GROUP
S5_STATE_LANES = S5_TILES * S5_TILE_LANES

C_XS, C_ZS, C_K, C_ZA, C_GS, C_GA, C_END = 0, 512, 1024, 1536, 2048, 3072, 4096

NEG_BIG = -1e30


def _sigmoid(x):
    return 1.0 / (1.0 + jnp.exp(-x))


def _silu(x):
    return x * _sigmoid(x)


def _cparams(*sem):
    return pltpu.CompilerParams(dimension_semantics=sem, vmem_limit_bytes=VMEM_LIMIT)


def _ada_kernel(c_ref, w_ref, b_ref, o_ref):
    c = c_ref[...]
    ca = _silu(c)
    o_ref[...] = jnp.dot(ca, w_ref[...], preferred_element_type=F32,
                         precision=lax.Precision.HIGHEST) + b_ref[...]


def _ada(c, w_ada, b_ada):
    n = w_ada.shape[1]
    tn = 512
    return pl.pallas_call(
        _ada_kernel,
        grid=(n // tn,),
        in_specs=[pl.BlockSpec((BATCH, D_MODEL), lambda j: (0, 0)),
                  pl.BlockSpec((D_MODEL, tn), lambda j: (0, j)),
                  pl.BlockSpec((1, tn), lambda j: (0, j))],
        out_specs=pl.BlockSpec((BATCH, tn), lambda j: (0, j)),
        out_shape=jax.ShapeDtypeStruct((BATCH, n), F32),
        compiler_params=_cparams("arbitrary"),
        name="ada",
    )(c, w_ada, b_ada.reshape(1, n))


def _zoh_kernel(lr_ref, li_ref, ldt_ref, br_ref, bi_ref, cim_ref,
                abr_ref, abi_ref, bbr_ref, bbi_ref, ncim_ref):
    lr = lr_ref[...]
    li = li_ref[...]
    dt = jnp.exp(ldt_ref[...])
    mag = jnp.exp(lr * dt)
    abr = mag * jnp.cos(li * dt)
    abi = mag * jnp.sin(li * dt)
    den = lr * lr + li * li
    cr = ((abr - 1.0) * lr + abi * li) / den
    ci = (abi * lr - (abr - 1.0) * li) / den
    br = br_ref[...]
    bi = bi_ref[...]
    abr_ref[...] = abr
    abi_ref[...] = abi
    bbr_ref[...] = cr * br - ci * bi
    bbi_ref[...] = cr * bi + ci * br
    ncim_ref[...] = -cim_ref[...]


def _zoh(lam_re, lam_im, log_dt, b_re, b_im, c_im):
    rows = SSM_GROUPS * SSM_GROUP
    rep = lambda a: jnp.repeat(a, SSM_GROUP, axis=0)
    lr = rep(lam_re)
    li = rep(lam_im)
    ldt = jnp.broadcast_to(rep(log_dt[:, None]), (rows, SSM_STATE))
    brt = b_re.transpose(0, 2, 1).reshape(rows, SSM_STATE)
    bit = b_im.transpose(0, 2, 1).reshape(rows, SSM_STATE)
    cim = c_im.reshape(rows, SSM_STATE)
    shp = jax.ShapeDtypeStruct((rows, SSM_STATE), F32)
    return pl.pallas_call(
        _zoh_kernel,
        out_shape=(shp,) * 5,
        name="zoh",
    )(lr, li, ldt, brt, bit, cim)


def _inproj_kernel(x_ref, scale_ref, shift_ref, wm_ref, wt_ref, wf_ref, bf_ref, tri_ref,
                   xs_ref, szs_ref, k_ref, sza_ref, sgs_ref, sga_ref, qt_ref, vt_ref, caug_ref,
                   carry_ref, *, ts):
    i = pl.program_id(1)
    x = x_ref[0]
    mu = jnp.mean(x, axis=-1, keepdims=True)
    xc = x - mu
    var = jnp.mean(xc * xc, axis=-1, keepdims=True)
    xn = xc * lax.rsqrt(var + LN_EPS)
    u = xn * (1.0 + scale_ref[0]) + shift_ref[0]
    ub = u.astype(BF16)

    def mm(lo, hi):
        return jnp.dot(ub, wm_ref[:, lo:hi], preferred_element_type=F32)

    xs_ref[...] = mm(C_XS, C_ZS).astype(BF16)
    szs_ref[0] = _silu(mm(C_ZS, C_K)).astype(BF16)
    k_ref[0] = mm(C_K, C_ZA).astype(BF16)
    sza_ref[0] = _silu(mm(C_ZA, C_GS)).astype(BF16)
    sgs_ref[0] = _sigmoid(mm(C_GS, C_GA)).astype(BF16)
    sga_ref[0] = _sigmoid(mm(C_GA, C_END)).astype(BF16)

    qv = lax.dot_general(wt_ref[...], ub, (((1,), (1,)), ((), ())),
                         preferred_element_type=F32)
    qt_ref[0] = (qv[:ATTN_WIDTH] * (ATTN_HEAD_DIM ** -0.5)).astype(BF16)
    vt_ref[0] = qv[ATTN_WIDTH:].astype(BF16)

    f = jnp.dot(ub, wf_ref[...], preferred_element_type=F32) + bf_ref[...]
    lf = jnp.minimum(f, 0.0) - jnp.log(1.0 + jnp.exp(-jnp.abs(f)))

    @pl.when(i == 0)
    def _():
        carry_ref[...] = jnp.zeros_like(carry_ref)

    l1 = lf.astype(BF16)
    r1 = lf - l1.astype(F32)
    l2 = r1.astype(BF16)
    l3 = (r1 - l2.astype(F32)).astype(BF16)
    tri = tri_ref[...]
    cf = (jnp.dot(tri, l1, preferred_element_type=F32)
          + jnp.dot(tri, l2, preferred_element_type=F32)
          + jnp.dot(tri, l3, preferred_element_type=F32)) + carry_ref[0:1, :]
    carry_ref[...] = jnp.broadcast_to(cf[ts - 1:ts, :], carry_ref.shape)

    c1 = cf.astype(BF16).astype(F32)
    r = cf - c1
    c2 = r.astype(BF16).astype(F32)
    c3 = r - c2
    lane = lax.broadcasted_iota(jnp.int32, cf.shape, 1)
    parts = jnp.where(lane < 8, c1, jnp.where(lane < 16, c2, jnp.where(lane < 24, c3, 0.0)))
    caug_ref[0] = (-parts).astype(BF16)


def _inproj(x, scale, shift, w_main, w_t, w_f3, b_f3, *, ts):
    nb, s, d = x.shape
    nt = s // ts
    tri = jnp.tril(jnp.ones((ts, ts), BF16))
    tok3 = lambda w: pl.BlockSpec((1, ts, w), lambda b, i: (b, i, 0))
    const2 = lambda a: pl.BlockSpec(a.shape, lambda b, i: (0, 0))
    bf = lambda shape: jax.ShapeDtypeStruct(shape, BF16)
    return pl.pallas_call(
        functools.partial(_inproj_kernel, ts=ts),
        grid=(nb, nt),
        in_specs=[tok3(d),
                  pl.BlockSpec((1, 1, d), lambda b, i: (b, 0, 0)),
                  pl.BlockSpec((1, 1, d), lambda b, i: (b, 0, 0)),
                  const2(w_main), const2(w_t), const2(w_f3), const2(b_f3), const2(tri)],
        out_specs=[pl.BlockSpec((ts, SSM_WIDTH), lambda b, i: (i, b)),
                   tok3(SSM_WIDTH), tok3(ATTN_WIDTH), tok3(ATTN_WIDTH),
                   tok3(D_MODEL), tok3(D_MODEL),
                   pl.BlockSpec((1, ATTN_WIDTH, ts), lambda b, i: (b, 0, i)),
                   pl.BlockSpec((1, ATTN_WIDTH, ts), lambda b, i: (b, 0, i)),
                   tok3(LANES)],
        out_shape=[bf((s, nb * SSM_WIDTH)),
                   bf((nb, s, SSM_WIDTH)), bf((nb, s, ATTN_WIDTH)), bf((nb, s, ATTN_WIDTH)),
                   bf((nb, s, D_MODEL)), bf((nb, s, D_MODEL)),
                   bf((nb, ATTN_WIDTH, s)), bf((nb, ATTN_WIDTH, s)),
                   bf((nb, s, LANES))],
        scratch_shapes=[pltpu.VMEM((SUBLANES, LANES), F32)],
        compiler_params=_cparams("arbitrary", "arbitrary"),
        name="inproj",
    )(x, scale, shift, w_main, w_t, w_f3, b_f3, tri)


def _gelu_tanh(x):
    return 0.5 * x * (1.0 + jnp.tanh(math.sqrt(2.0 / math.pi) * (x + 0.044715 * (x * x * x))))


def _s5_kernel(xs_ref, bmat_ref, cmat_ref, are_ref, aim_ref, d_ref, wglu_ref,
               y_ref, buf_ref, h_ref, *, steps):
    @pl.when(pl.program_id(0) == 0)
    def _():
        h_ref[...] = jnp.zeros_like(h_ref)

    xs = xs_ref[...]
    for j in range(S5_TILES):
        lo = j * S5_TILE_LANES
        buf_ref[:, lo:lo + S5_TILE_LANES] = jnp.dot(
            xs[:, j * S5_TILE_IN:(j + 1) * S5_TILE_IN], bmat_ref[j],
            preferred_element_type=F32)

    for j in range(S5_TILES):
        re_lo = j * S5_TILE_LANES
        im_lo = re_lo + S5_HALF
        ar = jnp.broadcast_to(are_ref[:, j * S5_HALF:(j + 1) * S5_HALF], (SUBLANES, S5_HALF))
        ai = jnp.broadcast_to(aim_ref[:, j * S5_HALF:(j + 1) * S5_HALF], (SUBLANES, S5_HALF))

        def step(t, carry, re_lo=re_lo, im_lo=im_lo, ar=ar, ai=ai):
            hr, hi = carry
            r0 = pl.multiple_of(t * SUBLANES, SUBLANES)
            bur = buf_ref[pl.ds(r0, SUBLANES), re_lo:re_lo + S5_HALF]
            bui = buf_ref[pl.ds(r0, SUBLANES), im_lo:im_lo + S5_HALF]
            nhr = ar * hr - ai * hi + bur
            nhi = ar * hi + ai * hr + bui
            buf_ref[pl.ds(r0, SUBLANES), re_lo:re_lo + S5_HALF] = nhr
            buf_ref[pl.ds(r0, SUBLANES), im_lo:im_lo + S5_HALF] = nhi
            return nhr, nhi

        hr0 = h_ref[:, re_lo:re_lo + S5_HALF]
        hi0 = h_ref[:, im_lo:im_lo + S5_HALF]
        hr, hi = lax.fori_loop(0, steps, step, (hr0, hi0), unroll=8)
        h_ref[:, re_lo:re_lo + S5_HALF] = hr
        h_ref[:, im_lo:im_lo + S5_HALF] = hi

    ys = []
    for j in range(S5_TILES):
        lo = j * S5_TILE_LANES
        hb = buf_ref[:, lo:lo + S5_TILE_LANES].astype(BF16)
        ys.append(jnp.dot(hb, cmat_ref[j], preferred_element_type=F32))
    y = jnp.concatenate(ys, axis=1) + d_ref[...] * xs.astype(F32)
    g = _gelu_tanh(y).astype(BF16)
    ab = jnp.dot(g, wglu_ref[...], preferred_element_type=F32)
    y_ref[...] = (ab[:, :SSM_WIDTH] * _sigmoid(ab[:, SSM_WIDTH:])).astype(BF16)


def _s5(xs_t, bmat, cmat, a_re, a_im, d_skip, w_glu, *, steps):
    t = xs_t.shape[0]
    rows = steps * SUBLANES
    const = lambda a: pl.BlockSpec(a.shape, lambda i: (0,) * a.ndim)
    return pl.pallas_call(
        functools.partial(_s5_kernel, steps=steps),
        grid=(t // rows,),
        in_specs=[pl.BlockSpec((rows, SSM_WIDTH), lambda i: (i, 0)),
                  const(bmat), const(cmat), const(a_re), const(a_im), const(d_skip), const(w_glu)],
        out_specs=pl.BlockSpec((rows, SSM_WIDTH), lambda i: (i, 0)),
        out_shape=jax.ShapeDtypeStruct((t, SSM_WIDTH), BF16),
        scratch_shapes=[pltpu.VMEM((rows, S5_STATE_LANES), F32),
                        pltpu.VMEM((SUBLANES, S5_STATE_LANES), F32)],
        compiler_params=_cparams("arbitrary"),
        name="s5",
    )(xs_t, bmat, cmat, a_re, a_im, d_skip, w_glu)


def _attn_kernel(qt_ref, k_ref, caug_ref, vt_ref, sza_ref, o_ref, ot_ref, *, tq):
    qi = pl.program_id(1)
    tk = tq
    row = lax.broadcasted_iota(jnp.int32, (LANES, tq), 0)
    kpos = lax.broadcasted_iota(jnp.int32, (tk, tq), 0)
    qpos = lax.broadcasted_iota(jnp.int32, (tk, tq), 1)
    causal = kpos <= qpos

    for h in range(ATTN_HEADS):
        j = h // 2
        q2 = qt_ref[0, j * LANES:(j + 1) * LANES, :]
        qm = jnp.where((row // ATTN_HEAD_DIM) == (h % 2), q2, jnp.zeros_like(q2))
        sel = jnp.where((row == h) | (row == 8 + h) | (row == 16 + h), 1.0, 0.0).astype(BF16)
        rhs = jnp.concatenate([qm, sel], axis=0)

        def tile(k0, carry, masked, j=j, h=h, rhs=rhs):
            m, l, acc = carry
            lhs = jnp.concatenate([k_ref[0, pl.ds(k0, tk), j * LANES:(j + 1) * LANES],
                                   caug_ref[0, pl.ds(k0, tk), :]], axis=1)
            s = jnp.dot(lhs, rhs, preferred_element_type=F32)
            if masked:
                s = jnp.where(causal, s, NEG_BIG)
            m_new = jnp.maximum(m, jnp.max(s, axis=0, keepdims=True))
            alpha = jnp.exp(m - m_new)
            p = jnp.exp(s - m_new)
            l_new = alpha * l + jnp.sum(p, axis=0, keepdims=True)
            vt = vt_ref[0, h * ATTN_HEAD_DIM:(h + 1) * ATTN_HEAD_DIM, pl.ds(k0, tk)]
            acc_new = alpha * acc + jnp.dot(vt, p.astype(BF16), preferred_element_type=F32)
            return m_new, l_new, acc_new

        def body(kj, carry):
            return tile(pl.multiple_of(kj * tk, tk), carry, False)

        init = (jnp.full((1, tq), -jnp.inf, F32), jnp.zeros((1, tq), F32),
                jnp.zeros((ATTN_HEAD_DIM, tq), F32))
        carry = lax.fori_loop(0, qi, body, init)
        m, l, acc = tile(pl.multiple_of(qi * tk, tk), carry, True)
        ot_ref[h * ATTN_HEAD_DIM:(h + 1) * ATTN_HEAD_DIM, :] = acc / l

    o = ot_ref[...].T
    o_ref[0] = (o * sza_ref[0].astype(F32)).astype(BF16)


def _attn(qt, k, caug, vt, sza, *, tq):
    nb, s, w = k.shape
    return pl.pallas_call(
        functools.partial(_attn_kernel, tq=tq),
        grid=(nb, s // tq),
        in_specs=[pl.BlockSpec((1, w, tq), lambda b, i: (b, 0, i)),
                  pl.BlockSpec((1, s, w), lambda b, i: (b, 0, 0)),
                  pl.BlockSpec((1, s, LANES), lambda b, i: (b, 0, 0)),
                  pl.BlockSpec((1, w, s), lambda b, i: (b, 0, 0)),
                  pl.BlockSpec((1, tq, w), lambda b, i: (b, i, 0))],
        out_specs=pl.BlockSpec((1, tq, w), lambda b, i: (b, i, 0)),
        out_shape=jax.ShapeDtypeStruct((nb, s, w), BF16),
        scratch_shapes=[pltpu.VMEM((w, tq), F32)],
        compiler_params=_cparams("arbitrary", "arbitrary"),
        name="attn",
    )(qt, k, caug, vt, sza)


def _merge_kernel(x_ref, ys_ref, szs_ref, oz_ref, sgs_ref, sga_ref, gate_ref,
                  wps_ref, wpa_ref, wout_ref, lng_ref, lnb_ref, o_ref):
    a = (ys_ref[...].astype(F32) * szs_ref[0].astype(F32)).astype(BF16)
    m1 = jnp.dot(a, wps_ref[...], preferred_element_type=F32) * sgs_ref[0].astype(F32)
    m2 = jnp.dot(oz_ref[0], wpa_ref[...], preferred_element_type=F32) * sga_ref[0].astype(F32)
    merged = (m1 + m2).astype(BF16)
    sub = jnp.dot(merged, wout_ref[...], preferred_element_type=F32) * gate_ref[0]
    y = DEEPNORM_ALPHA * x_ref[0] + sub
    mu = jnp.mean(y, axis=-1, keepdims=True)
    yc = y - mu
    var = jnp.mean(yc * yc, axis=-1, keepdims=True)
    o_ref[0] = yc * lax.rsqrt(var + LN_EPS) * lng_ref[...] + lnb_ref[...]


def _merge(x, ys_t, szs, oz, sgs, sga, gate, wps, wpa, wout, ln_g, ln_b, *, ts):
    nb, s, d = x.shape
    tok3 = lambda w: pl.BlockSpec((1, ts, w), lambda b, i: (b, i, 0))
    const2 = lambda a: pl.BlockSpec(a.shape, lambda b, i: (0, 0))
    return pl.pallas_call(
        _merge_kernel,
        grid=(nb, s // ts),
        in_specs=[tok3(d),
                  pl.BlockSpec((ts, SSM_WIDTH), lambda b, i: (i, b)),
                  tok3(SSM_WIDTH), tok3(ATTN_WIDTH), tok3(d), tok3(d),
                  pl.BlockSpec((1, 1, d), lambda b, i: (b, 0, 0)),
                  const2(wps), const2(wpa), const2(wout), const2(ln_g), const2(ln_b)],
        out_specs=tok3(d),
        out_shape=jax.ShapeDtypeStruct((nb, s, d), x.dtype),
        compiler_params=_cparams("arbitrary", "arbitrary"),
        name="merge",
    )(x, ys_t, szs, oz, sgs, sga, gate, wps, wpa, wout, ln_g, ln_b)


def _block_diag_tiles(blocks):
    eye = jnp.eye(S5_TILE_GROUPS, dtype=blocks.dtype)
    t, g, r, c = blocks.shape
    return jnp.einsum('tgrc,gG->tgrGc', blocks, eye).reshape(t, g * r, g * c)


def kernel(x, c, w_ada, b_ada, w_in, b_f, lam_re, lam_im, log_dt, ssm_b_re, ssm_b_im,
           ssm_c_re, ssm_c_im, ssm_d, w_glu, w_proj_ssm, w_proj_attn, w_out, ln_g, ln_b):
    nb, s, d = x.shape
    l = 0
    mod = _ada(c, w_ada[l], b_ada[l])
    shift = mod[:, None, :d]
    scale = mod[:, None, d:2 * d]
    gate = mod[:, None, 2 * d:]

    w = w_in[l]
    o_xs, o_zs, o_q, o_k, o_v, o_f, o_za, o_gs, o_ga, o_end = (
        0, 512, 1024, 1536, 2048, 2560, 2568, 3080, 4104, 5128)
    w_main = jnp.concatenate([w[:, o_xs:o_zs], w[:, o_zs:o_q], w[:, o_k:o_v],
                              w[:, o_za:o_gs], w[:, o_gs:o_ga], w[:, o_ga:o_end]],
                             axis=1).astype(BF16)
    w_t = jnp.concatenate([w[:, o_q:o_k], w[:, o_v:o_f]], axis=1).T.astype(BF16)
    wf = w[:, o_f:o_za]
    w_f3 = jnp.concatenate([wf, wf, wf, jnp.zeros((d, LANES - 3 * ATTN_HEADS), w.dtype)],
                           axis=1).astype(BF16)
    bfv = b_f[l].astype(F32)
    b_f3 = jnp.concatenate([bfv, bfv, bfv, jnp.zeros((LANES - 3 * ATTN_HEADS,), F32)])[None, :]

    (xs_t, szs, k, sza, sgs, sga, qt, vt, caug) = _inproj(
        x, scale, shift, w_main, w_t, w_f3, b_f3, ts=512)

    abr, abi, bbr, bbi, ncim = _zoh(lam_re[l], lam_im[l], log_dt[l], ssm_b_re[l], ssm_b_im[l],
                                    ssm_c_im[l])
    g4 = (S5_TILES, S5_TILE_GROUPS)
    bbr = bbr.reshape(g4 + (SSM_GROUP, SSM_STATE))
    bbi = bbi.reshape(g4 + (SSM_GROUP, SSM_STATE))
    bmat = jnp.concatenate([_block_diag_tiles(bbr), _block_diag_tiles(bbi)], axis=2).astype(BF16)
    cre_t = ssm_c_re[l].reshape(g4 + (SSM_GROUP, SSM_STATE)).transpose(0, 1, 3, 2)
    ncim_t = ncim.reshape(g4 + (SSM_GROUP, SSM_STATE)).transpose(0, 1, 3, 2)
    cmat = jnp.concatenate([_block_diag_tiles(cre_t), _block_diag_tiles(ncim_t)],
                           axis=1).astype(BF16)
    a_re = abr.reshape(SSM_GROUPS, SSM_GROUP, SSM_STATE)[:, 0, :].reshape(1, SSM_GROUPS * SSM_STATE)
    a_im = abi.reshape(SSM_GROUPS, SSM_GROUP, SSM_STATE)[:, 0, :].reshape(1, SSM_GROUPS * SSM_STATE)

    ys_t = _s5(xs_t.reshape(s * nb, SSM_WIDTH), bmat, cmat, a_re, a_im,
               ssm_d[l].reshape(1, SSM_WIDTH).astype(F32), w_glu[l].astype(BF16), steps=64)

    oz = _attn(qt, k, caug, vt, sza, tq=256)

    return _merge(x, ys_t.reshape(s, nb * SSM_WIDTH), szs, oz, sgs, sga, gate,
                  w_proj_ssm[l].astype(BF16), w_proj_attn[l].astype(BF16), w_out[l].astype(BF16),
                  ln_g[l].reshape(1, d).astype(F32), ln_b[l].astype(F32).reshape(1, d), ts=512)
```

```python
import functools
import math

import jax
import jax.numpy as jnp
from jax import lax
from jax.experimental import pallas as pl
from jax.experimental.pallas import tpu as pltpu

D_MODEL = 1024
BATCH = 8
SEQ = 2048
SSM_WIDTH = 512
SSM_GROUP = 16
SSM_GROUPS = 32
SSM_STATE = 64
ATTN_HEADS = 8
ATTN_HEAD_DIM = 64
ATTN_WIDTH = 512
LN_EPS = 1e-5
DEEPNORM_ALPHA = 2.0 ** 0.25

F32 = jnp.float32
BF16 = jnp.bfloat16

LANES = 128
SUBLANES = 8
VMEM_LIMIT = 52 * 1024 * 1024

S5_TILES = 4
S5_TILE_GROUPS = SSM_GROUPS // S5_TILES
S5_HALF = S5_TILE_GROUPS * SSM_STATE
S5_TILE_LANES = 2 * S5_HALF
S5_TILE_IN = S5_TILE_GROUPS * SSM_GROUP
S5_STATE_LANES = S5_TILES * S5_TILE_LANES

C_XS, C_ZS, C_K, C_ZA, C_GS, C_GA, C_END = 0, 512, 1024, 1536, 2048, 3072, 4096

NEG_BIG = -1e30

ATTN_LOOKAHEAD = 3
ATTN_SLOTS = 4
assert ATTN_SLOTS > ATTN_LOOKAHEAD and ATTN_HEADS % ATTN_SLOTS == 0


def _sigmoid(x):
    return 1.0 / (1.0 + jnp.exp(-x))


def _silu(x):
    return x * _sigmoid(x)


def _cparams(*sem):
    return pltpu.CompilerParams(dimension_semantics=sem, vmem_limit_bytes=VMEM_LIMIT)


def _ada_kernel(c_ref, w_ref, b_ref, o_ref):
    c = c_ref[...]
    ca = _silu(c)
    o_ref[...] = jnp.dot(ca, w_ref[...], preferred_element_type=F32,
                         precision=lax.Precision.HIGHEST) + b_ref[...]


def _ada(c, w_ada, b_ada):
    n = w_ada.shape[1]
    tn = 512
    return pl.pallas_call(
        _ada_kernel,
        grid=(n // tn,),
        in_specs=[pl.BlockSpec((BATCH, D_MODEL), lambda j: (0, 0)),
                  pl.BlockSpec((D_MODEL, tn), lambda j: (0, j)),
                  pl.BlockSpec((1, tn), lambda j: (0, j))],
        out_specs=pl.BlockSpec((BATCH, tn), lambda j: (0, j)),
        out_shape=jax.ShapeDtypeStruct((BATCH, n), F32),
        compiler_params=_cparams("arbitrary"),
        name="ada",
    )(c, w_ada, b_ada.reshape(1, n))


def _zoh_kernel(lr_ref, li_ref, ldt_ref, br_ref, bi_ref, cim_ref,
                abr_ref, abi_ref, bbr_ref, bbi_ref, ncim_ref):
    lr = lr_ref[...]
    li = li_ref[...]
    dt = jnp.exp(ldt_ref[...])
    mag = jnp.exp(lr * dt)
    abr = mag * jnp.cos(li * dt)
    abi = mag * jnp.sin(li * dt)
    den = lr * lr + li * li
    cr = ((abr - 1.0) * lr + abi * li) / den
    ci = (abi * lr - (abr - 1.0) * li) / den
    br = br_ref[...]
    bi = bi_ref[...]
    abr_ref[...] = abr
    abi_ref[...] = abi
    bbr_ref[...] = cr * br - ci * bi
    bbi_ref[...] = cr * bi + ci * br
    ncim_ref[...] = -cim_ref[...]


def _zoh(lam_re, lam_im, log_dt, b_re, b_im, c_im):
    rows = SSM_GROUPS * SSM_GROUP
    rep = lambda a: jnp.repeat(a, SSM_GROUP, axis=0)
    lr = rep(lam_re)
    li = rep(lam_im)
    ldt = jnp.broadcast_to(rep(log_dt[:, None]), (rows, SSM_STATE))
    brt = b_re.transpose(0, 2, 1).reshape(rows, SSM_STATE)
    bit = b_im.transpose(0, 2, 1).reshape(rows, SSM_STATE)
    cim = c_im.reshape(rows, SSM_STATE)
    shp = jax.ShapeDtypeStruct((rows, SSM_STATE), F32)
    return pl.pallas_call(
        _zoh_kernel,
        out_shape=(shp,) * 5,
        name="zoh",
    )(lr, li, ldt, brt, bit, cim)


def _inproj_kernel(x_ref, scale_ref, shift_ref, wm_ref, wt_ref, wf_ref, bf_ref, tri_ref,
                   xs_ref, szs_ref, k_ref, sza_ref, sgs_ref, sga_ref, qt_ref, vt_ref, caug_ref,
                   carry_ref, *, ts):
    i = pl.program_id(1)
    x = x_ref[0]
    mu = jnp.mean(x, axis=-1, keepdims=True)
    xc = x - mu
    var = jnp.mean(xc * xc, axis=-1, keepdims=True)
    xn = xc * lax.rsqrt(var + LN_EPS)
    u = xn * (1.0 + scale_ref[0]) + shift_ref[0]
    ub = u.astype(BF16)

    def mm(lo, hi):
        return jnp.dot(ub, wm_ref[:, lo:hi], preferred_element_type=F32)

    xs_ref[...] = mm(C_XS, C_ZS).astype(BF16)
    szs_ref[0] = _silu(mm(C_ZS, C_K)).astype(BF16)
    k_ref[0] = mm(C_K, C_ZA).astype(BF16)
    sza_ref[0] = _silu(mm(C_ZA, C_GS)).astype(BF16)
    sgs_ref[0] = _sigmoid(mm(C_GS, C_GA)).astype(BF16)
    sga_ref[0] = _sigmoid(mm(C_GA, C_END)).astype(BF16)

    qv = lax.dot_general(wt_ref[...], ub, (((1,), (1,)), ((), ())),
                         preferred_element_type=F32)
    qt_ref[0] = (qv[:ATTN_WIDTH] * (ATTN_HEAD_DIM ** -0.5)).astype(BF16)
    vt_ref[0] = qv[ATTN_WIDTH:].astype(BF16)

    f = jnp.dot(ub, wf_ref[...], preferred_element_type=F32) + bf_ref[...]
    lf = jnp.minimum(f, 0.0) - jnp.log(1.0 + jnp.exp(-jnp.abs(f)))

    @pl.when(i == 0)
    def _():
        carry_ref[...] = jnp.zeros_like(carry_ref)

    l1 = lf.astype(BF16)
    r1 = lf - l1.astype(F32)
    l2 = r1.astype(BF16)
    l3 = (r1 - l2.astype(F32)).astype(BF16)
    tri = tri_ref[...]
    cf = (jnp.dot(tri, l1, preferred_element_type=F32)
          + jnp.dot(tri, l2, preferred_element_type=F32)
          + jnp.dot(tri, l3, preferred_element_type=F32)) + carry_ref[0:1, :]
    carry_ref[...] = jnp.broadcast_to(cf[ts - 1:ts, :], carry_ref.shape)

    c1 = cf.astype(BF16).astype(F32)
    r = cf - c1
    c2 = r.astype(BF16).astype(F32)
    c3 = r - c2
    lane = lax.broadcasted_iota(jnp.int32, cf.shape, 1)
    parts = jnp.where(lane < 8, c1, jnp.where(lane < 16, c2, jnp.where(lane < 24, c3, 0.0)))
    caug_ref[0] = (-parts).astype(BF16)


def _inproj(x, scale, shift, w_main, w_t, w_f3, b_f3, *, ts):
    nb, s, d = x.shape
    nt = s // ts
    tri = jnp.tril(jnp.ones((ts, ts), BF16))
    tok3 = lambda w: pl.BlockSpec((1, ts, w), lambda b, i: (b, i, 0))
    const2 = lambda a: pl.BlockSpec(a.shape, lambda b, i: (0, 0))
    bf = lambda shape: jax.ShapeDtypeStruct(shape, BF16)
    return pl.pallas_call(
        functools.partial(_inproj_kernel, ts=ts),
        grid=(nb, nt),
        in_specs=[tok3(d),
                  pl.BlockSpec((1, 1, d), lambda b, i: (b, 0, 0)),
                  pl.BlockSpec((1, 1, d), lambda b, i: (b, 0, 0)),
                  const2(w_main), const2(w_t), const2(w_f3), const2(b_f3), const2(tri)],
        out_specs=[pl.BlockSpec((ts, SSM_WIDTH), lambda b, i: (i, b)),
                   tok3(SSM_WIDTH), tok3(ATTN_WIDTH), tok3(ATTN_WIDTH),
                   tok3(D_MODEL), tok3(D_MODEL),
                   pl.BlockSpec((1, ATTN_WIDTH, ts), lambda b, i: (b, 0, i)),
                   pl.BlockSpec((1, ATTN_WIDTH, ts), lambda b, i: (b, 0, i)),
                   tok3(LANES)],
        out_shape=[bf((s, nb * SSM_WIDTH)),
                   bf((nb, s, SSM_WIDTH)), bf((nb, s, ATTN_WIDTH)), bf((nb, s, ATTN_WIDTH)),
                   bf((nb, s, D_MODEL)), bf((nb, s, D_MODEL)),
                   bf((nb, ATTN_WIDTH, s)), bf((nb, ATTN_WIDTH, s)),
                   bf((nb, s, LANES))],
        scratch_shapes=[pltpu.VMEM((SUBLANES, LANES), F32)],
        compiler_params=_cparams("arbitrary", "arbitrary"),
        name="inproj",
    )(x, scale, shift, w_main, w_t, w_f3, b_f3, tri)


def _gelu_tanh(x):
    return 0.5 * x * (1.0 + jnp.tanh(math.sqrt(2.0 / math.pi) * (x + 0.044715 * (x * x * x))))


def _s5_kernel(xs_ref, bmat_ref, cmat_ref, are_ref, aim_ref, d_ref, wglu_ref,
               y_ref, buf_ref, h_ref, *, steps):
    @pl.when(pl.program_id(0) == 0)
    def _():
        h_ref[...] = jnp.zeros_like(h_ref)

    xs = xs_ref[...]
    for j in range(S5_TILES):
        lo = j * S5_TILE_LANES
        buf_ref[:, lo:lo + S5_TILE_LANES] = jnp.dot(
            xs[:, j * S5_TILE_IN:(j + 1) * S5_TILE_IN], bmat_ref[j],
            preferred_element_type=F32)

    for j in range(S5_TILES):
        re_lo = j * S5_TILE_LANES
        im_lo = re_lo + S5_HALF
        ar = jnp.broadcast_to(are_ref[:, j * S5_HALF:(j + 1) * S5_HALF], (SUBLANES, S5_HALF))
        ai = jnp.broadcast_to(aim_ref[:, j * S5_HALF:(j + 1) * S5_HALF], (SUBLANES, S5_HALF))

        def step(t, carry, re_lo=re_lo, im_lo=im_lo, ar=ar, ai=ai):
            hr, hi = carry
            r0 = pl.multiple_of(t * SUBLANES, SUBLANES)
            bur = buf_ref[pl.ds(r0, SUBLANES), re_lo:re_lo + S5_HALF]
            bui = buf_ref[pl.ds(r0, SUBLANES), im_lo:im_lo + S5_HALF]
            nhr = ar * hr - ai * hi + bur
            nhi = ar * hi + ai * hr + bui
            buf_ref[pl.ds(r0, SUBLANES), re_lo:re_lo + S5_HALF] = nhr
            buf_ref[pl.ds(r0, SUBLANES), im_lo:im_lo + S5_HALF] = nhi
            return nhr, nhi

        hr0 = h_ref[:, re_lo:re_lo + S5_HALF]
        hi0 = h_ref[:, im_lo:im_lo + S5_HALF]
        hr, hi = lax.fori_loop(0, steps, step, (hr0, hi0), unroll=8)
        h_ref[:, re_lo:re_lo + S5_HALF] = hr
        h_ref[:, im_lo:im_lo + S5_HALF] = hi

    ys = []
    for j in range(S5_TILES):
        lo = j * S5_TILE_LANES
        hb = buf_ref[:, lo:lo + S5_TILE_LANES].astype(BF16)
        ys.append(jnp.dot(hb, cmat_ref[j], preferred_element_type=F32))
    y = jnp.concatenate(ys, axis=1) + d_ref[...] * xs.astype(F32)
    g = _gelu_tanh(y).astype(BF16)
    ab = jnp.dot(g, wglu_ref[...], preferred_element_type=F32)
    y_ref[...] = (ab[:, :SSM_WIDTH] * _sigmoid(ab[:, SSM_WIDTH:])).astype(BF16)


def _s5(xs_t, bmat, cmat, a_re, a_im, d_skip, w_glu, *, steps):
    t = xs_t.shape[0]
    rows = steps * SUBLANES
    const = lambda a: pl.BlockSpec(a.shape, lambda i: (0,) * a.ndim)
    return pl.pallas_call(
        functools.partial(_s5_kernel, steps=steps),
        grid=(t // rows,),
        in_specs=[pl.BlockSpec((rows, SSM_WIDTH), lambda i: (i, 0)),
                  const(bmat), const(cmat), const(a_re), const(a_im), const(d_skip), const(w_glu)],
        out_specs=pl.BlockSpec((rows, SSM_WIDTH), lambda i: (i, 0)),
        out_shape=jax.ShapeDtypeStruct((t, SSM_WIDTH), BF16),
        scratch_shapes=[pltpu.VMEM((rows, S5_STATE_LANES), F32),
                        pltpu.VMEM((SUBLANES, S5_STATE_LANES), F32)],
        compiler_params=_cparams("arbitrary"),
        name="s5",
    )(xs_t, bmat, cmat, a_re, a_im, d_skip, w_glu)


def _attn_kernel(qt_ref, k_ref, caug_ref, vt_ref, sza_ref, o_ref,
                 rhs_ref, s_ref, m_ref, l_ref, acc_ref, *, tq):
    qi = pl.program_id(1)
    tk = tq
    hd = ATTN_HEAD_DIM
    row = lax.broadcasted_iota(jnp.int32, (LANES, tq), 0)

    for h in range(ATTN_HEADS):
        j = h // 2
        q2 = qt_ref[0, j * LANES:(j + 1) * LANES, :]
        rhs_ref[h, 0:LANES, :] = jnp.where((row // hd) == (h % 2), q2, jnp.zeros_like(q2))
        rhs_ref[h, LANES:2 * LANES, :] = jnp.where(
            (row == h) | (row == 8 + h) | (row == 16 + h), 1.0, 0.0).astype(BF16)
    m_ref[...] = jnp.full(m_ref.shape, -jnp.inf, F32)
    l_ref[...] = jnp.zeros_like(l_ref)
    acc_ref[...] = jnp.zeros_like(acc_ref)

    def scores(k0, h):
        j = h // 2
        lhs = jnp.concatenate([k_ref[0, pl.ds(k0, tk), j * LANES:(j + 1) * LANES],
                               caug_ref[0, pl.ds(k0, tk), :]], axis=1)
        s_ref[h % ATTN_SLOTS] = jnp.dot(lhs, rhs_ref[h], preferred_element_type=F32)

    def update(k0, h, masked):
        s = s_ref[h % ATTN_SLOTS]
        if masked:
            kpos = lax.broadcasted_iota(jnp.int32, (tk, tq), 0)
            qpos = lax.broadcasted_iota(jnp.int32, (tk, tq), 1)
            s = jnp.where(kpos <= qpos, s, NEG_BIG)
        hs = slice(h * SUBLANES, h * SUBLANES + 1)
        m = m_ref[hs, :]
        m_new = jnp.maximum(m, jnp.max(s, axis=0, keepdims=True))
        alpha = jnp.exp(m - m_new)
        p = jnp.exp(s - m_new)
        l_ref[hs, :] = alpha * l_ref[hs, :] + jnp.sum(p, axis=0, keepdims=True)
        m_ref[hs, :] = m_new
        vt = vt_ref[0, h * hd:(h + 1) * hd, pl.ds(k0, tk)]
        acc_ref[h * hd:(h + 1) * hd, :] = (
            alpha * acc_ref[h * hd:(h + 1) * hd, :]
            + jnp.dot(vt, p.astype(BF16), preferred_element_type=F32))

    def body(kj, carry):
        k0 = pl.multiple_of(kj * tk, tk)
        k1 = pl.multiple_of(kj * tk + tk, tk)
        for h in range(ATTN_HEADS):
            ha = h + ATTN_LOOKAHEAD
            if ha < ATTN_HEADS:
                scores(k0, ha)
            else:
                scores(k1, ha - ATTN_HEADS)
            update(k0, h, False)
        return carry

    kd = pl.multiple_of(qi * tk, tk)
    for h in range(ATTN_LOOKAHEAD):
        scores(0, h)
    lax.fori_loop(0, qi, body, 0)
    for h in range(ATTN_HEADS):
        if h + ATTN_LOOKAHEAD < ATTN_HEADS:
            scores(kd, h + ATTN_LOOKAHEAD)
        update(kd, h, True)

    for h in range(ATTN_HEADS):
        acc_ref[h * hd:(h + 1) * hd, :] = (acc_ref[h * hd:(h + 1) * hd, :]
                                           / l_ref[h * SUBLANES:h * SUBLANES + 1, :])
    o = acc_ref[...].T
    o_ref[0] = (o * sza_ref[0].astype(F32)).astype(BF16)


def _attn(qt, k, caug, vt, sza, *, tq):
    nb, s, w = k.shape
    return pl.pallas_call(
        functools.partial(_attn_kernel, tq=tq),
        grid=(nb, s // tq),
        in_specs=[pl.BlockSpec((1, w, tq), lambda b, i: (b, 0, i)),
                  pl.BlockSpec((1, s, w), lambda b, i: (b, 0, 0)),
                  pl.BlockSpec((1, s, LANES), lambda b, i: (b, 0, 0)),
                  pl.BlockSpec((1, w, s), lambda b, i: (b, 0, 0)),
                  pl.BlockSpec((1, tq, w), lambda b, i: (b, i, 0))],
        out_specs=pl.BlockSpec((1, tq, w), lambda b, i: (b, i, 0)),
        out_shape=jax.ShapeDtypeStruct((nb, s, w), BF16),
        scratch_shapes=[pltpu.VMEM((ATTN_HEADS, 2 * LANES, tq), BF16),
                        pltpu.VMEM((ATTN_SLOTS, tq, tq), F32),
                        pltpu.VMEM((ATTN_HEADS * SUBLANES, tq), F32),
                        pltpu.VMEM((ATTN_HEADS * SUBLANES, tq), F32),
                        pltpu.VMEM((w, tq), F32)],
        compiler_params=_cparams("arbitrary", "arbitrary"),
        name="attn",
    )(qt, k, caug, vt, sza)


def _merge_kernel(x_ref, ys_ref, szs_ref, oz_ref, sgs_ref, sga_ref, gate_ref,
                  wps_ref, wpa_ref, wout_ref, lng_ref, lnb_ref, o_ref):
    a = (ys_ref[...].astype(F32) * szs_ref[0].astype(F32)).astype(BF16)
    m1 = jnp.dot(a, wps_ref[...], preferred_element_type=F32) * sgs_ref[0].astype(F32)
    m2 = jnp.dot(oz_ref[0], wpa_ref[...], preferred_element_type=F32) * sga_ref[0].astype(F32)
    merged = (m1 + m2).astype(BF16)
    sub = jnp.dot(merged, wout_ref[...], preferred_element_type=F32) * gate_ref[0]
    y = DEEPNORM_ALPHA * x_ref[0] + sub
    mu = jnp.mean(y, axis=-1, keepdims=True)
    yc = y - mu
    var = jnp.mean(yc * yc, axis=-1, keepdims=True)
    o_ref[0] = yc * lax.rsqrt(var + LN_EPS) * lng_ref[...] + lnb_ref[...]


def _merge(x, ys_t, szs, oz, sgs, sga, gate, wps, wpa, wout, ln_g, ln_b, *, ts):
    nb, s, d = x.shape
    tok3 = lambda w: pl.BlockSpec((1, ts, w), lambda b, i: (b, i, 0))
    const2 = lambda a: pl.BlockSpec(a.shape, lambda b, i: (0, 0))
    return pl.pallas_call(
        _merge_kernel,
        grid=(nb, s // ts),
        in_specs=[tok3(d),
                  pl.BlockSpec((ts, SSM_WIDTH), lambda b, i: (i, b)),
                  tok3(SSM_WIDTH), tok3(ATTN_WIDTH), tok3(d), tok3(d),
                  pl.BlockSpec((1, 1, d), lambda b, i: (b, 0, 0)),
                  const2(wps), const2(wpa), const2(wout), const2(ln_g), const2(ln_b)],
        out_specs=tok3(d),
        out_shape=jax.ShapeDtypeStruct((nb, s, d), x.dtype),
        compiler_params=_cparams("arbitrary", "arbitrary"),
        name="merge",
    )(x, ys_t, szs, oz, sgs, sga, gate, wps, wpa, wout, ln_g, ln_b)


def _block_diag_tiles(blocks):
    eye = jnp.eye(S5_TILE_GROUPS, dtype=blocks.dtype)
    t, g, r, c = blocks.shape
    return jnp.einsum('tgrc,gG->tgrGc', blocks, eye).reshape(t, g * r, g * c)


def kernel(x, c, w_ada, b_ada, w_in, b_f, lam_re, lam_im, log_dt, ssm_b_re, ssm_b_im,
           ssm_c_re, ssm_c_im, ssm_d, w_glu, w_proj_ssm, w_proj_attn, w_out, ln_g, ln_b):
    nb, s, d = x.shape
    l = 0
    mod = _ada(c, w_ada[l], b_ada[l])
    shift = mod[:, None, :d]
    scale = mod[:, None, d:2 * d]
    gate = mod[:, None, 2 * d:]

    w = w_in[l]
    o_xs, o_zs, o_q, o_k, o_v, o_f, o_za, o_gs, o_ga, o_end = (
        0, 512, 1024, 1536, 2048, 2560, 2568, 3080, 4104, 5128)
    w_main = jnp.concatenate([w[:, o_xs:o_zs], w[:, o_zs:o_q], w[:, o_k:o_v],
                              w[:, o_za:o_gs], w[:, o_gs:o_ga], w[:, o_ga:o_end]],
                             axis=1).astype(BF16)
    w_t = jnp.concatenate([w[:, o_q:o_k], w[:, o_v:o_f]], axis=1).T.astype(BF16)
    wf = w[:, o_f:o_za]
    w_f3 = jnp.concatenate([wf, wf, wf, jnp.zeros((d, LANES - 3 * ATTN_HEADS), w.dtype)],
                           axis=1).astype(BF16)
    bfv = b_f[l].astype(F32)
    b_f3 = jnp.concatenate([bfv, bfv, bfv, jnp.zeros((LANES - 3 * ATTN_HEADS,), F32)])[None, :]

    (xs_t, szs, k, sza, sgs, sga, qt, vt, caug) = _inproj(
        x, scale, shift, w_main, w_t, w_f3, b_f3, ts=512)

    abr, abi, bbr, bbi, ncim = _zoh(lam_re[l], lam_im[l], log_dt[l], ssm_b_re[l], ssm_b_im[l],
                                    ssm_c_im[l])
    g4 = (S5_TILES, S5_TILE_GROUPS)
    bbr = bbr.reshape(g4 + (SSM_GROUP, SSM_STATE))
    bbi = bbi.reshape(g4 + (SSM_GROUP, SSM_STATE))
    bmat = jnp.concatenate([_block_diag_tiles(bbr), _block_diag_tiles(bbi)], axis=2).astype(BF16)
    cre_t = ssm_c_re[l].reshape(g4 + (SSM_GROUP, SSM_STATE)).transpose(0, 1, 3, 2)
    ncim_t = ncim.reshape(g4 + (SSM_GROUP, SSM_STATE)).transpose(0, 1, 3, 2)
    cmat = jnp.concatenate([_block_diag_tiles(cre_t), _block_diag_tiles(ncim_t)],
                           axis=1).astype(BF16)
    a_re = abr.reshape(SSM_GROUPS, SSM_GROUP, SSM_STATE)[:, 0, :].reshape(1, SSM_GROUPS * SSM_STATE)
    a_im = abi.reshape(SSM_GROUPS, SSM_GROUP, SSM_STATE)[:, 0, :].reshape(1, SSM_GROUPS * SSM_STATE)

    ys_t = _s5(xs_t.reshape(s * nb, SSM_WIDTH), bmat, cmat, a_re, a_im,
               ssm_d[l].reshape(1, SSM_WIDTH).astype(F32), w_glu[l].astype(BF16), steps=64)

    oz = _attn(qt, k, caug, vt, sza, tq=256)

    return _merge(x, ys_t.reshape(s, nb * SSM_WIDTH), szs, oz, sgs, sga, gate,
                  w_proj_ssm[l].astype(BF16), w_proj_attn[l].astype(BF16), w_out[l].astype(BF16),
                  ln_g[l].reshape(1, d).astype(F32), ln_b[l].astype(F32).reshape(1, d), ts=512)
```

```python
import functools
import math

import jax
import jax.numpy as jnp
from jax import lax
from jax.experimental import pallas as pl
from jax.experimental.pallas import tpu as pltpu

D_MODEL = 1024
BATCH = 8
SEQ = 2048
SSM_WIDTH = 512
SSM_GROUP = 16
SSM_GROUPS = 32
SSM_STATE = 64
ATTN_HEADS = 8
ATTN_HEAD_DIM = 64
ATTN_WIDTH = 512
LN_EPS = 1e-5
DEEPNORM_ALPHA = 2.0 ** 0.25

F32 = jnp.float32
BF16 = jnp.bfloat16

LANES = 128
SUBLANES = 8
VMEM_LIMIT = 52 * 1024 * 1024

S5_TILES = 4
S5_TILE_GROUPS = SSM_GROUPS // S5_TILES
S5_HALF = S5_TILE_GROUPS * SSM_STATE
S5_TILE_LANES = 2 * S5_HALF
S5_TILE_IN = S5_TILE_GROUPS * SSM_GROUP
S5_STATE_LANES = S5_TILES * S5_TILE_LANES

C_XS, C_ZS, C_K, C_ZA, C_GS, C_GA, C_END = 0, 512, 1024, 1536, 2048, 3072, 4096

NEG_BIG = -1e30

ATTN_LOOKAHEAD = 3
ATTN_SLOTS = 4
assert ATTN_SLOTS > ATTN_LOOKAHEAD and ATTN_HEADS % ATTN_SLOTS == 0


def _sigmoid(x):
    return 1.0 / (1.0 + jnp.exp(-x))


def _silu(x):
    return x * _sigmoid(x)


def _cparams(*sem):
    return pltpu.CompilerParams(dimension_semantics=sem, vmem_limit_bytes=VMEM_LIMIT)


def _ada_kernel(c_ref, w_ref, b_ref, o_ref):
    c = c_ref[...]
    ca = _silu(c)
    o_ref[...] = jnp.dot(ca, w_ref[...], preferred_element_type=F32,
                         precision=lax.Precision.HIGHEST) + b_ref[...]


def _ada(c, w_ada, b_ada):
    n = w_ada.shape[1]
    tn = 512
    return pl.pallas_call(
        _ada_kernel,
        grid=(n // tn,),
        in_specs=[pl.BlockSpec((BATCH, D_MODEL), lambda j: (0, 0)),
                  pl.BlockSpec((D_MODEL, tn), lambda j: (0, j)),
                  pl.BlockSpec((1, tn), lambda j: (0, j))],
        out_specs=pl.BlockSpec((BATCH, tn), lambda j: (0, j)),
        out_shape=jax.ShapeDtypeStruct((BATCH, n), F32),
        compiler_params=_cparams("arbitrary"),
        name="ada",
    )(c, w_ada, b_ada.reshape(1, n))


def _zoh_kernel(lr_ref, li_ref, ldt_ref, br_ref, bi_ref, cim_ref,
                abr_ref, abi_ref, bbr_ref, bbi_ref, ncim_ref):
    lr = lr_ref[...]
    li = li_ref[...]
    dt = jnp.exp(ldt_ref[...])
    mag = jnp.exp(lr * dt)
    abr = mag * jnp.cos(li * dt)
    abi = mag * jnp.sin(li * dt)
    den = lr * lr + li * li
    cr = ((abr - 1.0) * lr + abi * li) / den
    ci = (abi * lr - (abr - 1.0) * li) / den
    br = br_ref[...]
    bi = bi_ref[...]
    abr_ref[...] = abr
    abi_ref[...] = abi
    bbr_ref[...] = cr * br - ci * bi
    bbi_ref[...] = cr * bi + ci * br
    ncim_ref[...] = -cim_ref[...]


def _zoh(lam_re, lam_im, log_dt, b_re, b_im, c_im):
    rows = SSM_GROUPS * SSM_GROUP
    rep = lambda a: jnp.repeat(a, SSM_GROUP, axis=0)
    lr = rep(lam_re)
    li = rep(lam_im)
    ldt = jnp.broadcast_to(rep(log_dt[:, None]), (rows, SSM_STATE))
    brt = b_re.transpose(0, 2, 1).reshape(rows, SSM_STATE)
    bit = b_im.transpose(0, 2, 1).reshape(rows, SSM_STATE)
    cim = c_im.reshape(rows, SSM_STATE)
    shp = jax.ShapeDtypeStruct((rows, SSM_STATE), F32)
    return pl.pallas_call(
        _zoh_kernel,
        out_shape=(shp,) * 5,
        name="zoh",
    )(lr, li, ldt, brt, bit, cim)


def _inproj_kernel(x_ref, scale_ref, shift_ref, wm_ref, wt_ref, wf_ref, bf_ref, tri_ref,
                   xs_ref, szs_ref, k_ref, sza_ref, sgs_ref, sga_ref, qt_ref, vt_ref, caug_ref,
                   carry_ref, *, ts):
    i = pl.program_id(0)
    b = pl.program_id(1)
    x = x_ref[0]
    mu = jnp.mean(x, axis=-1, keepdims=True)
    xc = x - mu
    var = jnp.mean(xc * xc, axis=-1, keepdims=True)
    xn = xc * lax.rsqrt(var + LN_EPS)
    u = xn * (1.0 + scale_ref[0]) + shift_ref[0]
    ub = u.astype(BF16)

    def mm(lo, hi):
        return jnp.dot(ub, wm_ref[:, lo:hi], preferred_element_type=F32)

    xs = mm(C_XS, C_ZS)
    for j in range(S5_TILES):
        xs_ref[j, pl.ds(b, ts, stride=SUBLANES), :] = xs[:, j * LANES:(j + 1) * LANES]
    szs_ref[0] = _silu(mm(C_ZS, C_K)).astype(BF16)
    k_ref[0] = mm(C_K, C_ZA).astype(BF16)
    sza_ref[0] = _silu(mm(C_ZA, C_GS)).astype(BF16)
    sgs_ref[0] = _sigmoid(mm(C_GS, C_GA)).astype(BF16)
    sga_ref[0] = _sigmoid(mm(C_GA, C_END)).astype(BF16)

    qv = lax.dot_general(wt_ref[...], ub, (((1,), (1,)), ((), ())),
                         preferred_element_type=F32)
    qt_ref[0] = (qv[:ATTN_WIDTH] * (ATTN_HEAD_DIM ** -0.5)).astype(BF16)
    vt_ref[0] = qv[ATTN_WIDTH:].astype(BF16)

    f = jnp.dot(ub, wf_ref[...], preferred_element_type=F32) + bf_ref[...]
    lf = jnp.minimum(f, 0.0) - jnp.log(1.0 + jnp.exp(-jnp.abs(f)))

    @pl.when(i == 0)
    def _():
        carry_ref[b] = jnp.zeros(carry_ref.shape[1:], F32)

    l1 = lf.astype(BF16)
    r1 = lf - l1.astype(F32)
    l2 = r1.astype(BF16)
    l3 = (r1 - l2.astype(F32)).astype(BF16)
    tri = tri_ref[...]
    cf = (jnp.dot(tri, l1, preferred_element_type=F32)
          + jnp.dot(tri, l2, preferred_element_type=F32)
          + jnp.dot(tri, l3, preferred_element_type=F32)) + carry_ref[b][0:1, :]
    carry_ref[b] = jnp.broadcast_to(cf[ts - 1:ts, :], carry_ref.shape[1:])

    c1 = cf.astype(BF16).astype(F32)
    r = cf - c1
    c2 = r.astype(BF16).astype(F32)
    c3 = r - c2
    lane = lax.broadcasted_iota(jnp.int32, cf.shape, 1)
    parts = jnp.where(lane < 8, c1, jnp.where(lane < 16, c2, jnp.where(lane < 24, c3, 0.0)))
    caug_ref[0] = (-parts).astype(BF16)


def _inproj(x, scale, shift, w_main, w_t, w_f3, b_f3, *, ts):
    nb, s, d = x.shape
    nt = s // ts
    tri = jnp.tril(jnp.ones((ts, ts), BF16))
    tok3 = lambda w: pl.BlockSpec((1, ts, w), lambda i, b: (b, i, 0))
    const2 = lambda a: pl.BlockSpec(a.shape, lambda i, b: (0, 0))
    bf = lambda shape: jax.ShapeDtypeStruct(shape, BF16)
    return pl.pallas_call(
        functools.partial(_inproj_kernel, ts=ts),
        grid=(nt, nb),
        in_specs=[tok3(d),
                  pl.BlockSpec((1, 1, d), lambda i, b: (b, 0, 0)),
                  pl.BlockSpec((1, 1, d), lambda i, b: (b, 0, 0)),
                  const2(w_main), const2(w_t), const2(w_f3), const2(b_f3), const2(tri)],
        out_specs=[pl.BlockSpec((S5_TILES, ts * nb, LANES), lambda i, b: (0, i, 0)),
                   tok3(SSM_WIDTH), tok3(ATTN_WIDTH), tok3(ATTN_WIDTH),
                   tok3(D_MODEL), tok3(D_MODEL),
                   pl.BlockSpec((1, ATTN_WIDTH, ts), lambda i, b: (b, 0, i)),
                   pl.BlockSpec((1, ATTN_WIDTH, ts), lambda i, b: (b, 0, i)),
                   tok3(LANES)],
        out_shape=[jax.ShapeDtypeStruct((S5_TILES, s * nb, LANES), F32),
                   bf((nb, s, SSM_WIDTH)), bf((nb, s, ATTN_WIDTH)), bf((nb, s, ATTN_WIDTH)),
                   bf((nb, s, D_MODEL)), bf((nb, s, D_MODEL)),
                   bf((nb, ATTN_WIDTH, s)), bf((nb, ATTN_WIDTH, s)),
                   bf((nb, s, LANES))],
        scratch_shapes=[pltpu.VMEM((nb, SUBLANES, LANES), F32)],
        compiler_params=_cparams("arbitrary", "arbitrary"),
        name="inproj",
    )(x, scale, shift, w_main, w_t, w_f3, b_f3, tri)


def _gelu_tanh(x):
    return 0.5 * x * (1.0 + jnp.tanh(math.sqrt(2.0 / math.pi) * (x + 0.044715 * (x * x * x))))


def _s5_kernel(xs_ref, bmat_ref, cmat_ref, are_ref, aim_ref, d_ref, wglu_ref,
               y_ref, buf_ref, h_ref, *, steps):
    @pl.when(pl.program_id(0) == 0)
    def _():
        h_ref[...] = jnp.zeros_like(h_ref)

    for j in range(S5_TILES):
        lo = j * S5_TILE_LANES
        buf_ref[:, lo:lo + S5_TILE_LANES] = jnp.dot(
            xs_ref[j].astype(BF16), bmat_ref[j], preferred_element_type=F32)

    for j in range(S5_TILES):
        re_lo = j * S5_TILE_LANES
        im_lo = re_lo + S5_HALF
        ar = jnp.broadcast_to(are_ref[:, j * S5_HALF:(j + 1) * S5_HALF], (SUBLANES, S5_HALF))
        ai = jnp.broadcast_to(aim_ref[:, j * S5_HALF:(j + 1) * S5_HALF], (SUBLANES, S5_HALF))

        def step(t, carry, re_lo=re_lo, im_lo=im_lo, ar=ar, ai=ai):
            hr, hi = carry
            r0 = pl.multiple_of(t * SUBLANES, SUBLANES)
            bur = buf_ref[pl.ds(r0, SUBLANES), re_lo:re_lo + S5_HALF]
            bui = buf_ref[pl.ds(r0, SUBLANES), im_lo:im_lo + S5_HALF]
            nhr = ar * hr - ai * hi + bur
            nhi = ar * hi + ai * hr + bui
            buf_ref[pl.ds(r0, SUBLANES), re_lo:re_lo + S5_HALF] = nhr
            buf_ref[pl.ds(r0, SUBLANES), im_lo:im_lo + S5_HALF] = nhi
            return nhr, nhi

        hr0 = h_ref[:, re_lo:re_lo + S5_HALF]
        hi0 = h_ref[:, im_lo:im_lo + S5_HALF]
        hr, hi = lax.fori_loop(0, steps, step, (hr0, hi0), unroll=8)
        h_ref[:, re_lo:re_lo + S5_HALF] = hr
        h_ref[:, im_lo:im_lo + S5_HALF] = hi

    ys = []
    for j in range(S5_TILES):
        lo = j * S5_TILE_LANES
        hb = buf_ref[:, lo:lo + S5_TILE_LANES].astype(BF16)
        ys.append(jnp.dot(hb, cmat_ref[j], preferred_element_type=F32)
                  + d_ref[:, j * LANES:(j + 1) * LANES] * xs_ref[j])
    g = _gelu_tanh(jnp.concatenate(ys, axis=1)).astype(BF16)
    ab = jnp.dot(g, wglu_ref[...], preferred_element_type=F32)
    out = ab[:, :SSM_WIDTH] * _sigmoid(ab[:, SSM_WIDTH:])
    for j in range(S5_TILES):
        y_ref[j] = out[:, j * LANES:(j + 1) * LANES]


def _s5(xs_t, bmat, cmat, a_re, a_im, d_skip, w_glu, *, steps):
    t = xs_t.shape[1]
    rows = steps * SUBLANES
    const = lambda a: pl.BlockSpec(a.shape, lambda i: (0,) * a.ndim)
    tiles = pl.BlockSpec((S5_TILES, rows, LANES), lambda i: (0, i, 0))
    return pl.pallas_call(
        functools.partial(_s5_kernel, steps=steps),
        grid=(t // rows,),
        in_specs=[tiles,
                  const(bmat), const(cmat), const(a_re), const(a_im), const(d_skip), const(w_glu)],
        out_specs=tiles,
        out_shape=jax.ShapeDtypeStruct(xs_t.shape, F32),
        scratch_shapes=[pltpu.VMEM((rows, S5_STATE_LANES), F32),
                        pltpu.VMEM((SUBLANES, S5_STATE_LANES), F32)],
        compiler_params=_cparams("arbitrary"),
        name="s5",
    )(xs_t, bmat, cmat, a_re, a_im, d_skip, w_glu)


def _attn_kernel(qt_ref, k_ref, caug_ref, vt_ref, sza_ref, o_ref,
                 rhs_ref, s_ref, m_ref, l_ref, acc_ref, *, tq):
    qi = pl.program_id(1)
    tk = tq
    hd = ATTN_HEAD_DIM
    row = lax.broadcasted_iota(jnp.int32, (LANES, tq), 0)

    for h in range(ATTN_HEADS):
        j = h // 2
        q2 = qt_ref[0, j * LANES:(j + 1) * LANES, :]
        rhs_ref[h, 0:LANES, :] = jnp.where((row // hd) == (h % 2), q2, jnp.zeros_like(q2))
        rhs_ref[h, LANES:2 * LANES, :] = jnp.where(
            (row == h) | (row == 8 + h) | (row == 16 + h), 1.0, 0.0).astype(BF16)
    m_ref[...] = jnp.full(m_ref.shape, -jnp.inf, F32)
    l_ref[...] = jnp.zeros_like(l_ref)
    acc_ref[...] = jnp.zeros_like(acc_ref)

    def scores(k0, h):
        j = h // 2
        lhs = jnp.concatenate([k_ref[0, pl.ds(k0, tk), j * LANES:(j + 1) * LANES],
                               caug_ref[0, pl.ds(k0, tk), :]], axis=1)
        s_ref[h % ATTN_SLOTS] = jnp.dot(lhs, rhs_ref[h], preferred_element_type=F32)

    def update(k0, h, masked):
        s = s_ref[h % ATTN_SLOTS]
        if masked:
            kpos = lax.broadcasted_iota(jnp.int32, (tk, tq), 0)
            qpos = lax.broadcasted_iota(jnp.int32, (tk, tq), 1)
            s = jnp.where(kpos <= qpos, s, NEG_BIG)
        hs = slice(h * SUBLANES, h * SUBLANES + 1)
        m = m_ref[hs, :]
        m_new = jnp.maximum(m, jnp.max(s, axis=0, keepdims=True))
        alpha = jnp.exp(m - m_new)
        p = jnp.exp(s - m_new)
        l_ref[hs, :] = alpha * l_ref[hs, :] + jnp.sum(p, axis=0, keepdims=True)
        m_ref[hs, :] = m_new
        vt = vt_ref[0, h * hd:(h + 1) * hd, pl.ds(k0, tk)]
        acc_ref[h * hd:(h + 1) * hd, :] = (
            alpha * acc_ref[h * hd:(h + 1) * hd, :]
            + jnp.dot(vt, p.astype(BF16), preferred_element_type=F32))

    def body(kj, carry):
        k0 = pl.multiple_of(kj * tk, tk)
        k1 = pl.multiple_of(kj * tk + tk, tk)
        for h in range(ATTN_HEADS):
            ha = h + ATTN_LOOKAHEAD
            if ha < ATTN_HEADS:
                scores(k0, ha)
            else:
                scores(k1, ha - ATTN_HEADS)
            update(k0, h, False)
        return carry

    kd = pl.multiple_of(qi * tk, tk)
    for h in range(ATTN_LOOKAHEAD):
        scores(0, h)
    lax.fori_loop(0, qi, body, 0)
    for h in range(ATTN_HEADS):
        if h + ATTN_LOOKAHEAD < ATTN_HEADS:
            scores(kd, h + ATTN_LOOKAHEAD)
        update(kd, h, True)

    for h in range(ATTN_HEADS):
        acc_ref[h * hd:(h + 1) * hd, :] = (acc_ref[h * hd:(h + 1) * hd, :]
                                           / l_ref[h * SUBLANES:h * SUBLANES + 1, :])
    o = acc_ref[...].T
    o_ref[0] = (o * sza_ref[0].astype(F32)).astype(BF16)


def _attn(qt, k, caug, vt, sza, *, tq):
    nb, s, w = k.shape
    return pl.pallas_call(
        functools.partial(_attn_kernel, tq=tq),
        grid=(nb, s // tq),
        in_specs=[pl.BlockSpec((1, w, tq), lambda b, i: (b, 0, i)),
                  pl.BlockSpec((1, s, w), lambda b, i: (b, 0, 0)),
                  pl.BlockSpec((1, s, LANES), lambda b, i: (b, 0, 0)),
                  pl.BlockSpec((1, w, s), lambda b, i: (b, 0, 0)),
                  pl.BlockSpec((1, tq, w), lambda b, i: (b, i, 0))],
        out_specs=pl.BlockSpec((1, tq, w), lambda b, i: (b, i, 0)),
        out_shape=jax.ShapeDtypeStruct((nb, s, w), BF16),
        scratch_shapes=[pltpu.VMEM((ATTN_HEADS, 2 * LANES, tq), BF16),
                        pltpu.VMEM((ATTN_SLOTS, tq, tq), F32),
                        pltpu.VMEM((ATTN_HEADS * SUBLANES, tq), F32),
                        pltpu.VMEM((ATTN_HEADS * SUBLANES, tq), F32),
                        pltpu.VMEM((w, tq), F32)],
        compiler_params=_cparams("arbitrary", "arbitrary"),
        name="attn",
    )(qt, k, caug, vt, sza)


def _merge_kernel(x_ref, ys_ref, szs_ref, oz_ref, sgs_ref, sga_ref, gate_ref,
                  wps_ref, wpa_ref, wout_ref, lng_ref, lnb_ref, o_ref, *, ts):
    b = pl.program_id(1)
    ys = jnp.concatenate([ys_ref[j, pl.ds(b, ts, stride=SUBLANES), :] for j in range(S5_TILES)],
                         axis=1)
    a = (ys * szs_ref[0].astype(F32)).astype(BF16)
    m1 = jnp.dot(a, wps_ref[...], preferred_element_type=F32) * sgs_ref[0].astype(F32)
    m2 = jnp.dot(oz_ref[0], wpa_ref[...], preferred_element_type=F32) * sga_ref[0].astype(F32)
    merged = (m1 + m2).astype(BF16)
    sub = jnp.dot(merged, wout_ref[...], preferred_element_type=F32) * gate_ref[0]
    y = DEEPNORM_ALPHA * x_ref[0] + sub
    mu = jnp.mean(y, axis=-1, keepdims=True)
    yc = y - mu
    var = jnp.mean(yc * yc, axis=-1, keepdims=True)
    o_ref[0] = yc * lax.rsqrt(var + LN_EPS) * lng_ref[...] + lnb_ref[...]


def _merge(x, ys_t, szs, oz, sgs, sga, gate, wps, wpa, wout, ln_g, ln_b, *, ts):
    nb, s, d = x.shape
    tok3 = lambda w: pl.BlockSpec((1, ts, w), lambda i, b: (b, i, 0))
    const2 = lambda a: pl.BlockSpec(a.shape, lambda i, b: (0, 0))
    return pl.pallas_call(
        functools.partial(_merge_kernel, ts=ts),
        grid=(s // ts, nb),
        in_specs=[tok3(d),
                  pl.BlockSpec((S5_TILES, ts * nb, LANES), lambda i, b: (0, i, 0)),
                  tok3(SSM_WIDTH), tok3(ATTN_WIDTH), tok3(d), tok3(d),
                  pl.BlockSpec((1, 1, d), lambda i, b: (b, 0, 0)),
                  const2(wps), const2(wpa), const2(wout), const2(ln_g), const2(ln_b)],
        out_specs=tok3(d),
        out_shape=jax.ShapeDtypeStruct((nb, s, d), x.dtype),
        compiler_params=_cparams("arbitrary", "arbitrary"),
        name="merge",
    )(x, ys_t, szs, oz, sgs, sga, gate, wps, wpa, wout, ln_g, ln_b)


def _block_diag_tiles(blocks):
    eye = jnp.eye(S5_TILE_GROUPS, dtype=blocks.dtype)
    t, g, r, c = blocks.shape
    return jnp.einsum('tgrc,gG->tgrGc', blocks, eye).reshape(t, g * r, g * c)


def kernel(x, c, w_ada, b_ada, w_in, b_f, lam_re, lam_im, log_dt, ssm_b_re, ssm_b_im,
           ssm_c_re, ssm_c_im, ssm_d, w_glu, w_proj_ssm, w_proj_attn, w_out, ln_g, ln_b):
    nb, s, d = x.shape
    l = 0
    mod = _ada(c, w_ada[l], b_ada[l])
    shift = mod[:, None, :d]
    scale = mod[:, None, d:2 * d]
    gate = mod[:, None, 2 * d:]

    w = w_in[l]
    o_xs, o_zs, o_q, o_k, o_v, o_f, o_za, o_gs, o_ga, o_end = (
        0, 512, 1024, 1536, 2048, 2560, 2568, 3080, 4104, 5128)
    w_main = jnp.concatenate([w[:, o_xs:o_zs], w[:, o_zs:o_q], w[:, o_k:o_v],
                              w[:, o_za:o_gs], w[:, o_gs:o_ga], w[:, o_ga:o_end]],
                             axis=1).astype(BF16)
    w_t = jnp.concatenate([w[:, o_q:o_k], w[:, o_v:o_f]], axis=1).T.astype(BF16)
    wf = w[:, o_f:o_za]
    w_f3 = jnp.concatenate([wf, wf, wf, jnp.zeros((d, LANES - 3 * ATTN_HEADS), w.dtype)],
                           axis=1).astype(BF16)
    bfv = b_f[l].astype(F32)
    b_f3 = jnp.concatenate([bfv, bfv, bfv, jnp.zeros((LANES - 3 * ATTN_HEADS,), F32)])[None, :]

    (xs_t, szs, k, sza, sgs, sga, qt, vt, caug) = _inproj(
        x, scale, shift, w_main, w_t, w_f3, b_f3, ts=256)

    abr, abi, bbr, bbi, ncim = _zoh(lam_re[l], lam_im[l], log_dt[l], ssm_b_re[l], ssm_b_im[l],
                                    ssm_c_im[l])
    g4 = (S5_TILES, S5_TILE_GROUPS)
    bbr = bbr.reshape(g4 + (SSM_GROUP, SSM_STATE))
    bbi = bbi.reshape(g4 + (SSM_GROUP, SSM_STATE))
    bmat = jnp.concatenate([_block_diag_tiles(bbr), _block_diag_tiles(bbi)], axis=2).astype(BF16)
    cre_t = ssm_c_re[l].reshape(g4 + (SSM_GROUP, SSM_STATE)).transpose(0, 1, 3, 2)
    ncim_t = ncim.reshape(g4 + (SSM_GROUP, SSM_STATE)).transpose(0, 1, 3, 2)
    cmat = jnp.concatenate([_block_diag_tiles(cre_t), _block_diag_tiles(ncim_t)],
                           axis=1).astype(BF16)
    a_re = abr.reshape(SSM_GROUPS, SSM_GROUP, SSM_STATE)[:, 0, :].reshape(1, SSM_GROUPS * SSM_STATE)
    a_im = abi.reshape(SSM_GROUPS, SSM_GROUP, SSM_STATE)[:, 0, :].reshape(1, SSM_GROUPS * SSM_STATE)

    ys_t = _s5(xs_t, bmat, cmat, a_re, a_im,
               ssm_d[l].reshape(1, SSM_WIDTH).astype(F32), w_glu[l].astype(BF16), steps=64)

    oz = _attn(qt, k, caug, vt, sza, tq=256)

    return _merge(x, ys_t, szs, oz, sgs, sga, gate,
                  w_proj_ssm[l].astype(BF16), w_proj_attn[l].astype(BF16), w_out[l].astype(BF16),
                  ln_g[l].reshape(1, d).astype(F32), ln_b[l].astype(F32).reshape(1, d), ts=512)
```

```python
import functools
import math

import jax
import jax.numpy as jnp
from jax import lax
from jax.experimental import pallas as pl
from jax.experimental.pallas import tpu as pltpu

D_MODEL = 1024
BATCH = 8
SEQ = 2048
SSM_WIDTH = 512
SSM_GROUP = 16
SSM_GROUPS = 32
SSM_STATE = 64
ATTN_HEADS = 8
ATTN_HEAD_DIM = 64
ATTN_WIDTH = 512
LN_EPS = 1e-5
DEEPNORM_ALPHA = 2.0 ** 0.25

F32 = jnp.float32
BF16 = jnp.bfloat16

LANES = 128
SUBLANES = 8
VMEM_LIMIT = 52 * 1024 * 1024

S5_TILES = 4
S5_TILE_GROUPS = SSM_GROUPS // S5_TILES
S5_HALF = S5_TILE_GROUPS * SSM_STATE
S5_TILE_LANES = 2 * S5_HALF
S5_TILE_IN = S5_TILE_GROUPS * SSM_GROUP
S5_STATE_LANES = S5_TILES * S5_TILE_LANES

C_XS, C_ZS, C_K, C_ZA, C_GS, C_GA, C_END = 0, 512, 1024, 1536, 2048, 3072, 4096

NEG_BIG = -1e30
LOG2E = math.log2(math.e)

ATTN_LOOKAHEAD = 3
ATTN_SLOTS = 4
assert ATTN_SLOTS > ATTN_LOOKAHEAD and ATTN_HEADS % ATTN_SLOTS == 0


def _sigmoid(x):
    return 1.0 / (1.0 + jnp.exp(-x))


def _silu(x):
    return x * _sigmoid(x)


def _cparams(*sem):
    return pltpu.CompilerParams(dimension_semantics=sem, vmem_limit_bytes=VMEM_LIMIT)


def _ada_kernel(c_ref, w_ref, b_ref, o_ref):
    c = c_ref[...]
    ca = _silu(c)
    o_ref[...] = jnp.dot(ca, w_ref[...], preferred_element_type=F32,
                         precision=lax.Precision.HIGHEST) + b_ref[...]


def _ada(c, w_ada, b_ada):
    n = w_ada.shape[1]
    tn = 512
    return pl.pallas_call(
        _ada_kernel,
        grid=(n // tn,),
        in_specs=[pl.BlockSpec((BATCH, D_MODEL), lambda j: (0, 0)),
                  pl.BlockSpec((D_MODEL, tn), lambda j: (0, j)),
                  pl.BlockSpec((1, tn), lambda j: (0, j))],
        out_specs=pl.BlockSpec((BATCH, tn), lambda j: (0, j)),
        out_shape=jax.ShapeDtypeStruct((BATCH, n), F32),
        compiler_params=_cparams("arbitrary"),
        name="ada",
    )(c, w_ada, b_ada.reshape(1, n))


def _zoh_kernel(lr_ref, li_ref, ldt_ref, br_ref, bi_ref, cim_ref,
                abr_ref, abi_ref, bbr_ref, bbi_ref, ncim_ref):
    lr = lr_ref[...]
    li = li_ref[...]
    dt = jnp.exp(ldt_ref[...])
    mag = jnp.exp(lr * dt)
    abr = mag * jnp.cos(li * dt)
    abi = mag * jnp.sin(li * dt)
    den = lr * lr + li * li
    cr = ((abr - 1.0) * lr + abi * li) / den
    ci = (abi * lr - (abr - 1.0) * li) / den
    br = br_ref[...]
    bi = bi_ref[...]
    abr_ref[...] = abr
    abi_ref[...] = abi
    bbr_ref[...] = cr * br - ci * bi
    bbi_ref[...] = cr * bi + ci * br
    ncim_ref[...] = -cim_ref[...]


def _zoh(lam_re, lam_im, log_dt, b_re, b_im, c_im):
    rows = SSM_GROUPS * SSM_GROUP
    rep = lambda a: jnp.repeat(a, SSM_GROUP, axis=0)
    lr = rep(lam_re)
    li = rep(lam_im)
    ldt = jnp.broadcast_to(rep(log_dt[:, None]), (rows, SSM_STATE))
    brt = b_re.transpose(0, 2, 1).reshape(rows, SSM_STATE)
    bit = b_im.transpose(0, 2, 1).reshape(rows, SSM_STATE)
    cim = c_im.reshape(rows, SSM_STATE)
    shp = jax.ShapeDtypeStruct((rows, SSM_STATE), F32)
    return pl.pallas_call(
        _zoh_kernel,
        out_shape=(shp,) * 5,
        name="zoh",
    )(lr, li, ldt, brt, bit, cim)


def _inproj_kernel(x_ref, scale_ref, shift_ref, wm_ref, wt_ref, wf_ref, bf_ref, tri_ref,
                   xs_ref, szs_ref, k_ref, sza_ref, sgs_ref, sga_ref, qt_ref, vt_ref, caug_ref,
                   carry_ref, *, ts):
    i = pl.program_id(0)
    b = pl.program_id(1)
    x = x_ref[0]
    mu = jnp.mean(x, axis=-1, keepdims=True)
    xc = x - mu
    var = jnp.mean(xc * xc, axis=-1, keepdims=True)
    xn = xc * lax.rsqrt(var + LN_EPS)
    u = xn * (1.0 + scale_ref[0]) + shift_ref[0]
    ub = u.astype(BF16)

    def mm(lo, hi):
        return jnp.dot(ub, wm_ref[:, lo:hi], preferred_element_type=F32)

    xs = mm(C_XS, C_ZS)
    for j in range(S5_TILES):
        xs_ref[j, pl.ds(b, ts, stride=SUBLANES), :] = xs[:, j * LANES:(j + 1) * LANES]
    szs_ref[0] = _silu(mm(C_ZS, C_K)).astype(BF16)
    k_ref[0] = mm(C_K, C_ZA).astype(BF16)
    sza_ref[0] = _silu(mm(C_ZA, C_GS)).astype(BF16)
    sgs_ref[0] = _sigmoid(mm(C_GS, C_GA)).astype(BF16)
    sga_ref[0] = _sigmoid(mm(C_GA, C_END)).astype(BF16)

    qv = lax.dot_general(wt_ref[...], ub, (((1,), (1,)), ((), ())),
                         preferred_element_type=F32)
    qt_ref[0] = (qv[:ATTN_WIDTH] * (LOG2E * ATTN_HEAD_DIM ** -0.5)).astype(BF16)
    vt_ref[0] = qv[ATTN_WIDTH:].astype(BF16)

    f = jnp.dot(ub, wf_ref[...], preferred_element_type=F32) + bf_ref[...]
    lf = jnp.minimum(f, 0.0) - jnp.log(1.0 + jnp.exp(-jnp.abs(f)))

    @pl.when(i == 0)
    def _():
        carry_ref[b] = jnp.zeros(carry_ref.shape[1:], F32)

    l1 = lf.astype(BF16)
    r1 = lf - l1.astype(F32)
    l2 = r1.astype(BF16)
    l3 = (r1 - l2.astype(F32)).astype(BF16)
    tri = tri_ref[...]
    cf = (jnp.dot(tri, l1, preferred_element_type=F32)
          + jnp.dot(tri, l2, preferred_element_type=F32)
          + jnp.dot(tri, l3, preferred_element_type=F32)) + carry_ref[b][0:1, :]
    carry_ref[b] = jnp.broadcast_to(cf[ts - 1:ts, :], carry_ref.shape[1:])

    cf2 = cf * LOG2E
    c1 = cf2.astype(BF16).astype(F32)
    r = cf2 - c1
    c2 = r.astype(BF16).astype(F32)
    c3 = r - c2
    lane = lax.broadcasted_iota(jnp.int32, cf.shape, 1)
    parts = jnp.where(lane < 8, c1, jnp.where(lane < 16, c2, jnp.where(lane < 24, c3, 0.0)))
    caug_ref[0] = (-parts).astype(BF16)


def _inproj(x, scale, shift, w_main, w_t, w_f3, b_f3, *, ts):
    nb, s, d = x.shape
    nt = s // ts
    tri = jnp.tril(jnp.ones((ts, ts), BF16))
    tok3 = lambda w: pl.BlockSpec((1, ts, w), lambda i, b: (b, i, 0))
    const2 = lambda a: pl.BlockSpec(a.shape, lambda i, b: (0, 0), pipeline_mode=pl.Buffered(1))
    bf = lambda shape: jax.ShapeDtypeStruct(shape, BF16)
    return pl.pallas_call(
        functools.partial(_inproj_kernel, ts=ts),
        grid=(nt, nb),
        in_specs=[tok3(d),
                  pl.BlockSpec((1, 1, d), lambda i, b: (b, 0, 0)),
                  pl.BlockSpec((1, 1, d), lambda i, b: (b, 0, 0)),
                  const2(w_main), const2(w_t), const2(w_f3), const2(b_f3), const2(tri)],
        out_specs=[pl.BlockSpec((S5_TILES, ts * nb, LANES), lambda i, b: (0, i, 0)),
                   tok3(SSM_WIDTH), tok3(ATTN_WIDTH), tok3(ATTN_WIDTH),
                   tok3(D_MODEL), tok3(D_MODEL),
                   pl.BlockSpec((1, ATTN_WIDTH, ts), lambda i, b: (b, 0, i)),
                   pl.BlockSpec((1, ATTN_WIDTH, ts), lambda i, b: (b, 0, i)),
                   tok3(LANES)],
        out_shape=[jax.ShapeDtypeStruct((S5_TILES, s * nb, LANES), F32),
                   bf((nb, s, SSM_WIDTH)), bf((nb, s, ATTN_WIDTH)), bf((nb, s, ATTN_WIDTH)),
                   bf((nb, s, D_MODEL)), bf((nb, s, D_MODEL)),
                   bf((nb, ATTN_WIDTH, s)), bf((nb, ATTN_WIDTH, s)),
                   bf((nb, s, LANES))],
        scratch_shapes=[pltpu.VMEM((nb, SUBLANES, LANES), F32)],
        compiler_params=_cparams("arbitrary", "arbitrary"),
        name="inproj",
    )(x, scale, shift, w_main, w_t, w_f3, b_f3, tri)


def _gelu_tanh(x):
    return 0.5 * x * (1.0 + jnp.tanh(math.sqrt(2.0 / math.pi) * (x + 0.044715 * (x * x * x))))


def _s5_kernel(xs_ref, bmat_ref, cmat_ref, are_ref, aim_ref, d_ref, wglu_ref,
               y_ref, buf_ref, h_ref, *, steps):
    @pl.when(pl.program_id(0) == 0)
    def _():
        h_ref[...] = jnp.zeros_like(h_ref)

    for j in range(S5_TILES):
        lo = j * S5_TILE_LANES
        buf_ref[:, lo:lo + S5_TILE_LANES] = jnp.dot(
            xs_ref[j].astype(BF16), bmat_ref[j], preferred_element_type=F32)

    for j in range(S5_TILES):
        re_lo = j * S5_TILE_LANES
        im_lo = re_lo + S5_HALF
        ar = jnp.broadcast_to(are_ref[:, j * S5_HALF:(j + 1) * S5_HALF], (SUBLANES, S5_HALF))
        ai = jnp.broadcast_to(aim_ref[:, j * S5_HALF:(j + 1) * S5_HALF], (SUBLANES, S5_HALF))

        def step(t, carry, re_lo=re_lo, im_lo=im_lo, ar=ar, ai=ai):
            hr, hi = carry
            r0 = pl.multiple_of(t * SUBLANES, SUBLANES)
            bur = buf_ref[pl.ds(r0, SUBLANES), re_lo:re_lo + S5_HALF]
            bui = buf_ref[pl.ds(r0, SUBLANES), im_lo:im_lo + S5_HALF]
            nhr = ar * hr - ai * hi + bur
            nhi = ar * hi + ai * hr + bui
            buf_ref[pl.ds(r0, SUBLANES), re_lo:re_lo + S5_HALF] = nhr
            buf_ref[pl.ds(r0, SUBLANES), im_lo:im_lo + S5_HALF] = nhi
            return nhr, nhi

        hr0 = h_ref[:, re_lo:re_lo + S5_HALF]
        hi0 = h_ref[:, im_lo:im_lo + S5_HALF]
        hr, hi = lax.fori_loop(0, steps, step, (hr0, hi0), unroll=8)
        h_ref[:, re_lo:re_lo + S5_HALF] = hr
        h_ref[:, im_lo:im_lo + S5_HALF] = hi

    ys = []
    for j in range(S5_TILES):
        lo = j * S5_TILE_LANES
        hb = buf_ref[:, lo:lo + S5_TILE_LANES].astype(BF16)
        ys.append(jnp.dot(hb, cmat_ref[j], preferred_element_type=F32)
                  + d_ref[:, j * LANES:(j + 1) * LANES] * xs_ref[j])
    g = _gelu_tanh(jnp.concatenate(ys, axis=1)).astype(BF16)
    ab = jnp.dot(g, wglu_ref[...], preferred_element_type=F32)
    out = ab[:, :SSM_WIDTH] * _sigmoid(ab[:, SSM_WIDTH:])
    for j in range(S5_TILES):
        y_ref[j] = out[:, j * LANES:(j + 1) * LANES]


def _s5(xs_t, bmat, cmat, a_re, a_im, d_skip, w_glu, *, steps):
    t = xs_t.shape[1]
    rows = steps * SUBLANES
    const = lambda a: pl.BlockSpec(a.shape, lambda i: (0,) * a.ndim)
    tiles = pl.BlockSpec((S5_TILES, rows, LANES), lambda i: (0, i, 0))
    return pl.pallas_call(
        functools.partial(_s5_kernel, steps=steps),
        grid=(t // rows,),
        in_specs=[tiles,
                  const(bmat), const(cmat), const(a_re), const(a_im), const(d_skip), const(w_glu)],
        out_specs=tiles,
        out_shape=jax.ShapeDtypeStruct(xs_t.shape, F32),
        scratch_shapes=[pltpu.VMEM((rows, S5_STATE_LANES), F32),
                        pltpu.VMEM((SUBLANES, S5_STATE_LANES), F32)],
        compiler_params=_cparams("arbitrary"),
        name="s5",
    )(xs_t, bmat, cmat, a_re, a_im, d_skip, w_glu)


def _attn_kernel(qt_ref, k_ref, caug_ref, vt_ref, sza_ref, o_ref,
                 rhs_ref, s_ref, m_ref, l_ref, acc_ref, *, tq):
    qi = pl.program_id(1)
    tk = tq
    hd = ATTN_HEAD_DIM
    row = lax.broadcasted_iota(jnp.int32, (LANES, tq), 0)

    for h in range(ATTN_HEADS):
        j = h // 2
        q2 = qt_ref[0, j * LANES:(j + 1) * LANES, :]
        rhs_ref[h, 0:LANES, :] = jnp.where((row // hd) == (h % 2), q2, jnp.zeros_like(q2))
        rhs_ref[h, LANES:2 * LANES, :] = jnp.where(
            (row == h) | (row == 8 + h) | (row == 16 + h), 1.0, 0.0).astype(BF16)
    m_ref[...] = jnp.full(m_ref.shape, -jnp.inf, F32)
    l_ref[...] = jnp.zeros_like(l_ref)
    acc_ref[...] = jnp.zeros_like(acc_ref)

    def scores(k0, h):
        j = h // 2
        lhs = jnp.concatenate([k_ref[0, pl.ds(k0, tk), j * LANES:(j + 1) * LANES],
                               caug_ref[0, pl.ds(k0, tk), :]], axis=1)
        s_ref[h % ATTN_SLOTS] = jnp.dot(lhs, rhs_ref[h], preferred_element_type=F32)

    def update(k0, h, masked):
        s = s_ref[h % ATTN_SLOTS]
        if masked:
            kpos = lax.broadcasted_iota(jnp.int32, (tk, tq), 0)
            qpos = lax.broadcasted_iota(jnp.int32, (tk, tq), 1)
            s = jnp.where(kpos <= qpos, s, NEG_BIG)
        hs = slice(h * SUBLANES, h * SUBLANES + 1)
        m = m_ref[hs, :]
        m_new = jnp.maximum(m, jnp.max(s, axis=0, keepdims=True))
        alpha = jnp.exp2(m - m_new)
        p = jnp.exp2(s - m_new).astype(BF16)
        m_ref[hs, :] = m_new
        vt1 = jnp.concatenate([vt_ref[0, h * hd:(h + 1) * hd, pl.ds(k0, tk)],
                               jnp.ones((2 * SUBLANES, tk), BF16)], axis=0)
        pv = jnp.dot(vt1, p, preferred_element_type=F32)
        l_ref[hs, :] = alpha * l_ref[hs, :] + pv[hd:hd + 1, :]
        acc_ref[h * hd:(h + 1) * hd, :] = alpha * acc_ref[h * hd:(h + 1) * hd, :] + pv[:hd, :]

    def body(kj, carry):
        k0 = pl.multiple_of(kj * tk, tk)
        k1 = pl.multiple_of(kj * tk + tk, tk)
        for h in range(ATTN_HEADS):
            ha = h + ATTN_LOOKAHEAD
            if ha < ATTN_HEADS:
                scores(k0, ha)
            else:
                scores(k1, ha - ATTN_HEADS)
            update(k0, h, False)
        return carry

    kd = pl.multiple_of(qi * tk, tk)
    for h in range(ATTN_LOOKAHEAD):
        scores(0, h)
    lax.fori_loop(0, qi, body, 0)
    for h in range(ATTN_HEADS):
        if h + ATTN_LOOKAHEAD < ATTN_HEADS:
            scores(kd, h + ATTN_LOOKAHEAD)
        update(kd, h, True)

    for h in range(ATTN_HEADS):
        acc_ref[h * hd:(h + 1) * hd, :] = (acc_ref[h * hd:(h + 1) * hd, :]
                                           / l_ref[h * SUBLANES:h * SUBLANES + 1, :])
    o = acc_ref[...].T
    o_ref[0] = (o * sza_ref[0].astype(F32)).astype(BF16)


def _attn(qt, k, caug, vt, sza, *, tq):
    nb, s, w = k.shape
    return pl.pallas_call(
        functools.partial(_attn_kernel, tq=tq),
        grid=(nb, s // tq),
        in_specs=[pl.BlockSpec((1, w, tq), lambda b, i: (b, 0, i)),
                  pl.BlockSpec((1, s, w), lambda b, i: (b, 0, 0)),
                  pl.BlockSpec((1, s, LANES), lambda b, i: (b, 0, 0)),
                  pl.BlockSpec((1, w, s), lambda b, i: (b, 0, 0)),
                  pl.BlockSpec((1, tq, w), lambda b, i: (b, i, 0))],
        out_specs=pl.BlockSpec((1, tq, w), lambda b, i: (b, i, 0)),
        out_shape=jax.ShapeDtypeStruct((nb, s, w), BF16),
        scratch_shapes=[pltpu.VMEM((ATTN_HEADS, 2 * LANES, tq), BF16),
                        pltpu.VMEM((ATTN_SLOTS, tq, tq), F32),
                        pltpu.VMEM((ATTN_HEADS * SUBLANES, tq), F32),
                        pltpu.VMEM((ATTN_HEADS * SUBLANES, tq), F32),
                        pltpu.VMEM((w, tq), F32)],
        compiler_params=_cparams("arbitrary", "arbitrary"),
        name="attn",
    )(qt, k, caug, vt, sza)


def _merge_kernel(x_ref, ys_ref, szs_ref, oz_ref, sgs_ref, sga_ref, gate_ref,
                  wps_ref, wpa_ref, wout_ref, lng_ref, lnb_ref, o_ref, *, ts):
    b = pl.program_id(1)
    ys = jnp.concatenate([ys_ref[j, pl.ds(b, ts, stride=SUBLANES), :] for j in range(S5_TILES)],
                         axis=1)
    a = (ys * szs_ref[0].astype(F32)).astype(BF16)
    m1 = jnp.dot(a, wps_ref[...], preferred_element_type=F32) * sgs_ref[0].astype(F32)
    m2 = jnp.dot(oz_ref[0], wpa_ref[...], preferred_element_type=F32) * sga_ref[0].astype(F32)
    merged = (m1 + m2).astype(BF16)
    sub = jnp.dot(merged, wout_ref[...], preferred_element_type=F32) * gate_ref[0]
    y = DEEPNORM_ALPHA * x_ref[0] + sub
    mu = jnp.mean(y, axis=-1, keepdims=True)
    yc = y - mu
    var = jnp.mean(yc * yc, axis=-1, keepdims=True)
    o_ref[0] = yc * lax.rsqrt(var + LN_EPS) * lng_ref[...] + lnb_ref[...]


def _merge(x, ys_t, szs, oz, sgs, sga, gate, wps, wpa, wout, ln_g, ln_b, *, ts):
    nb, s, d = x.shape
    tok3 = lambda w: pl.BlockSpec((1, ts, w), lambda i, b: (b, i, 0))
    const2 = lambda a: pl.BlockSpec(a.shape, lambda i, b: (0, 0))
    return pl.pallas_call(
        functools.partial(_merge_kernel, ts=ts),
        grid=(s // ts, nb),
        in_specs=[tok3(d),
                  pl.BlockSpec((S5_TILES, ts * nb, LANES), lambda i, b: (0, i, 0)),
                  tok3(SSM_WIDTH), tok3(ATTN_WIDTH), tok3(d), tok3(d),
                  pl.BlockSpec((1, 1, d), lambda i, b: (b, 0, 0)),
                  const2(wps), const2(wpa), const2(wout), const2(ln_g), const2(ln_b)],
        out_specs=tok3(d),
        out_shape=jax.ShapeDtypeStruct((nb, s, d), x.dtype),
        compiler_params=_cparams("arbitrary", "arbitrary"),
        name="merge",
    )(x, ys_t, szs, oz, sgs, sga, gate, wps, wpa, wout, ln_g, ln_b)


def _block_diag_tiles(blocks):
    eye = jnp.eye(S5_TILE_GROUPS, dtype=blocks.dtype)
    t, g, r, c = blocks.shape
    return jnp.einsum('tgrc,gG->tgrGc', blocks, eye).reshape(t, g * r, g * c)


def kernel(x, c, w_ada, b_ada, w_in, b_f, lam_re, lam_im, log_dt, ssm_b_re, ssm_b_im,
           ssm_c_re, ssm_c_im, ssm_d, w_glu, w_proj_ssm, w_proj_attn, w_out, ln_g, ln_b):
    nb, s, d = x.shape
    l = 0
    mod = _ada(c, w_ada[l], b_ada[l])
    shift = mod[:, None, :d]
    scale = mod[:, None, d:2 * d]
    gate = mod[:, None, 2 * d:]

    w = w_in[l]
    o_xs, o_zs, o_q, o_k, o_v, o_f, o_za, o_gs, o_ga, o_end = (
        0, 512, 1024, 1536, 2048, 2560, 2568, 3080, 4104, 5128)
    w_main = jnp.concatenate([w[:, o_xs:o_zs], w[:, o_zs:o_q], w[:, o_k:o_v],
                              w[:, o_za:o_gs], w[:, o_gs:o_ga], w[:, o_ga:o_end]],
                             axis=1).astype(BF16)
    w_t = jnp.concatenate([w[:, o_q:o_k], w[:, o_v:o_f]], axis=1).T.astype(BF16)
    wf = w[:, o_f:o_za]
    w_f3 = jnp.concatenate([wf, wf, wf, jnp.zeros((d, LANES - 3 * ATTN_HEADS), w.dtype)],
                           axis=1).astype(BF16)
    bfv = b_f[l].astype(F32)
    b_f3 = jnp.concatenate([bfv, bfv, bfv, jnp.zeros((LANES - 3 * ATTN_HEADS,), F32)])[None, :]

    (xs_t, szs, k, sza, sgs, sga, qt, vt, caug) = _inproj(
        x, scale, shift, w_main, w_t, w_f3, b_f3, ts=512)

    abr, abi, bbr, bbi, ncim = _zoh(lam_re[l], lam_im[l], log_dt[l], ssm_b_re[l], ssm_b_im[l],
                                    ssm_c_im[l])
    g4 = (S5_TILES, S5_TILE_GROUPS)
    bbr = bbr.reshape(g4 + (SSM_GROUP, SSM_STATE))
    bbi = bbi.reshape(g4 + (SSM_GROUP, SSM_STATE))
    bmat = jnp.concatenate([_block_diag_tiles(bbr), _block_diag_tiles(bbi)], axis=2).astype(BF16)
    cre_t = ssm_c_re[l].reshape(g4 + (SSM_GROUP, SSM_STATE)).transpose(0, 1, 3, 2)
    ncim_t = ncim.reshape(g4 + (SSM_GROUP, SSM_STATE)).transpose(0, 1, 3, 2)
    cmat = jnp.concatenate([_block_diag_tiles(cre_t), _block_diag_tiles(ncim_t)],
                           axis=1).astype(BF16)
    a_re = abr.reshape(SSM_GROUPS, SSM_GROUP, SSM_STATE)[:, 0, :].reshape(1, SSM_GROUPS * SSM_STATE)
    a_im = abi.reshape(SSM_GROUPS, SSM_GROUP, SSM_STATE)[:, 0, :].reshape(1, SSM_GROUPS * SSM_STATE)

    ys_t = _s5(xs_t, bmat, cmat, a_re, a_im,
               ssm_d[l].reshape(1, SSM_WIDTH).astype(F32), w_glu[l].astype(BF16), steps=64)

    oz = _attn(qt, k, caug, vt, sza, tq=256)

    return _merge(x, ys_t, szs, oz, sgs, sga, gate,
                  w_proj_ssm[l].astype(BF16), w_proj_attn[l].astype(BF16), w_out[l].astype(BF16),
                  ln_g[l].reshape(1, d).astype(F32), ln_b[l].astype(F32).reshape(1, d), ts=512)
```

```python
import functools
import math

import jax
import jax.numpy as jnp
from jax import lax
from jax.experimental import pallas as pl
from jax.experimental.pallas import tpu as pltpu

D_MODEL = 1024
BATCH = 8
SEQ = 2048
SSM_WIDTH = 512
SSM_GROUP = 16
SSM_GROUPS = 32
SSM_STATE = 64
ATTN_HEADS = 8
ATTN_HEAD_DIM = 64
ATTN_WIDTH = 512
LN_EPS = 1e-5
DEEPNORM_ALPHA = 2.0 ** 0.25

F32 = jnp.float32
BF16 = jnp.bfloat16

LANES = 128
SUBLANES = 8
VMEM_LIMIT = 52 * 1024 * 1024

S5_TILES = 4
S5_TILE_GROUPS = SSM_GROUPS // S5_TILES
S5_HALF = S5_TILE_GROUPS * SSM_STATE
S5_TILE_LANES = 2 * S5_HALF
S5_TILE_IN = S5_TILE_GROUPS * SSM_GROUP
S5_STATE_LANES = S5_TILES * S5_TILE_LANES

C_XS, C_ZS, C_K, C_ZA, C_GS, C_GA, C_END = 0, 512, 1024, 1536, 2048, 3072, 4096

NEG_BIG = -1e30
LOG2E = math.log2(math.e)

ATTN_TK = 256
ATTN_LOOKAHEAD = 3
ATTN_SLOTS = 4
assert ATTN_SLOTS > ATTN_LOOKAHEAD and ATTN_HEADS % ATTN_SLOTS == 0


def _sigmoid(x):
    return 1.0 / (1.0 + jnp.exp(-x))


def _silu(x):
    return x * _sigmoid(x)


def _cparams(*sem):
    return pltpu.CompilerParams(dimension_semantics=sem, vmem_limit_bytes=VMEM_LIMIT)


def _ada_kernel(c_ref, w_ref, b_ref, o_ref):
    c = c_ref[...]
    ca = _silu(c)
    o_ref[...] = jnp.dot(ca, w_ref[...], preferred_element_type=F32,
                         precision=lax.Precision.HIGHEST) + b_ref[...]


def _ada(c, w_ada, b_ada):
    n = w_ada.shape[1]
    tn = 512
    return pl.pallas_call(
        _ada_kernel,
        grid=(n // tn,),
        in_specs=[pl.BlockSpec((BATCH, D_MODEL), lambda j: (0, 0)),
                  pl.BlockSpec((D_MODEL, tn), lambda j: (0, j)),
                  pl.BlockSpec((1, tn), lambda j: (0, j))],
        out_specs=pl.BlockSpec((BATCH, tn), lambda j: (0, j)),
        out_shape=jax.ShapeDtypeStruct((BATCH, n), F32),
        compiler_params=_cparams("arbitrary"),
        name="ada",
    )(c, w_ada, b_ada.reshape(1, n))


def _zoh_kernel(lr_ref, li_ref, ldt_ref, br_ref, bi_ref, cim_ref,
                abr_ref, abi_ref, bbr_ref, bbi_ref, ncim_ref):
    lr = lr_ref[...]
    li = li_ref[...]
    dt = jnp.exp(ldt_ref[...])
    mag = jnp.exp(lr * dt)
    abr = mag * jnp.cos(li * dt)
    abi = mag * jnp.sin(li * dt)
    den = lr * lr + li * li
    cr = ((abr - 1.0) * lr + abi * li) / den
    ci = (abi * lr - (abr - 1.0) * li) / den
    br = br_ref[...]
    bi = bi_ref[...]
    abr_ref[...] = abr
    abi_ref[...] = abi
    bbr_ref[...] = cr * br - ci * bi
    bbi_ref[...] = cr * bi + ci * br
    ncim_ref[...] = -cim_ref[...]


def _zoh(lam_re, lam_im, log_dt, b_re, b_im, c_im):
    rows = SSM_GROUPS * SSM_GROUP
    rep = lambda a: jnp.repeat(a, SSM_GROUP, axis=0)
    lr = rep(lam_re)
    li = rep(lam_im)
    ldt = jnp.broadcast_to(rep(log_dt[:, None]), (rows, SSM_STATE))
    brt = b_re.transpose(0, 2, 1).reshape(rows, SSM_STATE)
    bit = b_im.transpose(0, 2, 1).reshape(rows, SSM_STATE)
    cim = c_im.reshape(rows, SSM_STATE)
    shp = jax.ShapeDtypeStruct((rows, SSM_STATE), F32)
    return pl.pallas_call(
        _zoh_kernel,
        out_shape=(shp,) * 5,
        name="zoh",
    )(lr, li, ldt, brt, bit, cim)


def _inproj_kernel(x_ref, scale_ref, shift_ref, wm_ref, wt_ref, wf_ref, bf_ref, tri_ref,
                   xs_ref, szs_ref, k_ref, sza_ref, sgs_ref, sga_ref, qt_ref, vt_ref, caug_ref,
                   carry_ref, *, ts):
    i = pl.program_id(0)
    b = pl.program_id(1)
    x = x_ref[0]
    mu = jnp.mean(x, axis=-1, keepdims=True)
    xc = x - mu
    var = jnp.mean(xc * xc, axis=-1, keepdims=True)
    xn = xc * lax.rsqrt(var + LN_EPS)
    u = xn * (1.0 + scale_ref[0]) + shift_ref[0]
    ub = u.astype(BF16)

    def mm(lo, hi):
        return jnp.dot(ub, wm_ref[:, lo:hi], preferred_element_type=F32)

    xs = mm(C_XS, C_ZS)
    for j in range(S5_TILES):
        xs_ref[j, pl.ds(b, ts, stride=SUBLANES), :] = xs[:, j * LANES:(j + 1) * LANES]
    szs_ref[0] = _silu(mm(C_ZS, C_K)).astype(BF16)
    k_ref[0] = mm(C_K, C_ZA).astype(BF16)
    sza_ref[0] = _silu(mm(C_ZA, C_GS)).astype(BF16)
    sgs_ref[0] = _sigmoid(mm(C_GS, C_GA)).astype(BF16)
    sga_ref[0] = _sigmoid(mm(C_GA, C_END)).astype(BF16)

    qv = lax.dot_general(wt_ref[...], ub, (((1,), (1,)), ((), ())),
                         preferred_element_type=F32)
    qt_ref[0] = (qv[:ATTN_WIDTH] * (LOG2E * ATTN_HEAD_DIM ** -0.5)).astype(BF16)
    vt_ref[0] = qv[ATTN_WIDTH:].astype(BF16)

    f = jnp.dot(ub, wf_ref[...], preferred_element_type=F32) + bf_ref[...]
    lf = jnp.minimum(f, 0.0) - jnp.log(1.0 + jnp.exp(-jnp.abs(f)))

    @pl.when(i == 0)
    def _():
        carry_ref[b] = jnp.zeros(carry_ref.shape[1:], F32)

    l1 = lf.astype(BF16)
    r1 = lf - l1.astype(F32)
    l2 = r1.astype(BF16)
    l3 = (r1 - l2.astype(F32)).astype(BF16)
    tri = tri_ref[...]
    cf = (jnp.dot(tri, l1, preferred_element_type=F32)
          + jnp.dot(tri, l2, preferred_element_type=F32)
          + jnp.dot(tri, l3, preferred_element_type=F32)) + carry_ref[b][0:1, :]
    carry_ref[b] = jnp.broadcast_to(cf[ts - 1:ts, :], carry_ref.shape[1:])

    cf2 = cf * LOG2E
    c1 = cf2.astype(BF16).astype(F32)
    r = cf2 - c1
    c2 = r.astype(BF16).astype(F32)
    c3 = r - c2
    lane = lax.broadcasted_iota(jnp.int32, cf.shape, 1)
    parts = jnp.where(lane < 8, c1, jnp.where(lane < 16, c2, jnp.where(lane < 24, c3, 0.0)))
    caug_ref[0] = (-parts).astype(BF16)


def _inproj(x, scale, shift, w_main, w_t, w_f3, b_f3, *, ts):
    nb, s, d = x.shape
    nt = s // ts
    tri = jnp.tril(jnp.ones((ts, ts), BF16))
    tok3 = lambda w: pl.BlockSpec((1, ts, w), lambda i, b: (b, i, 0))
    const2 = lambda a: pl.BlockSpec(a.shape, lambda i, b: (0, 0), pipeline_mode=pl.Buffered(1))
    bf = lambda shape: jax.ShapeDtypeStruct(shape, BF16)
    return pl.pallas_call(
        functools.partial(_inproj_kernel, ts=ts),
        grid=(nt, nb),
        in_specs=[tok3(d),
                  pl.BlockSpec((1, 1, d), lambda i, b: (b, 0, 0)),
                  pl.BlockSpec((1, 1, d), lambda i, b: (b, 0, 0)),
                  const2(w_main), const2(w_t), const2(w_f3), const2(b_f3), const2(tri)],
        out_specs=[pl.BlockSpec((S5_TILES, ts * nb, LANES), lambda i, b: (0, i, 0)),
                   tok3(SSM_WIDTH), tok3(ATTN_WIDTH), tok3(ATTN_WIDTH),
                   tok3(D_MODEL), tok3(D_MODEL),
                   pl.BlockSpec((1, ATTN_WIDTH, ts), lambda i, b: (b, 0, i)),
                   pl.BlockSpec((1, ATTN_WIDTH, ts), lambda i, b: (b, 0, i)),
                   tok3(LANES)],
        out_shape=[jax.ShapeDtypeStruct((S5_TILES, s * nb, LANES), F32),
                   bf((nb, s, SSM_WIDTH)), bf((nb, s, ATTN_WIDTH)), bf((nb, s, ATTN_WIDTH)),
                   bf((nb, s, D_MODEL)), bf((nb, s, D_MODEL)),
                   bf((nb, ATTN_WIDTH, s)), bf((nb, ATTN_WIDTH, s)),
                   bf((nb, s, LANES))],
        scratch_shapes=[pltpu.VMEM((nb, SUBLANES, LANES), F32)],
        compiler_params=_cparams("arbitrary", "arbitrary"),
        name="inproj",
    )(x, scale, shift, w_main, w_t, w_f3, b_f3, tri)


def _gelu_tanh(x):
    return 0.5 * x * (1.0 + jnp.tanh(math.sqrt(2.0 / math.pi) * (x + 0.044715 * (x * x * x))))


def _s5_kernel(xs_ref, bmat_ref, cmat_ref, are_ref, aim_ref, d_ref, wglu_ref,
               y_ref, buf_ref, h_ref, *, steps):
    @pl.when(pl.program_id(0) == 0)
    def _():
        h_ref[...] = jnp.zeros_like(h_ref)

    for j in range(S5_TILES):
        lo = j * S5_TILE_LANES
        buf_ref[:, lo:lo + S5_TILE_LANES] = jnp.dot(
            xs_ref[j].astype(BF16), bmat_ref[j], preferred_element_type=F32)

    for j in range(S5_TILES):
        re_lo = j * S5_TILE_LANES
        im_lo = re_lo + S5_HALF
        ar = jnp.broadcast_to(are_ref[:, j * S5_HALF:(j + 1) * S5_HALF], (SUBLANES, S5_HALF))
        ai = jnp.broadcast_to(aim_ref[:, j * S5_HALF:(j + 1) * S5_HALF], (SUBLANES, S5_HALF))

        def step(t, carry, re_lo=re_lo, im_lo=im_lo, ar=ar, ai=ai):
            hr, hi = carry
            r0 = pl.multiple_of(t * SUBLANES, SUBLANES)
            bur = buf_ref[pl.ds(r0, SUBLANES), re_lo:re_lo + S5_HALF]
            bui = buf_ref[pl.ds(r0, SUBLANES), im_lo:im_lo + S5_HALF]
            nhr = ar * hr - ai * hi + bur
            nhi = ar * hi + ai * hr + bui
            buf_ref[pl.ds(r0, SUBLANES), re_lo:re_lo + S5_HALF] = nhr
            buf_ref[pl.ds(r0, SUBLANES), im_lo:im_lo + S5_HALF] = nhi
            return nhr, nhi

        hr0 = h_ref[:, re_lo:re_lo + S5_HALF]
        hi0 = h_ref[:, im_lo:im_lo + S5_HALF]
        hr, hi = lax.fori_loop(0, steps, step, (hr0, hi0), unroll=8)
        h_ref[:, re_lo:re_lo + S5_HALF] = hr
        h_ref[:, im_lo:im_lo + S5_HALF] = hi

    ys = []
    for j in range(S5_TILES):
        lo = j * S5_TILE_LANES
        hb = buf_ref[:, lo:lo + S5_TILE_LANES].astype(BF16)
        ys.append(jnp.dot(hb, cmat_ref[j], preferred_element_type=F32)
                  + d_ref[:, j * LANES:(j + 1) * LANES] * xs_ref[j])
    g = _gelu_tanh(jnp.concatenate(ys, axis=1)).astype(BF16)
    ab = jnp.dot(g, wglu_ref[...], preferred_element_type=F32)
    out = ab[:, :SSM_WIDTH] * _sigmoid(ab[:, SSM_WIDTH:])
    for j in range(S5_TILES):
        y_ref[j] = out[:, j * LANES:(j + 1) * LANES]


def _s5(xs_t, bmat, cmat, a_re, a_im, d_skip, w_glu, *, steps):
    t = xs_t.shape[1]
    rows = steps * SUBLANES
    const = lambda a: pl.BlockSpec(a.shape, lambda i: (0,) * a.ndim)
    tiles = pl.BlockSpec((S5_TILES, rows, LANES), lambda i: (0, i, 0))
    return pl.pallas_call(
        functools.partial(_s5_kernel, steps=steps),
        grid=(t // rows,),
        in_specs=[tiles,
                  const(bmat), const(cmat), const(a_re), const(a_im), const(d_skip), const(w_glu)],
        out_specs=tiles,
        out_shape=jax.ShapeDtypeStruct(xs_t.shape, F32),
        scratch_shapes=[pltpu.VMEM((rows, S5_STATE_LANES), F32),
                        pltpu.VMEM((SUBLANES, S5_STATE_LANES), F32)],
        compiler_params=_cparams("arbitrary"),
        name="s5",
    )(xs_t, bmat, cmat, a_re, a_im, d_skip, w_glu)


def _attn_kernel(qt_ref, k_ref, caug_ref, vt_ref, sza_ref, o_ref,
                 rhs_ref, s_ref, m_ref, l_ref, acc_ref, *, tq):
    qi = pl.program_id(1)
    tk = ATTN_TK
    nd = tq // tk
    hd = ATTN_HEAD_DIM
    row = lax.broadcasted_iota(jnp.int32, (LANES, tq), 0)

    for h in range(ATTN_HEADS):
        j = h // 2
        q2 = qt_ref[0, j * LANES:(j + 1) * LANES, :]
        rhs_ref[h, 0:LANES, :] = jnp.where((row // hd) == (h % 2), q2, jnp.zeros_like(q2))
        rhs_ref[h, LANES:2 * LANES, :] = jnp.where(
            (row == h) | (row == 8 + h) | (row == 16 + h), 1.0, 0.0).astype(BF16)
    m_ref[...] = jnp.full(m_ref.shape, -jnp.inf, F32)
    l_ref[...] = jnp.zeros_like(l_ref)
    acc_ref[...] = jnp.zeros_like(acc_ref)

    def scores(k0, h, diag=None):
        j = h // 2
        lo = 0 if diag is None else diag
        lhs = jnp.concatenate([k_ref[0, pl.ds(k0, tk), j * LANES:(j + 1) * LANES],
                               caug_ref[0, pl.ds(k0, tk), :]], axis=1)
        s_ref[h % ATTN_SLOTS, :, lo:tq] = jnp.dot(lhs, rhs_ref[h, :, lo:tq],
                                                  preferred_element_type=F32)

    def update(k0, h, diag=None):
        lo = 0 if diag is None else diag
        s = s_ref[h % ATTN_SLOTS, :, lo:tq]
        if diag is not None:
            kpos = lax.broadcasted_iota(jnp.int32, s.shape, 0)
            qpos = lax.broadcasted_iota(jnp.int32, s.shape, 1)
            s = jnp.where(kpos <= qpos, s, NEG_BIG)
        hs = slice(h * SUBLANES, h * SUBLANES + 1)
        m = m_ref[hs, lo:tq]
        m_new = jnp.maximum(m, jnp.max(s, axis=0, keepdims=True))
        alpha = jnp.exp2(m - m_new)
        p = jnp.exp2(s - m_new).astype(BF16)
        m_ref[hs, lo:tq] = m_new
        vt1 = jnp.concatenate([vt_ref[0, h * hd:(h + 1) * hd, pl.ds(k0, tk)],
                               jnp.ones((2 * SUBLANES, tk), BF16)], axis=0)
        pv = jnp.dot(vt1, p, preferred_element_type=F32)
        l_ref[hs, lo:tq] = alpha * l_ref[hs, lo:tq] + pv[hd:hd + 1, :]
        acc_ref[h * hd:(h + 1) * hd, lo:tq] = (alpha * acc_ref[h * hd:(h + 1) * hd, lo:tq]
                                               + pv[:hd, :])

    def body(kj, carry):
        k0 = pl.multiple_of(kj * tk, tk)
        k1 = pl.multiple_of(kj * tk + tk, tk)
        for h in range(ATTN_HEADS):
            ha = h + ATTN_LOOKAHEAD
            if ha < ATTN_HEADS:
                scores(k0, ha)
            else:
                scores(k1, ha - ATTN_HEADS)
            update(k0, h)
        return carry

    for h in range(ATTN_LOOKAHEAD):
        scores(0, h)
    lax.fori_loop(0, qi * nd, body, 0)
    for d in range(nd):
        kd = pl.multiple_of(qi * tq + d * tk, tk)
        for h in range(ATTN_HEADS):
            ha = h + ATTN_LOOKAHEAD
            if ha < ATTN_HEADS:
                scores(kd, ha, d * tk)
            elif d + 1 < nd:
                scores(pl.multiple_of(kd + tk, tk), ha - ATTN_HEADS, (d + 1) * tk)
            update(kd, h, d * tk)

    for h in range(ATTN_HEADS):
        acc_ref[h * hd:(h + 1) * hd, :] = (acc_ref[h * hd:(h + 1) * hd, :]
                                           / l_ref[h * SUBLANES:h * SUBLANES + 1, :])
    o = acc_ref[...].T
    o_ref[0] = (o * sza_ref[0].astype(F32)).astype(BF16)


def _attn(qt, k, caug, vt, sza, *, tq):
    nb, s, w = k.shape
    return pl.pallas_call(
        functools.partial(_attn_kernel, tq=tq),
        grid=(nb, s // tq),
        in_specs=[pl.BlockSpec((1, w, tq), lambda b, i: (b, 0, i)),
                  pl.BlockSpec((1, s, w), lambda b, i: (b, 0, 0)),
                  pl.BlockSpec((1, s, LANES), lambda b, i: (b, 0, 0)),
                  pl.BlockSpec((1, w, s), lambda b, i: (b, 0, 0)),
                  pl.BlockSpec((1, tq, w), lambda b, i: (b, i, 0))],
        out_specs=pl.BlockSpec((1, tq, w), lambda b, i: (b, i, 0)),
        out_shape=jax.ShapeDtypeStruct((nb, s, w), BF16),
        scratch_shapes=[pltpu.VMEM((ATTN_HEADS, 2 * LANES, tq), BF16),
                        pltpu.VMEM((ATTN_SLOTS, ATTN_TK, tq), F32),
                        pltpu.VMEM((ATTN_HEADS * SUBLANES, tq), F32),
                        pltpu.VMEM((ATTN_HEADS * SUBLANES, tq), F32),
                        pltpu.VMEM((w, tq), F32)],
        compiler_params=_cparams("arbitrary", "arbitrary"),
        name="attn",
    )(qt, k, caug, vt, sza)


def _merge_kernel(x_ref, ys_ref, szs_ref, oz_ref, sgs_ref, sga_ref, gate_ref,
                  wps_ref, wpa_ref, wout_ref, lng_ref, lnb_ref, o_ref, *, ts):
    b = pl.program_id(1)
    ys = jnp.concatenate([ys_ref[j, pl.ds(b, ts, stride=SUBLANES), :] for j in range(S5_TILES)],
                         axis=1)
    a = (ys * szs_ref[0].astype(F32)).astype(BF16)
    m1 = jnp.dot(a, wps_ref[...], preferred_element_type=F32) * sgs_ref[0].astype(F32)
    m2 = jnp.dot(oz_ref[0], wpa_ref[...], preferred_element_type=F32) * sga_ref[0].astype(F32)
    merged = (m1 + m2).astype(BF16)
    sub = jnp.dot(merged, wout_ref[...], preferred_element_type=F32) * gate_ref[0]
    y = DEEPNORM_ALPHA * x_ref[0] + sub
    mu = jnp.mean(y, axis=-1, keepdims=True)
    yc = y - mu
    var = jnp.mean(yc * yc, axis=-1, keepdims=True)
    o_ref[0] = yc * lax.rsqrt(var + LN_EPS) * lng_ref[...] + lnb_ref[...]


def _merge(x, ys_t, szs, oz, sgs, sga, gate, wps, wpa, wout, ln_g, ln_b, *, ts):
    nb, s, d = x.shape
    tok3 = lambda w: pl.BlockSpec((1, ts, w), lambda i, b: (b, i, 0))
    const2 = lambda a: pl.BlockSpec(a.shape, lambda i, b: (0, 0))
    return pl.pallas_call(
        functools.partial(_merge_kernel, ts=ts),
        grid=(s // ts, nb),
        in_specs=[tok3(d),
                  pl.BlockSpec((S5_TILES, ts * nb, LANES), lambda i, b: (0, i, 0)),
                  tok3(SSM_WIDTH), tok3(ATTN_WIDTH), tok3(d), tok3(d),
                  pl.BlockSpec((1, 1, d), lambda i, b: (b, 0, 0)),
                  const2(wps), const2(wpa), const2(wout), const2(ln_g), const2(ln_b)],
        out_specs=tok3(d),
        out_shape=jax.ShapeDtypeStruct((nb, s, d), x.dtype),
        compiler_params=_cparams("arbitrary", "arbitrary"),
        name="merge",
    )(x, ys_t, szs, oz, sgs, sga, gate, wps, wpa, wout, ln_g, ln_b)


def _block_diag_tiles(blocks):
    eye = jnp.eye(S5_TILE_GROUPS, dtype=blocks.dtype)
    t, g, r, c = blocks.shape
    return jnp.einsum('tgrc,gG->tgrGc', blocks, eye).reshape(t, g * r, g * c)


def kernel(x, c, w_ada, b_ada, w_in, b_f, lam_re, lam_im, log_dt, ssm_b_re, ssm_b_im,
           ssm_c_re, ssm_c_im, ssm_d, w_glu, w_proj_ssm, w_proj_attn, w_out, ln_g, ln_b):
    nb, s, d = x.shape
    l = 0
    mod = _ada(c, w_ada[l], b_ada[l])
    shift = mod[:, None, :d]
    scale = mod[:, None, d:2 * d]
    gate = mod[:, None, 2 * d:]

    w = w_in[l]
    o_xs, o_zs, o_q, o_k, o_v, o_f, o_za, o_gs, o_ga, o_end = (
        0, 512, 1024, 1536, 2048, 2560, 2568, 3080, 4104, 5128)
    w_main = jnp.concatenate([w[:, o_xs:o_zs], w[:, o_zs:o_q], w[:, o_k:o_v],
                              w[:, o_za:o_gs], w[:, o_gs:o_ga], w[:, o_ga:o_end]],
                             axis=1).astype(BF16)
    w_t = jnp.concatenate([w[:, o_q:o_k], w[:, o_v:o_f]], axis=1).T.astype(BF16)
    wf = w[:, o_f:o_za]
    w_f3 = jnp.concatenate([wf, wf, wf, jnp.zeros((d, LANES - 3 * ATTN_HEADS), w.dtype)],
                           axis=1).astype(BF16)
    bfv = b_f[l].astype(F32)
    b_f3 = jnp.concatenate([bfv, bfv, bfv, jnp.zeros((LANES - 3 * ATTN_HEADS,), F32)])[None, :]

    (xs_t, szs, k, sza, sgs, sga, qt, vt, caug) = _inproj(
        x, scale, shift, w_main, w_t, w_f3, b_f3, ts=256)

    abr, abi, bbr, bbi, ncim = _zoh(lam_re[l], lam_im[l], log_dt[l], ssm_b_re[l], ssm_b_im[l],
                                    ssm_c_im[l])
    g4 = (S5_TILES, S5_TILE_GROUPS)
    bbr = bbr.reshape(g4 + (SSM_GROUP, SSM_STATE))
    bbi = bbi.reshape(g4 + (SSM_GROUP, SSM_STATE))
    bmat = jnp.concatenate([_block_diag_tiles(bbr), _block_diag_tiles(bbi)], axis=2).astype(BF16)
    cre_t = ssm_c_re[l].reshape(g4 + (SSM_GROUP, SSM_STATE)).transpose(0, 1, 3, 2)
    ncim_t = ncim.reshape(g4 + (SSM_GROUP, SSM_STATE)).transpose(0, 1, 3, 2)
    cmat = jnp.concatenate([_block_diag_tiles(cre_t), _block_diag_tiles(ncim_t)],
                           axis=1).astype(BF16)
    a_re = abr.reshape(SSM_GROUPS, SSM_GROUP, SSM_STATE)[:, 0, :].reshape(1, SSM_GROUPS * SSM_STATE)
    a_im = abi.reshape(SSM_GROUPS, SSM_GROUP, SSM_STATE)[:, 0, :].reshape(1, SSM_GROUPS * SSM_STATE)

    ys_t = _s5(xs_t, bmat, cmat, a_re, a_im,
               ssm_d[l].reshape(1, SSM_WIDTH).astype(F32), w_glu[l].astype(BF16), steps=64)

    oz = _attn(qt, k, caug, vt, sza, tq=512)

    return _merge(x, ys_t, szs, oz, sgs, sga, gate,
                  w_proj_ssm[l].astype(BF16), w_proj_attn[l].astype(BF16), w_out[l].astype(BF16),
                  ln_g[l].reshape(1, d).astype(F32), ln_b[l].astype(F32).reshape(1, d), ts=512)
```

```python
import functools
import math

import jax
import jax.numpy as jnp
from jax import lax
from jax.experimental import pallas as pl
from jax.experimental.pallas import tpu as pltpu

D_MODEL = 1024
BATCH = 8
SEQ = 2048
SSM_WIDTH = 512
SSM_GROUP = 16
SSM_GROUPS = 32
SSM_STATE = 64
ATTN_HEADS = 8
ATTN_HEAD_DIM = 64
ATTN_WIDTH = 512
LN_EPS = 1e-5
DEEPNORM_ALPHA = 2.0 ** 0.25

F32 = jnp.float32
BF16 = jnp.bfloat16

LANES = 128
SUBLANES = 8
VMEM_LIMIT = 52 * 1024 * 1024

S5_TILES = 4
S5_TILE_GROUPS = SSM_GROUPS // S5_TILES
S5_HALF = S5_TILE_GROUPS * SSM_STATE
S5_TILE_LANES = 2 * S5_HALF
S5_TILE_IN = S5_TILE_GROUPS * SSM_GROUP
S5_STATE_LANES = S5_TILES * S5_TILE_LANES

C_XS, C_ZS, C_K, C_ZA, C_GS, C_GA, C_END = 0, 512, 1024, 1536, 2048, 3072, 4096

NEG_BIG = -1e30
LOG2E = math.log2(math.e)

MERGE_SUB = 256
ATTN_TK = 256
ATTN_LOOKAHEAD = 3
ATTN_SLOTS = 4
assert ATTN_SLOTS > ATTN_LOOKAHEAD and ATTN_HEADS % ATTN_SLOTS == 0


def _sigmoid(x):
    return 1.0 / (1.0 + jnp.exp(-x))


def _silu(x):
    return x * _sigmoid(x)


def _cparams(*sem):
    return pltpu.CompilerParams(dimension_semantics=sem, vmem_limit_bytes=VMEM_LIMIT)


def _ada_kernel(c_ref, w_ref, b_ref, o_ref):
    c = c_ref[...]
    ca = _silu(c)
    o_ref[...] = jnp.dot(ca, w_ref[...], preferred_element_type=F32,
                         precision=lax.Precision.HIGHEST) + b_ref[...]


def _ada(c, w_ada, b_ada):
    n = w_ada.shape[1]
    tn = 512
    return pl.pallas_call(
        _ada_kernel,
        grid=(n // tn,),
        in_specs=[pl.BlockSpec((BATCH, D_MODEL), lambda j: (0, 0)),
                  pl.BlockSpec((D_MODEL, tn), lambda j: (0, j)),
                  pl.BlockSpec((1, tn), lambda j: (0, j))],
        out_specs=pl.BlockSpec((BATCH, tn), lambda j: (0, j)),
        out_shape=jax.ShapeDtypeStruct((BATCH, n), F32),
        compiler_params=_cparams("arbitrary"),
        name="ada",
    )(c, w_ada, b_ada.reshape(1, n))


def _zoh_kernel(lr_ref, li_ref, ldt_ref, br_ref, bi_ref, cim_ref,
                abr_ref, abi_ref, bbr_ref, bbi_ref, ncim_ref):
    lr = lr_ref[...]
    li = li_ref[...]
    dt = jnp.exp(ldt_ref[...])
    mag = jnp.exp(lr * dt)
    abr = mag * jnp.cos(li * dt)
    abi = mag * jnp.sin(li * dt)
    den = lr * lr + li * li
    cr = ((abr - 1.0) * lr + abi * li) / den
    ci = (abi * lr - (abr - 1.0) * li) / den
    br = br_ref[...]
    bi = bi_ref[...]
    abr_ref[...] = abr
    abi_ref[...] = abi
    bbr_ref[...] = cr * br - ci * bi
    bbi_ref[...] = cr * bi + ci * br
    ncim_ref[...] = -cim_ref[...]


def _zoh(lam_re, lam_im, log_dt, b_re, b_im, c_im):
    rows = SSM_GROUPS * SSM_GROUP
    rep = lambda a: jnp.repeat(a, SSM_GROUP, axis=0)
    lr = rep(lam_re)
    li = rep(lam_im)
    ldt = jnp.broadcast_to(rep(log_dt[:, None]), (rows, SSM_STATE))
    brt = b_re.transpose(0, 2, 1).reshape(rows, SSM_STATE)
    bit = b_im.transpose(0, 2, 1).reshape(rows, SSM_STATE)
    cim = c_im.reshape(rows, SSM_STATE)
    shp = jax.ShapeDtypeStruct((rows, SSM_STATE), F32)
    return pl.pallas_call(
        _zoh_kernel,
        out_shape=(shp,) * 5,
        name="zoh",
    )(lr, li, ldt, brt, bit, cim)


def _inproj_kernel(x_ref, scale_ref, shift_ref, wm_ref, wt_ref, wf_ref, bf_ref, tri_ref,
                   xs_ref, szs_ref, k_ref, sza_ref, sgs_ref, sga_ref, qt_ref, vt_ref, caug_ref,
                   carry_ref, *, ts):
    i = pl.program_id(0)
    b = pl.program_id(1)
    x = x_ref[0]
    mu = jnp.mean(x, axis=-1, keepdims=True)
    xc = x - mu
    var = jnp.mean(xc * xc, axis=-1, keepdims=True)
    xn = xc * lax.rsqrt(var + LN_EPS)
    u = xn * (1.0 + scale_ref[0]) + shift_ref[0]
    ub = u.astype(BF16)

    def mm(lo, hi):
        return jnp.dot(ub, wm_ref[:, lo:hi], preferred_element_type=F32)

    xs = mm(C_XS, C_ZS)
    for j in range(S5_TILES):
        xs_ref[j, pl.ds(b, ts, stride=SUBLANES), :] = xs[:, j * LANES:(j + 1) * LANES]
    szs_ref[0] = _silu(mm(C_ZS, C_K)).astype(BF16)
    k_ref[0] = mm(C_K, C_ZA).astype(BF16)
    sza_ref[0] = _silu(mm(C_ZA, C_GS)).astype(BF16)
    sgs_ref[0] = _sigmoid(mm(C_GS, C_GA)).astype(BF16)
    sga_ref[0] = _sigmoid(mm(C_GA, C_END)).astype(BF16)

    qv = lax.dot_general(wt_ref[...], ub, (((1,), (1,)), ((), ())),
                         preferred_element_type=F32)
    qt_ref[0] = (qv[:ATTN_WIDTH] * (LOG2E * ATTN_HEAD_DIM ** -0.5)).astype(BF16)
    vt_ref[0] = qv[ATTN_WIDTH:].astype(BF16)

    f = jnp.dot(ub, wf_ref[...], preferred_element_type=F32) + bf_ref[...]
    lf = jnp.minimum(f, 0.0) - jnp.log(1.0 + jnp.exp(-jnp.abs(f)))

    @pl.when(i == 0)
    def _():
        carry_ref[b] = jnp.zeros(carry_ref.shape[1:], F32)

    l1 = lf.astype(BF16)
    r1 = lf - l1.astype(F32)
    l2 = r1.astype(BF16)
    l3 = (r1 - l2.astype(F32)).astype(BF16)
    tri = tri_ref[...]
    cf = (jnp.dot(tri, l1, preferred_element_type=F32)
          + jnp.dot(tri, l2, preferred_element_type=F32)
          + jnp.dot(tri, l3, preferred_element_type=F32)) + carry_ref[b][0:1, :]
    carry_ref[b] = jnp.broadcast_to(cf[ts - 1:ts, :], carry_ref.shape[1:])

    cf2 = cf * LOG2E
    c1 = cf2.astype(BF16).astype(F32)
    r = cf2 - c1
    c2 = r.astype(BF16).astype(F32)
    c3 = r - c2
    lane = lax.broadcasted_iota(jnp.int32, cf.shape, 1)
    parts = jnp.where(lane < 8, c1, jnp.where(lane < 16, c2, jnp.where(lane < 24, c3, 0.0)))
    caug_ref[0] = (-parts).astype(BF16)


def _inproj(x, scale, shift, w_main, w_t, w_f3, b_f3, *, ts):
    nb, s, d = x.shape
    nt = s // ts
    tri = jnp.tril(jnp.ones((ts, ts), BF16))
    tok3 = lambda w: pl.BlockSpec((1, ts, w), lambda i, b: (b, i, 0))
    const2 = lambda a: pl.BlockSpec(a.shape, lambda i, b: (0, 0), pipeline_mode=pl.Buffered(1))
    bf = lambda shape: jax.ShapeDtypeStruct(shape, BF16)
    return pl.pallas_call(
        functools.partial(_inproj_kernel, ts=ts),
        grid=(nt, nb),
        in_specs=[tok3(d),
                  pl.BlockSpec((1, 1, d), lambda i, b: (b, 0, 0)),
                  pl.BlockSpec((1, 1, d), lambda i, b: (b, 0, 0)),
                  const2(w_main), const2(w_t), const2(w_f3), const2(b_f3), const2(tri)],
        out_specs=[pl.BlockSpec((S5_TILES, ts * nb, LANES), lambda i, b: (0, i, 0)),
                   tok3(SSM_WIDTH), tok3(ATTN_WIDTH), tok3(ATTN_WIDTH),
                   tok3(D_MODEL), tok3(D_MODEL),
                   pl.BlockSpec((1, ATTN_WIDTH, ts), lambda i, b: (b, 0, i)),
                   pl.BlockSpec((1, ATTN_WIDTH, ts), lambda i, b: (b, 0, i)),
                   tok3(LANES)],
        out_shape=[jax.ShapeDtypeStruct((S5_TILES, s * nb, LANES), F32),
                   bf((nb, s, SSM_WIDTH)), bf((nb, s, ATTN_WIDTH)), bf((nb, s, ATTN_WIDTH)),
                   bf((nb, s, D_MODEL)), bf((nb, s, D_MODEL)),
                   bf((nb, ATTN_WIDTH, s)), bf((nb, ATTN_WIDTH, s)),
                   bf((nb, s, LANES))],
        scratch_shapes=[pltpu.VMEM((nb, SUBLANES, LANES), F32)],
        compiler_params=_cparams("arbitrary", "arbitrary"),
        name="inproj",
    )(x, scale, shift, w_main, w_t, w_f3, b_f3, tri)


def _gelu_tanh(x):
    return 0.5 * x * (1.0 + jnp.tanh(math.sqrt(2.0 / math.pi) * (x + 0.044715 * (x * x * x))))


def _s5_kernel(xs_ref, bmat_ref, cmat_ref, are_ref, aim_ref, d_ref, wglu_ref,
               y_ref, buf_ref, h_ref, *, steps, sub):
    @pl.when(pl.program_id(0) == 0)
    def _():
        h_ref[...] = jnp.zeros_like(h_ref)

    srows = sub * SUBLANES

    def expand(c):
        r = slice(c * srows, (c + 1) * srows)
        for j in range(S5_TILES):
            lo = j * S5_TILE_LANES
            buf_ref[r, lo:lo + S5_TILE_LANES] = jnp.dot(
                xs_ref[j, r, :].astype(BF16), bmat_ref[j], preferred_element_type=F32)

    def recur(c, state):
        new_state = []
        for j in range(S5_TILES):
            re = slice(j * S5_TILE_LANES, j * S5_TILE_LANES + S5_HALF)
            im = slice(j * S5_TILE_LANES + S5_HALF, (j + 1) * S5_TILE_LANES)
            ar = jnp.broadcast_to(are_ref[:, j * S5_HALF:(j + 1) * S5_HALF], (SUBLANES, S5_HALF))
            ai = jnp.broadcast_to(aim_ref[:, j * S5_HALF:(j + 1) * S5_HALF], (SUBLANES, S5_HALF))
            hr, hi = state[j]
            for t in range(sub):
                rt = slice(c * srows + t * SUBLANES, c * srows + (t + 1) * SUBLANES)
                nhr = ar * hr - ai * hi + buf_ref[rt, re]
                nhi = ar * hi + ai * hr + buf_ref[rt, im]
                buf_ref[rt, re] = nhr
                buf_ref[rt, im] = nhi
                hr, hi = nhr, nhi
            new_state.append((hr, hi))
        return new_state

    def project(c):
        r = slice(c * srows, (c + 1) * srows)
        ys = []
        for j in range(S5_TILES):
            lo = j * S5_TILE_LANES
            hb = buf_ref[r, lo:lo + S5_TILE_LANES].astype(BF16)
            ys.append(jnp.dot(hb, cmat_ref[j], preferred_element_type=F32)
                      + d_ref[:, j * LANES:(j + 1) * LANES] * xs_ref[j, r, :])
        g = _gelu_tanh(jnp.concatenate(ys, axis=1)).astype(BF16)
        ab = jnp.dot(g, wglu_ref[...], preferred_element_type=F32)
        out = ab[:, :SSM_WIDTH] * _sigmoid(ab[:, SSM_WIDTH:])
        for j in range(S5_TILES):
            y_ref[j, r, :] = out[:, j * LANES:(j + 1) * LANES]

    state = [(h_ref[:, j * S5_TILE_LANES:j * S5_TILE_LANES + S5_HALF],
              h_ref[:, j * S5_TILE_LANES + S5_HALF:(j + 1) * S5_TILE_LANES])
             for j in range(S5_TILES)]
    nsub = steps // sub
    expand(0)
    for c in range(nsub):
        if c + 1 < nsub:
            expand(c + 1)
        state = recur(c, state)
        project(c)
    for j in range(S5_TILES):
        h_ref[:, j * S5_TILE_LANES:j * S5_TILE_LANES + S5_HALF] = state[j][0]
        h_ref[:, j * S5_TILE_LANES + S5_HALF:(j + 1) * S5_TILE_LANES] = state[j][1]


def _s5(xs_t, bmat, cmat, a_re, a_im, d_skip, w_glu, *, steps, sub):
    t = xs_t.shape[1]
    rows = steps * SUBLANES
    const = lambda a: pl.BlockSpec(a.shape, lambda i: (0,) * a.ndim)
    tiles = pl.BlockSpec((S5_TILES, rows, LANES), lambda i: (0, i, 0))
    return pl.pallas_call(
        functools.partial(_s5_kernel, steps=steps, sub=sub),
        grid=(t // rows,),
        in_specs=[tiles,
                  const(bmat), const(cmat), const(a_re), const(a_im), const(d_skip), const(w_glu)],
        out_specs=tiles,
        out_shape=jax.ShapeDtypeStruct(xs_t.shape, F32),
        scratch_shapes=[pltpu.VMEM((rows, S5_STATE_LANES), F32),
                        pltpu.VMEM((SUBLANES, S5_STATE_LANES), F32)],
        compiler_params=_cparams("arbitrary"),
        name="s5",
    )(xs_t, bmat, cmat, a_re, a_im, d_skip, w_glu)


def _attn_kernel(qt_ref, k_ref, caug_ref, vt_ref, sza_ref, o_ref,
                 rhs_ref, s_ref, m_ref, l_ref, acc_ref, *, tq):
    qi = pl.program_id(1)
    tk = ATTN_TK
    nd = tq // tk
    hd = ATTN_HEAD_DIM
    row = lax.broadcasted_iota(jnp.int32, (LANES, tq), 0)

    for h in range(ATTN_HEADS):
        j = h // 2
        q2 = qt_ref[0, j * LANES:(j + 1) * LANES, :]
        rhs_ref[h, 0:LANES, :] = jnp.where((row // hd) == (h % 2), q2, jnp.zeros_like(q2))
        rhs_ref[h, LANES:2 * LANES, :] = jnp.where(
            (row == h) | (row == 8 + h) | (row == 16 + h), 1.0, 0.0).astype(BF16)
    m_ref[...] = jnp.full(m_ref.shape, -jnp.inf, F32)
    l_ref[...] = jnp.zeros_like(l_ref)
    acc_ref[...] = jnp.zeros_like(acc_ref)

    def scores(k0, h, diag=None):
        j = h // 2
        lo = 0 if diag is None else diag
        lhs = jnp.concatenate([k_ref[0, pl.ds(k0, tk), j * LANES:(j + 1) * LANES],
                               caug_ref[0, pl.ds(k0, tk), :]], axis=1)
        s_ref[h % ATTN_SLOTS, :, lo:tq] = jnp.dot(lhs, rhs_ref[h, :, lo:tq],
                                                  preferred_element_type=F32)

    def update(k0, h, diag=None):
        lo = 0 if diag is None else diag
        s = s_ref[h % ATTN_SLOTS, :, lo:tq]
        if diag is not None:
            kpos = lax.broadcasted_iota(jnp.int32, s.shape, 0)
            qpos = lax.broadcasted_iota(jnp.int32, s.shape, 1)
            s = jnp.where(kpos <= qpos, s, NEG_BIG)
        hs = slice(h * SUBLANES, h * SUBLANES + 1)
        m = m_ref[hs, lo:tq]
        m_new = jnp.maximum(m, jnp.max(s, axis=0, keepdims=True))
        alpha = jnp.exp2(m - m_new)
        p = jnp.exp2(s - m_new).astype(BF16)
        m_ref[hs, lo:tq] = m_new
        vt1 = jnp.concatenate([vt_ref[0, h * hd:(h + 1) * hd, pl.ds(k0, tk)],
                               jnp.ones((2 * SUBLANES, tk), BF16)], axis=0)
        pv = jnp.dot(vt1, p, preferred_element_type=F32)
        l_ref[hs, lo:tq] = alpha * l_ref[hs, lo:tq] + pv[hd:hd + 1, :]
        acc_ref[h * hd:(h + 1) * hd, lo:tq] = (alpha * acc_ref[h * hd:(h + 1) * hd, lo:tq]
                                               + pv[:hd, :])

    def body(kj, carry):
        k0 = pl.multiple_of(kj * tk, tk)
        k1 = pl.multiple_of(kj * tk + tk, tk)
        for h in range(ATTN_HEADS):
            ha = h + ATTN_LOOKAHEAD
            if ha < ATTN_HEADS:
                scores(k0, ha)
            else:
                scores(k1, ha - ATTN_HEADS)
            update(k0, h)
        return carry

    for h in range(ATTN_LOOKAHEAD):
        scores(0, h)
    lax.fori_loop(0, qi * nd, body, 0)
    for d in range(nd):
        kd = pl.multiple_of(qi * tq + d * tk, tk)
        for h in range(ATTN_HEADS):
            ha = h + ATTN_LOOKAHEAD
            if ha < ATTN_HEADS:
                scores(kd, ha, d * tk)
            elif d + 1 < nd:
                scores(pl.multiple_of(kd + tk, tk), ha - ATTN_HEADS, (d + 1) * tk)
            update(kd, h, d * tk)

    for h in range(ATTN_HEADS):
        acc_ref[h * hd:(h + 1) * hd, :] = (acc_ref[h * hd:(h + 1) * hd, :]
                                           / l_ref[h * SUBLANES:h * SUBLANES + 1, :])
    o = acc_ref[...].T
    o_ref[0] = (o * sza_ref[0].astype(F32)).astype(BF16)


def _attn(qt, k, caug, vt, sza, *, tq):
    nb, s, w = k.shape
    return pl.pallas_call(
        functools.partial(_attn_kernel, tq=tq),
        grid=(nb, s // tq),
        in_specs=[pl.BlockSpec((1, w, tq), lambda b, i: (b, 0, i)),
                  pl.BlockSpec((1, s, w), lambda b, i: (b, 0, 0)),
                  pl.BlockSpec((1, s, LANES), lambda b, i: (b, 0, 0)),
                  pl.BlockSpec((1, w, s), lambda b, i: (b, 0, 0)),
                  pl.BlockSpec((1, tq, w), lambda b, i: (b, i, 0))],
        out_specs=pl.BlockSpec((1, tq, w), lambda b, i: (b, i, 0)),
        out_shape=jax.ShapeDtypeStruct((nb, s, w), BF16),
        scratch_shapes=[pltpu.VMEM((ATTN_HEADS, 2 * LANES, tq), BF16),
                        pltpu.VMEM((ATTN_SLOTS, ATTN_TK, tq), F32),
                        pltpu.VMEM((ATTN_HEADS * SUBLANES, tq), F32),
                        pltpu.VMEM((ATTN_HEADS * SUBLANES, tq), F32),
                        pltpu.VMEM((w, tq), F32)],
        compiler_params=_cparams("arbitrary", "arbitrary"),
        name="attn",
    )(qt, k, caug, vt, sza)


def _merge_kernel(x_ref, ys_ref, szs_ref, oz_ref, sgs_ref, sga_ref, gate_ref,
                  wps_ref, wpa_ref, wout_ref, lng_ref, lnb_ref, o_ref, *, ts):
    b = pl.program_id(1)
    for c in range(ts // MERGE_SUB):
        r = slice(c * MERGE_SUB, (c + 1) * MERGE_SUB)
        ys = jnp.concatenate(
            [ys_ref[j, pl.ds(b + c * MERGE_SUB * SUBLANES, MERGE_SUB, stride=SUBLANES), :]
             for j in range(S5_TILES)], axis=1)
        a = (ys * szs_ref[0, r, :].astype(F32)).astype(BF16)
        m1 = jnp.dot(a, wps_ref[...], preferred_element_type=F32) * sgs_ref[0, r, :].astype(F32)
        m2 = (jnp.dot(oz_ref[0, r, :], wpa_ref[...], preferred_element_type=F32)
              * sga_ref[0, r, :].astype(F32))
        merged = (m1 + m2).astype(BF16)
        sub = jnp.dot(merged, wout_ref[...], preferred_element_type=F32) * gate_ref[0]
        y = DEEPNORM_ALPHA * x_ref[0, r, :] + sub
        mu = jnp.mean(y, axis=-1, keepdims=True)
        yc = y - mu
        var = jnp.mean(yc * yc, axis=-1, keepdims=True)
        o_ref[0, r, :] = yc * lax.rsqrt(var + LN_EPS) * lng_ref[...] + lnb_ref[...]


def _merge(x, ys_t, szs, oz, sgs, sga, gate, wps, wpa, wout, ln_g, ln_b, *, ts):
    nb, s, d = x.shape
    tok3 = lambda w: pl.BlockSpec((1, ts, w), lambda i, b: (b, i, 0))
    const2 = lambda a: pl.BlockSpec(a.shape, lambda i, b: (0, 0))
    return pl.pallas_call(
        functools.partial(_merge_kernel, ts=ts),
        grid=(s // ts, nb),
        in_specs=[tok3(d),
                  pl.BlockSpec((S5_TILES, ts * nb, LANES), lambda i, b: (0, i, 0)),
                  tok3(SSM_WIDTH), tok3(ATTN_WIDTH), tok3(d), tok3(d),
                  pl.BlockSpec((1, 1, d), lambda i, b: (b, 0, 0)),
                  const2(wps), const2(wpa), const2(wout), const2(ln_g), const2(ln_b)],
        out_specs=tok3(d),
        out_shape=jax.ShapeDtypeStruct((nb, s, d), x.dtype),
        compiler_params=_cparams("arbitrary", "arbitrary"),
        name="merge",
    )(x, ys_t, szs, oz, sgs, sga, gate, wps, wpa, wout, ln_g, ln_b)


def _block_diag_tiles(blocks):
    eye = jnp.eye(S5_TILE_GROUPS, dtype=blocks.dtype)
    t, g, r, c = blocks.shape
    return jnp.einsum('tgrc,gG->tgrGc', blocks, eye).reshape(t, g * r, g * c)


def kernel(x, c, w_ada, b_ada, w_in, b_f, lam_re, lam_im, log_dt, ssm_b_re, ssm_b_im,
           ssm_c_re, ssm_c_im, ssm_d, w_glu, w_proj_ssm, w_proj_attn, w_out, ln_g, ln_b):
    nb, s, d = x.shape
    l = 0
    mod = _ada(c, w_ada[l], b_ada[l])
    shift = mod[:, None, :d]
    scale = mod[:, None, d:2 * d]
    gate = mod[:, None, 2 * d:]

    w = w_in[l]
    o_xs, o_zs, o_q, o_k, o_v, o_f, o_za, o_gs, o_ga, o_end = (
        0, 512, 1024, 1536, 2048, 2560, 2568, 3080, 4104, 5128)
    w_main = jnp.concatenate([w[:, o_xs:o_zs], w[:, o_zs:o_q], w[:, o_k:o_v],
                              w[:, o_za:o_gs], w[:, o_gs:o_ga], w[:, o_ga:o_end]],
                             axis=1).astype(BF16)
    w_t = jnp.concatenate([w[:, o_q:o_k], w[:, o_v:o_f]], axis=1).T.astype(BF16)
    wf = w[:, o_f:o_za]
    w_f3 = jnp.concatenate([wf, wf, wf, jnp.zeros((d, LANES - 3 * ATTN_HEADS), w.dtype)],
                           axis=1).astype(BF16)
    bfv = b_f[l].astype(F32)
    b_f3 = jnp.concatenate([bfv, bfv, bfv, jnp.zeros((LANES - 3 * ATTN_HEADS,), F32)])[None, :]

    (xs_t, szs, k, sza, sgs, sga, qt, vt, caug) = _inproj(
        x, scale, shift, w_main, w_t, w_f3, b_f3, ts=256)

    abr, abi, bbr, bbi, ncim = _zoh(lam_re[l], lam_im[l], log_dt[l], ssm_b_re[l], ssm_b_im[l],
                                    ssm_c_im[l])
    g4 = (S5_TILES, S5_TILE_GROUPS)
    bbr = bbr.reshape(g4 + (SSM_GROUP, SSM_STATE))
    bbi = bbi.reshape(g4 + (SSM_GROUP, SSM_STATE))
    bmat = jnp.concatenate([_block_diag_tiles(bbr), _block_diag_tiles(bbi)], axis=2).astype(BF16)
    cre_t = ssm_c_re[l].reshape(g4 + (SSM_GROUP, SSM_STATE)).transpose(0, 1, 3, 2)
    ncim_t = ncim.reshape(g4 + (SSM_GROUP, SSM_STATE)).transpose(0, 1, 3, 2)
    cmat = jnp.concatenate([_block_diag_tiles(cre_t), _block_diag_tiles(ncim_t)],
                           axis=1).astype(BF16)
    a_re = abr.reshape(SSM_GROUPS, SSM_GROUP, SSM_STATE)[:, 0, :].reshape(1, SSM_GROUPS * SSM_STATE)
    a_im = abi.reshape(SSM_GROUPS, SSM_GROUP, SSM_STATE)[:, 0, :].reshape(1, SSM_GROUPS * SSM_STATE)

    ys_t = _s5(xs_t, bmat, cmat, a_re, a_im,
               ssm_d[l].reshape(1, SSM_WIDTH).astype(F32), w_glu[l].astype(BF16), steps=128, sub=32)

    oz = _attn(qt, k, caug, vt, sza, tq=512)

    return _merge(x, ys_t, szs, oz, sgs, sga, gate,
                  w_proj_ssm[l].astype(BF16), w_proj_attn[l].astype(BF16), w_out[l].astype(BF16),
                  ln_g[l].reshape(1, d).astype(F32), ln_b[l].astype(F32).reshape(1, d), ts=512)
```

```python
import functools
import math

import jax
import jax.numpy as jnp
from jax import lax
from jax.experimental import pallas as pl
from jax.experimental.pallas import tpu as pltpu

D_MODEL = 1024
BATCH = 8
SEQ = 2048
SSM_WIDTH = 512
SSM_GROUP = 16
SSM_GROUPS = 32
SSM_STATE = 64
ATTN_HEADS = 8
ATTN_HEAD_DIM = 64
ATTN_WIDTH = 512
LN_EPS = 1e-5
DEEPNORM_ALPHA = 2.0 ** 0.25

F32 = jnp.float32
BF16 = jnp.bfloat16

LANES = 128
SUBLANES = 8
VMEM_LIMIT = 52 * 1024 * 1024

S5_TILES = 4
S5_TILE_GROUPS = SSM_GROUPS // S5_TILES
S5_HALF = S5_TILE_GROUPS * SSM_STATE
S5_TILE_LANES = 2 * S5_HALF
S5_TILE_IN = S5_TILE_GROUPS * SSM_GROUP
S5_STATE_LANES = S5_TILES * S5_TILE_LANES

C_XS, C_ZS, C_K, C_ZA, C_GS, C_GA, C_END = 0, 512, 1024, 1536, 2048, 3072, 4096

NEG_BIG = -1e30
LOG2E = math.log2(math.e)

MERGE_SUB = 256
ATTN_TK = 256
ATTN_LOOKAHEAD = 3
ATTN_SLOTS = 4
assert ATTN_SLOTS > ATTN_LOOKAHEAD and ATTN_HEADS % ATTN_SLOTS == 0


def _sigmoid(x):
    return 1.0 / (1.0 + jnp.exp(-x))


def _silu(x):
    return x * _sigmoid(x)


def _cparams(*sem):
    return pltpu.CompilerParams(dimension_semantics=sem, vmem_limit_bytes=VMEM_LIMIT)


def _ada_kernel(c_ref, w_ref, b_ref, o_ref):
    c = c_ref[...]
    ca = _silu(c)
    o_ref[...] = jnp.dot(ca, w_ref[...], preferred_element_type=F32,
                         precision=lax.Precision.HIGHEST) + b_ref[...]


def _ada(c, w_ada, b_ada):
    n = w_ada.shape[1]
    tn = 512
    return pl.pallas_call(
        _ada_kernel,
        grid=(n // tn,),
        in_specs=[pl.BlockSpec((BATCH, D_MODEL), lambda j: (0, 0)),
                  pl.BlockSpec((D_MODEL, tn), lambda j: (0, j)),
                  pl.BlockSpec((1, tn), lambda j: (0, j))],
        out_specs=pl.BlockSpec((BATCH, tn), lambda j: (0, j)),
        out_shape=jax.ShapeDtypeStruct((BATCH, n), F32),
        compiler_params=_cparams("arbitrary"),
        name="ada",
    )(c, w_ada, b_ada.reshape(1, n))


def _zoh_kernel(lr_ref, li_ref, ldt_ref, br_ref, bi_ref, cim_ref,
                abr_ref, abi_ref, bbr_ref, bbi_ref, ncim_ref):
    lr = lr_ref[...]
    li = li_ref[...]
    dt = jnp.exp(ldt_ref[...])
    mag = jnp.exp(lr * dt)
    abr = mag * jnp.cos(li * dt)
    abi = mag * jnp.sin(li * dt)
    den = lr * lr + li * li
    cr = ((abr - 1.0) * lr + abi * li) / den
    ci = (abi * lr - (abr - 1.0) * li) / den
    br = br_ref[...]
    bi = bi_ref[...]
    abr_ref[...] = abr
    abi_ref[...] = abi
    bbr_ref[...] = cr * br - ci * bi
    bbi_ref[...] = cr * bi + ci * br
    ncim_ref[...] = -cim_ref[...]


def _zoh(lam_re, lam_im, log_dt, b_re, b_im, c_im):
    rows = SSM_GROUPS * SSM_GROUP
    rep = lambda a: jnp.repeat(a, SSM_GROUP, axis=0)
    lr = rep(lam_re)
    li = rep(lam_im)
    ldt = jnp.broadcast_to(rep(log_dt[:, None]), (rows, SSM_STATE))
    brt = b_re.transpose(0, 2, 1).reshape(rows, SSM_STATE)
    bit = b_im.transpose(0, 2, 1).reshape(rows, SSM_STATE)
    cim = c_im.reshape(rows, SSM_STATE)
    shp = jax.ShapeDtypeStruct((rows, SSM_STATE), F32)
    return pl.pallas_call(
        _zoh_kernel,
        out_shape=(shp,) * 5,
        name="zoh",
    )(lr, li, ldt, brt, bit, cim)


def _inproj_kernel(x_ref, scale_ref, shift_ref, wm_ref, wt_ref, wf_ref, bf_ref, tri_ref,
                   xs_ref, szs_ref, k_ref, sza_ref, sgs_ref, sga_ref, qt_ref, vt_ref, caug_ref,
                   carry_ref, *, ts):
    i = pl.program_id(0)
    b = pl.program_id(1)

    @pl.when(i == 0)
    def _():
        carry_ref[b] = jnp.zeros(carry_ref.shape[1:], F32)

    x = x_ref[0]
    mu = jnp.mean(x, axis=-1, keepdims=True)
    xc = x - mu
    var = jnp.mean(xc * xc, axis=-1, keepdims=True)
    xn = xc * lax.rsqrt(var + LN_EPS)
    u = xn * (1.0 + scale_ref[0]) + shift_ref[0]
    ub = u.astype(BF16)

    def mm(lo, hi):
        return jnp.dot(ub, wm_ref[:, lo:hi], preferred_element_type=F32)

    f = jnp.dot(ub, wf_ref[...], preferred_element_type=F32) + bf_ref[...]
    lf = jnp.minimum(f, 0.0) - jnp.log(1.0 + jnp.exp(-jnp.abs(f)))
    l1 = lf.astype(BF16)
    r1 = lf - l1.astype(F32)
    l2 = r1.astype(BF16)
    l3 = (r1 - l2.astype(F32)).astype(BF16)

    xs = mm(C_XS, C_ZS)
    for j in range(S5_TILES):
        xs_ref[j, pl.ds(b, ts, stride=SUBLANES), :] = xs[:, j * LANES:(j + 1) * LANES]
    szs_ref[0] = _silu(mm(C_ZS, C_K)).astype(BF16)

    tri = tri_ref[...]
    cf = (jnp.dot(tri, l1, preferred_element_type=F32)
          + jnp.dot(tri, l2, preferred_element_type=F32)
          + jnp.dot(tri, l3, preferred_element_type=F32)) + carry_ref[b][0:1, :]
    carry_ref[b] = jnp.broadcast_to(cf[ts - 1:ts, :], carry_ref.shape[1:])

    k_ref[0] = mm(C_K, C_ZA).astype(BF16)
    sza_ref[0] = _silu(mm(C_ZA, C_GS)).astype(BF16)

    cf2 = cf * LOG2E
    c1 = cf2.astype(BF16).astype(F32)
    r = cf2 - c1
    c2 = r.astype(BF16).astype(F32)
    c3 = r - c2
    lane = lax.broadcasted_iota(jnp.int32, cf.shape, 1)
    parts = jnp.where(lane < 8, c1, jnp.where(lane < 16, c2, jnp.where(lane < 24, c3, 0.0)))
    caug_ref[0] = (-parts).astype(BF16)

    sgs_ref[0] = _sigmoid(mm(C_GS, C_GA)).astype(BF16)
    sga_ref[0] = _sigmoid(mm(C_GA, C_END)).astype(BF16)

    qv = lax.dot_general(wt_ref[...], ub, (((1,), (1,)), ((), ())),
                         preferred_element_type=F32)
    qt_ref[0] = (qv[:ATTN_WIDTH] * (LOG2E * ATTN_HEAD_DIM ** -0.5)).astype(BF16)
    vt_ref[0] = qv[ATTN_WIDTH:].astype(BF16)


def _inproj(x, scale, shift, w_main, w_t, w_f3, b_f3, *, ts):
    nb, s, d = x.shape
    nt = s // ts
    tri = jnp.tril(jnp.ones((ts, ts), BF16))
    tok3 = lambda w: pl.BlockSpec((1, ts, w), lambda i, b: (b, i, 0))
    const2 = lambda a: pl.BlockSpec(a.shape, lambda i, b: (0, 0), pipeline_mode=pl.Buffered(1))
    bf = lambda shape: jax.ShapeDtypeStruct(shape, BF16)
    return pl.pallas_call(
        functools.partial(_inproj_kernel, ts=ts),
        grid=(nt, nb),
        in_specs=[tok3(d),
                  pl.BlockSpec((1, 1, d), lambda i, b: (b, 0, 0)),
                  pl.BlockSpec((1, 1, d), lambda i, b: (b, 0, 0)),
                  const2(w_main), const2(w_t), const2(w_f3), const2(b_f3), const2(tri)],
        out_specs=[pl.BlockSpec((S5_TILES, ts * nb, LANES), lambda i, b: (0, i, 0)),
                   tok3(SSM_WIDTH), tok3(ATTN_WIDTH), tok3(ATTN_WIDTH),
                   tok3(D_MODEL), tok3(D_MODEL),
                   pl.BlockSpec((1, ATTN_WIDTH, ts), lambda i, b: (b, 0, i)),
                   pl.BlockSpec((1, ATTN_WIDTH, ts), lambda i, b: (b, 0, i)),
                   tok3(LANES)],
        out_shape=[jax.ShapeDtypeStruct((S5_TILES, s * nb, LANES), F32),
                   bf((nb, s, SSM_WIDTH)), bf((nb, s, ATTN_WIDTH)), bf((nb, s, ATTN_WIDTH)),
                   bf((nb, s, D_MODEL)), bf((nb, s, D_MODEL)),
                   bf((nb, ATTN_WIDTH, s)), bf((nb, ATTN_WIDTH, s)),
                   bf((nb, s, LANES))],
        scratch_shapes=[pltpu.VMEM((nb, SUBLANES, LANES), F32)],
        compiler_params=_cparams("arbitrary", "arbitrary"),
        name="inproj",
    )(x, scale, shift, w_main, w_t, w_f3, b_f3, tri)


def _gelu_tanh(x):
    return 0.5 * x * (1.0 + jnp.tanh(math.sqrt(2.0 / math.pi) * (x + 0.044715 * (x * x * x))))


def _s5_kernel(xs_ref, bmat_ref, cmat_ref, are_ref, aim_ref, d_ref, wglu_ref,
               y_ref, buf_ref, h_ref, ybuf_ref, *, steps, sub):
    @pl.when(pl.program_id(0) == 0)
    def _():
        h_ref[...] = jnp.zeros_like(h_ref)

    srows = sub * SUBLANES

    def expand(c):
        r = slice(c * srows, (c + 1) * srows)
        for j in range(S5_TILES):
            lo = j * S5_TILE_LANES
            buf_ref[r, lo:lo + S5_TILE_LANES] = jnp.dot(
                xs_ref[j, r, :].astype(BF16), bmat_ref[j], preferred_element_type=F32)

    def recur(c, state):
        new_state = []
        for j in range(S5_TILES):
            re = slice(j * S5_TILE_LANES, j * S5_TILE_LANES + S5_HALF)
            im = slice(j * S5_TILE_LANES + S5_HALF, (j + 1) * S5_TILE_LANES)
            ar = jnp.broadcast_to(are_ref[:, j * S5_HALF:(j + 1) * S5_HALF], (SUBLANES, S5_HALF))
            ai = jnp.broadcast_to(aim_ref[:, j * S5_HALF:(j + 1) * S5_HALF], (SUBLANES, S5_HALF))
            hr, hi = state[j]
            for t in range(sub):
                rt = slice(c * srows + t * SUBLANES, c * srows + (t + 1) * SUBLANES)
                nhr = ar * hr - ai * hi + buf_ref[rt, re]
                nhi = ar * hi + ai * hr + buf_ref[rt, im]
                buf_ref[rt, re] = nhr
                buf_ref[rt, im] = nhi
                hr, hi = nhr, nhi
            new_state.append((hr, hi))
        return new_state

    def project(c):
        r = slice(c * srows, (c + 1) * srows)
        ys = []
        for j in range(S5_TILES):
            lo = j * S5_TILE_LANES
            hb = buf_ref[r, lo:lo + S5_TILE_LANES].astype(BF16)
            ys.append(jnp.dot(hb, cmat_ref[j], preferred_element_type=F32)
                      + d_ref[:, j * LANES:(j + 1) * LANES] * xs_ref[j, r, :])
        g = _gelu_tanh(jnp.concatenate(ys, axis=1)).astype(BF16)
        ab = jnp.dot(g, wglu_ref[...], preferred_element_type=F32)
        out = ab[:, :SSM_WIDTH] * _sigmoid(ab[:, SSM_WIDTH:])
        for j in range(S5_TILES):
            ybuf_ref[j] = out[:, j * LANES:(j + 1) * LANES]
        for b in range(BATCH):
            yb = jnp.concatenate([ybuf_ref[j, pl.ds(b, sub, stride=SUBLANES), :]
                                  for j in range(S5_TILES)], axis=1)
            y_ref[b, c * sub:(c + 1) * sub, :] = yb.astype(BF16)

    state = [(h_ref[:, j * S5_TILE_LANES:j * S5_TILE_LANES + S5_HALF],
              h_ref[:, j * S5_TILE_LANES + S5_HALF:(j + 1) * S5_TILE_LANES])
             for j in range(S5_TILES)]
    nsub = steps // sub
    expand(0)
    for c in range(nsub):
        if c + 1 < nsub:
            expand(c + 1)
        state = recur(c, state)
        project(c)
    for j in range(S5_TILES):
        h_ref[:, j * S5_TILE_LANES:j * S5_TILE_LANES + S5_HALF] = state[j][0]
        h_ref[:, j * S5_TILE_LANES + S5_HALF:(j + 1) * S5_TILE_LANES] = state[j][1]


def _s5(xs_t, bmat, cmat, a_re, a_im, d_skip, w_glu, *, steps, sub):
    t = xs_t.shape[1]
    rows = steps * SUBLANES
    const = lambda a: pl.BlockSpec(a.shape, lambda i: (0,) * a.ndim)
    tiles = pl.BlockSpec((S5_TILES, rows, LANES), lambda i: (0, i, 0))
    return pl.pallas_call(
        functools.partial(_s5_kernel, steps=steps, sub=sub),
        grid=(t // rows,),
        in_specs=[tiles,
                  const(bmat), const(cmat), const(a_re), const(a_im), const(d_skip), const(w_glu)],
        out_specs=pl.BlockSpec((BATCH, steps, SSM_WIDTH), lambda i: (0, i, 0)),
        out_shape=jax.ShapeDtypeStruct((BATCH, t // SUBLANES, SSM_WIDTH), BF16),
        scratch_shapes=[pltpu.VMEM((rows, S5_STATE_LANES), F32),
                        pltpu.VMEM((SUBLANES, S5_STATE_LANES), F32),
                        pltpu.VMEM((S5_TILES, sub * SUBLANES, LANES), F32)],
        compiler_params=_cparams("arbitrary"),
        name="s5",
    )(xs_t, bmat, cmat, a_re, a_im, d_skip, w_glu)


def _attn_kernel(qt_ref, k_ref, caug_ref, vt_ref, sza_ref, o_ref,
                 rhs_ref, s_ref, m_ref, l_ref, acc_ref, *, tq):
    qi = pl.program_id(1)
    tk = ATTN_TK
    nd = tq // tk
    hd = ATTN_HEAD_DIM
    row = lax.broadcasted_iota(jnp.int32, (LANES, tq), 0)

    for h in range(ATTN_HEADS):
        j = h // 2
        q2 = qt_ref[0, j * LANES:(j + 1) * LANES, :]
        rhs_ref[h, 0:LANES, :] = jnp.where((row // hd) == (h % 2), q2, jnp.zeros_like(q2))
        rhs_ref[h, LANES:2 * LANES, :] = jnp.where(
            (row == h) | (row == 8 + h) | (row == 16 + h), 1.0, 0.0).astype(BF16)
    m_ref[...] = jnp.full(m_ref.shape, -jnp.inf, F32)
    l_ref[...] = jnp.zeros_like(l_ref)
    acc_ref[...] = jnp.zeros_like(acc_ref)

    def scores(k0, h, diag=None):
        j = h // 2
        lo = 0 if diag is None else diag
        lhs = jnp.concatenate([k_ref[0, pl.ds(k0, tk), j * LANES:(j + 1) * LANES],
                               caug_ref[0, pl.ds(k0, tk), :]], axis=1)
        s_ref[h % ATTN_SLOTS, :, lo:tq] = jnp.dot(lhs, rhs_ref[h, :, lo:tq],
                                                  preferred_element_type=F32)

    def update(k0, h, diag=None):
        lo = 0 if diag is None else diag
        s = s_ref[h % ATTN_SLOTS, :, lo:tq]
        if diag is not None:
            kpos = lax.broadcasted_iota(jnp.int32, s.shape, 0)
            qpos = lax.broadcasted_iota(jnp.int32, s.shape, 1)
            s = jnp.where(kpos <= qpos, s, NEG_BIG)
        hs = slice(h * SUBLANES, h * SUBLANES + 1)
        m = m_ref[hs, lo:tq]
        m_new = jnp.maximum(m, jnp.max(s, axis=0, keepdims=True))
        alpha = jnp.exp2(m - m_new)
        p = jnp.exp2(s - m_new).astype(BF16)
        m_ref[hs, lo:tq] = m_new
        vt1 = jnp.concatenate([vt_ref[0, h * hd:(h + 1) * hd, pl.ds(k0, tk)],
                               jnp.ones((2 * SUBLANES, tk), BF16)], axis=0)
        pv = jnp.dot(vt1, p, preferred_element_type=F32)
        l_ref[hs, lo:tq] = alpha * l_ref[hs, lo:tq] + pv[hd:hd + 1, :]
        acc_ref[h * hd:(h + 1) * hd, lo:tq] = (alpha * acc_ref[h * hd:(h + 1) * hd, lo:tq]
                                               + pv[:hd, :])

    def body(kj, carry):
        k0 = pl.multiple_of(kj * tk, tk)
        k1 = pl.multiple_of(kj * tk + tk, tk)
        for h in range(ATTN_HEADS):
            ha = h + ATTN_LOOKAHEAD
            if ha < ATTN_HEADS:
                scores(k0, ha)
            else:
                scores(k1, ha - ATTN_HEADS)
            update(k0, h)
        return carry

    for h in range(ATTN_LOOKAHEAD):
        scores(0, h)
    lax.fori_loop(0, qi * nd, body, 0)
    for d in range(nd):
        kd = pl.multiple_of(qi * tq + d * tk, tk)
        for h in range(ATTN_HEADS):
            ha = h + ATTN_LOOKAHEAD
            if ha < ATTN_HEADS:
                scores(kd, ha, d * tk)
            elif d + 1 < nd:
                scores(pl.multiple_of(kd + tk, tk), ha - ATTN_HEADS, (d + 1) * tk)
            update(kd, h, d * tk)

    for h in range(ATTN_HEADS):
        acc_ref[h * hd:(h + 1) * hd, :] = (acc_ref[h * hd:(h + 1) * hd, :]
                                           / l_ref[h * SUBLANES:h * SUBLANES + 1, :])
    o = acc_ref[...].T
    o_ref[0] = (o * sza_ref[0].astype(F32)).astype(BF16)


def _attn(qt, k, caug, vt, sza, *, tq):
    nb, s, w = k.shape
    return pl.pallas_call(
        functools.partial(_attn_kernel, tq=tq),
        grid=(nb, s // tq),
        in_specs=[pl.BlockSpec((1, w, tq), lambda b, i: (b, 0, i)),
                  pl.BlockSpec((1, s, w), lambda b, i: (b, 0, 0)),
                  pl.BlockSpec((1, s, LANES), lambda b, i: (b, 0, 0)),
                  pl.BlockSpec((1, w, s), lambda b, i: (b, 0, 0)),
                  pl.BlockSpec((1, tq, w), lambda b, i: (b, i, 0))],
        out_specs=pl.BlockSpec((1, tq, w), lambda b, i: (b, i, 0)),
        out_shape=jax.ShapeDtypeStruct((nb, s, w), BF16),
        scratch_shapes=[pltpu.VMEM((ATTN_HEADS, 2 * LANES, tq), BF16),
                        pltpu.VMEM((ATTN_SLOTS, ATTN_TK, tq), F32),
                        pltpu.VMEM((ATTN_HEADS * SUBLANES, tq), F32),
                        pltpu.VMEM((ATTN_HEADS * SUBLANES, tq), F32),
                        pltpu.VMEM((w, tq), F32)],
        compiler_params=_cparams("arbitrary", "arbitrary"),
        name="attn",
    )(qt, k, caug, vt, sza)


def _merge_kernel(x_ref, ys_ref, szs_ref, oz_ref, sgs_ref, sga_ref, gate_ref,
                  wps_ref, wpa_ref, wout_ref, lng_ref, lnb_ref, o_ref, *, ts):
    for c in range(ts // MERGE_SUB):
        r = slice(c * MERGE_SUB, (c + 1) * MERGE_SUB)
        a = (ys_ref[0, r, :].astype(F32) * szs_ref[0, r, :].astype(F32)).astype(BF16)
        m1 = jnp.dot(a, wps_ref[...], preferred_element_type=F32) * sgs_ref[0, r, :].astype(F32)
        m2 = (jnp.dot(oz_ref[0, r, :], wpa_ref[...], preferred_element_type=F32)
              * sga_ref[0, r, :].astype(F32))
        merged = (m1 + m2).astype(BF16)
        sub = jnp.dot(merged, wout_ref[...], preferred_element_type=F32) * gate_ref[0]
        y = DEEPNORM_ALPHA * x_ref[0, r, :] + sub
        mu = jnp.mean(y, axis=-1, keepdims=True)
        yc = y - mu
        var = jnp.mean(yc * yc, axis=-1, keepdims=True)
        o_ref[0, r, :] = yc * lax.rsqrt(var + LN_EPS) * lng_ref[...] + lnb_ref[...]


def _merge(x, ys_t, szs, oz, sgs, sga, gate, wps, wpa, wout, ln_g, ln_b, *, ts):
    nb, s, d = x.shape
    tok3 = lambda w: pl.BlockSpec((1, ts, w), lambda i, b: (b, i, 0))
    const2 = lambda a: pl.BlockSpec(a.shape, lambda i, b: (0, 0))
    return pl.pallas_call(
        functools.partial(_merge_kernel, ts=ts),
        grid=(s // ts, nb),
        in_specs=[tok3(d), tok3(SSM_WIDTH),
                  tok3(SSM_WIDTH), tok3(ATTN_WIDTH), tok3(d), tok3(d),
                  pl.BlockSpec((1, 1, d), lambda i, b: (b, 0, 0)),
                  const2(wps), const2(wpa), const2(wout), const2(ln_g), const2(ln_b)],
        out_specs=tok3(d),
        out_shape=jax.ShapeDtypeStruct((nb, s, d), x.dtype),
        compiler_params=_cparams("arbitrary", "arbitrary"),
        name="merge",
    )(x, ys_t, szs, oz, sgs, sga, gate, wps, wpa, wout, ln_g, ln_b)


def _block_diag_tiles(blocks):
    eye = jnp.eye(S5_TILE_GROUPS, dtype=blocks.dtype)
    t, g, r, c = blocks.shape
    return jnp.einsum('tgrc,gG->tgrGc', blocks, eye).reshape(t, g * r, g * c)


def kernel(x, c, w_ada, b_ada, w_in, b_f, lam_re, lam_im, log_dt, ssm_b_re, ssm_b_im,
           ssm_c_re, ssm_c_im, ssm_d, w_glu, w_proj_ssm, w_proj_attn, w_out, ln_g, ln_b):
    nb, s, d = x.shape
    l = 0
    mod = _ada(c, w_ada[l], b_ada[l])
    shift = mod[:, None, :d]
    scale = mod[:, None, d:2 * d]
    gate = mod[:, None, 2 * d:]

    w = w_in[l]
    o_xs, o_zs, o_q, o_k, o_v, o_f, o_za, o_gs, o_ga, o_end = (
        0, 512, 1024, 1536, 2048, 2560, 2568, 3080, 4104, 5128)
    w_main = jnp.concatenate([w[:, o_xs:o_zs], w[:, o_zs:o_q], w[:, o_k:o_v],
                              w[:, o_za:o_gs], w[:, o_gs:o_ga], w[:, o_ga:o_end]],
                             axis=1).astype(BF16)
    w_t = jnp.concatenate([w[:, o_q:o_k], w[:, o_v:o_f]], axis=1).T.astype(BF16)
    wf = w[:, o_f:o_za]
    w_f3 = jnp.concatenate([wf, wf, wf, jnp.zeros((d, LANES - 3 * ATTN_HEADS), w.dtype)],
                           axis=1).astype(BF16)
    bfv = b_f[l].astype(F32)
    b_f3 = jnp.concatenate([bfv, bfv, bfv, jnp.zeros((LANES - 3 * ATTN_HEADS,), F32)])[None, :]

    (xs_t, szs, k, sza, sgs, sga, qt, vt, caug) = _inproj(
        x, scale, shift, w_main, w_t, w_f3, b_f3, ts=256)

    abr, abi, bbr, bbi, ncim = _zoh(lam_re[l], lam_im[l], log_dt[l], ssm_b_re[l], ssm_b_im[l],
                                    ssm_c_im[l])
    g4 = (S5_TILES, S5_TILE_GROUPS)
    bbr = bbr.reshape(g4 + (SSM_GROUP, SSM_STATE))
    bbi = bbi.reshape(g4 + (SSM_GROUP, SSM_STATE))
    bmat = jnp.concatenate([_block_diag_tiles(bbr), _block_diag_tiles(bbi)], axis=2).astype(BF16)
    cre_t = ssm_c_re[l].reshape(g4 + (SSM_GROUP, SSM_STATE)).transpose(0, 1, 3, 2)
    ncim_t = ncim.reshape(g4 + (SSM_GROUP, SSM_STATE)).transpose(0, 1, 3, 2)
    cmat = jnp.concatenate([_block_diag_tiles(cre_t), _block_diag_tiles(ncim_t)],
                           axis=1).astype(BF16)
    a_re = abr.reshape(SSM_GROUPS, SSM_GROUP, SSM_STATE)[:, 0, :].reshape(1, SSM_GROUPS * SSM_STATE)
    a_im = abi.reshape(SSM_GROUPS, SSM_GROUP, SSM_STATE)[:, 0, :].reshape(1, SSM_GROUPS * SSM_STATE)

    ys_t = _s5(xs_t, bmat, cmat, a_re, a_im,
               ssm_d[l].reshape(1, SSM_WIDTH).astype(F32), w_glu[l].astype(BF16), steps=128, sub=32)

    oz = _attn(qt, k, caug, vt, sza, tq=512)

    return _merge(x, ys_t, szs, oz, sgs, sga, gate,
                  w_proj_ssm[l].astype(BF16), w_proj_attn[l].astype(BF16), w_out[l].astype(BF16),
                  ln_g[l].reshape(1, d).astype(F32), ln_b[l].astype(F32).reshape(1, d), ts=512)
```

```python
import functools
import math

import jax
import jax.numpy as jnp
from jax import lax
from jax.experimental import pallas as pl
from jax.experimental.pallas import tpu as pltpu

D_MODEL = 1024
BATCH = 8
SEQ = 2048
SSM_WIDTH = 512
SSM_GROUP = 16
SSM_GROUPS = 32
SSM_STATE = 64
ATTN_HEADS = 8
ATTN_HEAD_DIM = 64
ATTN_WIDTH = 512
LN_EPS = 1e-5
DEEPNORM_ALPHA = 2.0 ** 0.25

F32 = jnp.float32
BF16 = jnp.bfloat16

LANES = 128
SUBLANES = 8
VMEM_LIMIT = 52 * 1024 * 1024

S5_TILES = 4
S5_TILE_GROUPS = SSM_GROUPS // S5_TILES
S5_HALF = S5_TILE_GROUPS * SSM_STATE
S5_TILE_LANES = 2 * S5_HALF
S5_TILE_IN = S5_TILE_GROUPS * SSM_GROUP
S5_STATE_LANES = S5_TILES * S5_TILE_LANES

C_XS, C_ZS, C_K, C_ZA, C_GS, C_GA, C_END = 0, 512, 1024, 1536, 2048, 3072, 4096

NEG_BIG = -1e30
LOG2E = math.log2(math.e)

MERGE_SUB = 256
ATTN_TK = 256
ATTN_LOOKAHEAD = 3
ATTN_SLOTS = 4
assert ATTN_SLOTS > ATTN_LOOKAHEAD and ATTN_HEADS % ATTN_SLOTS == 0


def _sigmoid(x):
    return 1.0 / (1.0 + jnp.exp(-x))


def _silu(x):
    return x * _sigmoid(x)


def _cparams(*sem):
    return pltpu.CompilerParams(dimension_semantics=sem, vmem_limit_bytes=VMEM_LIMIT)


def _ada_kernel(c_ref, w_ref, b_ref, o_ref):
    ca = _silu(c_ref[...]).astype(BF16)
    o_ref[...] = jnp.dot(ca, w_ref[...].astype(BF16), preferred_element_type=F32) + b_ref[...]


def _ada(c, w_ada, b_ada):
    n = w_ada.shape[1]
    tn = 512
    return pl.pallas_call(
        _ada_kernel,
        grid=(n // tn,),
        in_specs=[pl.BlockSpec((BATCH, D_MODEL), lambda j: (0, 0)),
                  pl.BlockSpec((D_MODEL, tn), lambda j: (0, j)),
                  pl.BlockSpec((1, tn), lambda j: (0, j))],
        out_specs=pl.BlockSpec((BATCH, tn), lambda j: (0, j)),
        out_shape=jax.ShapeDtypeStruct((BATCH, n), F32),
        compiler_params=_cparams("arbitrary"),
        name="ada",
    )(c, w_ada, b_ada.reshape(1, n))


def _zoh_kernel(lr_ref, li_ref, ldt_ref, br_ref, bi_ref, cim_ref,
                abr_ref, abi_ref, bbr_ref, bbi_ref, ncim_ref):
    lr = lr_ref[...]
    li = li_ref[...]
    dt = jnp.exp(ldt_ref[...])
    mag = jnp.exp(lr * dt)
    abr = mag * jnp.cos(li * dt)
    abi = mag * jnp.sin(li * dt)
    den = lr * lr + li * li
    cr = ((abr - 1.0) * lr + abi * li) / den
    ci = (abi * lr - (abr - 1.0) * li) / den
    br = br_ref[...]
    bi = bi_ref[...]
    abr_ref[...] = abr
    abi_ref[...] = abi
    bbr_ref[...] = cr * br - ci * bi
    bbi_ref[...] = cr * bi + ci * br
    ncim_ref[...] = -cim_ref[...]


def _zoh(lam_re, lam_im, log_dt, b_re, b_im, c_im):
    rows = SSM_GROUPS * SSM_GROUP
    rep = lambda a: jnp.repeat(a, SSM_GROUP, axis=0)
    lr = rep(lam_re)
    li = rep(lam_im)
    ldt = jnp.broadcast_to(rep(log_dt[:, None]), (rows, SSM_STATE))
    brt = b_re.transpose(0, 2, 1).reshape(rows, SSM_STATE)
    bit = b_im.transpose(0, 2, 1).reshape(rows, SSM_STATE)
    cim = c_im.reshape(rows, SSM_STATE)
    shp = jax.ShapeDtypeStruct((rows, SSM_STATE), F32)
    return pl.pallas_call(
        _zoh_kernel,
        out_shape=(shp,) * 5,
        name="zoh",
    )(lr, li, ldt, brt, bit, cim)


O_XS, O_ZS, O_Q, O_K, O_V, O_F, O_ZA, O_GS, O_GA, O_END = (
    0, 512, 1024, 1536, 2048, 2560, 2568, 3080, 4104, 5128)


def _wprep_kernel(w_ref, wm_ref, wt_ref, wf_ref):
    w = w_ref[...]
    bf = lambda lo, hi: w[:, lo:hi].astype(BF16)
    wm_ref[:, C_XS:C_K] = bf(O_XS, O_Q)
    wm_ref[:, C_K:C_ZA] = bf(O_K, O_V)
    wm_ref[:, C_ZA:C_GS] = bf(O_ZA, O_GS)
    wm_ref[:, C_GS:C_END] = bf(O_GS, O_END)
    wt_ref[0:ATTN_WIDTH, :] = w[:, O_Q:O_K].T.astype(BF16)
    wt_ref[ATTN_WIDTH:2 * ATTN_WIDTH, :] = w[:, O_V:O_F].T.astype(BF16)
    blk = w[:, O_F:O_F + LANES]
    lane = lax.broadcasted_iota(jnp.int32, blk.shape, 1)
    f0 = jnp.where(lane < ATTN_HEADS, blk, 0.0)
    wf_ref[...] = (f0 + pltpu.roll(f0, ATTN_HEADS, 1) + pltpu.roll(f0, 2 * ATTN_HEADS, 1)).astype(BF16)


def _wprep(w):
    d, n = w.shape
    tr = 128
    return pl.pallas_call(
        _wprep_kernel,
        grid=(d // tr,),
        in_specs=[pl.BlockSpec((tr, n), lambda r: (r, 0))],
        out_specs=[pl.BlockSpec((tr, C_END), lambda r: (r, 0)),
                   pl.BlockSpec((2 * ATTN_WIDTH, tr), lambda r: (0, r)),
                   pl.BlockSpec((tr, LANES), lambda r: (r, 0))],
        out_shape=[jax.ShapeDtypeStruct((d, C_END), BF16),
                   jax.ShapeDtypeStruct((2 * ATTN_WIDTH, d), BF16),
                   jax.ShapeDtypeStruct((d, LANES), BF16)],
        compiler_params=_cparams("arbitrary"),
        name="wprep",
    )(w)


def _inproj_kernel(x_ref, scale_ref, shift_ref, wm_ref, wt_ref, wf_ref, bf_ref, tri_ref,
                   xs_ref, szs_ref, k_ref, sza_ref, sgs_ref, sga_ref, qt_ref, vt_ref, caug_ref,
                   carry_ref, *, ts):
    i = pl.program_id(0)
    b = pl.program_id(1)

    @pl.when(i == 0)
    def _():
        carry_ref[b] = jnp.zeros(carry_ref.shape[1:], F32)

    x = x_ref[0]
    mu = jnp.mean(x, axis=-1, keepdims=True)
    xc = x - mu
    var = jnp.mean(xc * xc, axis=-1, keepdims=True)
    xn = xc * lax.rsqrt(var + LN_EPS)
    u = xn * (1.0 + scale_ref[0]) + shift_ref[0]
    ub = u.astype(BF16)

    def mm(lo, hi):
        return jnp.dot(ub, wm_ref[:, lo:hi], preferred_element_type=F32)

    f = jnp.dot(ub, wf_ref[...], preferred_element_type=F32) + bf_ref[...]
    lf = jnp.minimum(f, 0.0) - jnp.log(1.0 + jnp.exp(-jnp.abs(f)))
    l1 = lf.astype(BF16)
    r1 = lf - l1.astype(F32)
    l2 = r1.astype(BF16)
    l3 = (r1 - l2.astype(F32)).astype(BF16)

    xs = mm(C_XS, C_ZS)
    for j in range(S5_TILES):
        xs_ref[j, pl.ds(b, ts, stride=SUBLANES), :] = xs[:, j * LANES:(j + 1) * LANES]
    szs_ref[0] = _silu(mm(C_ZS, C_K)).astype(BF16)

    tri = tri_ref[...]
    cf = (jnp.dot(tri, l1, preferred_element_type=F32)
          + jnp.dot(tri, l2, preferred_element_type=F32)
          + jnp.dot(tri, l3, preferred_element_type=F32)) + carry_ref[b][0:1, :]
    carry_ref[b] = jnp.broadcast_to(cf[ts - 1:ts, :], carry_ref.shape[1:])

    k_ref[0] = mm(C_K, C_ZA).astype(BF16)
    sza_ref[0] = _silu(mm(C_ZA, C_GS)).astype(BF16)

    cf2 = cf * LOG2E
    c1 = cf2.astype(BF16).astype(F32)
    r = cf2 - c1
    c2 = r.astype(BF16).astype(F32)
    c3 = r - c2
    lane = lax.broadcasted_iota(jnp.int32, cf.shape, 1)
    parts = jnp.where(lane < 8, c1, jnp.where(lane < 16, c2, jnp.where(lane < 24, c3, 0.0)))
    caug_ref[0] = (-parts).astype(BF16)

    sgs_ref[0] = _sigmoid(mm(C_GS, C_GA)).astype(BF16)
    sga_ref[0] = _sigmoid(mm(C_GA, C_END)).astype(BF16)

    qv = lax.dot_general(wt_ref[...], ub, (((1,), (1,)), ((), ())),
                         preferred_element_type=F32)
    qt_ref[0] = (qv[:ATTN_WIDTH] * (LOG2E * ATTN_HEAD_DIM ** -0.5)).astype(BF16)
    vt_ref[0] = qv[ATTN_WIDTH:].astype(BF16)


def _inproj(x, scale, shift, w_main, w_t, w_f3, b_f3, *, ts):
    nb, s, d = x.shape
    nt = s // ts
    tri = jnp.tril(jnp.ones((ts, ts), BF16))
    tok3 = lambda w: pl.BlockSpec((1, ts, w), lambda i, b: (b, i, 0))
    const2 = lambda a: pl.BlockSpec(a.shape, lambda i, b: (0, 0), pipeline_mode=pl.Buffered(1))
    bf = lambda shape: jax.ShapeDtypeStruct(shape, BF16)
    return pl.pallas_call(
        functools.partial(_inproj_kernel, ts=ts),
        grid=(nt, nb),
        in_specs=[tok3(d),
                  pl.BlockSpec((1, 1, d), lambda i, b: (b, 0, 0)),
                  pl.BlockSpec((1, 1, d), lambda i, b: (b, 0, 0)),
                  const2(w_main), const2(w_t), const2(w_f3), const2(b_f3), const2(tri)],
        out_specs=[pl.BlockSpec((S5_TILES, ts * nb, LANES), lambda i, b: (0, i, 0)),
                   tok3(SSM_WIDTH), tok3(ATTN_WIDTH), tok3(ATTN_WIDTH),
                   tok3(D_MODEL), tok3(D_MODEL),
                   pl.BlockSpec((1, ATTN_WIDTH, ts), lambda i, b: (b, 0, i)),
                   pl.BlockSpec((1, ATTN_WIDTH, ts), lambda i, b: (b, 0, i)),
                   tok3(LANES)],
        out_shape=[jax.ShapeDtypeStruct((S5_TILES, s * nb, LANES), F32),
                   bf((nb, s, SSM_WIDTH)), bf((nb, s, ATTN_WIDTH)), bf((nb, s, ATTN_WIDTH)),
                   bf((nb, s, D_MODEL)), bf((nb, s, D_MODEL)),
                   bf((nb, ATTN_WIDTH, s)), bf((nb, ATTN_WIDTH, s)),
                   bf((nb, s, LANES))],
        scratch_shapes=[pltpu.VMEM((nb, SUBLANES, LANES), F32)],
        compiler_params=_cparams("arbitrary", "arbitrary"),
        name="inproj",
    )(x, scale, shift, w_main, w_t, w_f3, b_f3, tri)


def _gelu_tanh(x):
    return 0.5 * x * (1.0 + jnp.tanh(math.sqrt(2.0 / math.pi) * (x + 0.044715 * (x * x * x))))


def _s5_kernel(xs_ref, bmat_ref, cmat_ref, are_ref, aim_ref, d_ref, wglu_ref,
               y_ref, buf_ref, h_ref, ybuf_ref, *, steps, sub):
    @pl.when(pl.program_id(0) == 0)
    def _():
        h_ref[...] = jnp.zeros_like(h_ref)

    srows = sub * SUBLANES

    def expand(c):
        r = slice(c * srows, (c + 1) * srows)
        for j in range(S5_TILES):
            lo = j * S5_TILE_LANES
            buf_ref[r, lo:lo + S5_TILE_LANES] = jnp.dot(
                xs_ref[j, r, :].astype(BF16), bmat_ref[j], preferred_element_type=F32)

    def recur(c, state):
        new_state = []
        for j in range(S5_TILES):
            re = slice(j * S5_TILE_LANES, j * S5_TILE_LANES + S5_HALF)
            im = slice(j * S5_TILE_LANES + S5_HALF, (j + 1) * S5_TILE_LANES)
            ar = jnp.broadcast_to(are_ref[:, j * S5_HALF:(j + 1) * S5_HALF], (SUBLANES, S5_HALF))
            ai = jnp.broadcast_to(aim_ref[:, j * S5_HALF:(j + 1) * S5_HALF], (SUBLANES, S5_HALF))
            hr, hi = state[j]
            for t in range(sub):
                rt = slice(c * srows + t * SUBLANES, c * srows + (t + 1) * SUBLANES)
                nhr = ar * hr - ai * hi + buf_ref[rt, re]
                nhi = ar * hi + ai * hr + buf_ref[rt, im]
                buf_ref[rt, re] = nhr
                buf_ref[rt, im] = nhi
                hr, hi = nhr, nhi
            new_state.append((hr, hi))
        return new_state

    def project(c):
        r = slice(c * srows, (c + 1) * srows)
        ys = []
        for j in range(S5_TILES):
            lo = j * S5_TILE_LANES
            hb = buf_ref[r, lo:lo + S5_TILE_LANES].astype(BF16)
            ys.append(jnp.dot(hb, cmat_ref[j], preferred_element_type=F32)
                      + d_ref[:, j * LANES:(j + 1) * LANES] * xs_ref[j, r, :])
        g = _gelu_tanh(jnp.concatenate(ys, axis=1)).astype(BF16)
        ab = jnp.dot(g, wglu_ref[...], preferred_element_type=F32)
        out = ab[:, :SSM_WIDTH] * _sigmoid(ab[:, SSM_WIDTH:])
        for j in range(S5_TILES):
            ybuf_ref[j] = out[:, j * LANES:(j + 1) * LANES]
        for b in range(BATCH):
            yb = jnp.concatenate([ybuf_ref[j, pl.ds(b, sub, stride=SUBLANES), :]
                                  for j in range(S5_TILES)], axis=1)
            y_ref[b, c * sub:(c + 1) * sub, :] = yb.astype(BF16)

    state = [(h_ref[:, j * S5_TILE_LANES:j * S5_TILE_LANES + S5_HALF],
              h_ref[:, j * S5_TILE_LANES + S5_HALF:(j + 1) * S5_TILE_LANES])
             for j in range(S5_TILES)]
    nsub = steps // sub
    expand(0)
    for c in range(nsub):
        if c + 1 < nsub:
            expand(c + 1)
        state = recur(c, state)
        project(c)
    for j in range(S5_TILES):
        h_ref[:, j * S5_TILE_LANES:j * S5_TILE_LANES + S5_HALF] = state[j][0]
        h_ref[:, j * S5_TILE_LANES + S5_HALF:(j + 1) * S5_TILE_LANES] = state[j][1]


def _s5(xs_t, bmat, cmat, a_re, a_im, d_skip, w_glu, *, steps, sub):
    t = xs_t.shape[1]
    rows = steps * SUBLANES
    const = lambda a: pl.BlockSpec(a.shape, lambda i: (0,) * a.ndim)
    tiles = pl.BlockSpec((S5_TILES, rows, LANES), lambda i: (0, i, 0))
    return pl.pallas_call(
        functools.partial(_s5_kernel, steps=steps, sub=sub),
        grid=(t // rows,),
        in_specs=[tiles,
                  const(bmat), const(cmat), const(a_re), const(a_im), const(d_skip), const(w_glu)],
        out_specs=pl.BlockSpec((BATCH, steps, SSM_WIDTH), lambda i: (0, i, 0)),
        out_shape=jax.ShapeDtypeStruct((BATCH, t // SUBLANES, SSM_WIDTH), BF16),
        scratch_shapes=[pltpu.VMEM((rows, S5_STATE_LANES), F32),
                        pltpu.VMEM((SUBLANES, S5_STATE_LANES), F32),
                        pltpu.VMEM((S5_TILES, sub * SUBLANES, LANES), F32)],
        compiler_params=_cparams("arbitrary"),
        name="s5",
    )(xs_t, bmat, cmat, a_re, a_im, d_skip, w_glu)


def _attn_kernel(qt_ref, k_ref, caug_ref, vt_ref, sza_ref, o_ref,
                 rhs_ref, s_ref, m_ref, l_ref, acc_ref, *, tq):
    qi = pl.program_id(1)
    tk = ATTN_TK
    nd = tq // tk
    hd = ATTN_HEAD_DIM
    row = lax.broadcasted_iota(jnp.int32, (LANES, tq), 0)

    for h in range(ATTN_HEADS):
        j = h // 2
        q2 = qt_ref[0, j * LANES:(j + 1) * LANES, :]
        rhs_ref[h, 0:LANES, :] = jnp.where((row // hd) == (h % 2), q2, jnp.zeros_like(q2))
        rhs_ref[h, LANES:2 * LANES, :] = jnp.where(
            (row == h) | (row == 8 + h) | (row == 16 + h), 1.0, 0.0).astype(BF16)
    m_ref[...] = jnp.full(m_ref.shape, -jnp.inf, F32)
    l_ref[...] = jnp.zeros_like(l_ref)
    acc_ref[...] = jnp.zeros_like(acc_ref)

    def scores(k0, h, diag=None):
        j = h // 2
        lo = 0 if diag is None else diag
        lhs = jnp.concatenate([k_ref[0, pl.ds(k0, tk), j * LANES:(j + 1) * LANES],
                               caug_ref[0, pl.ds(k0, tk), :]], axis=1)
        s_ref[h % ATTN_SLOTS, :, lo:tq] = jnp.dot(lhs, rhs_ref[h, :, lo:tq],
                                                  preferred_element_type=F32)

    def update(k0, h, diag=None):
        lo = 0 if diag is None else diag
        s = s_ref[h % ATTN_SLOTS, :, lo:tq]
        if diag is not None:
            kpos = lax.broadcasted_iota(jnp.int32, s.shape, 0)
            qpos = lax.broadcasted_iota(jnp.int32, s.shape, 1)
            s = jnp.where(kpos <= qpos, s, NEG_BIG)
        hs = slice(h * SUBLANES, h * SUBLANES + 1)
        m = m_ref[hs, lo:tq]
        m_new = jnp.maximum(m, jnp.max(s, axis=0, keepdims=True))
        alpha = jnp.exp2(m - m_new)
        p = jnp.exp2(s - m_new).astype(BF16)
        m_ref[hs, lo:tq] = m_new
        vt1 = jnp.concatenate([vt_ref[0, h * hd:(h + 1) * hd, pl.ds(k0, tk)],
                               jnp.ones((2 * SUBLANES, tk), BF16)], axis=0)
        pv = jnp.dot(vt1, p, preferred_element_type=F32)
        l_ref[hs, lo:tq] = alpha * l_ref[hs, lo:tq] + pv[hd:hd + 1, :]
        acc_ref[h * hd:(h + 1) * hd, lo:tq] = (alpha * acc_ref[h * hd:(h + 1) * hd, lo:tq]
                                               + pv[:hd, :])

    def body(kj, carry):
        k0 = pl.multiple_of(kj * tk, tk)
        k1 = pl.multiple_of(kj * tk + tk, tk)
        for h in range(ATTN_HEADS):
            ha = h + ATTN_LOOKAHEAD
            if ha < ATTN_HEADS:
                scores(k0, ha)
            else:
                scores(k1, ha - ATTN_HEADS)
            update(k0, h)
        return carry

    for h in range(ATTN_LOOKAHEAD):
        scores(0, h)
    lax.fori_loop(0, qi * nd, body, 0)
    for d in range(nd):
        kd = pl.multiple_of(qi * tq + d * tk, tk)
        for h in range(ATTN_HEADS):
            ha = h + ATTN_LOOKAHEAD
            if ha < ATTN_HEADS:
                scores(kd, ha, d * tk)
            elif d + 1 < nd:
                scores(pl.multiple_of(kd + tk, tk), ha - ATTN_HEADS, (d + 1) * tk)
            update(kd, h, d * tk)

    for h in range(ATTN_HEADS):
        acc_ref[h * hd:(h + 1) * hd, :] = (acc_ref[h * hd:(h + 1) * hd, :]
                                           / l_ref[h * SUBLANES:h * SUBLANES + 1, :])
    o = acc_ref[...].T
    o_ref[0] = (o * sza_ref[0].astype(F32)).astype(BF16)


def _attn(qt, k, caug, vt, sza, *, tq):
    nb, s, w = k.shape
    return pl.pallas_call(
        functools.partial(_attn_kernel, tq=tq),
        grid=(nb, s // tq),
        in_specs=[pl.BlockSpec((1, w, tq), lambda b, i: (b, 0, i)),
                  pl.BlockSpec((1, s, w), lambda b, i: (b, 0, 0)),
                  pl.BlockSpec((1, s, LANES), lambda b, i: (b, 0, 0)),
                  pl.BlockSpec((1, w, s), lambda b, i: (b, 0, 0)),
                  pl.BlockSpec((1, tq, w), lambda b, i: (b, i, 0))],
        out_specs=pl.BlockSpec((1, tq, w), lambda b, i: (b, i, 0)),
        out_shape=jax.ShapeDtypeStruct((nb, s, w), BF16),
        scratch_shapes=[pltpu.VMEM((ATTN_HEADS, 2 * LANES, tq), BF16),
                        pltpu.VMEM((ATTN_SLOTS, ATTN_TK, tq), F32),
                        pltpu.VMEM((ATTN_HEADS * SUBLANES, tq), F32),
                        pltpu.VMEM((ATTN_HEADS * SUBLANES, tq), F32),
                        pltpu.VMEM((w, tq), F32)],
        compiler_params=_cparams("arbitrary", "arbitrary"),
        name="attn",
    )(qt, k, caug, vt, sza)


def _merge_kernel(x_ref, ys_ref, szs_ref, oz_ref, sgs_ref, sga_ref, gate_ref,
                  wps_ref, wpa_ref, wout_ref, lng_ref, lnb_ref, o_ref, *, ts):
    for c in range(ts // MERGE_SUB):
        r = slice(c * MERGE_SUB, (c + 1) * MERGE_SUB)
        a = (ys_ref[0, r, :].astype(F32) * szs_ref[0, r, :].astype(F32)).astype(BF16)
        m1 = jnp.dot(a, wps_ref[...], preferred_element_type=F32) * sgs_ref[0, r, :].astype(F32)
        m2 = (jnp.dot(oz_ref[0, r, :], wpa_ref[...], preferred_element_type=F32)
              * sga_ref[0, r, :].astype(F32))
        merged = (m1 + m2).astype(BF16)
        sub = jnp.dot(merged, wout_ref[...], preferred_element_type=F32) * gate_ref[0]
        y = DEEPNORM_ALPHA * x_ref[0, r, :] + sub
        mu = jnp.mean(y, axis=-1, keepdims=True)
        yc = y - mu
        var = jnp.mean(yc * yc, axis=-1, keepdims=True)
        o_ref[0, r, :] = yc * lax.rsqrt(var + LN_EPS) * lng_ref[...] + lnb_ref[...]


def _merge(x, ys_t, szs, oz, sgs, sga, gate, wps, wpa, wout, ln_g, ln_b, *, ts):
    nb, s, d = x.shape
    tok3 = lambda w: pl.BlockSpec((1, ts, w), lambda i, b: (b, i, 0))
    const2 = lambda a: pl.BlockSpec(a.shape, lambda i, b: (0, 0))
    return pl.pallas_call(
        functools.partial(_merge_kernel, ts=ts),
        grid=(s // ts, nb),
        in_specs=[tok3(d), tok3(SSM_WIDTH),
                  tok3(SSM_WIDTH), tok3(ATTN_WIDTH), tok3(d), tok3(d),
                  pl.BlockSpec((1, 1, d), lambda i, b: (b, 0, 0)),
                  const2(wps), const2(wpa), const2(wout), const2(ln_g), const2(ln_b)],
        out_specs=tok3(d),
        out_shape=jax.ShapeDtypeStruct((nb, s, d), x.dtype),
        compiler_params=_cparams("arbitrary", "arbitrary"),
        name="merge",
    )(x, ys_t, szs, oz, sgs, sga, gate, wps, wpa, wout, ln_g, ln_b)


def _block_diag_tiles(blocks):
    eye = jnp.eye(S5_TILE_GROUPS, dtype=blocks.dtype)
    t, g, r, c = blocks.shape
    return jnp.einsum('tgrc,gG->tgrGc', blocks, eye).reshape(t, g * r, g * c)


def kernel(x, c, w_ada, b_ada, w_in, b_f, lam_re, lam_im, log_dt, ssm_b_re, ssm_b_im,
           ssm_c_re, ssm_c_im, ssm_d, w_glu, w_proj_ssm, w_proj_attn, w_out, ln_g, ln_b):
    nb, s, d = x.shape
    l = 0
    mod = _ada(c, w_ada[l], b_ada[l])
    shift = mod[:, None, :d]
    scale = mod[:, None, d:2 * d]
    gate = mod[:, None, 2 * d:]

    w_main, w_t, w_f3 = _wprep(w_in[l])
    bfv = b_f[l].astype(F32)
    b_f3 = jnp.concatenate([bfv, bfv, bfv, jnp.zeros((LANES - 3 * ATTN_HEADS,), F32)])[None, :]

    (xs_t, szs, k, sza, sgs, sga, qt, vt, caug) = _inproj(
        x, scale, shift, w_main, w_t, w_f3, b_f3, ts=256)

    abr, abi, bbr, bbi, ncim = _zoh(lam_re[l], lam_im[l], log_dt[l], ssm_b_re[l], ssm_b_im[l],
                                    ssm_c_im[l])
    g4 = (S5_TILES, S5_TILE_GROUPS)
    bbr = bbr.reshape(g4 + (SSM_GROUP, SSM_STATE))
    bbi = bbi.reshape(g4 + (SSM_GROUP, SSM_STATE))
    bmat = jnp.concatenate([_block_diag_tiles(bbr), _block_diag_tiles(bbi)], axis=2).astype(BF16)
    cre_t = ssm_c_re[l].reshape(g4 + (SSM_GROUP, SSM_STATE)).transpose(0, 1, 3, 2)
    ncim_t = ncim.reshape(g4 + (SSM_GROUP, SSM_STATE)).transpose(0, 1, 3, 2)
    cmat = jnp.concatenate([_block_diag_tiles(cre_t), _block_diag_tiles(ncim_t)],
                           axis=1).astype(BF16)
    a_re = abr.reshape(SSM_GROUPS, SSM_GROUP, SSM_STATE)[:, 0, :].reshape(1, SSM_GROUPS * SSM_STATE)
    a_im = abi.reshape(SSM_GROUPS, SSM_GROUP, SSM_STATE)[:, 0, :].reshape(1, SSM_GROUPS * SSM_STATE)

    ys_t = _s5(xs_t, bmat, cmat, a_re, a_im,
               ssm_d[l].reshape(1, SSM_WIDTH).astype(F32), w_glu[l].astype(BF16), steps=128, sub=32)

    oz = _attn(qt, k, caug, vt, sza, tq=512)

    return _merge(x, ys_t, szs, oz, sgs, sga, gate,
                  w_proj_ssm[l].astype(BF16), w_proj_attn[l].astype(BF16), w_out[l].astype(BF16),
                  ln_g[l].reshape(1, d).astype(F32), ln_b[l].astype(F32).reshape(1, d), ts=512)
```

```python
import functools
import math

import jax
import jax.numpy as jnp
from jax import lax
from jax.experimental import pallas as pl
from jax.experimental.pallas import tpu as pltpu

D_MODEL = 1024
BATCH = 8
SEQ = 2048
SSM_WIDTH = 512
SSM_GROUP = 16
SSM_GROUPS = 32
SSM_STATE = 64
ATTN_HEADS = 8
ATTN_HEAD_DIM = 64
ATTN_WIDTH = 512
LN_EPS = 1e-5
DEEPNORM_ALPHA = 2.0 ** 0.25

F32 = jnp.float32
BF16 = jnp.bfloat16

LANES = 128
SUBLANES = 8
VMEM_LIMIT = 52 * 1024 * 1024

S5_TILES = 4
S5_TILE_GROUPS = SSM_GROUPS // S5_TILES
S5_HALF = S5_TILE_GROUPS * SSM_STATE
S5_TILE_LANES = 2 * S5_HALF
S5_TILE_IN = S5_TILE_GROUPS * SSM_GROUP
S5_STATE_LANES = S5_TILES * S5_TILE_LANES

C_XS, C_ZS, C_K, C_ZA, C_GS, C_GA, C_END = 0, 512, 1024, 1536, 2048, 3072, 4096

NEG_BIG = -1e30
LOG2E = math.log2(math.e)

MERGE_SUB = 256
ATTN_TK = 256
ATTN_LOOKAHEAD = 3
ATTN_SLOTS = 4
assert ATTN_SLOTS > ATTN_LOOKAHEAD and ATTN_HEADS % ATTN_SLOTS == 0


def _sigmoid(x):
    return 1.0 / (1.0 + jnp.exp(-x))


def _silu(x):
    return x * _sigmoid(x)


def _cparams(*sem):
    return pltpu.CompilerParams(dimension_semantics=sem, vmem_limit_bytes=VMEM_LIMIT)


def _ada_kernel(c_ref, w_ref, b_ref, o_ref):
    ca = _silu(c_ref[...]).astype(BF16)
    o_ref[...] = jnp.dot(ca, w_ref[...].astype(BF16), preferred_element_type=F32) + b_ref[...]


def _ada(c, w_ada, b_ada):
    n = w_ada.shape[1]
    tn = 512
    return pl.pallas_call(
        _ada_kernel,
        grid=(n // tn,),
        in_specs=[pl.BlockSpec((BATCH, D_MODEL), lambda j: (0, 0)),
                  pl.BlockSpec((D_MODEL, tn), lambda j: (0, j)),
                  pl.BlockSpec((1, tn), lambda j: (0, j))],
        out_specs=pl.BlockSpec((BATCH, tn), lambda j: (0, j)),
        out_shape=jax.ShapeDtypeStruct((BATCH, n), F32),
        compiler_params=_cparams("arbitrary"),
        name="ada",
    )(c, w_ada, b_ada.reshape(1, n))


def _zoh_kernel(lr_ref, li_ref, ldt_ref, br_ref, bi_ref, cim_ref,
                abr_ref, abi_ref, bbr_ref, bbi_ref, ncim_ref):
    lr = lr_ref[...]
    li = li_ref[...]
    dt = jnp.exp(ldt_ref[...])
    mag = jnp.exp(lr * dt)
    abr = mag * jnp.cos(li * dt)
    abi = mag * jnp.sin(li * dt)
    den = lr * lr + li * li
    cr = ((abr - 1.0) * lr + abi * li) / den
    ci = (abi * lr - (abr - 1.0) * li) / den
    br = br_ref[...]
    bi = bi_ref[...]
    abr_ref[...] = abr
    abi_ref[...] = abi
    bbr_ref[...] = cr * br - ci * bi
    bbi_ref[...] = cr * bi + ci * br
    ncim_ref[...] = -cim_ref[...]


def _zoh(lam_re, lam_im, log_dt, b_re, b_im, c_im):
    rows = SSM_GROUPS * SSM_GROUP
    rep = lambda a: jnp.repeat(a, SSM_GROUP, axis=0)
    lr = rep(lam_re)
    li = rep(lam_im)
    ldt = jnp.broadcast_to(rep(log_dt[:, None]), (rows, SSM_STATE))
    brt = b_re.transpose(0, 2, 1).reshape(rows, SSM_STATE)
    bit = b_im.transpose(0, 2, 1).reshape(rows, SSM_STATE)
    cim = c_im.reshape(rows, SSM_STATE)
    shp = jax.ShapeDtypeStruct((rows, SSM_STATE), F32)
    return pl.pallas_call(
        _zoh_kernel,
        out_shape=(shp,) * 5,
        name="zoh",
    )(lr, li, ldt, brt, bit, cim)


O_XS, O_ZS, O_Q, O_K, O_V, O_F, O_ZA, O_GS, O_GA, O_END = (
    0, 512, 1024, 1536, 2048, 2560, 2568, 3080, 4104, 5128)


WPREP_CHUNK = 256


def _wprep_kernel(wt_ref, wm_ref, wqv_ref, wf_ref):
    def put(c_lo, o_lo, n):
        for s in range(0, n, WPREP_CHUNK):
            wm_ref[:, c_lo + s:c_lo + s + WPREP_CHUNK] = (
                wt_ref[o_lo + s:o_lo + s + WPREP_CHUNK, :].T.astype(BF16))

    put(C_XS, O_XS, O_Q - O_XS)
    put(C_K, O_K, O_V - O_K)
    put(C_ZA, O_ZA, O_GS - O_ZA)
    put(C_GS, O_GS, O_END - O_GS)
    wqv_ref[0:ATTN_WIDTH, :] = wt_ref[O_Q:O_K, :].astype(BF16)
    wqv_ref[ATTN_WIDTH:2 * ATTN_WIDTH, :] = wt_ref[O_V:O_F, :].astype(BF16)
    f8 = wt_ref[O_F:O_ZA, :]
    blk = jnp.concatenate([f8, f8, f8, jnp.zeros((LANES - 3 * ATTN_HEADS, f8.shape[1]), F32)], axis=0)
    wf_ref[...] = blk.T.astype(BF16)


def _wprep(wt):
    n, d = wt.shape
    return pl.pallas_call(
        _wprep_kernel,
        out_shape=[jax.ShapeDtypeStruct((d, C_END), BF16),
                   jax.ShapeDtypeStruct((2 * ATTN_WIDTH, d), BF16),
                   jax.ShapeDtypeStruct((d, LANES), BF16)],
        compiler_params=pltpu.CompilerParams(vmem_limit_bytes=VMEM_LIMIT),
        name="wprep",
    )(wt)


def _inproj_kernel(x_ref, scale_ref, shift_ref, wm_ref, wt_ref, wf_ref, bf_ref, tri_ref,
                   xs_ref, szs_ref, k_ref, sza_ref, sgs_ref, sga_ref, qt_ref, vt_ref, caug_ref,
                   carry_ref, *, ts):
    i = pl.program_id(0)
    b = pl.program_id(1)

    @pl.when(i == 0)
    def _():
        carry_ref[b] = jnp.zeros(carry_ref.shape[1:], F32)

    x = x_ref[0]
    mu = jnp.mean(x, axis=-1, keepdims=True)
    xc = x - mu
    var = jnp.mean(xc * xc, axis=-1, keepdims=True)
    xn = xc * lax.rsqrt(var + LN_EPS)
    u = xn * (1.0 + scale_ref[0]) + shift_ref[0]
    ub = u.astype(BF16)

    def mm(lo, hi):
        return jnp.dot(ub, wm_ref[:, lo:hi], preferred_element_type=F32)

    f = jnp.dot(ub, wf_ref[...], preferred_element_type=F32) + bf_ref[...]
    lf = jnp.minimum(f, 0.0) - jnp.log(1.0 + jnp.exp(-jnp.abs(f)))
    l1 = lf.astype(BF16)
    r1 = lf - l1.astype(F32)
    l2 = r1.astype(BF16)
    l3 = (r1 - l2.astype(F32)).astype(BF16)

    xs = mm(C_XS, C_ZS)
    for j in range(S5_TILES):
        xs_ref[j, pl.ds(b, ts, stride=SUBLANES), :] = xs[:, j * LANES:(j + 1) * LANES]
    szs_ref[0] = _silu(mm(C_ZS, C_K)).astype(BF16)

    tri = tri_ref[...]
    cf = (jnp.dot(tri, l1, preferred_element_type=F32)
          + jnp.dot(tri, l2, preferred_element_type=F32)
          + jnp.dot(tri, l3, preferred_element_type=F32)) + carry_ref[b][0:1, :]
    carry_ref[b] = jnp.broadcast_to(cf[ts - 1:ts, :], carry_ref.shape[1:])

    k_ref[0] = mm(C_K, C_ZA).astype(BF16)
    sza_ref[0] = _silu(mm(C_ZA, C_GS)).astype(BF16)

    cf2 = cf * LOG2E
    c1 = cf2.astype(BF16).astype(F32)
    r = cf2 - c1
    c2 = r.astype(BF16).astype(F32)
    c3 = r - c2
    lane = lax.broadcasted_iota(jnp.int32, cf.shape, 1)
    parts = jnp.where(lane < 8, c1, jnp.where(lane < 16, c2, jnp.where(lane < 24, c3, 0.0)))
    caug_ref[0] = (-parts).astype(BF16)

    sgs_ref[0] = _sigmoid(mm(C_GS, C_GA)).astype(BF16)
    sga_ref[0] = _sigmoid(mm(C_GA, C_END)).astype(BF16)

    qv = lax.dot_general(wt_ref[...], ub, (((1,), (1,)), ((), ())),
                         preferred_element_type=F32)
    qt_ref[0] = (qv[:ATTN_WIDTH] * (LOG2E * ATTN_HEAD_DIM ** -0.5)).astype(BF16)
    vt_ref[0] = qv[ATTN_WIDTH:].astype(BF16)


def _inproj(x, scale, shift, w_main, w_t, w_f3, b_f3, *, ts):
    nb, s, d = x.shape
    nt = s // ts
    tri = jnp.tril(jnp.ones((ts, ts), BF16))
    tok3 = lambda w: pl.BlockSpec((1, ts, w), lambda i, b: (b, i, 0))
    const2 = lambda a: pl.BlockSpec(a.shape, lambda i, b: (0, 0), pipeline_mode=pl.Buffered(1))
    bf = lambda shape: jax.ShapeDtypeStruct(shape, BF16)
    return pl.pallas_call(
        functools.partial(_inproj_kernel, ts=ts),
        grid=(nt, nb),
        in_specs=[tok3(d),
                  pl.BlockSpec((1, 1, d), lambda i, b: (b, 0, 0)),
                  pl.BlockSpec((1, 1, d), lambda i, b: (b, 0, 0)),
                  const2(w_main), const2(w_t), const2(w_f3), const2(b_f3), const2(tri)],
        out_specs=[pl.BlockSpec((S5_TILES, ts * nb, LANES), lambda i, b: (0, i, 0)),
                   tok3(SSM_WIDTH), tok3(ATTN_WIDTH), tok3(ATTN_WIDTH),
                   tok3(D_MODEL), tok3(D_MODEL),
                   pl.BlockSpec((1, ATTN_WIDTH, ts), lambda i, b: (b, 0, i)),
                   pl.BlockSpec((1, ATTN_WIDTH, ts), lambda i, b: (b, 0, i)),
                   tok3(LANES)],
        out_shape=[jax.ShapeDtypeStruct((S5_TILES, s * nb, LANES), F32),
                   bf((nb, s, SSM_WIDTH)), bf((nb, s, ATTN_WIDTH)), bf((nb, s, ATTN_WIDTH)),
                   bf((nb, s, D_MODEL)), bf((nb, s, D_MODEL)),
                   bf((nb, ATTN_WIDTH, s)), bf((nb, ATTN_WIDTH, s)),
                   bf((nb, s, LANES))],
        scratch_shapes=[pltpu.VMEM((nb, SUBLANES, LANES), F32)],
        compiler_params=_cparams("arbitrary", "arbitrary"),
        name="inproj",
    )(x, scale, shift, w_main, w_t, w_f3, b_f3, tri)


def _gelu_tanh(x):
    return 0.5 * x * (1.0 + jnp.tanh(math.sqrt(2.0 / math.pi) * (x + 0.044715 * (x * x * x))))


def _s5_kernel(xs_ref, bmat_ref, cmat_ref, are_ref, aim_ref, d_ref, wglu_ref,
               y_ref, buf_ref, h_ref, ybuf_ref, *, steps, sub):
    @pl.when(pl.program_id(0) == 0)
    def _():
        h_ref[...] = jnp.zeros_like(h_ref)

    srows = sub * SUBLANES

    def expand(c):
        r = slice(c * srows, (c + 1) * srows)
        for j in range(S5_TILES):
            lo = j * S5_TILE_LANES
            buf_ref[r, lo:lo + S5_TILE_LANES] = jnp.dot(
                xs_ref[j, r, :].astype(BF16), bmat_ref[j], preferred_element_type=F32)

    def recur(c, state):
        new_state = []
        for j in range(S5_TILES):
            re = slice(j * S5_TILE_LANES, j * S5_TILE_LANES + S5_HALF)
            im = slice(j * S5_TILE_LANES + S5_HALF, (j + 1) * S5_TILE_LANES)
            ar = jnp.broadcast_to(are_ref[:, j * S5_HALF:(j + 1) * S5_HALF], (SUBLANES, S5_HALF))
            ai = jnp.broadcast_to(aim_ref[:, j * S5_HALF:(j + 1) * S5_HALF], (SUBLANES, S5_HALF))
            hr, hi = state[j]
            for t in range(sub):
                rt = slice(c * srows + t * SUBLANES, c * srows + (t + 1) * SUBLANES)
                nhr = ar * hr - ai * hi + buf_ref[rt, re]
                nhi = ar * hi + ai * hr + buf_ref[rt, im]
                buf_ref[rt, re] = nhr
                buf_ref[rt, im] = nhi
                hr, hi = nhr, nhi
            new_state.append((hr, hi))
        return new_state

    def project(c):
        r = slice(c * srows, (c + 1) * srows)
        ys = []
        for j in range(S5_TILES):
            lo = j * S5_TILE_LANES
            hb = buf_ref[r, lo:lo + S5_TILE_LANES].astype(BF16)
            ys.append(jnp.dot(hb, cmat_ref[j], preferred_element_type=F32)
                      + d_ref[:, j * LANES:(j + 1) * LANES] * xs_ref[j, r, :])
        g = _gelu_tanh(jnp.concatenate(ys, axis=1)).astype(BF16)
        ab = jnp.dot(g, wglu_ref[...], preferred_element_type=F32)
        out = ab[:, :SSM_WIDTH] * _sigmoid(ab[:, SSM_WIDTH:])
        for j in range(S5_TILES):
            ybuf_ref[j] = out[:, j * LANES:(j + 1) * LANES]
        for b in range(BATCH):
            yb = jnp.concatenate([ybuf_ref[j, pl.ds(b, sub, stride=SUBLANES), :]
                                  for j in range(S5_TILES)], axis=1)
            y_ref[b, c * sub:(c + 1) * sub, :] = yb.astype(BF16)

    state = [(h_ref[:, j * S5_TILE_LANES:j * S5_TILE_LANES + S5_HALF],
              h_ref[:, j * S5_TILE_LANES + S5_HALF:(j + 1) * S5_TILE_LANES])
             for j in range(S5_TILES)]
    nsub = steps // sub
    expand(0)
    for c in range(nsub):
        if c + 1 < nsub:
            expand(c + 1)
        state = recur(c, state)
        project(c)
    for j in range(S5_TILES):
        h_ref[:, j * S5_TILE_LANES:j * S5_TILE_LANES + S5_HALF] = state[j][0]
        h_ref[:, j * S5_TILE_LANES + S5_HALF:(j + 1) * S5_TILE_LANES] = state[j][1]


def _s5(xs_t, bmat, cmat, a_re, a_im, d_skip, w_glu, *, steps, sub):
    t = xs_t.shape[1]
    rows = steps * SUBLANES
    const = lambda a: pl.BlockSpec(a.shape, lambda i: (0,) * a.ndim)
    tiles = pl.BlockSpec((S5_TILES, rows, LANES), lambda i: (0, i, 0))
    return pl.pallas_call(
        functools.partial(_s5_kernel, steps=steps, sub=sub),
        grid=(t // rows,),
        in_specs=[tiles,
                  const(bmat), const(cmat), const(a_re), const(a_im), const(d_skip), const(w_glu)],
        out_specs=pl.BlockSpec((BATCH, steps, SSM_WIDTH), lambda i: (0, i, 0)),
        out_shape=jax.ShapeDtypeStruct((BATCH, t // SUBLANES, SSM_WIDTH), BF16),
        scratch_shapes=[pltpu.VMEM((rows, S5_STATE_LANES), F32),
                        pltpu.VMEM((SUBLANES, S5_STATE_LANES), F32),
                        pltpu.VMEM((S5_TILES, sub * SUBLANES, LANES), F32)],
        compiler_params=_cparams("arbitrary"),
        name="s5",
    )(xs_t, bmat, cmat, a_re, a_im, d_skip, w_glu)


def _attn_kernel(qt_ref, k_ref, caug_ref, vt_ref, sza_ref, o_ref,
                 rhs_ref, s_ref, m_ref, l_ref, acc_ref, *, tq):
    qi = pl.program_id(1)
    tk = ATTN_TK
    nd = tq // tk
    hd = ATTN_HEAD_DIM
    row = lax.broadcasted_iota(jnp.int32, (LANES, tq), 0)

    for h in range(ATTN_HEADS):
        j = h // 2
        q2 = qt_ref[0, j * LANES:(j + 1) * LANES, :]
        rhs_ref[h, 0:LANES, :] = jnp.where((row // hd) == (h % 2), q2, jnp.zeros_like(q2))
        rhs_ref[h, LANES:2 * LANES, :] = jnp.where(
            (row == h) | (row == 8 + h) | (row == 16 + h), 1.0, 0.0).astype(BF16)
    m_ref[...] = jnp.full(m_ref.shape, -jnp.inf, F32)
    l_ref[...] = jnp.zeros_like(l_ref)
    acc_ref[...] = jnp.zeros_like(acc_ref)

    def scores(k0, h, diag=None):
        j = h // 2
        lo = 0 if diag is None else diag
        lhs = jnp.concatenate([k_ref[0, pl.ds(k0, tk), j * LANES:(j + 1) * LANES],
                               caug_ref[0, pl.ds(k0, tk), :]], axis=1)
        s_ref[h % ATTN_SLOTS, :, lo:tq] = jnp.dot(lhs, rhs_ref[h, :, lo:tq],
                                                  preferred_element_type=F32)

    def update(k0, h, diag=None):
        lo = 0 if diag is None else diag
        s = s_ref[h % ATTN_SLOTS, :, lo:tq]
        if diag is not None:
            kpos = lax.broadcasted_iota(jnp.int32, s.shape, 0)
            qpos = lax.broadcasted_iota(jnp.int32, s.shape, 1)
            s = jnp.where(kpos <= qpos, s, NEG_BIG)
        hs = slice(h * SUBLANES, h * SUBLANES + 1)
        m = m_ref[hs, lo:tq]
        m_new = jnp.maximum(m, jnp.max(s, axis=0, keepdims=True))
        alpha = jnp.exp2(m - m_new)
        p = jnp.exp2(s - m_new).astype(BF16)
        m_ref[hs, lo:tq] = m_new
        vt1 = jnp.concatenate([vt_ref[0, h * hd:(h + 1) * hd, pl.ds(k0, tk)],
                               jnp.ones((2 * SUBLANES, tk), BF16)], axis=0)
        pv = jnp.dot(vt1, p, preferred_element_type=F32)
        l_ref[hs, lo:tq] = alpha * l_ref[hs, lo:tq] + pv[hd:hd + 1, :]
        acc_ref[h * hd:(h + 1) * hd, lo:tq] = (alpha * acc_ref[h * hd:(h + 1) * hd, lo:tq]
                                               + pv[:hd, :])

    def body(kj, carry):
        k0 = pl.multiple_of(kj * tk, tk)
        k1 = pl.multiple_of(kj * tk + tk, tk)
        for h in range(ATTN_HEADS):
            ha = h + ATTN_LOOKAHEAD
            if ha < ATTN_HEADS:
                scores(k0, ha)
            else:
                scores(k1, ha - ATTN_HEADS)
            update(k0, h)
        return carry

    for h in range(ATTN_LOOKAHEAD):
        scores(0, h)
    lax.fori_loop(0, qi * nd, body, 0)
    for d in range(nd):
        kd = pl.multiple_of(qi * tq + d * tk, tk)
        for h in range(ATTN_HEADS):
            ha = h + ATTN_LOOKAHEAD
            if ha < ATTN_HEADS:
                scores(kd, ha, d * tk)
            elif d + 1 < nd:
                scores(pl.multiple_of(kd + tk, tk), ha - ATTN_HEADS, (d + 1) * tk)
            update(kd, h, d * tk)

    for h in range(ATTN_HEADS):
        acc_ref[h * hd:(h + 1) * hd, :] = (acc_ref[h * hd:(h + 1) * hd, :]
                                           / l_ref[h * SUBLANES:h * SUBLANES + 1, :])
    o = acc_ref[...].T
    o_ref[0] = (o * sza_ref[0].astype(F32)).astype(BF16)


def _attn(qt, k, caug, vt, sza, *, tq):
    nb, s, w = k.shape
    return pl.pallas_call(
        functools.partial(_attn_kernel, tq=tq),
        grid=(nb, s // tq),
        in_specs=[pl.BlockSpec((1, w, tq), lambda b, i: (b, 0, i)),
                  pl.BlockSpec((1, s, w), lambda b, i: (b, 0, 0)),
                  pl.BlockSpec((1, s, LANES), lambda b, i: (b, 0, 0)),
                  pl.BlockSpec((1, w, s), lambda b, i: (b, 0, 0)),
                  pl.BlockSpec((1, tq, w), lambda b, i: (b, i, 0))],
        out_specs=pl.BlockSpec((1, tq, w), lambda b, i: (b, i, 0)),
        out_shape=jax.ShapeDtypeStruct((nb, s, w), BF16),
        scratch_shapes=[pltpu.VMEM((ATTN_HEADS, 2 * LANES, tq), BF16),
                        pltpu.VMEM((ATTN_SLOTS, ATTN_TK, tq), F32),
                        pltpu.VMEM((ATTN_HEADS * SUBLANES, tq), F32),
                        pltpu.VMEM((ATTN_HEADS * SUBLANES, tq), F32),
                        pltpu.VMEM((w, tq), F32)],
        compiler_params=_cparams("arbitrary", "arbitrary"),
        name="attn",
    )(qt, k, caug, vt, sza)


def _merge_kernel(x_ref, ys_ref, szs_ref, oz_ref, sgs_ref, sga_ref, gate_ref,
                  wps_ref, wpa_ref, wout_ref, lng_ref, lnb_ref, o_ref, *, ts):
    for c in range(ts // MERGE_SUB):
        r = slice(c * MERGE_SUB, (c + 1) * MERGE_SUB)
        a = (ys_ref[0, r, :].astype(F32) * szs_ref[0, r, :].astype(F32)).astype(BF16)
        m1 = jnp.dot(a, wps_ref[...], preferred_element_type=F32) * sgs_ref[0, r, :].astype(F32)
        m2 = (jnp.dot(oz_ref[0, r, :], wpa_ref[...], preferred_element_type=F32)
              * sga_ref[0, r, :].astype(F32))
        merged = (m1 + m2).astype(BF16)
        sub = jnp.dot(merged, wout_ref[...], preferred_element_type=F32) * gate_ref[0]
        y = DEEPNORM_ALPHA * x_ref[0, r, :] + sub
        mu = jnp.mean(y, axis=-1, keepdims=True)
        yc = y - mu
        var = jnp.mean(yc * yc, axis=-1, keepdims=True)
        o_ref[0, r, :] = yc * lax.rsqrt(var + LN_EPS) * lng_ref[...] + lnb_ref[...]


def _merge(x, ys_t, szs, oz, sgs, sga, gate, wps, wpa, wout, ln_g, ln_b, *, ts):
    nb, s, d = x.shape
    tok3 = lambda w: pl.BlockSpec((1, ts, w), lambda i, b: (b, i, 0))
    const2 = lambda a: pl.BlockSpec(a.shape, lambda i, b: (0, 0))
    return pl.pallas_call(
        functools.partial(_merge_kernel, ts=ts),
        grid=(s // ts, nb),
        in_specs=[tok3(d), tok3(SSM_WIDTH),
                  tok3(SSM_WIDTH), tok3(ATTN_WIDTH), tok3(d), tok3(d),
                  pl.BlockSpec((1, 1, d), lambda i, b: (b, 0, 0)),
                  const2(wps), const2(wpa), const2(wout), const2(ln_g), const2(ln_b)],
        out_specs=tok3(d),
        out_shape=jax.ShapeDtypeStruct((nb, s, d), x.dtype),
        compiler_params=_cparams("arbitrary", "arbitrary"),
        name="merge",
    )(x, ys_t, szs, oz, sgs, sga, gate, wps, wpa, wout, ln_g, ln_b)


def _block_diag_tiles(blocks):
    eye = jnp.eye(S5_TILE_GROUPS, dtype=blocks.dtype)
    t, g, r, c = blocks.shape
    return jnp.einsum('tgrc,gG->tgrGc', blocks, eye).reshape(t, g * r, g * c)


def kernel(x, c, w_ada, b_ada, w_in, b_f, lam_re, lam_im, log_dt, ssm_b_re, ssm_b_im,
           ssm_c_re, ssm_c_im, ssm_d, w_glu, w_proj_ssm, w_proj_attn, w_out, ln_g, ln_b):
    nb, s, d = x.shape
    l = 0
    mod = _ada(c, w_ada[l], b_ada[l])
    shift = mod[:, None, :d]
    scale = mod[:, None, d:2 * d]
    gate = mod[:, None, 2 * d:]

    w_main, w_t, w_f3 = _wprep(jnp.swapaxes(w_in[l], 0, 1))
    bfv = b_f[l].astype(F32)
    b_f3 = jnp.concatenate([bfv, bfv, bfv, jnp.zeros((LANES - 3 * ATTN_HEADS,), F32)])[None, :]

    (xs_t, szs, k, sza, sgs, sga, qt, vt, caug) = _inproj(
        x, scale, shift, w_main, w_t, w_f3, b_f3, ts=256)

    abr, abi, bbr, bbi, ncim = _zoh(lam_re[l], lam_im[l], log_dt[l], ssm_b_re[l], ssm_b_im[l],
                                    ssm_c_im[l])
    g4 = (S5_TILES, S5_TILE_GROUPS)
    bbr = bbr.reshape(g4 + (SSM_GROUP, SSM_STATE))
    bbi = bbi.reshape(g4 + (SSM_GROUP, SSM_STATE))
    bmat = jnp.concatenate([_block_diag_tiles(bbr), _block_diag_tiles(bbi)], axis=2).astype(BF16)
    cre_t = ssm_c_re[l].reshape(g4 + (SSM_GROUP, SSM_STATE)).transpose(0, 1, 3, 2)
    ncim_t = ncim.reshape(g4 + (SSM_GROUP, SSM_STATE)).transpose(0, 1, 3, 2)
    cmat = jnp.concatenate([_block_diag_tiles(cre_t), _block_diag_tiles(ncim_t)],
                           axis=1).astype(BF16)
    a_re = abr.reshape(SSM_GROUPS, SSM_GROUP, SSM_STATE)[:, 0, :].reshape(1, SSM_GROUPS * SSM_STATE)
    a_im = abi.reshape(SSM_GROUPS, SSM_GROUP, SSM_STATE)[:, 0, :].reshape(1, SSM_GROUPS * SSM_STATE)

    ys_t = _s5(xs_t, bmat, cmat, a_re, a_im,
               ssm_d[l].reshape(1, SSM_WIDTH).astype(F32), w_glu[l].astype(BF16), steps=128, sub=32)

    oz = _attn(qt, k, caug, vt, sza, tq=512)

    return _merge(x, ys_t, szs, oz, sgs, sga, gate,
                  w_proj_ssm[l].astype(BF16), w_proj_attn[l].astype(BF16), w_out[l].astype(BF16),
                  ln_g[l].reshape(1, d).astype(F32), ln_b[l].astype(F32).reshape(1, d), ts=512)
```

```python
import functools
import math

import jax
import jax.numpy as jnp
from jax import lax
from jax.experimental import pallas as pl
from jax.experimental.pallas import tpu as pltpu

D_MODEL = 1024
BATCH = 8
SEQ = 2048
SSM_WIDTH = 512
SSM_GROUP = 16
SSM_GROUPS = 32
SSM_STATE = 64
ATTN_HEADS = 8
ATTN_HEAD_DIM = 64
ATTN_WIDTH = 512
LN_EPS = 1e-5
DEEPNORM_ALPHA = 2.0 ** 0.25

F32 = jnp.float32
BF16 = jnp.bfloat16

LANES = 128
SUBLANES = 8
VMEM_LIMIT = 52 * 1024 * 1024

S5_TILES = 4
S5_TILE_GROUPS = SSM_GROUPS // S5_TILES
S5_HALF = S5_TILE_GROUPS * SSM_STATE
S5_TILE_LANES = 2 * S5_HALF
S5_TILE_IN = S5_TILE_GROUPS * SSM_GROUP
S5_STATE_LANES = S5_TILES * S5_TILE_LANES

C_XS, C_ZS, C_K, C_ZA, C_GS, C_GA, C_END = 0, 512, 1024, 1536, 2048, 3072, 4096

NEG_BIG = -1e30
LOG2E = math.log2(math.e)

MERGE_SUB = 256
ATTN_TK = 256
ATTN_LOOKAHEAD = 3
ATTN_SLOTS = 4
assert ATTN_SLOTS > ATTN_LOOKAHEAD and ATTN_HEADS % ATTN_SLOTS == 0


def _sigmoid(x):
    return 1.0 / (1.0 + jnp.exp(-x))


def _silu(x):
    return x * _sigmoid(x)


def _cparams(*sem):
    return pltpu.CompilerParams(dimension_semantics=sem, vmem_limit_bytes=VMEM_LIMIT)


def _ada_kernel(c_ref, w_ref, b_ref, o_ref):
    ca = _silu(c_ref[...]).astype(BF16)
    o_ref[...] = jnp.dot(ca, w_ref[...].astype(BF16), preferred_element_type=F32) + b_ref[...]


def _ada(c, w_ada, b_ada):
    n = w_ada.shape[1]
    tn = 512
    return pl.pallas_call(
        _ada_kernel,
        grid=(n // tn,),
        in_specs=[pl.BlockSpec((BATCH, D_MODEL), lambda j: (0, 0)),
                  pl.BlockSpec((D_MODEL, tn), lambda j: (0, j)),
                  pl.BlockSpec((1, tn), lambda j: (0, j))],
        out_specs=pl.BlockSpec((BATCH, tn), lambda j: (0, j)),
        out_shape=jax.ShapeDtypeStruct((BATCH, n), F32),
        compiler_params=_cparams("arbitrary"),
        name="ada",
    )(c, w_ada, b_ada.reshape(1, n))


def _zoh_kernel(lr_ref, li_ref, ldt_ref, br_ref, bi_ref, cim_ref,
                abr_ref, abi_ref, bbr_ref, bbi_ref, ncim_ref):
    lr = lr_ref[...]
    li = li_ref[...]
    dt = jnp.exp(ldt_ref[...])
    mag = jnp.exp(lr * dt)
    abr = mag * jnp.cos(li * dt)
    abi = mag * jnp.sin(li * dt)
    den = lr * lr + li * li
    cr = ((abr - 1.0) * lr + abi * li) / den
    ci = (abi * lr - (abr - 1.0) * li) / den
    br = br_ref[...]
    bi = bi_ref[...]
    abr_ref[...] = abr
    abi_ref[...] = abi
    bbr_ref[...] = cr * br - ci * bi
    bbi_ref[...] = cr * bi + ci * br
    ncim_ref[...] = -cim_ref[...]


def _zoh(lam_re, lam_im, log_dt, b_re, b_im, c_im):
    rows = SSM_GROUPS * SSM_GROUP
    rep = lambda a: jnp.repeat(a, SSM_GROUP, axis=0)
    lr = rep(lam_re)
    li = rep(lam_im)
    ldt = jnp.broadcast_to(rep(log_dt[:, None]), (rows, SSM_STATE))
    brt = b_re.transpose(0, 2, 1).reshape(rows, SSM_STATE)
    bit = b_im.transpose(0, 2, 1).reshape(rows, SSM_STATE)
    cim = c_im.reshape(rows, SSM_STATE)
    shp = jax.ShapeDtypeStruct((rows, SSM_STATE), F32)
    return pl.pallas_call(
        _zoh_kernel,
        out_shape=(shp,) * 5,
        name="zoh",
    )(lr, li, ldt, brt, bit, cim)


O_XS, O_ZS, O_Q, O_K, O_V, O_F, O_ZA, O_GS, O_GA, O_END = (
    0, 512, 1024, 1536, 2048, 2560, 2568, 3080, 4104, 5128)


WPREP_CHUNK = 256


def _wprep_kernel(wt_ref, wm_ref, wqv_ref, wf_ref):
    def put(c_lo, o_lo, n):
        for s in range(0, n, WPREP_CHUNK):
            wm_ref[:, c_lo + s:c_lo + s + WPREP_CHUNK] = (
                wt_ref[o_lo + s:o_lo + s + WPREP_CHUNK, :].T.astype(BF16))

    put(C_XS, O_XS, O_Q - O_XS)
    put(C_K, O_K, O_V - O_K)
    put(C_ZA, O_ZA, O_GS - O_ZA)
    put(C_GS, O_GS, O_END - O_GS)
    wqv_ref[0:ATTN_WIDTH, :] = wt_ref[O_Q:O_K, :].astype(BF16)
    wqv_ref[ATTN_WIDTH:2 * ATTN_WIDTH, :] = wt_ref[O_V:O_F, :].astype(BF16)
    f8 = wt_ref[O_F:O_ZA, :]
    blk = jnp.concatenate([f8, f8, f8, jnp.zeros((LANES - 3 * ATTN_HEADS, f8.shape[1]), F32)], axis=0)
    wf_ref[...] = blk.T.astype(BF16)


def _wprep(wt):
    n, d = wt.shape
    return pl.pallas_call(
        _wprep_kernel,
        out_shape=[jax.ShapeDtypeStruct((d, C_END), BF16),
                   jax.ShapeDtypeStruct((2 * ATTN_WIDTH, d), BF16),
                   jax.ShapeDtypeStruct((d, LANES), BF16)],
        compiler_params=pltpu.CompilerParams(vmem_limit_bytes=VMEM_LIMIT),
        name="wprep",
    )(wt)


def _inproj_kernel(x_ref, scale_ref, shift_ref, wm_ref, wt_ref, wf_ref, bf_ref, tri_ref,
                   xs_ref, szs_ref, k_ref, sza_ref, sgs_ref, sga_ref, qt_ref, vt_ref, caug_ref,
                   carry_ref, *, ts):
    i = pl.program_id(0)
    b = pl.program_id(1)

    @pl.when(i == 0)
    def _():
        carry_ref[b] = jnp.zeros(carry_ref.shape[1:], F32)

    x = x_ref[0]
    mu = jnp.mean(x, axis=-1, keepdims=True)
    xc = x - mu
    var = jnp.mean(xc * xc, axis=-1, keepdims=True)
    xn = xc * lax.rsqrt(var + LN_EPS)
    u = xn * (1.0 + scale_ref[0]) + shift_ref[0]
    ub = u.astype(BF16)

    def mm(lo, hi):
        return jnp.dot(ub, wm_ref[:, lo:hi], preferred_element_type=F32)

    f = jnp.dot(ub, wf_ref[...], preferred_element_type=F32) + bf_ref[...]
    lf = jnp.minimum(f, 0.0) - jnp.log(1.0 + jnp.exp(-jnp.abs(f)))
    l1 = lf.astype(BF16)
    r1 = lf - l1.astype(F32)
    l2 = r1.astype(BF16)
    l3 = (r1 - l2.astype(F32)).astype(BF16)

    xs = mm(C_XS, C_ZS)
    for j in range(S5_TILES):
        xs_ref[j, pl.ds(b, ts, stride=SUBLANES), :] = xs[:, j * LANES:(j + 1) * LANES]
    szs_ref[0] = _silu(mm(C_ZS, C_K)).astype(BF16)

    tri = tri_ref[...]
    cf = (jnp.dot(tri, l1, preferred_element_type=F32)
          + jnp.dot(tri, l2, preferred_element_type=F32)
          + jnp.dot(tri, l3, preferred_element_type=F32)) + carry_ref[b][0:1, :]
    carry_ref[b] = jnp.broadcast_to(cf[ts - 1:ts, :], carry_ref.shape[1:])

    k_ref[0] = mm(C_K, C_ZA).astype(BF16)
    sza_ref[0] = _silu(mm(C_ZA, C_GS)).astype(BF16)

    cf2 = cf * LOG2E
    c1 = cf2.astype(BF16).astype(F32)
    r = cf2 - c1
    c2 = r.astype(BF16).astype(F32)
    c3 = r - c2
    lane = lax.broadcasted_iota(jnp.int32, cf.shape, 1)
    parts = jnp.where(lane < 8, c1, jnp.where(lane < 16, c2, jnp.where(lane < 24, c3, 0.0)))
    caug_ref[0] = (-parts).astype(BF16)

    sgs_ref[0] = _sigmoid(mm(C_GS, C_GA)).astype(BF16)
    sga_ref[0] = _sigmoid(mm(C_GA, C_END)).astype(BF16)

    qv = lax.dot_general(wt_ref[...], ub, (((1,), (1,)), ((), ())),
                         preferred_element_type=F32)
    qt_ref[0] = (qv[:ATTN_WIDTH] * (LOG2E * ATTN_HEAD_DIM ** -0.5)).astype(BF16)
    vt_ref[0] = qv[ATTN_WIDTH:].astype(BF16)


def _inproj(x, scale, shift, w_main, w_t, w_f3, b_f3, *, ts):
    nb, s, d = x.shape
    nt = s // ts
    tri = jnp.tril(jnp.ones((ts, ts), BF16))
    tok3 = lambda w: pl.BlockSpec((1, ts, w), lambda i, b: (b, i, 0))
    const2 = lambda a: pl.BlockSpec(a.shape, lambda i, b: (0, 0), pipeline_mode=pl.Buffered(1))
    bf = lambda shape: jax.ShapeDtypeStruct(shape, BF16)
    return pl.pallas_call(
        functools.partial(_inproj_kernel, ts=ts),
        grid=(nt, nb),
        in_specs=[tok3(d),
                  pl.BlockSpec((1, 1, d), lambda i, b: (b, 0, 0)),
                  pl.BlockSpec((1, 1, d), lambda i, b: (b, 0, 0)),
                  const2(w_main), const2(w_t), const2(w_f3), const2(b_f3), const2(tri)],
        out_specs=[pl.BlockSpec((S5_TILES, ts * nb, LANES), lambda i, b: (0, i, 0)),
                   tok3(SSM_WIDTH), tok3(ATTN_WIDTH), tok3(ATTN_WIDTH),
                   tok3(D_MODEL), tok3(D_MODEL),
                   pl.BlockSpec((1, ATTN_WIDTH, ts), lambda i, b: (b, 0, i)),
                   pl.BlockSpec((1, ATTN_WIDTH, ts), lambda i, b: (b, 0, i)),
                   tok3(LANES)],
        out_shape=[jax.ShapeDtypeStruct((S5_TILES, s * nb, LANES), F32),
                   bf((nb, s, SSM_WIDTH)), bf((nb, s, ATTN_WIDTH)), bf((nb, s, ATTN_WIDTH)),
                   bf((nb, s, D_MODEL)), bf((nb, s, D_MODEL)),
                   bf((nb, ATTN_WIDTH, s)), bf((nb, ATTN_WIDTH, s)),
                   bf((nb, s, LANES))],
        scratch_shapes=[pltpu.VMEM((nb, SUBLANES, LANES), F32)],
        compiler_params=_cparams("arbitrary", "arbitrary"),
        name="inproj",
    )(x, scale, shift, w_main, w_t, w_f3, b_f3, tri)


def _gelu_tanh(x):
    return 0.5 * x * (1.0 + jnp.tanh(math.sqrt(2.0 / math.pi) * (x + 0.044715 * (x * x * x))))


def _s5_kernel(xs_ref, bmat_ref, cmat_ref, are_ref, aim_ref, d_ref, wglu_ref,
               y_ref, buf_ref, h_ref, ybuf_ref, *, steps, sub):
    @pl.when(pl.program_id(0) == 0)
    def _():
        h_ref[...] = jnp.zeros_like(h_ref)

    srows = sub * SUBLANES

    def expand(c):
        r = slice(c * srows, (c + 1) * srows)
        for j in range(S5_TILES):
            lo = j * S5_TILE_LANES
            buf_ref[r, lo:lo + S5_TILE_LANES] = jnp.dot(
                xs_ref[j, r, :].astype(BF16), bmat_ref[j], preferred_element_type=F32)

    def recur(c, state):
        new_state = []
        for j in range(S5_TILES):
            re = slice(j * S5_TILE_LANES, j * S5_TILE_LANES + S5_HALF)
            im = slice(j * S5_TILE_LANES + S5_HALF, (j + 1) * S5_TILE_LANES)
            ar = jnp.broadcast_to(are_ref[:, j * S5_HALF:(j + 1) * S5_HALF], (SUBLANES, S5_HALF))
            ai = jnp.broadcast_to(aim_ref[:, j * S5_HALF:(j + 1) * S5_HALF], (SUBLANES, S5_HALF))
            hr, hi = state[j]
            for t in range(sub):
                rt = slice(c * srows + t * SUBLANES, c * srows + (t + 1) * SUBLANES)
                nhr = ar * hr - ai * hi + buf_ref[rt, re]
                nhi = ar * hi + ai * hr + buf_ref[rt, im]
                buf_ref[rt, re] = nhr
                buf_ref[rt, im] = nhi
                hr, hi = nhr, nhi
            new_state.append((hr, hi))
        return new_state

    def project(c):
        r = slice(c * srows, (c + 1) * srows)
        ys = []
        for j in range(S5_TILES):
            lo = j * S5_TILE_LANES
            hb = buf_ref[r, lo:lo + S5_TILE_LANES].astype(BF16)
            ys.append(jnp.dot(hb, cmat_ref[j], preferred_element_type=F32)
                      + d_ref[:, j * LANES:(j + 1) * LANES] * xs_ref[j, r, :])
        g = _gelu_tanh(jnp.concatenate(ys, axis=1)).astype(BF16)
        ab = jnp.dot(g, wglu_ref[...], preferred_element_type=F32)
        out = ab[:, :SSM_WIDTH] * _sigmoid(ab[:, SSM_WIDTH:])
        for j in range(S5_TILES):
            ybuf_ref[j] = out[:, j * LANES:(j + 1) * LANES]
        for b in range(BATCH):
            yb = jnp.concatenate([ybuf_ref[j, pl.ds(b, sub, stride=SUBLANES), :]
                                  for j in range(S5_TILES)], axis=1)
            y_ref[b, c * sub:(c + 1) * sub, :] = yb.astype(BF16)

    state = [(h_ref[:, j * S5_TILE_LANES:j * S5_TILE_LANES + S5_HALF],
              h_ref[:, j * S5_TILE_LANES + S5_HALF:(j + 1) * S5_TILE_LANES])
             for j in range(S5_TILES)]
    nsub = steps // sub
    expand(0)
    for c in range(nsub):
        if c + 1 < nsub:
            expand(c + 1)
        state = recur(c, state)
        project(c)
    for j in range(S5_TILES):
        h_ref[:, j * S5_TILE_LANES:j * S5_TILE_LANES + S5_HALF] = state[j][0]
        h_ref[:, j * S5_TILE_LANES + S5_HALF:(j + 1) * S5_TILE_LANES] = state[j][1]


def _s5(xs_t, bmat, cmat, a_re, a_im, d_skip, w_glu, *, steps, sub):
    t = xs_t.shape[1]
    rows = steps * SUBLANES
    const = lambda a: pl.BlockSpec(a.shape, lambda i: (0,) * a.ndim)
    tiles = pl.BlockSpec((S5_TILES, rows, LANES), lambda i: (0, i, 0))
    return pl.pallas_call(
        functools.partial(_s5_kernel, steps=steps, sub=sub),
        grid=(t // rows,),
        in_specs=[tiles,
                  const(bmat), const(cmat), const(a_re), const(a_im), const(d_skip), const(w_glu)],
        out_specs=pl.BlockSpec((BATCH, steps, SSM_WIDTH), lambda i: (0, i, 0)),
        out_shape=jax.ShapeDtypeStruct((BATCH, t // SUBLANES, SSM_WIDTH), BF16),
        scratch_shapes=[pltpu.VMEM((rows, S5_STATE_LANES), F32),
                        pltpu.VMEM((SUBLANES, S5_STATE_LANES), F32),
                        pltpu.VMEM((S5_TILES, sub * SUBLANES, LANES), F32)],
        compiler_params=_cparams("arbitrary"),
        name="s5",
    )(xs_t, bmat, cmat, a_re, a_im, d_skip, w_glu)


def _attn_kernel(qt_ref, k_ref, caug_ref, vt_ref, sza_ref, o_ref,
                 rhs_ref, s_ref, m_ref, l_ref, acc_ref, *, tq):
    qi = pl.program_id(1)
    tk = ATTN_TK
    nd = tq // tk
    hd = ATTN_HEAD_DIM
    row = lax.broadcasted_iota(jnp.int32, (LANES, tq), 0)

    for h in range(ATTN_HEADS):
        j = h // 2
        q2 = qt_ref[0, j * LANES:(j + 1) * LANES, :]
        rhs_ref[h, 0:LANES, :] = jnp.where((row // hd) == (h % 2), q2, jnp.zeros_like(q2))
        rhs_ref[h, LANES:2 * LANES, :] = jnp.where(
            (row == h) | (row == 8 + h) | (row == 16 + h), 1.0, 0.0).astype(BF16)
    m_ref[...] = jnp.full(m_ref.shape, -jnp.inf, F32)
    l_ref[...] = jnp.zeros_like(l_ref)
    acc_ref[...] = jnp.zeros_like(acc_ref)

    def scores(k0, h, diag=None):
        j = h // 2
        lo = 0 if diag is None else diag
        lhs = jnp.concatenate([k_ref[0, pl.ds(k0, tk), j * LANES:(j + 1) * LANES],
                               caug_ref[0, pl.ds(k0, tk), :]], axis=1)
        s_ref[h % ATTN_SLOTS, :, lo:tq] = jnp.dot(lhs, rhs_ref[h, :, lo:tq],
                                                  preferred_element_type=F32)

    def update(k0, h, diag=None):
        lo = 0 if diag is None else diag
        s = s_ref[h % ATTN_SLOTS, :, lo:tq]
        if diag is not None:
            kpos = lax.broadcasted_iota(jnp.int32, s.shape, 0)
            qpos = lax.broadcasted_iota(jnp.int32, s.shape, 1)
            s = jnp.where(kpos <= qpos, s, NEG_BIG)
        hs = slice(h * SUBLANES, h * SUBLANES + 1)
        m = m_ref[hs, lo:tq]
        m_new = jnp.maximum(m, jnp.max(s, axis=0, keepdims=True))
        alpha = jnp.exp2(m - m_new)
        p = jnp.exp2(s - m_new).astype(BF16)
        m_ref[hs, lo:tq] = m_new
        vt1 = jnp.concatenate([vt_ref[0, h * hd:(h + 1) * hd, pl.ds(k0, tk)],
                               jnp.ones((2 * SUBLANES, tk), BF16)], axis=0)
        pv = jnp.dot(vt1, p, preferred_element_type=F32)
        l_ref[hs, lo:tq] = alpha * l_ref[hs, lo:tq] + pv[hd:hd + 1, :]
        acc_ref[h * hd:(h + 1) * hd, lo:tq] = (alpha * acc_ref[h * hd:(h + 1) * hd, lo:tq]
                                               + pv[:hd, :])

    def body(kj, carry):
        k0 = pl.multiple_of(kj * tk, tk)
        k1 = pl.multiple_of(kj * tk + tk, tk)
        for h in range(ATTN_HEADS):
            ha = h + ATTN_LOOKAHEAD
            if ha < ATTN_HEADS:
                scores(k0, ha)
            else:
                scores(k1, ha - ATTN_HEADS)
            update(k0, h)
        return carry

    for h in range(ATTN_LOOKAHEAD):
        scores(0, h)
    lax.fori_loop(0, qi * nd, body, 0)
    for d in range(nd):
        kd = pl.multiple_of(qi * tq + d * tk, tk)
        for h in range(ATTN_HEADS):
            ha = h + ATTN_LOOKAHEAD
            if ha < ATTN_HEADS:
                scores(kd, ha, d * tk)
            elif d + 1 < nd:
                scores(pl.multiple_of(kd + tk, tk), ha - ATTN_HEADS, (d + 1) * tk)
            update(kd, h, d * tk)

    for h in range(ATTN_HEADS):
        acc_ref[h * hd:(h + 1) * hd, :] = (acc_ref[h * hd:(h + 1) * hd, :]
                                           / l_ref[h * SUBLANES:h * SUBLANES + 1, :])
    o = acc_ref[...].T
    o_ref[0] = (o * sza_ref[0].astype(F32)).astype(BF16)


def _attn(qt, k, caug, vt, sza, *, tq):
    nb, s, w = k.shape
    return pl.pallas_call(
        functools.partial(_attn_kernel, tq=tq),
        grid=(nb, s // tq),
        in_specs=[pl.BlockSpec((1, w, tq), lambda b, i: (b, 0, i)),
                  pl.BlockSpec((1, s, w), lambda b, i: (b, 0, 0)),
                  pl.BlockSpec((1, s, LANES), lambda b, i: (b, 0, 0)),
                  pl.BlockSpec((1, w, s), lambda b, i: (b, 0, 0)),
                  pl.BlockSpec((1, tq, w), lambda b, i: (b, i, 0))],
        out_specs=pl.BlockSpec((1, tq, w), lambda b, i: (b, i, 0)),
        out_shape=jax.ShapeDtypeStruct((nb, s, w), BF16),
        scratch_shapes=[pltpu.VMEM((ATTN_HEADS, 2 * LANES, tq), BF16),
                        pltpu.VMEM((ATTN_SLOTS, ATTN_TK, tq), F32),
                        pltpu.VMEM((ATTN_HEADS * SUBLANES, tq), F32),
                        pltpu.VMEM((ATTN_HEADS * SUBLANES, tq), F32),
                        pltpu.VMEM((w, tq), F32)],
        compiler_params=_cparams("arbitrary", "arbitrary"),
        name="attn",
    )(qt, k, caug, vt, sza)


def _merge_kernel(x_ref, ys_ref, szs_ref, oz_ref, sgs_ref, sga_ref, gate_ref,
                  wps_ref, wpa_ref, wout_ref, lng_ref, lnb_ref, o_ref, *, ts):
    for c in range(ts // MERGE_SUB):
        r = slice(c * MERGE_SUB, (c + 1) * MERGE_SUB)
        a = (ys_ref[0, r, :].astype(F32) * szs_ref[0, r, :].astype(F32)).astype(BF16)
        m1 = jnp.dot(a, wps_ref[...], preferred_element_type=F32) * sgs_ref[0, r, :].astype(F32)
        m2 = (jnp.dot(oz_ref[0, r, :], wpa_ref[...], preferred_element_type=F32)
              * sga_ref[0, r, :].astype(F32))
        merged = (m1 + m2).astype(BF16)
        sub = jnp.dot(merged, wout_ref[...], preferred_element_type=F32) * gate_ref[0]
        y = DEEPNORM_ALPHA * x_ref[0, r, :] + sub
        mu = jnp.mean(y, axis=-1, keepdims=True)
        yc = y - mu
        var = jnp.mean(yc * yc, axis=-1, keepdims=True)
        o_ref[0, r, :] = yc * lax.rsqrt(var + LN_EPS) * lng_ref[...] + lnb_ref[...]


def _merge(x, ys_t, szs, oz, sgs, sga, gate, wps, wpa, wout, ln_g, ln_b, *, ts):
    nb, s, d = x.shape
    tok3 = lambda w: pl.BlockSpec((1, ts, w), lambda i, b: (b, i, 0))
    const2 = lambda a: pl.BlockSpec(a.shape, lambda i, b: (0, 0))
    return pl.pallas_call(
        functools.partial(_merge_kernel, ts=ts),
        grid=(s // ts, nb),
        in_specs=[tok3(d), tok3(SSM_WIDTH),
                  tok3(SSM_WIDTH), tok3(ATTN_WIDTH), tok3(d), tok3(d),
                  pl.BlockSpec((1, 1, d), lambda i, b: (b, 0, 0)),
                  const2(wps), const2(wpa), const2(wout), const2(ln_g), const2(ln_b)],
        out_specs=tok3(d),
        out_shape=jax.ShapeDtypeStruct((nb, s, d), x.dtype),
        compiler_params=_cparams("arbitrary", "arbitrary"),
        name="merge",
    )(x, ys_t, szs, oz, sgs, sga, gate, wps, wpa, wout, ln_g, ln_b)


def _block_diag_tiles(blocks):
    eye = jnp.eye(S5_TILE_GROUPS, dtype=blocks.dtype)
    t, g, r, c = blocks.shape
    return jnp.einsum('tgrc,gG->tgrGc', blocks, eye).reshape(t, g * r, g * c)


def kernel(x, c, w_ada, b_ada, w_in, b_f, lam_re, lam_im, log_dt, ssm_b_re, ssm_b_im,
           ssm_c_re, ssm_c_im, ssm_d, w_glu, w_proj_ssm, w_proj_attn, w_out, ln_g, ln_b):
    nb, s, d = x.shape
    l = 0
    mod = _ada(c, w_ada[l], b_ada[l])
    shift = mod[:, None, :d]
    scale = mod[:, None, d:2 * d]
    gate = mod[:, None, 2 * d:]

    w_main, w_t, w_f3 = _wprep(jnp.swapaxes(w_in[l], 0, 1))
    bfv = b_f[l].astype(F32)
    b_f3 = jnp.concatenate([bfv, bfv, bfv, jnp.zeros((LANES - 3 * ATTN_HEADS,), F32)])[None, :]

    (xs_t, szs, k, sza, sgs, sga, qt, vt, caug) = _inproj(
        x, scale, shift, w_main, w_t, w_f3, b_f3, ts=256)

    abr, abi, bbr, bbi, ncim = _zoh(lam_re[l], lam_im[l], log_dt[l], ssm_b_re[l], ssm_b_im[l],
                                    ssm_c_im[l])
    g4 = (S5_TILES, S5_TILE_GROUPS)
    bbr = bbr.reshape(g4 + (SSM_GROUP, SSM_STATE))
    bbi = bbi.reshape(g4 + (SSM_GROUP, SSM_STATE))
    bmat = jnp.concatenate([_block_diag_tiles(bbr), _block_diag_tiles(bbi)], axis=2).astype(BF16)
    cre_t = ssm_c_re[l].reshape(g4 + (SSM_GROUP, SSM_STATE)).transpose(0, 1, 3, 2)
    ncim_t = ncim.reshape(g4 + (SSM_GROUP, SSM_STATE)).transpose(0, 1, 3, 2)
    cmat = jnp.concatenate([_block_diag_tiles(cre_t), _block_diag_tiles(ncim_t)],
                           axis=1).astype(BF16)
    a_re = abr.reshape(SSM_GROUPS, SSM_GROUP, SSM_STATE)[:, 0, :].reshape(1, SSM_GROUPS * SSM_STATE)
    a_im = abi.reshape(SSM_GROUPS, SSM_GROUP, SSM_STATE)[:, 0, :].reshape(1, SSM_GROUPS * SSM_STATE)

    ys_t = _s5(xs_t, bmat, cmat, a_re, a_im,
               ssm_d[l].reshape(1, SSM_WIDTH).astype(F32), w_glu[l].astype(BF16), steps=128, sub=32)

    oz = _attn(qt, k, caug, vt, sza, tq=1024)

    return _merge(x, ys_t, szs, oz, sgs, sga, gate,
                  w_proj_ssm[l].astype(BF16), w_proj_attn[l].astype(BF16), w_out[l].astype(BF16),
                  ln_g[l].reshape(1, d).astype(F32), ln_b[l].astype(F32).reshape(1, d), ts=512)
```

```python
import functools
import math

import jax
import jax.numpy as jnp
from jax import lax
from jax.experimental import pallas as pl
from jax.experimental.pallas import tpu as pltpu

D_MODEL = 1024
BATCH = 8
SEQ = 2048
SSM_WIDTH = 512
SSM_GROUP = 16
SSM_GROUPS = 32
SSM_STATE = 64
ATTN_HEADS = 8
ATTN_HEAD_DIM = 64
ATTN_WIDTH = 512
LN_EPS = 1e-5
DEEPNORM_ALPHA = 2.0 ** 0.25

F32 = jnp.float32
BF16 = jnp.bfloat16

LANES = 128
SUBLANES = 8
VMEM_LIMIT = 52 * 1024 * 1024

S5_TILES = 4
S5_TILE_GROUPS = SSM_GROUPS // S5_TILES
S5_HALF = S5_TILE_GROUPS * SSM_STATE
S5_TILE_LANES = 2 * S5_HALF
S5_TILE_IN = S5_TILE_GROUPS * SSM_GROUP
S5_STATE_LANES = S5_TILES * S5_TILE_LANES

C_XS, C_ZS, C_K, C_ZA, C_GS, C_GA, C_END = 0, 512, 1024, 1536, 2048, 3072, 4096

NEG_BIG = -1e30
LOG2E = math.log2(math.e)

MERGE_SUB = 256
ATTN_TK = 256
ATTN_LOOKAHEAD = 3
ATTN_SLOTS = 4
assert ATTN_SLOTS > ATTN_LOOKAHEAD and ATTN_HEADS % ATTN_SLOTS == 0


def _sigmoid(x):
    return 1.0 / (1.0 + jnp.exp(-x))


def _silu(x):
    return x * _sigmoid(x)


def _cparams(*sem):
    return pltpu.CompilerParams(dimension_semantics=sem, vmem_limit_bytes=VMEM_LIMIT)


def _ada_kernel(c_ref, w_ref, b_ref, o_ref):
    ca = _silu(c_ref[...]).astype(BF16)
    o_ref[...] = jnp.dot(ca, w_ref[...].astype(BF16), preferred_element_type=F32) + b_ref[...]


def _ada(c, w_ada, b_ada):
    n = w_ada.shape[1]
    tn = 512
    return pl.pallas_call(
        _ada_kernel,
        grid=(n // tn,),
        in_specs=[pl.BlockSpec((BATCH, D_MODEL), lambda j: (0, 0)),
                  pl.BlockSpec((D_MODEL, tn), lambda j: (0, j)),
                  pl.BlockSpec((1, tn), lambda j: (0, j))],
        out_specs=pl.BlockSpec((BATCH, tn), lambda j: (0, j)),
        out_shape=jax.ShapeDtypeStruct((BATCH, n), F32),
        compiler_params=_cparams("arbitrary"),
        name="ada",
    )(c, w_ada, b_ada.reshape(1, n))


def _zoh_kernel(lr_ref, li_ref, ldt_ref, br_ref, bi_ref, cim_ref,
                abr_ref, abi_ref, bbr_ref, bbi_ref, ncim_ref):
    lr = lr_ref[...]
    li = li_ref[...]
    dt = jnp.exp(ldt_ref[...])
    mag = jnp.exp(lr * dt)
    abr = mag * jnp.cos(li * dt)
    abi = mag * jnp.sin(li * dt)
    den = lr * lr + li * li
    cr = ((abr - 1.0) * lr + abi * li) / den
    ci = (abi * lr - (abr - 1.0) * li) / den
    br = br_ref[...]
    bi = bi_ref[...]
    abr_ref[...] = abr
    abi_ref[...] = abi
    bbr_ref[...] = cr * br - ci * bi
    bbi_ref[...] = cr * bi + ci * br
    ncim_ref[...] = -cim_ref[...]


def _zoh(lam_re, lam_im, log_dt, b_re, b_im, c_im):
    rows = SSM_GROUPS * SSM_GROUP
    rep = lambda a: jnp.repeat(a, SSM_GROUP, axis=0)
    lr = rep(lam_re)
    li = rep(lam_im)
    ldt = jnp.broadcast_to(rep(log_dt[:, None]), (rows, SSM_STATE))
    brt = b_re.transpose(0, 2, 1).reshape(rows, SSM_STATE)
    bit = b_im.transpose(0, 2, 1).reshape(rows, SSM_STATE)
    cim = c_im.reshape(rows, SSM_STATE)
    shp = jax.ShapeDtypeStruct((rows, SSM_STATE), F32)
    return pl.pallas_call(
        _zoh_kernel,
        out_shape=(shp,) * 5,
        name="zoh",
    )(lr, li, ldt, brt, bit, cim)


O_XS, O_ZS, O_Q, O_K, O_V, O_F, O_ZA, O_GS, O_GA, O_END = (
    0, 512, 1024, 1536, 2048, 2560, 2568, 3080, 4104, 5128)


WPREP_CHUNK = 256


def _wprep_kernel(wt_ref, wm_ref, wqv_ref, wf_ref):
    def put(c_lo, o_lo, n):
        for s in range(0, n, WPREP_CHUNK):
            wm_ref[:, c_lo + s:c_lo + s + WPREP_CHUNK] = (
                wt_ref[o_lo + s:o_lo + s + WPREP_CHUNK, :].T.astype(BF16))

    put(C_XS, O_XS, O_Q - O_XS)
    put(C_K, O_K, O_V - O_K)
    put(C_ZA, O_ZA, O_GS - O_ZA)
    put(C_GS, O_GS, O_END - O_GS)
    wqv_ref[0:ATTN_WIDTH, :] = wt_ref[O_Q:O_K, :].astype(BF16)
    wqv_ref[ATTN_WIDTH:2 * ATTN_WIDTH, :] = wt_ref[O_V:O_F, :].astype(BF16)
    f8 = wt_ref[O_F:O_ZA, :]
    blk = jnp.concatenate([f8, f8, f8, jnp.zeros((LANES - 3 * ATTN_HEADS, f8.shape[1]), F32)], axis=0)
    wf_ref[...] = blk.T.astype(BF16)


def _wprep(wt):
    n, d = wt.shape
    return pl.pallas_call(
        _wprep_kernel,
        out_shape=[jax.ShapeDtypeStruct((d, C_END), BF16),
                   jax.ShapeDtypeStruct((2 * ATTN_WIDTH, d), BF16),
                   jax.ShapeDtypeStruct((d, LANES), BF16)],
        compiler_params=pltpu.CompilerParams(vmem_limit_bytes=VMEM_LIMIT),
        name="wprep",
    )(wt)


def _inproj_kernel(x_ref, scale_ref, shift_ref, wm_ref, wt_ref, wf_ref, bf_ref, tri_ref,
                   xs_ref, szs_ref, k_ref, sza_ref, sgs_ref, sga_ref, qt_ref, vt_ref, caug_ref,
                   carry_ref, u_ref, *, ts, nb, nblk):
    n = pl.program_id(0)
    blk_a = jnp.minimum(n, nblk - 1)
    b_a = blk_a % nb
    b = jnp.maximum(n - 1, 0) % nb

    @pl.when(n == 0)
    def _():
        u_ref[0] = jnp.zeros(u_ref.shape[1:], BF16)

    @pl.when(blk_a // nb == 0)
    def _():
        carry_ref[b_a] = jnp.zeros(carry_ref.shape[1:], F32)

    ub = u_ref[0]

    def mm(lo, hi):
        return jnp.dot(ub, wm_ref[:, lo:hi], preferred_element_type=F32)

    xs = mm(C_XS, C_ZS)
    for j in range(S5_TILES):
        xs_ref[j, pl.ds(b, ts, stride=SUBLANES), :] = xs[:, j * LANES:(j + 1) * LANES]
    szs_ref[0] = _silu(mm(C_ZS, C_K)).astype(BF16)

    x = x_ref[0]
    mu = jnp.mean(x, axis=-1, keepdims=True)
    xc = x - mu
    var = jnp.mean(xc * xc, axis=-1, keepdims=True)
    xn = xc * lax.rsqrt(var + LN_EPS)
    un = (xn * (1.0 + scale_ref[0]) + shift_ref[0]).astype(BF16)
    u_ref[1] = un
    f = jnp.dot(un, wf_ref[...], preferred_element_type=F32) + bf_ref[...]
    lf = jnp.minimum(f, 0.0) - jnp.log(1.0 + jnp.exp(-jnp.abs(f)))
    l1 = lf.astype(BF16)
    r1 = lf - l1.astype(F32)
    l2 = r1.astype(BF16)
    l3 = (r1 - l2.astype(F32)).astype(BF16)

    k_ref[0] = mm(C_K, C_ZA).astype(BF16)
    sza_ref[0] = _silu(mm(C_ZA, C_GS)).astype(BF16)

    tri = tri_ref[...]
    cf = (jnp.dot(tri, l1, preferred_element_type=F32)
          + jnp.dot(tri, l2, preferred_element_type=F32)
          + jnp.dot(tri, l3, preferred_element_type=F32)) + carry_ref[b_a][0:1, :]
    carry_ref[b_a] = jnp.broadcast_to(cf[ts - 1:ts, :], carry_ref.shape[1:])
    cf2 = cf * LOG2E
    c1 = cf2.astype(BF16).astype(F32)
    r = cf2 - c1
    c2 = r.astype(BF16).astype(F32)
    c3 = r - c2
    lane = lax.broadcasted_iota(jnp.int32, cf.shape, 1)
    parts = jnp.where(lane < 8, c1, jnp.where(lane < 16, c2, jnp.where(lane < 24, c3, 0.0)))
    caug_ref[0] = (-parts).astype(BF16)

    sgs_ref[0] = _sigmoid(mm(C_GS, C_GA)).astype(BF16)
    sga_ref[0] = _sigmoid(mm(C_GA, C_END)).astype(BF16)

    qv = lax.dot_general(wt_ref[...], ub, (((1,), (1,)), ((), ())),
                         preferred_element_type=F32)
    qt_ref[0] = (qv[:ATTN_WIDTH] * (LOG2E * ATTN_HEAD_DIM ** -0.5)).astype(BF16)
    vt_ref[0] = qv[ATTN_WIDTH:].astype(BF16)

    u_ref[0] = u_ref[1]


def _inproj(x, scale, shift, w_main, w_t, w_f3, b_f3, *, ts):
    nb, s, d = x.shape
    nt = s // ts
    tri = jnp.tril(jnp.ones((ts, ts), BF16))
    nblk = nt * nb
    stg_a = lambda n: (jnp.minimum(n, nblk - 1) % nb, jnp.minimum(n, nblk - 1) // nb)
    stg_b = lambda n: (jnp.maximum(n - 1, 0) % nb, jnp.maximum(n - 1, 0) // nb)
    tok3 = lambda w: pl.BlockSpec((1, ts, w), lambda n: (stg_b(n)[0], stg_b(n)[1], 0))
    mod3 = pl.BlockSpec((1, 1, d), lambda n: (stg_a(n)[0], 0, 0))
    tr3 = pl.BlockSpec((1, ATTN_WIDTH, ts), lambda n: (stg_b(n)[0], 0, stg_b(n)[1]))
    caug3 = pl.BlockSpec((1, ts, LANES), lambda n: (stg_a(n)[0], stg_a(n)[1] + n // nblk, 0))
    const2 = lambda a: pl.BlockSpec(a.shape, lambda n: (0, 0), pipeline_mode=pl.Buffered(1))
    bf = lambda shape: jax.ShapeDtypeStruct(shape, BF16)
    return pl.pallas_call(
        functools.partial(_inproj_kernel, ts=ts, nb=nb, nblk=nblk),
        grid=(nblk + 1,),
        in_specs=[pl.BlockSpec((1, ts, d), lambda n: (stg_a(n)[0], stg_a(n)[1], 0)),
                  mod3, mod3,
                  const2(w_main), const2(w_t), const2(w_f3), const2(b_f3), const2(tri)],
        out_specs=[pl.BlockSpec((S5_TILES, ts * nb, LANES), lambda n: (0, stg_b(n)[1], 0)),
                   tok3(SSM_WIDTH), tok3(ATTN_WIDTH), tok3(ATTN_WIDTH),
                   tok3(D_MODEL), tok3(D_MODEL), tr3, tr3,
                   caug3],
        out_shape=[jax.ShapeDtypeStruct((S5_TILES, s * nb, LANES), F32),
                   bf((nb, s, SSM_WIDTH)), bf((nb, s, ATTN_WIDTH)), bf((nb, s, ATTN_WIDTH)),
                   bf((nb, s, D_MODEL)), bf((nb, s, D_MODEL)),
                   bf((nb, ATTN_WIDTH, s)), bf((nb, ATTN_WIDTH, s)),
                   bf((nb, s + ts, LANES))],
        scratch_shapes=[pltpu.VMEM((nb, SUBLANES, LANES), F32),
                        pltpu.VMEM((2, ts, d), BF16)],
        compiler_params=_cparams("arbitrary"),
        name="inproj",
    )(x, scale, shift, w_main, w_t, w_f3, b_f3, tri)


def _gelu_tanh(x):
    return 0.5 * x * (1.0 + jnp.tanh(math.sqrt(2.0 / math.pi) * (x + 0.044715 * (x * x * x))))


def _s5_kernel(xs_ref, bmat_ref, cmat_ref, are_ref, aim_ref, d_ref, wglu_ref,
               y_ref, buf_ref, h_ref, ybuf_ref, *, steps, sub):
    @pl.when(pl.program_id(0) == 0)
    def _():
        h_ref[...] = jnp.zeros_like(h_ref)

    srows = sub * SUBLANES

    def expand(c):
        r = slice(c * srows, (c + 1) * srows)
        for j in range(S5_TILES):
            lo = j * S5_TILE_LANES
            buf_ref[r, lo:lo + S5_TILE_LANES] = jnp.dot(
                xs_ref[j, r, :].astype(BF16), bmat_ref[j], preferred_element_type=F32)

    def recur(c, state):
        new_state = []
        for j in range(S5_TILES):
            re = slice(j * S5_TILE_LANES, j * S5_TILE_LANES + S5_HALF)
            im = slice(j * S5_TILE_LANES + S5_HALF, (j + 1) * S5_TILE_LANES)
            ar = jnp.broadcast_to(are_ref[:, j * S5_HALF:(j + 1) * S5_HALF], (SUBLANES, S5_HALF))
            ai = jnp.broadcast_to(aim_ref[:, j * S5_HALF:(j + 1) * S5_HALF], (SUBLANES, S5_HALF))
            hr, hi = state[j]
            for t in range(sub):
                rt = slice(c * srows + t * SUBLANES, c * srows + (t + 1) * SUBLANES)
                nhr = ar * hr - ai * hi + buf_ref[rt, re]
                nhi = ar * hi + ai * hr + buf_ref[rt, im]
                buf_ref[rt, re] = nhr
                buf_ref[rt, im] = nhi
                hr, hi = nhr, nhi
            new_state.append((hr, hi))
        return new_state

    def project(c):
        r = slice(c * srows, (c + 1) * srows)
        ys = []
        for j in range(S5_TILES):
            lo = j * S5_TILE_LANES
            hb = buf_ref[r, lo:lo + S5_TILE_LANES].astype(BF16)
            ys.append(jnp.dot(hb, cmat_ref[j], preferred_element_type=F32)
                      + d_ref[:, j * LANES:(j + 1) * LANES] * xs_ref[j, r, :])
        g = _gelu_tanh(jnp.concatenate(ys, axis=1)).astype(BF16)
        ab = jnp.dot(g, wglu_ref[...], preferred_element_type=F32)
        out = ab[:, :SSM_WIDTH] * _sigmoid(ab[:, SSM_WIDTH:])
        for j in range(S5_TILES):
            ybuf_ref[j] = out[:, j * LANES:(j + 1) * LANES]
        for b in range(BATCH):
            yb = jnp.concatenate([ybuf_ref[j, pl.ds(b, sub, stride=SUBLANES), :]
                                  for j in range(S5_TILES)], axis=1)
            y_ref[b, c * sub:(c + 1) * sub, :] = yb.astype(BF16)

    state = [(h_ref[:, j * S5_TILE_LANES:j * S5_TILE_LANES + S5_HALF],
              h_ref[:, j * S5_TILE_LANES + S5_HALF:(j + 1) * S5_TILE_LANES])
             for j in range(S5_TILES)]
    nsub = steps // sub
    expand(0)
    for c in range(nsub):
        if c + 1 < nsub:
            expand(c + 1)
        state = recur(c, state)
        project(c)
    for j in range(S5_TILES):
        h_ref[:, j * S5_TILE_LANES:j * S5_TILE_LANES + S5_HALF] = state[j][0]
        h_ref[:, j * S5_TILE_LANES + S5_HALF:(j + 1) * S5_TILE_LANES] = state[j][1]


def _s5(xs_t, bmat, cmat, a_re, a_im, d_skip, w_glu, *, steps, sub):
    t = xs_t.shape[1]
    rows = steps * SUBLANES
    const = lambda a: pl.BlockSpec(a.shape, lambda i: (0,) * a.ndim)
    tiles = pl.BlockSpec((S5_TILES, rows, LANES), lambda i: (0, i, 0))
    return pl.pallas_call(
        functools.partial(_s5_kernel, steps=steps, sub=sub),
        grid=(t // rows,),
        in_specs=[tiles,
                  const(bmat), const(cmat), const(a_re), const(a_im), const(d_skip), const(w_glu)],
        out_specs=pl.BlockSpec((BATCH, steps, SSM_WIDTH), lambda i: (0, i, 0)),
        out_shape=jax.ShapeDtypeStruct((BATCH, t // SUBLANES, SSM_WIDTH), BF16),
        scratch_shapes=[pltpu.VMEM((rows, S5_STATE_LANES), F32),
                        pltpu.VMEM((SUBLANES, S5_STATE_LANES), F32),
                        pltpu.VMEM((S5_TILES, sub * SUBLANES, LANES), F32)],
        compiler_params=_cparams("arbitrary"),
        name="s5",
    )(xs_t, bmat, cmat, a_re, a_im, d_skip, w_glu)


def _attn_kernel(qt_ref, k_ref, caug_ref, vt_ref, sza_ref, o_ref,
                 rhs_ref, s_ref, m_ref, l_ref, acc_ref, *, tq):
    qi = pl.program_id(1)
    tk = ATTN_TK
    nd = tq // tk
    hd = ATTN_HEAD_DIM
    row = lax.broadcasted_iota(jnp.int32, (LANES, tq), 0)

    for h in range(ATTN_HEADS):
        j = h // 2
        q2 = qt_ref[0, j * LANES:(j + 1) * LANES, :]
        rhs_ref[h, 0:LANES, :] = jnp.where((row // hd) == (h % 2), q2, jnp.zeros_like(q2))
        rhs_ref[h, LANES:2 * LANES, :] = jnp.where(
            (row == h) | (row == 8 + h) | (row == 16 + h), 1.0, 0.0).astype(BF16)
    m_ref[...] = jnp.full(m_ref.shape, -jnp.inf, F32)
    l_ref[...] = jnp.zeros_like(l_ref)
    acc_ref[...] = jnp.zeros_like(acc_ref)

    def scores(k0, h, diag=None):
        j = h // 2
        lo = 0 if diag is None else diag
        lhs = jnp.concatenate([k_ref[0, pl.ds(k0, tk), j * LANES:(j + 1) * LANES],
                               caug_ref[0, pl.ds(k0, tk), :]], axis=1)
        s_ref[h % ATTN_SLOTS, :, lo:tq] = jnp.dot(lhs, rhs_ref[h, :, lo:tq],
                                                  preferred_element_type=F32)

    def update(k0, h, diag=None):
        lo = 0 if diag is None else diag
        s = s_ref[h % ATTN_SLOTS, :, lo:tq]
        if diag is not None:
            kpos = lax.broadcasted_iota(jnp.int32, s.shape, 0)
            qpos = lax.broadcasted_iota(jnp.int32, s.shape, 1)
            s = jnp.where(kpos <= qpos, s, NEG_BIG)
        hs = slice(h * SUBLANES, h * SUBLANES + 1)
        m = m_ref[hs, lo:tq]
        m_new = jnp.maximum(m, jnp.max(s, axis=0, keepdims=True))
        alpha = jnp.exp2(m - m_new)
        p = jnp.exp2(s - m_new).astype(BF16)
        m_ref[hs, lo:tq] = m_new
        vt1 = jnp.concatenate([vt_ref[0, h * hd:(h + 1) * hd, pl.ds(k0, tk)],
                               jnp.ones((2 * SUBLANES, tk), BF16)], axis=0)
        pv = jnp.dot(vt1, p, preferred_element_type=F32)
        l_ref[hs, lo:tq] = alpha * l_ref[hs, lo:tq] + pv[hd:hd + 1, :]
        acc_ref[h * hd:(h + 1) * hd, lo:tq] = (alpha * acc_ref[h * hd:(h + 1) * hd, lo:tq]
                                               + pv[:hd, :])

    def body(kj, carry):
        k0 = pl.multiple_of(kj * tk, tk)
        k1 = pl.multiple_of(kj * tk + tk, tk)
        for h in range(ATTN_HEADS):
            ha = h + ATTN_LOOKAHEAD
            if ha < ATTN_HEADS:
                scores(k0, ha)
            else:
                scores(k1, ha - ATTN_HEADS)
            update(k0, h)
        return carry

    for h in range(ATTN_LOOKAHEAD):
        scores(0, h)
    lax.fori_loop(0, qi * nd, body, 0)
    for d in range(nd):
        kd = pl.multiple_of(qi * tq + d * tk, tk)
        for h in range(ATTN_HEADS):
            ha = h + ATTN_LOOKAHEAD
            if ha < ATTN_HEADS:
                scores(kd, ha, d * tk)
            elif d + 1 < nd:
                scores(pl.multiple_of(kd + tk, tk), ha - ATTN_HEADS, (d + 1) * tk)
            update(kd, h, d * tk)

    for h in range(ATTN_HEADS):
        acc_ref[h * hd:(h + 1) * hd, :] = (acc_ref[h * hd:(h + 1) * hd, :]
                                           / l_ref[h * SUBLANES:h * SUBLANES + 1, :])
    o = acc_ref[...].T
    o_ref[0] = (o * sza_ref[0].astype(F32)).astype(BF16)


def _attn(qt, k, caug, vt, sza, *, tq):
    nb, s, w = k.shape
    return pl.pallas_call(
        functools.partial(_attn_kernel, tq=tq),
        grid=(nb, s // tq),
        in_specs=[pl.BlockSpec((1, w, tq), lambda b, i: (b, 0, i)),
                  pl.BlockSpec((1, s, w), lambda b, i: (b, 0, 0)),
                  pl.BlockSpec((1, s, LANES), lambda b, i: (b, 0, 0)),
                  pl.BlockSpec((1, w, s), lambda b, i: (b, 0, 0)),
                  pl.BlockSpec((1, tq, w), lambda b, i: (b, i, 0))],
        out_specs=pl.BlockSpec((1, tq, w), lambda b, i: (b, i, 0)),
        out_shape=jax.ShapeDtypeStruct((nb, s, w), BF16),
        scratch_shapes=[pltpu.VMEM((ATTN_HEADS, 2 * LANES, tq), BF16),
                        pltpu.VMEM((ATTN_SLOTS, ATTN_TK, tq), F32),
                        pltpu.VMEM((ATTN_HEADS * SUBLANES, tq), F32),
                        pltpu.VMEM((ATTN_HEADS * SUBLANES, tq), F32),
                        pltpu.VMEM((w, tq), F32)],
        compiler_params=_cparams("arbitrary", "arbitrary"),
        name="attn",
    )(qt, k, caug, vt, sza)


def _merge_kernel(x_ref, ys_ref, szs_ref, oz_ref, sgs_ref, sga_ref, gate_ref,
                  wps_ref, wpa_ref, wout_ref, lng_ref, lnb_ref, o_ref, *, ts):
    for c in range(ts // MERGE_SUB):
        r = slice(c * MERGE_SUB, (c + 1) * MERGE_SUB)
        a = (ys_ref[0, r, :].astype(F32) * szs_ref[0, r, :].astype(F32)).astype(BF16)
        m1 = jnp.dot(a, wps_ref[...], preferred_element_type=F32) * sgs_ref[0, r, :].astype(F32)
        m2 = (jnp.dot(oz_ref[0, r, :], wpa_ref[...], preferred_element_type=F32)
              * sga_ref[0, r, :].astype(F32))
        merged = (m1 + m2).astype(BF16)
        sub = jnp.dot(merged, wout_ref[...], preferred_element_type=F32) * gate_ref[0]
        y = DEEPNORM_ALPHA * x_ref[0, r, :] + sub
        mu = jnp.mean(y, axis=-1, keepdims=True)
        yc = y - mu
        var = jnp.mean(yc * yc, axis=-1, keepdims=True)
        o_ref[0, r, :] = yc * lax.rsqrt(var + LN_EPS) * lng_ref[...] + lnb_ref[...]


def _merge(x, ys_t, szs, oz, sgs, sga, gate, wps, wpa, wout, ln_g, ln_b, *, ts):
    nb, s, d = x.shape
    tok3 = lambda w: pl.BlockSpec((1, ts, w), lambda i, b: (b, i, 0))
    const2 = lambda a: pl.BlockSpec(a.shape, lambda i, b: (0, 0))
    return pl.pallas_call(
        functools.partial(_merge_kernel, ts=ts),
        grid=(s // ts, nb),
        in_specs=[tok3(d), tok3(SSM_WIDTH),
                  tok3(SSM_WIDTH), tok3(ATTN_WIDTH), tok3(d), tok3(d),
                  pl.BlockSpec((1, 1, d), lambda i, b: (b, 0, 0)),
                  const2(wps), const2(wpa), const2(wout), const2(ln_g), const2(ln_b)],
        out_specs=tok3(d),
        out_shape=jax.ShapeDtypeStruct((nb, s, d), x.dtype),
        compiler_params=_cparams("arbitrary", "arbitrary"),
        name="merge",
    )(x, ys_t, szs, oz, sgs, sga, gate, wps, wpa, wout, ln_g, ln_b)


def _block_diag_tiles(blocks):
    eye = jnp.eye(S5_TILE_GROUPS, dtype=blocks.dtype)
    t, g, r, c = blocks.shape
    return jnp.einsum('tgrc,gG->tgrGc', blocks, eye).reshape(t, g * r, g * c)


def kernel(x, c, w_ada, b_ada, w_in, b_f, lam_re, lam_im, log_dt, ssm_b_re, ssm_b_im,
           ssm_c_re, ssm_c_im, ssm_d, w_glu, w_proj_ssm, w_proj_attn, w_out, ln_g, ln_b):
    nb, s, d = x.shape
    l = 0
    mod = _ada(c, w_ada[l], b_ada[l])
    shift = mod[:, None, :d]
    scale = mod[:, None, d:2 * d]
    gate = mod[:, None, 2 * d:]

    w_main, w_t, w_f3 = _wprep(jnp.swapaxes(w_in[l], 0, 1))
    bfv = b_f[l].astype(F32)
    b_f3 = jnp.concatenate([bfv, bfv, bfv, jnp.zeros((LANES - 3 * ATTN_HEADS,), F32)])[None, :]

    (xs_t, szs, k, sza, sgs, sga, qt, vt, caug) = _inproj(
        x, scale, shift, w_main, w_t, w_f3, b_f3, ts=256)

    abr, abi, bbr, bbi, ncim = _zoh(lam_re[l], lam_im[l], log_dt[l], ssm_b_re[l], ssm_b_im[l],
                                    ssm_c_im[l])
    g4 = (S5_TILES, S5_TILE_GROUPS)
    bbr = bbr.reshape(g4 + (SSM_GROUP, SSM_STATE))
    bbi = bbi.reshape(g4 + (SSM_GROUP, SSM_STATE))
    bmat = jnp.concatenate([_block_diag_tiles(bbr), _block_diag_tiles(bbi)], axis=2).astype(BF16)
    cre_t = ssm_c_re[l].reshape(g4 + (SSM_GROUP, SSM_STATE)).transpose(0, 1, 3, 2)
    ncim_t = ncim.reshape(g4 + (SSM_GROUP, SSM_STATE)).transpose(0, 1, 3, 2)
    cmat = jnp.concatenate([_block_diag_tiles(cre_t), _block_diag_tiles(ncim_t)],
                           axis=1).astype(BF16)
    a_re = abr.reshape(SSM_GROUPS, SSM_GROUP, SSM_STATE)[:, 0, :].reshape(1, SSM_GROUPS * SSM_STATE)
    a_im = abi.reshape(SSM_GROUPS, SSM_GROUP, SSM_STATE)[:, 0, :].reshape(1, SSM_GROUPS * SSM_STATE)

    ys_t = _s5(xs_t, bmat, cmat, a_re, a_im,
               ssm_d[l].reshape(1, SSM_WIDTH).astype(F32), w_glu[l].astype(BF16), steps=128, sub=32)

    oz = _attn(qt, k, caug, vt, sza, tq=1024)

    return _merge(x, ys_t, szs, oz, sgs, sga, gate,
                  w_proj_ssm[l].astype(BF16), w_proj_attn[l].astype(BF16), w_out[l].astype(BF16),
                  ln_g[l].reshape(1, d).astype(F32), ln_b[l].astype(F32).reshape(1, d), ts=512)
```

```python
import functools
import math

import jax
import jax.numpy as jnp
from jax import lax
from jax.experimental import pallas as pl
from jax.experimental.pallas import tpu as pltpu

D_MODEL = 1024
BATCH = 8
SEQ = 2048
SSM_WIDTH = 512
SSM_GROUP = 16
SSM_GROUPS = 32
SSM_STATE = 64
ATTN_HEADS = 8
ATTN_HEAD_DIM = 64
ATTN_WIDTH = 512
LN_EPS = 1e-5
DEEPNORM_ALPHA = 2.0 ** 0.25

F32 = jnp.float32
BF16 = jnp.bfloat16

LANES = 128
SUBLANES = 8
VMEM_LIMIT = 52 * 1024 * 1024

S5_TILES = 4
S5_TILE_GROUPS = SSM_GROUPS // S5_TILES
S5_HALF = S5_TILE_GROUPS * SSM_STATE
S5_TILE_LANES = 2 * S5_HALF
S5_TILE_IN = S5_TILE_GROUPS * SSM_GROUP
S5_STATE_LANES = S5_TILES * S5_TILE_LANES

C_XS, C_ZS, C_K, C_ZA, C_GS, C_GA, C_END = 0, 512, 1024, 1536, 2048, 3072, 4096

NEG_BIG = -1e30
LOG2E = math.log2(math.e)

MERGE_SUB = 256
ATTN_TK = 256
ATTN_LOOKAHEAD = 3
ATTN_SLOTS = 4
assert ATTN_SLOTS > ATTN_LOOKAHEAD and ATTN_HEADS % ATTN_SLOTS == 0


def _sigmoid(x):
    return 1.0 / (1.0 + jnp.exp(-x))


def _silu(x):
    return x * _sigmoid(x)


def _cparams(*sem):
    return pltpu.CompilerParams(dimension_semantics=sem, vmem_limit_bytes=VMEM_LIMIT)


def _ada_kernel(c_ref, w_ref, b_ref, o_ref):
    ca = _silu(c_ref[...]).astype(BF16)
    o_ref[...] = jnp.dot(ca, w_ref[...].astype(BF16), preferred_element_type=F32) + b_ref[...]


def _ada(c, w_ada, b_ada):
    n = w_ada.shape[1]
    tn = 512
    return pl.pallas_call(
        _ada_kernel,
        grid=(n // tn,),
        in_specs=[pl.BlockSpec((BATCH, D_MODEL), lambda j: (0, 0)),
                  pl.BlockSpec((D_MODEL, tn), lambda j: (0, j)),
                  pl.BlockSpec((1, tn), lambda j: (0, j))],
        out_specs=pl.BlockSpec((BATCH, tn), lambda j: (0, j)),
        out_shape=jax.ShapeDtypeStruct((BATCH, n), F32),
        compiler_params=_cparams("arbitrary"),
        name="ada",
    )(c, w_ada, b_ada.reshape(1, n))


def _zoh_kernel(lr_ref, li_ref, ldt_ref, br_ref, bi_ref, cim_ref,
                abr_ref, abi_ref, bbr_ref, bbi_ref, ncim_ref):
    lr = lr_ref[...]
    li = li_ref[...]
    dt = jnp.exp(ldt_ref[...])
    mag = jnp.exp(lr * dt)
    abr = mag * jnp.cos(li * dt)
    abi = mag * jnp.sin(li * dt)
    den = lr * lr + li * li
    cr = ((abr - 1.0) * lr + abi * li) / den
    ci = (abi * lr - (abr - 1.0) * li) / den
    br = br_ref[...]
    bi = bi_ref[...]
    abr_ref[...] = abr
    abi_ref[...] = abi
    bbr_ref[...] = cr * br - ci * bi
    bbi_ref[...] = cr * bi + ci * br
    ncim_ref[...] = -cim_ref[...]


def _zoh(lam_re, lam_im, log_dt, b_re, b_im, c_im):
    rows = SSM_GROUPS * SSM_GROUP
    rep = lambda a: jnp.repeat(a, SSM_GROUP, axis=0)
    lr = rep(lam_re)
    li = rep(lam_im)
    ldt = jnp.broadcast_to(rep(log_dt[:, None]), (rows, SSM_STATE))
    brt = b_re.transpose(0, 2, 1).reshape(rows, SSM_STATE)
    bit = b_im.transpose(0, 2, 1).reshape(rows, SSM_STATE)
    cim = c_im.reshape(rows, SSM_STATE)
    shp = jax.ShapeDtypeStruct((rows, SSM_STATE), F32)
    return pl.pallas_call(
        _zoh_kernel,
        out_shape=(shp,) * 5,
        name="zoh",
    )(lr, li, ldt, brt, bit, cim)


O_XS, O_ZS, O_Q, O_K, O_V, O_F, O_ZA, O_GS, O_GA, O_END = (
    0, 512, 1024, 1536, 2048, 2560, 2568, 3080, 4104, 5128)


WPREP_CHUNK = 256


def _wprep_kernel(wt_ref, wm_ref, wqv_ref, wf_ref):
    def put(c_lo, o_lo, n):
        for s in range(0, n, WPREP_CHUNK):
            wm_ref[:, c_lo + s:c_lo + s + WPREP_CHUNK] = (
                wt_ref[o_lo + s:o_lo + s + WPREP_CHUNK, :].T.astype(BF16))

    put(C_XS, O_XS, O_Q - O_XS)
    put(C_K, O_K, O_V - O_K)
    put(C_ZA, O_ZA, O_GS - O_ZA)
    put(C_GS, O_GS, O_END - O_GS)
    wqv_ref[0:ATTN_WIDTH, :] = wt_ref[O_Q:O_K, :].astype(BF16)
    wqv_ref[ATTN_WIDTH:2 * ATTN_WIDTH, :] = wt_ref[O_V:O_F, :].astype(BF16)
    f8 = wt_ref[O_F:O_ZA, :]
    blk = jnp.concatenate([f8, f8, f8, jnp.zeros((LANES - 3 * ATTN_HEADS, f8.shape[1]), F32)], axis=0)
    wf_ref[...] = blk.T.astype(BF16)


def _wprep(wt):
    n, d = wt.shape
    return pl.pallas_call(
        _wprep_kernel,
        out_shape=[jax.ShapeDtypeStruct((d, C_END), BF16),
                   jax.ShapeDtypeStruct((2 * ATTN_WIDTH, d), BF16),
                   jax.ShapeDtypeStruct((d, LANES), BF16)],
        compiler_params=pltpu.CompilerParams(vmem_limit_bytes=VMEM_LIMIT),
        name="wprep",
    )(wt)


def _inproj_kernel(x_ref, scale_ref, shift_ref, wm_ref, wt_ref, wf_ref, bf_ref, tri_ref,
                   xs_ref, szs_ref, k_ref, sza_ref, sgs_ref, sga_ref, qt_ref, vt_ref, caug_ref,
                   carry_ref, *, ts):
    i = pl.program_id(0)
    b = pl.program_id(1)

    @pl.when(i == 0)
    def _():
        carry_ref[b] = jnp.zeros(carry_ref.shape[1:], F32)

    x = x_ref[0]
    mu = jnp.mean(x, axis=-1, keepdims=True)
    xc = x - mu
    var = jnp.mean(xc * xc, axis=-1, keepdims=True)
    xn = xc * lax.rsqrt(var + LN_EPS)
    u = xn * (1.0 + scale_ref[0, 0]) + shift_ref[0, 0]
    ub = u.astype(BF16)

    def mm(lo, hi):
        return jnp.dot(ub, wm_ref[:, lo:hi], preferred_element_type=F32)

    f = jnp.dot(ub, wf_ref[...], preferred_element_type=F32) + bf_ref[...]
    lf = jnp.minimum(f, 0.0) - jnp.log(1.0 + jnp.exp(-jnp.abs(f)))
    l1 = lf.astype(BF16)
    r1 = lf - l1.astype(F32)
    l2 = r1.astype(BF16)
    l3 = (r1 - l2.astype(F32)).astype(BF16)

    xs = mm(C_XS, C_ZS)
    for j in range(S5_TILES):
        xs_ref[j, pl.ds(b, ts, stride=SUBLANES), :] = xs[:, j * LANES:(j + 1) * LANES]
    szs_ref[0] = _silu(mm(C_ZS, C_K)).astype(BF16)

    tri = tri_ref[...]
    cf = (jnp.dot(tri, l1, preferred_element_type=F32)
          + jnp.dot(tri, l2, preferred_element_type=F32)
          + jnp.dot(tri, l3, preferred_element_type=F32)) + carry_ref[b][0:1, :]
    carry_ref[b] = jnp.broadcast_to(cf[ts - 1:ts, :], carry_ref.shape[1:])

    k_ref[0] = mm(C_K, C_ZA).astype(BF16)
    sza_ref[0] = _silu(mm(C_ZA, C_GS)).astype(BF16)

    cf2 = cf * LOG2E
    c1 = cf2.astype(BF16).astype(F32)
    r = cf2 - c1
    c2 = r.astype(BF16).astype(F32)
    c3 = r - c2
    lane = lax.broadcasted_iota(jnp.int32, cf.shape, 1)
    parts = jnp.where(lane < 8, c1, jnp.where(lane < 16, c2, jnp.where(lane < 24, c3, 0.0)))
    caug_ref[0] = (-parts).astype(BF16)

    sgs_ref[0] = _sigmoid(mm(C_GS, C_GA)).astype(BF16)
    sga_ref[0] = _sigmoid(mm(C_GA, C_END)).astype(BF16)

    qv = lax.dot_general(wt_ref[...], ub, (((1,), (1,)), ((), ())),
                         preferred_element_type=F32)
    qt_ref[0] = (qv[:ATTN_WIDTH] * (LOG2E * ATTN_HEAD_DIM ** -0.5)).astype(BF16)
    vt_ref[0] = qv[ATTN_WIDTH:].astype(BF16)


def _inproj(x, mod4, w_main, w_t, w_f3, b_f3, *, ts):
    nb, s, d = x.shape
    nt = s // ts
    tri = jnp.tril(jnp.ones((ts, ts), BF16))
    tok3 = lambda w: pl.BlockSpec((1, ts, w), lambda i, b: (b, i, 0))
    const2 = lambda a: pl.BlockSpec(a.shape, lambda i, b: (0, 0), pipeline_mode=pl.Buffered(1))
    bf = lambda shape: jax.ShapeDtypeStruct(shape, BF16)
    return pl.pallas_call(
        functools.partial(_inproj_kernel, ts=ts),
        grid=(nt, nb),
        in_specs=[tok3(d), _mod_spec(1), _mod_spec(0),
                  const2(w_main), const2(w_t), const2(w_f3), const2(b_f3), const2(tri)],
        out_specs=[pl.BlockSpec((S5_TILES, ts * nb, LANES), lambda i, b: (0, i, 0)),
                   tok3(SSM_WIDTH), tok3(ATTN_WIDTH), tok3(ATTN_WIDTH),
                   tok3(D_MODEL), tok3(D_MODEL),
                   pl.BlockSpec((1, ATTN_WIDTH, ts), lambda i, b: (b, 0, i)),
                   pl.BlockSpec((1, ATTN_WIDTH, ts), lambda i, b: (b, 0, i)),
                   tok3(LANES)],
        out_shape=[jax.ShapeDtypeStruct((S5_TILES, s * nb, LANES), F32),
                   bf((nb, s, SSM_WIDTH)), bf((nb, s, ATTN_WIDTH)), bf((nb, s, ATTN_WIDTH)),
                   bf((nb, s, D_MODEL)), bf((nb, s, D_MODEL)),
                   bf((nb, ATTN_WIDTH, s)), bf((nb, ATTN_WIDTH, s)),
                   bf((nb, s, LANES))],
        scratch_shapes=[pltpu.VMEM((nb, SUBLANES, LANES), F32)],
        compiler_params=_cparams("arbitrary", "arbitrary"),
        name="inproj",
    )(x, mod4, mod4, w_main, w_t, w_f3, b_f3, tri)


def _gelu_tanh(x):
    return 0.5 * x * (1.0 + jnp.tanh(math.sqrt(2.0 / math.pi) * (x + 0.044715 * (x * x * x))))


def _s5_kernel(xs_ref, bmat_ref, cmat_ref, are_ref, aim_ref, d_ref, wglu_ref,
               y_ref, buf_ref, h_ref, ybuf_ref, *, steps, sub):
    @pl.when(pl.program_id(0) == 0)
    def _():
        h_ref[...] = jnp.zeros_like(h_ref)

    srows = sub * SUBLANES

    def expand(c):
        r = slice(c * srows, (c + 1) * srows)
        for j in range(S5_TILES):
            lo = j * S5_TILE_LANES
            buf_ref[r, lo:lo + S5_TILE_LANES] = jnp.dot(
                xs_ref[j, r, :].astype(BF16), bmat_ref[j], preferred_element_type=F32)

    def recur(c, state):
        new_state = []
        for j in range(S5_TILES):
            re = slice(j * S5_TILE_LANES, j * S5_TILE_LANES + S5_HALF)
            im = slice(j * S5_TILE_LANES + S5_HALF, (j + 1) * S5_TILE_LANES)
            ar = jnp.broadcast_to(are_ref[:, j * S5_HALF:(j + 1) * S5_HALF], (SUBLANES, S5_HALF))
            ai = jnp.broadcast_to(aim_ref[:, j * S5_HALF:(j + 1) * S5_HALF], (SUBLANES, S5_HALF))
            hr, hi = state[j]
            for t in range(sub):
                rt = slice(c * srows + t * SUBLANES, c * srows + (t + 1) * SUBLANES)
                nhr = ar * hr - ai * hi + buf_ref[rt, re]
                nhi = ar * hi + ai * hr + buf_ref[rt, im]
                buf_ref[rt, re] = nhr
                buf_ref[rt, im] = nhi
                hr, hi = nhr, nhi
            new_state.append((hr, hi))
        return new_state

    def project(c):
        r = slice(c * srows, (c + 1) * srows)
        ys = []
        for j in range(S5_TILES):
            lo = j * S5_TILE_LANES
            hb = buf_ref[r, lo:lo + S5_TILE_LANES].astype(BF16)
            ys.append(jnp.dot(hb, cmat_ref[j], preferred_element_type=F32)
                      + d_ref[:, j * LANES:(j + 1) * LANES] * xs_ref[j, r, :])
        g = _gelu_tanh(jnp.concatenate(ys, axis=1)).astype(BF16)
        ab = jnp.dot(g, wglu_ref[...], preferred_element_type=F32)
        out = ab[:, :SSM_WIDTH] * _sigmoid(ab[:, SSM_WIDTH:])
        for j in range(S5_TILES):
            ybuf_ref[j] = out[:, j * LANES:(j + 1) * LANES]
        for b in range(BATCH):
            yb = jnp.concatenate([ybuf_ref[j, pl.ds(b, sub, stride=SUBLANES), :]
                                  for j in range(S5_TILES)], axis=1)
            y_ref[b, c * sub:(c + 1) * sub, :] = yb.astype(BF16)

    state = [(h_ref[:, j * S5_TILE_LANES:j * S5_TILE_LANES + S5_HALF],
              h_ref[:, j * S5_TILE_LANES + S5_HALF:(j + 1) * S5_TILE_LANES])
             for j in range(S5_TILES)]
    nsub = steps // sub
    expand(0)
    for c in range(nsub):
        if c + 1 < nsub:
            expand(c + 1)
        state = recur(c, state)
        project(c)
    for j in range(S5_TILES):
        h_ref[:, j * S5_TILE_LANES:j * S5_TILE_LANES + S5_HALF] = state[j][0]
        h_ref[:, j * S5_TILE_LANES + S5_HALF:(j + 1) * S5_TILE_LANES] = state[j][1]


def _s5(xs_t, bmat, cmat, a_re, a_im, d_skip, w_glu, *, steps, sub):
    t = xs_t.shape[1]
    rows = steps * SUBLANES
    const = lambda a: pl.BlockSpec(a.shape, lambda i: (0,) * a.ndim)
    tiles = pl.BlockSpec((S5_TILES, rows, LANES), lambda i: (0, i, 0))
    return pl.pallas_call(
        functools.partial(_s5_kernel, steps=steps, sub=sub),
        grid=(t // rows,),
        in_specs=[tiles,
                  const(bmat), const(cmat), const(a_re), const(a_im), const(d_skip), const(w_glu)],
        out_specs=pl.BlockSpec((BATCH, steps, SSM_WIDTH), lambda i: (0, i, 0)),
        out_shape=jax.ShapeDtypeStruct((BATCH, t // SUBLANES, SSM_WIDTH), BF16),
        scratch_shapes=[pltpu.VMEM((rows, S5_STATE_LANES), F32),
                        pltpu.VMEM((SUBLANES, S5_STATE_LANES), F32),
                        pltpu.VMEM((S5_TILES, sub * SUBLANES, LANES), F32)],
        compiler_params=_cparams("arbitrary"),
        name="s5",
    )(xs_t, bmat, cmat, a_re, a_im, d_skip, w_glu)


def _attn_kernel(qt_ref, k_ref, caug_ref, vt_ref, sza_ref, o_ref,
                 rhs_ref, s_ref, m_ref, l_ref, acc_ref, *, tq):
    qi = pl.program_id(1)
    tk = ATTN_TK
    nd = tq // tk
    hd = ATTN_HEAD_DIM
    row = lax.broadcasted_iota(jnp.int32, (LANES, tq), 0)

    for h in range(ATTN_HEADS):
        j = h // 2
        q2 = qt_ref[0, j * LANES:(j + 1) * LANES, :]
        rhs_ref[h, 0:LANES, :] = jnp.where((row // hd) == (h % 2), q2, jnp.zeros_like(q2))
        rhs_ref[h, LANES:2 * LANES, :] = jnp.where(
            (row == h) | (row == 8 + h) | (row == 16 + h), 1.0, 0.0).astype(BF16)
    m_ref[...] = jnp.full(m_ref.shape, -jnp.inf, F32)
    l_ref[...] = jnp.zeros_like(l_ref)
    acc_ref[...] = jnp.zeros_like(acc_ref)

    def scores(k0, h, diag=None):
        j = h // 2
        lo = 0 if diag is None else diag
        lhs = jnp.concatenate([k_ref[0, pl.ds(k0, tk), j * LANES:(j + 1) * LANES],
                               caug_ref[0, pl.ds(k0, tk), :]], axis=1)
        s_ref[h % ATTN_SLOTS, :, lo:tq] = jnp.dot(lhs, rhs_ref[h, :, lo:tq],
                                                  preferred_element_type=F32)

    def update(k0, h, diag=None):
        lo = 0 if diag is None else diag
        s = s_ref[h % ATTN_SLOTS, :, lo:tq]
        if diag is not None:
            kpos = lax.broadcasted_iota(jnp.int32, s.shape, 0)
            qpos = lax.broadcasted_iota(jnp.int32, s.shape, 1)
            s = jnp.where(kpos <= qpos, s, NEG_BIG)
        hs = slice(h * SUBLANES, h * SUBLANES + 1)
        m = m_ref[hs, lo:tq]
        m_new = jnp.maximum(m, jnp.max(s, axis=0, keepdims=True))
        alpha = jnp.exp2(m - m_new)
        p = jnp.exp2(s - m_new).astype(BF16)
        m_ref[hs, lo:tq] = m_new
        vt1 = jnp.concatenate([vt_ref[0, h * hd:(h + 1) * hd, pl.ds(k0, tk)],
                               jnp.ones((2 * SUBLANES, tk), BF16)], axis=0)
        pv = jnp.dot(vt1, p, preferred_element_type=F32)
        l_ref[hs, lo:tq] = alpha * l_ref[hs, lo:tq] + pv[hd:hd + 1, :]
        acc_ref[h * hd:(h + 1) * hd, lo:tq] = (alpha * acc_ref[h * hd:(h + 1) * hd, lo:tq]
                                               + pv[:hd, :])

    def body(kj, carry):
        k0 = pl.multiple_of(kj * tk, tk)
        k1 = pl.multiple_of(kj * tk + tk, tk)
        for h in range(ATTN_HEADS):
            ha = h + ATTN_LOOKAHEAD
            if ha < ATTN_HEADS:
                scores(k0, ha)
            else:
                scores(k1, ha - ATTN_HEADS)
            update(k0, h)
        return carry

    for h in range(ATTN_LOOKAHEAD):
        scores(0, h)
    lax.fori_loop(0, qi * nd, body, 0)
    for d in range(nd):
        kd = pl.multiple_of(qi * tq + d * tk, tk)
        for h in range(ATTN_HEADS):
            ha = h + ATTN_LOOKAHEAD
            if ha < ATTN_HEADS:
                scores(kd, ha, d * tk)
            elif d + 1 < nd:
                scores(pl.multiple_of(kd + tk, tk), ha - ATTN_HEADS, (d + 1) * tk)
            update(kd, h, d * tk)

    for h in range(ATTN_HEADS):
        acc_ref[h * hd:(h + 1) * hd, :] = (acc_ref[h * hd:(h + 1) * hd, :]
                                           / l_ref[h * SUBLANES:h * SUBLANES + 1, :])
    o = acc_ref[...].T
    o_ref[0] = (o * sza_ref[0].astype(F32)).astype(BF16)


def _attn(qt, k, caug, vt, sza, *, tq):
    nb, s, w = k.shape
    return pl.pallas_call(
        functools.partial(_attn_kernel, tq=tq),
        grid=(nb, s // tq),
        in_specs=[pl.BlockSpec((1, w, tq), lambda b, i: (b, 0, i)),
                  pl.BlockSpec((1, s, w), lambda b, i: (b, 0, 0)),
                  pl.BlockSpec((1, s, LANES), lambda b, i: (b, 0, 0)),
                  pl.BlockSpec((1, w, s), lambda b, i: (b, 0, 0)),
                  pl.BlockSpec((1, tq, w), lambda b, i: (b, i, 0))],
        out_specs=pl.BlockSpec((1, tq, w), lambda b, i: (b, i, 0)),
        out_shape=jax.ShapeDtypeStruct((nb, s, w), BF16),
        scratch_shapes=[pltpu.VMEM((ATTN_HEADS, 2 * LANES, tq), BF16),
                        pltpu.VMEM((ATTN_SLOTS, ATTN_TK, tq), F32),
                        pltpu.VMEM((ATTN_HEADS * SUBLANES, tq), F32),
                        pltpu.VMEM((ATTN_HEADS * SUBLANES, tq), F32),
                        pltpu.VMEM((w, tq), F32)],
        compiler_params=_cparams("arbitrary", "arbitrary"),
        name="attn",
    )(qt, k, caug, vt, sza)


def _merge_kernel(x_ref, ys_ref, szs_ref, oz_ref, sgs_ref, sga_ref, gate_ref,
                  wps_ref, wpa_ref, wout_ref, lng_ref, lnb_ref, o_ref,
                  wps_bf, wpa_bf, wout_bf, *, ts):
    @pl.when((pl.program_id(0) == 0) & (pl.program_id(1) == 0))
    def _():
        wps_bf[...] = wps_ref[...].astype(BF16)
        wpa_bf[...] = wpa_ref[...].astype(BF16)
        wout_bf[...] = wout_ref[...].astype(BF16)

    gate = gate_ref[0, 0]
    for c in range(ts // MERGE_SUB):
        r = slice(c * MERGE_SUB, (c + 1) * MERGE_SUB)
        a = (ys_ref[0, r, :].astype(F32) * szs_ref[0, r, :].astype(F32)).astype(BF16)
        m1 = jnp.dot(a, wps_bf[...], preferred_element_type=F32) * sgs_ref[0, r, :].astype(F32)
        m2 = (jnp.dot(oz_ref[0, r, :], wpa_bf[...], preferred_element_type=F32)
              * sga_ref[0, r, :].astype(F32))
        merged = (m1 + m2).astype(BF16)
        sub = jnp.dot(merged, wout_bf[...], preferred_element_type=F32) * gate
        y = DEEPNORM_ALPHA * x_ref[0, r, :] + sub
        mu = jnp.mean(y, axis=-1, keepdims=True)
        yc = y - mu
        var = jnp.mean(yc * yc, axis=-1, keepdims=True)
        o_ref[0, r, :] = yc * lax.rsqrt(var + LN_EPS) * lng_ref[...] + lnb_ref[...]


def _mod_spec(part):
    return pl.BlockSpec((1, 1, 1, D_MODEL), lambda i, b: (b, part, 0, 0))


def _merge(x, ys_t, szs, oz, sgs, sga, mod4, wps, wpa, wout, ln_g, ln_b, *, ts):
    nb, s, d = x.shape
    tok3 = lambda w: pl.BlockSpec((1, ts, w), lambda i, b: (b, i, 0))
    const2 = lambda a: pl.BlockSpec(a.shape, lambda i, b: (0, 0), pipeline_mode=pl.Buffered(1))
    return pl.pallas_call(
        functools.partial(_merge_kernel, ts=ts),
        grid=(s // ts, nb),
        in_specs=[tok3(d), tok3(SSM_WIDTH),
                  tok3(SSM_WIDTH), tok3(ATTN_WIDTH), tok3(d), tok3(d),
                  _mod_spec(2),
                  const2(wps), const2(wpa), const2(wout), const2(ln_g), const2(ln_b)],
        out_specs=tok3(d),
        out_shape=jax.ShapeDtypeStruct((nb, s, d), x.dtype),
        scratch_shapes=[pltpu.VMEM(wps.shape, BF16), pltpu.VMEM(wpa.shape, BF16),
                        pltpu.VMEM(wout.shape, BF16)],
        compiler_params=_cparams("arbitrary", "arbitrary"),
        name="merge",
    )(x, ys_t, szs, oz, sgs, sga, mod4, wps, wpa, wout, ln_g, ln_b)


def _block_diag_tiles(blocks):
    eye = jnp.eye(S5_TILE_GROUPS, dtype=blocks.dtype)
    t, g, r, c = blocks.shape
    return jnp.einsum('tgrc,gG->tgrGc', blocks, eye).reshape(t, g * r, g * c)


def kernel(x, c, w_ada, b_ada, w_in, b_f, lam_re, lam_im, log_dt, ssm_b_re, ssm_b_im,
           ssm_c_re, ssm_c_im, ssm_d, w_glu, w_proj_ssm, w_proj_attn, w_out, ln_g, ln_b):
    nb, s, d = x.shape
    l = 0
    mod4 = _ada(c, w_ada[l], b_ada[l]).reshape(nb, 3, 1, d)

    w_main, w_t, w_f3 = _wprep(jnp.swapaxes(w_in[l], 0, 1))
    bfv = b_f[l].astype(F32)
    b_f3 = jnp.concatenate([bfv, bfv, bfv, jnp.zeros((LANES - 3 * ATTN_HEADS,), F32)])[None, :]

    (xs_t, szs, k, sza, sgs, sga, qt, vt, caug) = _inproj(
        x, mod4, w_main, w_t, w_f3, b_f3, ts=256)

    abr, abi, bbr, bbi, ncim = _zoh(lam_re[l], lam_im[l], log_dt[l], ssm_b_re[l], ssm_b_im[l],
                                    ssm_c_im[l])
    g4 = (S5_TILES, S5_TILE_GROUPS)
    bbr = bbr.reshape(g4 + (SSM_GROUP, SSM_STATE))
    bbi = bbi.reshape(g4 + (SSM_GROUP, SSM_STATE))
    bmat = jnp.concatenate([_block_diag_tiles(bbr), _block_diag_tiles(bbi)], axis=2).astype(BF16)
    cre_t = ssm_c_re[l].reshape(g4 + (SSM_GROUP, SSM_STATE)).transpose(0, 1, 3, 2)
    ncim_t = ncim.reshape(g4 + (SSM_GROUP, SSM_STATE)).transpose(0, 1, 3, 2)
    cmat = jnp.concatenate([_block_diag_tiles(cre_t), _block_diag_tiles(ncim_t)],
                           axis=1).astype(BF16)
    a_re = abr.reshape(SSM_GROUPS, SSM_GROUP, SSM_STATE)[:, 0, :].reshape(1, SSM_GROUPS * SSM_STATE)
    a_im = abi.reshape(SSM_GROUPS, SSM_GROUP, SSM_STATE)[:, 0, :].reshape(1, SSM_GROUPS * SSM_STATE)

    ys_t = _s5(xs_t, bmat, cmat, a_re, a_im,
               ssm_d[l].reshape(1, SSM_WIDTH).astype(F32), w_glu[l].astype(BF16), steps=128, sub=32)

    oz = _attn(qt, k, caug, vt, sza, tq=1024)

    return _merge(x, ys_t, szs, oz, sgs, sga, mod4,
                  w_proj_ssm[l], w_proj_attn[l], w_out[l],
                  ln_g[l].reshape(1, d).astype(F32), ln_b[l].astype(F32).reshape(1, d), ts=1024)
```

```python
import functools
import math

import jax
import jax.numpy as jnp
from jax import lax
from jax.experimental import pallas as pl
from jax.experimental.pallas import tpu as pltpu

D_MODEL = 1024
BATCH = 8
SEQ = 2048
SSM_WIDTH = 512
SSM_GROUP = 16
SSM_GROUPS = 32
SSM_STATE = 64
ATTN_HEADS = 8
ATTN_HEAD_DIM = 64
ATTN_WIDTH = 512
LN_EPS = 1e-5
DEEPNORM_ALPHA = 2.0 ** 0.25

F32 = jnp.float32
BF16 = jnp.bfloat16

LANES = 128
SUBLANES = 8
VMEM_LIMIT = 52 * 1024 * 1024

S5_TILES = 4
S5_TILE_GROUPS = SSM_GROUPS // S5_TILES
S5_HALF = S5_TILE_GROUPS * SSM_STATE
S5_TILE_LANES = 2 * S5_HALF
S5_TILE_IN = S5_TILE_GROUPS * SSM_GROUP
S5_STATE_LANES = S5_TILES * S5_TILE_LANES

C_XS, C_ZS, C_K, C_ZA, C_GS, C_GA, C_END = 0, 512, 1024, 1536, 2048, 3072, 4096

NEG_BIG = -1e30
LOG2E = math.log2(math.e)

INPROJ_SUB = 256
MERGE_SUB = 256
ATTN_TK = 256
ATTN_LOOKAHEAD = 3
ATTN_SLOTS = 4
assert ATTN_SLOTS > ATTN_LOOKAHEAD and ATTN_HEADS % ATTN_SLOTS == 0


def _sigmoid(x):
    return 1.0 / (1.0 + jnp.exp(-x))


def _silu(x):
    return x * _sigmoid(x)


def _cparams(*sem):
    return pltpu.CompilerParams(dimension_semantics=sem, vmem_limit_bytes=VMEM_LIMIT)


def _ada_kernel(c_ref, w_ref, b_ref, o_ref):
    ca = _silu(c_ref[...]).astype(BF16)
    o_ref[...] = jnp.dot(ca, w_ref[...].astype(BF16), preferred_element_type=F32) + b_ref[...]


def _ada(c, w_ada, b_ada):
    n = w_ada.shape[1]
    tn = 512
    return pl.pallas_call(
        _ada_kernel,
        grid=(n // tn,),
        in_specs=[pl.BlockSpec((BATCH, D_MODEL), lambda j: (0, 0)),
                  pl.BlockSpec((D_MODEL, tn), lambda j: (0, j)),
                  pl.BlockSpec((1, tn), lambda j: (0, j))],
        out_specs=pl.BlockSpec((BATCH, tn), lambda j: (0, j)),
        out_shape=jax.ShapeDtypeStruct((BATCH, n), F32),
        compiler_params=_cparams("arbitrary"),
        name="ada",
    )(c, w_ada, b_ada.reshape(1, n))


def _zoh_kernel(lr_ref, li_ref, ldt_ref, br_ref, bi_ref, cim_ref,
                abr_ref, abi_ref, bbr_ref, bbi_ref, ncim_ref):
    lr = lr_ref[...]
    li = li_ref[...]
    dt = jnp.exp(ldt_ref[...])
    mag = jnp.exp(lr * dt)
    abr = mag * jnp.cos(li * dt)
    abi = mag * jnp.sin(li * dt)
    den = lr * lr + li * li
    cr = ((abr - 1.0) * lr + abi * li) / den
    ci = (abi * lr - (abr - 1.0) * li) / den
    br = br_ref[...]
    bi = bi_ref[...]
    abr_ref[...] = abr
    abi_ref[...] = abi
    bbr_ref[...] = cr * br - ci * bi
    bbi_ref[...] = cr * bi + ci * br
    ncim_ref[...] = -cim_ref[...]


def _zoh(lam_re, lam_im, log_dt, b_re, b_im, c_im):
    rows = SSM_GROUPS * SSM_GROUP
    rep = lambda a: jnp.repeat(a, SSM_GROUP, axis=0)
    lr = rep(lam_re)
    li = rep(lam_im)
    ldt = jnp.broadcast_to(rep(log_dt[:, None]), (rows, SSM_STATE))
    brt = b_re.transpose(0, 2, 1).reshape(rows, SSM_STATE)
    bit = b_im.transpose(0, 2, 1).reshape(rows, SSM_STATE)
    cim = c_im.reshape(rows, SSM_STATE)
    shp = jax.ShapeDtypeStruct((rows, SSM_STATE), F32)
    return pl.pallas_call(
        _zoh_kernel,
        out_shape=(shp,) * 5,
        name="zoh",
    )(lr, li, ldt, brt, bit, cim)


O_XS, O_ZS, O_Q, O_K, O_V, O_F, O_ZA, O_GS, O_GA, O_END = (
    0, 512, 1024, 1536, 2048, 2560, 2568, 3080, 4104, 5128)


WPREP_CHUNK = 256


def _wprep_kernel(wt_ref, wm_ref, wqv_ref, wf_ref):
    def put(c_lo, o_lo, n):
        for s in range(0, n, WPREP_CHUNK):
            wm_ref[:, c_lo + s:c_lo + s + WPREP_CHUNK] = (
                wt_ref[o_lo + s:o_lo + s + WPREP_CHUNK, :].T.astype(BF16))

    put(C_XS, O_XS, O_Q - O_XS)
    put(C_K, O_K, O_V - O_K)
    put(C_ZA, O_ZA, O_GS - O_ZA)
    put(C_GS, O_GS, O_END - O_GS)
    wqv_ref[0:ATTN_WIDTH, :] = wt_ref[O_Q:O_K, :].astype(BF16)
    wqv_ref[ATTN_WIDTH:2 * ATTN_WIDTH, :] = wt_ref[O_V:O_F, :].astype(BF16)
    f8 = wt_ref[O_F:O_ZA, :]
    blk = jnp.concatenate([f8, f8, f8, jnp.zeros((LANES - 3 * ATTN_HEADS, f8.shape[1]), F32)], axis=0)
    wf_ref[...] = blk.T.astype(BF16)


def _wprep(wt):
    n, d = wt.shape
    return pl.pallas_call(
        _wprep_kernel,
        out_shape=[jax.ShapeDtypeStruct((d, C_END), BF16),
                   jax.ShapeDtypeStruct((2 * ATTN_WIDTH, d), BF16),
                   jax.ShapeDtypeStruct((d, LANES), BF16)],
        compiler_params=pltpu.CompilerParams(vmem_limit_bytes=VMEM_LIMIT),
        name="wprep",
    )(wt)


def _inproj_kernel(x_ref, scale_ref, shift_ref, wm_ref, wt_ref, wf_ref, bf_ref, tri_ref,
                   xs_ref, szs_ref, k_ref, sza_ref, sgs_ref, sga_ref, qt_ref, vt_ref, caug_ref,
                   carry_ref, *, ts):
    i = pl.program_id(0)
    b = pl.program_id(1)

    @pl.when(i == 0)
    def _():
        carry_ref[b] = jnp.zeros(carry_ref.shape[1:], F32)

    carry = carry_ref[b][0:1, :]
    for c in range(ts // INPROJ_SUB):
        r = slice(c * INPROJ_SUB, (c + 1) * INPROJ_SUB)
        x = x_ref[0, r, :]
        mu = jnp.mean(x, axis=-1, keepdims=True)
        xc = x - mu
        var = jnp.mean(xc * xc, axis=-1, keepdims=True)
        xn = xc * lax.rsqrt(var + LN_EPS)
        u = xn * (1.0 + scale_ref[0, 0]) + shift_ref[0, 0]
        ub = u.astype(BF16)

        def mm(lo, hi, ub=ub):
            return jnp.dot(ub, wm_ref[:, lo:hi], preferred_element_type=F32)

        f = jnp.dot(ub, wf_ref[...], preferred_element_type=F32) + bf_ref[...]
        lf = jnp.minimum(f, 0.0) - jnp.log(1.0 + jnp.exp(-jnp.abs(f)))
        l1 = lf.astype(BF16)
        r1 = lf - l1.astype(F32)
        l2 = r1.astype(BF16)
        l3 = (r1 - l2.astype(F32)).astype(BF16)

        xs = mm(C_XS, C_ZS)
        for j in range(S5_TILES):
            xs_ref[j, pl.ds(b + c * INPROJ_SUB * SUBLANES, INPROJ_SUB, stride=SUBLANES), :] = (
                xs[:, j * LANES:(j + 1) * LANES])
        szs_ref[0, r, :] = _silu(mm(C_ZS, C_K)).astype(BF16)

        tri = tri_ref[...]
        cf = (jnp.dot(tri, l1, preferred_element_type=F32)
              + jnp.dot(tri, l2, preferred_element_type=F32)
              + jnp.dot(tri, l3, preferred_element_type=F32)) + carry
        carry = cf[INPROJ_SUB - 1:INPROJ_SUB, :]

        k_ref[0, r, :] = mm(C_K, C_ZA).astype(BF16)
        sza_ref[0, r, :] = _silu(mm(C_ZA, C_GS)).astype(BF16)

        cf2 = cf * LOG2E
        c1 = cf2.astype(BF16).astype(F32)
        rem = cf2 - c1
        c2 = rem.astype(BF16).astype(F32)
        c3 = rem - c2
        lane = lax.broadcasted_iota(jnp.int32, cf.shape, 1)
        parts = jnp.where(lane < 8, c1, jnp.where(lane < 16, c2, jnp.where(lane < 24, c3, 0.0)))
        caug_ref[0, r, :] = (-parts).astype(BF16)

        sgs_ref[0, r, :] = _sigmoid(mm(C_GS, C_GA)).astype(BF16)
        sga_ref[0, r, :] = _sigmoid(mm(C_GA, C_END)).astype(BF16)

        qv = lax.dot_general(wt_ref[...], ub, (((1,), (1,)), ((), ())),
                             preferred_element_type=F32)
        qt_ref[0, :, r] = (qv[:ATTN_WIDTH] * (LOG2E * ATTN_HEAD_DIM ** -0.5)).astype(BF16)
        vt_ref[0, :, r] = qv[ATTN_WIDTH:].astype(BF16)
    carry_ref[b] = jnp.broadcast_to(carry, carry_ref.shape[1:])


def _inproj(x, mod4, w_main, w_t, w_f3, b_f3, *, ts):
    nb, s, d = x.shape
    nt = s // ts
    tri = jnp.tril(jnp.ones((INPROJ_SUB, INPROJ_SUB), BF16))
    tok3 = lambda w: pl.BlockSpec((1, ts, w), lambda i, b: (b, i, 0))
    const2 = lambda a: pl.BlockSpec(a.shape, lambda i, b: (0, 0), pipeline_mode=pl.Buffered(1))
    bf = lambda shape: jax.ShapeDtypeStruct(shape, BF16)
    return pl.pallas_call(
        functools.partial(_inproj_kernel, ts=ts),
        grid=(nt, nb),
        in_specs=[tok3(d), _mod_spec(1), _mod_spec(0),
                  const2(w_main), const2(w_t), const2(w_f3), const2(b_f3), const2(tri)],
        out_specs=[pl.BlockSpec((S5_TILES, ts * nb, LANES), lambda i, b: (0, i, 0)),
                   tok3(SSM_WIDTH), tok3(ATTN_WIDTH), tok3(ATTN_WIDTH),
                   tok3(D_MODEL), tok3(D_MODEL),
                   pl.BlockSpec((1, ATTN_WIDTH, ts), lambda i, b: (b, 0, i)),
                   pl.BlockSpec((1, ATTN_WIDTH, ts), lambda i, b: (b, 0, i)),
                   tok3(LANES)],
        out_shape=[jax.ShapeDtypeStruct((S5_TILES, s * nb, LANES), F32),
                   bf((nb, s, SSM_WIDTH)), bf((nb, s, ATTN_WIDTH)), bf((nb, s, ATTN_WIDTH)),
                   bf((nb, s, D_MODEL)), bf((nb, s, D_MODEL)),
                   bf((nb, ATTN_WIDTH, s)), bf((nb, ATTN_WIDTH, s)),
                   bf((nb, s, LANES))],
        scratch_shapes=[pltpu.VMEM((nb, SUBLANES, LANES), F32)],
        compiler_params=_cparams("arbitrary", "arbitrary"),
        name="inproj",
    )(x, mod4, mod4, w_main, w_t, w_f3, b_f3, tri)


def _gelu_tanh(x):
    return 0.5 * x * (1.0 + jnp.tanh(math.sqrt(2.0 / math.pi) * (x + 0.044715 * (x * x * x))))


def _s5_kernel(xs_ref, bmat_ref, cmat_ref, are_ref, aim_ref, d_ref, wglu_ref,
               y_ref, buf_ref, h_ref, ybuf_ref, *, steps, sub):
    @pl.when(pl.program_id(0) == 0)
    def _():
        h_ref[...] = jnp.zeros_like(h_ref)

    srows = sub * SUBLANES

    def expand(c):
        r = slice(c * srows, (c + 1) * srows)
        for j in range(S5_TILES):
            lo = j * S5_TILE_LANES
            buf_ref[r, lo:lo + S5_TILE_LANES] = jnp.dot(
                xs_ref[j, r, :].astype(BF16), bmat_ref[j], preferred_element_type=F32)

    def recur(c, state):
        new_state = []
        for j in range(S5_TILES):
            re = slice(j * S5_TILE_LANES, j * S5_TILE_LANES + S5_HALF)
            im = slice(j * S5_TILE_LANES + S5_HALF, (j + 1) * S5_TILE_LANES)
            ar = jnp.broadcast_to(are_ref[:, j * S5_HALF:(j + 1) * S5_HALF], (SUBLANES, S5_HALF))
            ai = jnp.broadcast_to(aim_ref[:, j * S5_HALF:(j + 1) * S5_HALF], (SUBLANES, S5_HALF))
            hr, hi = state[j]
            for t in range(sub):
                rt = slice(c * srows + t * SUBLANES, c * srows + (t + 1) * SUBLANES)
                nhr = ar * hr - ai * hi + buf_ref[rt, re]
                nhi = ar * hi + ai * hr + buf_ref[rt, im]
                buf_ref[rt, re] = nhr
                buf_ref[rt, im] = nhi
                hr, hi = nhr, nhi
            new_state.append((hr, hi))
        return new_state

    def project(c):
        r = slice(c * srows, (c + 1) * srows)
        ys = []
        for j in range(S5_TILES):
            lo = j * S5_TILE_LANES
            hb = buf_ref[r, lo:lo + S5_TILE_LANES].astype(BF16)
            ys.append(jnp.dot(hb, cmat_ref[j], preferred_element_type=F32)
                      + d_ref[:, j * LANES:(j + 1) * LANES] * xs_ref[j, r, :])
        g = _gelu_tanh(jnp.concatenate(ys, axis=1)).astype(BF16)
        ab = jnp.dot(g, wglu_ref[...], preferred_element_type=F32)
        out = ab[:, :SSM_WIDTH] * _sigmoid(ab[:, SSM_WIDTH:])
        for j in range(S5_TILES):
            ybuf_ref[j] = out[:, j * LANES:(j + 1) * LANES]
        for b in range(BATCH):
            yb = jnp.concatenate([ybuf_ref[j, pl.ds(b, sub, stride=SUBLANES), :]
                                  for j in range(S5_TILES)], axis=1)
            y_ref[b, c * sub:(c + 1) * sub, :] = yb.astype(BF16)

    state = [(h_ref[:, j * S5_TILE_LANES:j * S5_TILE_LANES + S5_HALF],
              h_ref[:, j * S5_TILE_LANES + S5_HALF:(j + 1) * S5_TILE_LANES])
             for j in range(S5_TILES)]
    nsub = steps // sub
    expand(0)
    for c in range(nsub):
        if c + 1 < nsub:
            expand(c + 1)
        state = recur(c, state)
        project(c)
    for j in range(S5_TILES):
        h_ref[:, j * S5_TILE_LANES:j * S5_TILE_LANES + S5_HALF] = state[j][0]
        h_ref[:, j * S5_TILE_LANES + S5_HALF:(j + 1) * S5_TILE_LANES] = state[j][1]


def _s5(xs_t, bmat, cmat, a_re, a_im, d_skip, w_glu, *, steps, sub):
    t = xs_t.shape[1]
    rows = steps * SUBLANES
    const = lambda a: pl.BlockSpec(a.shape, lambda i: (0,) * a.ndim)
    tiles = pl.BlockSpec((S5_TILES, rows, LANES), lambda i: (0, i, 0))
    return pl.pallas_call(
        functools.partial(_s5_kernel, steps=steps, sub=sub),
        grid=(t // rows,),
        in_specs=[tiles,
                  const(bmat), const(cmat), const(a_re), const(a_im), const(d_skip), const(w_glu)],
        out_specs=pl.BlockSpec((BATCH, steps, SSM_WIDTH), lambda i: (0, i, 0)),
        out_shape=jax.ShapeDtypeStruct((BATCH, t // SUBLANES, SSM_WIDTH), BF16),
        scratch_shapes=[pltpu.VMEM((rows, S5_STATE_LANES), F32),
                        pltpu.VMEM((SUBLANES, S5_STATE_LANES), F32),
                        pltpu.VMEM((S5_TILES, sub * SUBLANES, LANES), F32)],
        compiler_params=_cparams("arbitrary"),
        name="s5",
    )(xs_t, bmat, cmat, a_re, a_im, d_skip, w_glu)


def _attn_kernel(qt_ref, k_ref, caug_ref, vt_ref, sza_ref, o_ref,
                 rhs_ref, s_ref, m_ref, l_ref, acc_ref, *, tq):
    qi = pl.program_id(1)
    tk = ATTN_TK
    nd = tq // tk
    hd = ATTN_HEAD_DIM
    row = lax.broadcasted_iota(jnp.int32, (LANES, tq), 0)

    for h in range(ATTN_HEADS):
        j = h // 2
        q2 = qt_ref[0, j * LANES:(j + 1) * LANES, :]
        rhs_ref[h, 0:LANES, :] = jnp.where((row // hd) == (h % 2), q2, jnp.zeros_like(q2))
        rhs_ref[h, LANES:2 * LANES, :] = jnp.where(
            (row == h) | (row == 8 + h) | (row == 16 + h), 1.0, 0.0).astype(BF16)
    m_ref[...] = jnp.full(m_ref.shape, -jnp.inf, F32)
    l_ref[...] = jnp.zeros_like(l_ref)
    acc_ref[...] = jnp.zeros_like(acc_ref)

    def scores(k0, h, diag=None):
        j = h // 2
        lo = 0 if diag is None else diag
        lhs = jnp.concatenate([k_ref[0, pl.ds(k0, tk), j * LANES:(j + 1) * LANES],
                               caug_ref[0, pl.ds(k0, tk), :]], axis=1)
        s_ref[h % ATTN_SLOTS, :, lo:tq] = jnp.dot(lhs, rhs_ref[h, :, lo:tq],
                                                  preferred_element_type=F32)

    def update(k0, h, diag=None):
        lo = 0 if diag is None else diag
        s = s_ref[h % ATTN_SLOTS, :, lo:tq]
        if diag is not None:
            kpos = lax.broadcasted_iota(jnp.int32, s.shape, 0)
            qpos = lax.broadcasted_iota(jnp.int32, s.shape, 1)
            s = jnp.where(kpos <= qpos, s, NEG_BIG)
        hs = slice(h * SUBLANES, h * SUBLANES + 1)
        m = m_ref[hs, lo:tq]
        m_new = jnp.maximum(m, jnp.max(s, axis=0, keepdims=True))
        alpha = jnp.exp2(m - m_new)
        p = jnp.exp2(s - m_new).astype(BF16)
        m_ref[hs, lo:tq] = m_new
        vt1 = jnp.concatenate([vt_ref[0, h * hd:(h + 1) * hd, pl.ds(k0, tk)],
                               jnp.ones((2 * SUBLANES, tk), BF16)], axis=0)
        pv = jnp.dot(vt1, p, preferred_element_type=F32)
        l_ref[hs, lo:tq] = alpha * l_ref[hs, lo:tq] + pv[hd:hd + 1, :]
        acc_ref[h * hd:(h + 1) * hd, lo:tq] = (alpha * acc_ref[h * hd:(h + 1) * hd, lo:tq]
                                               + pv[:hd, :])

    def body(kj, carry):
        k0 = pl.multiple_of(kj * tk, tk)
        k1 = pl.multiple_of(kj * tk + tk, tk)
        for h in range(ATTN_HEADS):
            ha = h + ATTN_LOOKAHEAD
            if ha < ATTN_HEADS:
                scores(k0, ha)
            else:
                scores(k1, ha - ATTN_HEADS)
            update(k0, h)
        return carry

    for h in range(ATTN_LOOKAHEAD):
        scores(0, h)
    lax.fori_loop(0, qi * nd, body, 0)
    for d in range(nd):
        kd = pl.multiple_of(qi * tq + d * tk, tk)
        for h in range(ATTN_HEADS):
            ha = h + ATTN_LOOKAHEAD
            if ha < ATTN_HEADS:
                scores(kd, ha, d * tk)
            elif d + 1 < nd:
                scores(pl.multiple_of(kd + tk, tk), ha - ATTN_HEADS, (d + 1) * tk)
            update(kd, h, d * tk)

    for h in range(ATTN_HEADS):
        acc_ref[h * hd:(h + 1) * hd, :] = (acc_ref[h * hd:(h + 1) * hd, :]
                                           / l_ref[h * SUBLANES:h * SUBLANES + 1, :])
    o = acc_ref[...].T
    o_ref[0] = (o * sza_ref[0].astype(F32)).astype(BF16)


def _attn(qt, k, caug, vt, sza, *, tq):
    nb, s, w = k.shape
    return pl.pallas_call(
        functools.partial(_attn_kernel, tq=tq),
        grid=(nb, s // tq),
        in_specs=[pl.BlockSpec((1, w, tq), lambda b, i: (b, 0, i)),
                  pl.BlockSpec((1, s, w), lambda b, i: (b, 0, 0)),
                  pl.BlockSpec((1, s, LANES), lambda b, i: (b, 0, 0)),
                  pl.BlockSpec((1, w, s), lambda b, i: (b, 0, 0)),
                  pl.BlockSpec((1, tq, w), lambda b, i: (b, i, 0))],
        out_specs=pl.BlockSpec((1, tq, w), lambda b, i: (b, i, 0)),
        out_shape=jax.ShapeDtypeStruct((nb, s, w), BF16),
        scratch_shapes=[pltpu.VMEM((ATTN_HEADS, 2 * LANES, tq), BF16),
                        pltpu.VMEM((ATTN_SLOTS, ATTN_TK, tq), F32),
                        pltpu.VMEM((ATTN_HEADS * SUBLANES, tq), F32),
                        pltpu.VMEM((ATTN_HEADS * SUBLANES, tq), F32),
                        pltpu.VMEM((w, tq), F32)],
        compiler_params=_cparams("arbitrary", "arbitrary"),
        name="attn",
    )(qt, k, caug, vt, sza)


def _merge_kernel(x_ref, ys_ref, szs_ref, oz_ref, sgs_ref, sga_ref, gate_ref,
                  wps_ref, wpa_ref, wout_ref, lng_ref, lnb_ref, o_ref,
                  wps_bf, wpa_bf, wout_bf, *, ts):
    @pl.when((pl.program_id(0) == 0) & (pl.program_id(1) == 0))
    def _():
        wps_bf[...] = wps_ref[...].astype(BF16)
        wpa_bf[...] = wpa_ref[...].astype(BF16)
        wout_bf[...] = wout_ref[...].astype(BF16)

    gate = gate_ref[0, 0]
    for c in range(ts // MERGE_SUB):
        r = slice(c * MERGE_SUB, (c + 1) * MERGE_SUB)
        a = (ys_ref[0, r, :].astype(F32) * szs_ref[0, r, :].astype(F32)).astype(BF16)
        m1 = jnp.dot(a, wps_bf[...], preferred_element_type=F32) * sgs_ref[0, r, :].astype(F32)
        m2 = (jnp.dot(oz_ref[0, r, :], wpa_bf[...], preferred_element_type=F32)
              * sga_ref[0, r, :].astype(F32))
        merged = (m1 + m2).astype(BF16)
        sub = jnp.dot(merged, wout_bf[...], preferred_element_type=F32) * gate
        y = DEEPNORM_ALPHA * x_ref[0, r, :] + sub
        mu = jnp.mean(y, axis=-1, keepdims=True)
        yc = y - mu
        var = jnp.mean(yc * yc, axis=-1, keepdims=True)
        o_ref[0, r, :] = yc * lax.rsqrt(var + LN_EPS) * lng_ref[...] + lnb_ref[...]


def _mod_spec(part):
    return pl.BlockSpec((1, 1, 1, D_MODEL), lambda i, b: (b, part, 0, 0))


def _merge(x, ys_t, szs, oz, sgs, sga, mod4, wps, wpa, wout, ln_g, ln_b, *, ts):
    nb, s, d = x.shape
    tok3 = lambda w: pl.BlockSpec((1, ts, w), lambda i, b: (b, i, 0))
    const2 = lambda a: pl.BlockSpec(a.shape, lambda i, b: (0, 0), pipeline_mode=pl.Buffered(1))
    return pl.pallas_call(
        functools.partial(_merge_kernel, ts=ts),
        grid=(s // ts, nb),
        in_specs=[tok3(d), tok3(SSM_WIDTH),
                  tok3(SSM_WIDTH), tok3(ATTN_WIDTH), tok3(d), tok3(d),
                  _mod_spec(2),
                  const2(wps), const2(wpa), const2(wout), const2(ln_g), const2(ln_b)],
        out_specs=tok3(d),
        out_shape=jax.ShapeDtypeStruct((nb, s, d), x.dtype),
        scratch_shapes=[pltpu.VMEM(wps.shape, BF16), pltpu.VMEM(wpa.shape, BF16),
                        pltpu.VMEM(wout.shape, BF16)],
        compiler_params=_cparams("arbitrary", "arbitrary"),
        name="merge",
    )(x, ys_t, szs, oz, sgs, sga, mod4, wps, wpa, wout, ln_g, ln_b)


def _block_diag_tiles(blocks):
    eye = jnp.eye(S5_TILE_GROUPS, dtype=blocks.dtype)
    t, g, r, c = blocks.shape
    return jnp.einsum('tgrc,gG->tgrGc', blocks, eye).reshape(t, g * r, g * c)


def kernel(x, c, w_ada, b_ada, w_in, b_f, lam_re, lam_im, log_dt, ssm_b_re, ssm_b_im,
           ssm_c_re, ssm_c_im, ssm_d, w_glu, w_proj_ssm, w_proj_attn, w_out, ln_g, ln_b):
    nb, s, d = x.shape
    l = 0
    mod4 = _ada(c, w_ada[l], b_ada[l]).reshape(nb, 3, 1, d)

    w_main, w_t, w_f3 = _wprep(jnp.swapaxes(w_in[l], 0, 1))
    bfv = b_f[l].astype(F32)
    b_f3 = jnp.concatenate([bfv, bfv, bfv, jnp.zeros((LANES - 3 * ATTN_HEADS,), F32)])[None, :]

    (xs_t, szs, k, sza, sgs, sga, qt, vt, caug) = _inproj(
        x, mod4, w_main, w_t, w_f3, b_f3, ts=512)

    abr, abi, bbr, bbi, ncim = _zoh(lam_re[l], lam_im[l], log_dt[l], ssm_b_re[l], ssm_b_im[l],
                                    ssm_c_im[l])
    g4 = (S5_TILES, S5_TILE_GROUPS)
    bbr = bbr.reshape(g4 + (SSM_GROUP, SSM_STATE))
    bbi = bbi.reshape(g4 + (SSM_GROUP, SSM_STATE))
    bmat = jnp.concatenate([_block_diag_tiles(bbr), _block_diag_tiles(bbi)], axis=2).astype(BF16)
    cre_t = ssm_c_re[l].reshape(g4 + (SSM_GROUP, SSM_STATE)).transpose(0, 1, 3, 2)
    ncim_t = ncim.reshape(g4 + (SSM_GROUP, SSM_STATE)).transpose(0, 1, 3, 2)
    cmat = jnp.concatenate([_block_diag_tiles(cre_t), _block_diag_tiles(ncim_t)],
                           axis=1).astype(BF16)
    a_re = abr.reshape(SSM_GROUPS, SSM_GROUP, SSM_STATE)[:, 0, :].reshape(1, SSM_GROUPS * SSM_STATE)
    a_im = abi.reshape(SSM_GROUPS, SSM_GROUP, SSM_STATE)[:, 0, :].reshape(1, SSM_GROUPS * SSM_STATE)

    ys_t = _s5(xs_t, bmat, cmat, a_re, a_im,
               ssm_d[l].reshape(1, SSM_WIDTH).astype(F32), w_glu[l].astype(BF16), steps=128, sub=32)

    oz = _attn(qt, k, caug, vt, sza, tq=1024)

    return _merge(x, ys_t, szs, oz, sgs, sga, mod4,
                  w_proj_ssm[l], w_proj_attn[l], w_out[l],
                  ln_g[l].reshape(1, d).astype(F32), ln_b[l].astype(F32).reshape(1, d), ts=1024)
```

```python
import functools
import math

import jax
import jax.numpy as jnp
from jax import lax
from jax.experimental import pallas as pl
from jax.experimental.pallas import tpu as pltpu

D_MODEL = 1024
BATCH = 8
SEQ = 2048
SSM_WIDTH = 512
SSM_GROUP = 16
SSM_GROUPS = 32
SSM_STATE = 64
ATTN_HEADS = 8
ATTN_HEAD_DIM = 64
ATTN_WIDTH = 512
LN_EPS = 1e-5
DEEPNORM_ALPHA = 2.0 ** 0.25

F32 = jnp.float32
BF16 = jnp.bfloat16

LANES = 128
SUBLANES = 8
VMEM_LIMIT = 52 * 1024 * 1024

S5_TILES = 4
S5_TILE_GROUPS = SSM_GROUPS // S5_TILES
S5_HALF = S5_TILE_GROUPS * SSM_STATE
S5_TILE_LANES = 2 * S5_HALF
S5_TILE_IN = S5_TILE_GROUPS * SSM_GROUP
S5_STATE_LANES = S5_TILES * S5_TILE_LANES

C_XS, C_ZS, C_K, C_ZA, C_GS, C_GA, C_END = 0, 512, 1024, 1536, 2048, 3072, 4096

NEG_BIG = -1e30
LOG2E = math.log2(math.e)

INPROJ_SUB = 256
MERGE_SUB = 256
ATTN_TK = 256
ATTN_LOOKAHEAD = 3
ATTN_SLOTS = 4
assert ATTN_SLOTS > ATTN_LOOKAHEAD and ATTN_HEADS % ATTN_SLOTS == 0


def _sigmoid(x):
    return 1.0 / (1.0 + jnp.exp(-x))


def _silu(x):
    return x * _sigmoid(x)


def _cparams(*sem):
    return pltpu.CompilerParams(dimension_semantics=sem, vmem_limit_bytes=VMEM_LIMIT)


def _ada_kernel(c_ref, w_ref, b_ref, o_ref):
    ca = _silu(c_ref[...]).astype(BF16)
    o_ref[...] = jnp.dot(ca, w_ref[...].astype(BF16), preferred_element_type=F32) + b_ref[...]


def _ada(c, w_ada, b_ada):
    n = w_ada.shape[1]
    tn = 512
    return pl.pallas_call(
        _ada_kernel,
        grid=(n // tn,),
        in_specs=[pl.BlockSpec((BATCH, D_MODEL), lambda j: (0, 0)),
                  pl.BlockSpec((D_MODEL, tn), lambda j: (0, j)),
                  pl.BlockSpec((1, tn), lambda j: (0, j))],
        out_specs=pl.BlockSpec((BATCH, tn), lambda j: (0, j)),
        out_shape=jax.ShapeDtypeStruct((BATCH, n), F32),
        compiler_params=_cparams("arbitrary"),
        name="ada",
    )(c, w_ada, b_ada.reshape(1, n))


def _zoh_kernel(lr_ref, li_ref, ldt_ref, br_ref, bi_ref, cre_ref, cim_ref,
                bmat_ref, cmat_ref, are_ref, aim_ref):
    lr = lr_ref[...]
    li = li_ref[...]
    dt = jnp.exp(ldt_ref[...])
    mag = jnp.exp(lr * dt)
    abr = mag * jnp.cos(li * dt)
    abi = mag * jnp.sin(li * dt)
    den = lr * lr + li * li
    cr = ((abr - 1.0) * lr + abi * li) / den
    ci = (abi * lr - (abr - 1.0) * li) / den

    tg, gh, gp = S5_TILE_GROUPS, SSM_GROUP, SSM_STATE
    rep_lanes = (lax.broadcasted_iota(jnp.int32, (gp, tg * gp), 1) % gp
                 == lax.broadcasted_iota(jnp.int32, (gp, tg * gp), 0))
    rep_rows = (lax.broadcasted_iota(jnp.int32, (tg * gp, gp), 0) % gp
                == lax.broadcasted_iota(jnp.int32, (tg * gp, gp), 1))
    rep_lanes_bf = jnp.where(rep_lanes, 1.0, 0.0).astype(BF16)
    rep_rows_bf = jnp.where(rep_rows, 1.0, 0.0).astype(BF16)
    rep_lanes_f32 = jnp.where(rep_lanes, 1.0, 0.0)
    b_diag = (lax.broadcasted_iota(jnp.int32, (tg * gh, tg * gp), 0) // gh
              == lax.broadcasted_iota(jnp.int32, (tg * gh, tg * gp), 1) // gp)
    c_diag = (lax.broadcasted_iota(jnp.int32, (tg * gp, tg * gh), 0) // gp
              == lax.broadcasted_iota(jnp.int32, (tg * gp, tg * gh), 1) // gh)
    a_diag = (lax.broadcasted_iota(jnp.int32, (tg, tg * gp), 0)
              == lax.broadcasted_iota(jnp.int32, (tg, tg * gp), 1) // gp)

    for j in range(S5_TILES):
        bbr, bbi, crs, cis = [], [], [], []
        for g in range(j * tg, (j + 1) * tg):
            crg, cig = cr[g:g + 1, :], ci[g:g + 1, :]
            br, bi = br_ref[g], bi_ref[g]
            bbr.append(crg * br - cig * bi)
            bbi.append(crg * bi + cig * br)
            crs.append(cre_ref[g])
            cis.append(-cim_ref[g])
        halves = []
        for blk in (jnp.concatenate(bbr, axis=0), jnp.concatenate(bbi, axis=0)):
            tiled = jnp.dot(blk.astype(BF16), rep_lanes_bf, preferred_element_type=F32)
            halves.append(jnp.where(b_diag, tiled, 0.0))
        bmat_ref[j] = jnp.concatenate(halves, axis=1).astype(BF16)
        halves = []
        for blk in (jnp.concatenate(crs, axis=0), jnp.concatenate(cis, axis=0)):
            tiled = lax.dot_general(rep_rows_bf, blk.astype(BF16), (((1,), (1,)), ((), ())),
                                    preferred_element_type=F32)
            halves.append(jnp.where(c_diag, tiled, 0.0))
        cmat_ref[j] = jnp.concatenate(halves, axis=0).astype(BF16)
        for a, a_ref in ((abr, are_ref), (abi, aim_ref)):
            tiled = jnp.dot(a[j * tg:(j + 1) * tg, :], rep_lanes_f32, preferred_element_type=F32,
                            precision=lax.Precision.HIGHEST)
            a_ref[:, j * S5_HALF:(j + 1) * S5_HALF] = jnp.sum(
                jnp.where(a_diag, tiled, 0.0), axis=0, keepdims=True)


def _zoh(lam_re, lam_im, log_dt, b_re, b_im, c_re, c_im):
    swap = lambda a: jnp.swapaxes(a, 1, 2)
    return pl.pallas_call(
        _zoh_kernel,
        out_shape=(jax.ShapeDtypeStruct((S5_TILES, S5_TILE_IN, S5_TILE_LANES), BF16),
                   jax.ShapeDtypeStruct((S5_TILES, S5_TILE_LANES, S5_TILE_IN), BF16),
                   jax.ShapeDtypeStruct((1, SSM_GROUPS * SSM_STATE), F32),
                   jax.ShapeDtypeStruct((1, SSM_GROUPS * SSM_STATE), F32)),
        name="zoh",
    )(lam_re, lam_im, log_dt.reshape(SSM_GROUPS, 1), swap(b_re), swap(b_im), c_re, c_im)


O_XS, O_ZS, O_Q, O_K, O_V, O_F, O_ZA, O_GS, O_GA, O_END = (
    0, 512, 1024, 1536, 2048, 2560, 2568, 3080, 4104, 5128)


WPREP_CHUNK = 256


def _wprep_kernel(wt_ref, wm_ref, wqv_ref, wf_ref):
    def put(c_lo, o_lo, n):
        for s in range(0, n, WPREP_CHUNK):
            wm_ref[:, c_lo + s:c_lo + s + WPREP_CHUNK] = (
                wt_ref[o_lo + s:o_lo + s + WPREP_CHUNK, :].T.astype(BF16))

    put(C_XS, O_XS, O_Q - O_XS)
    put(C_K, O_K, O_V - O_K)
    put(C_ZA, O_ZA, O_GS - O_ZA)
    put(C_GS, O_GS, O_END - O_GS)
    wqv_ref[0:ATTN_WIDTH, :] = wt_ref[O_Q:O_K, :].astype(BF16)
    wqv_ref[ATTN_WIDTH:2 * ATTN_WIDTH, :] = wt_ref[O_V:O_F, :].astype(BF16)
    f8 = wt_ref[O_F:O_ZA, :]
    blk = jnp.concatenate([f8, f8, f8, jnp.zeros((LANES - 3 * ATTN_HEADS, f8.shape[1]), F32)], axis=0)
    wf_ref[...] = blk.T.astype(BF16)


def _wprep(wt):
    n, d = wt.shape
    return pl.pallas_call(
        _wprep_kernel,
        out_shape=[jax.ShapeDtypeStruct((d, C_END), BF16),
                   jax.ShapeDtypeStruct((2 * ATTN_WIDTH, d), BF16),
                   jax.ShapeDtypeStruct((d, LANES), BF16)],
        compiler_params=pltpu.CompilerParams(vmem_limit_bytes=VMEM_LIMIT),
        name="wprep",
    )(wt)


def _inproj_kernel(x_ref, scale_ref, shift_ref, wm_ref, wt_ref, wf_ref, bf_ref, tri_ref,
                   xs_ref, szs_ref, k_ref, sza_ref, sgs_ref, sga_ref, qt_ref, vt_ref, caug_ref,
                   carry_ref, *, ts):
    i = pl.program_id(0)
    b = pl.program_id(1)

    @pl.when(i == 0)
    def _():
        carry_ref[b] = jnp.zeros(carry_ref.shape[1:], F32)

    carry = carry_ref[b][0:1, :]
    for c in range(ts // INPROJ_SUB):
        r = slice(c * INPROJ_SUB, (c + 1) * INPROJ_SUB)
        x = x_ref[0, r, :]
        mu = jnp.mean(x, axis=-1, keepdims=True)
        xc = x - mu
        var = jnp.mean(xc * xc, axis=-1, keepdims=True)
        xn = xc * lax.rsqrt(var + LN_EPS)
        u = xn * (1.0 + scale_ref[0, 0]) + shift_ref[0, 0]
        ub = u.astype(BF16)

        def mm(lo, hi, ub=ub):
            return jnp.dot(ub, wm_ref[:, lo:hi], preferred_element_type=F32)

        f = jnp.dot(ub, wf_ref[...], preferred_element_type=F32) + bf_ref[...]
        lf = jnp.minimum(f, 0.0) - jnp.log(1.0 + jnp.exp(-jnp.abs(f)))
        l1 = lf.astype(BF16)
        r1 = lf - l1.astype(F32)
        l2 = r1.astype(BF16)
        l3 = (r1 - l2.astype(F32)).astype(BF16)

        xs = mm(C_XS, C_ZS)
        for j in range(S5_TILES):
            xs_ref[j, pl.ds(b + c * INPROJ_SUB * SUBLANES, INPROJ_SUB, stride=SUBLANES), :] = (
                xs[:, j * LANES:(j + 1) * LANES])
        szs_ref[0, r, :] = _silu(mm(C_ZS, C_K)).astype(BF16)

        tri = tri_ref[...]
        cf = (jnp.dot(tri, l1, preferred_element_type=F32)
              + jnp.dot(tri, l2, preferred_element_type=F32)
              + jnp.dot(tri, l3, preferred_element_type=F32)) + carry
        carry = cf[INPROJ_SUB - 1:INPROJ_SUB, :]

        k_ref[0, r, :] = mm(C_K, C_ZA).astype(BF16)
        sza_ref[0, r, :] = _silu(mm(C_ZA, C_GS)).astype(BF16)

        cf2 = cf * LOG2E
        c1 = cf2.astype(BF16).astype(F32)
        rem = cf2 - c1
        c2 = rem.astype(BF16).astype(F32)
        c3 = rem - c2
        lane = lax.broadcasted_iota(jnp.int32, cf.shape, 1)
        parts = jnp.where(lane < 8, c1, jnp.where(lane < 16, c2, jnp.where(lane < 24, c3, 0.0)))
        caug_ref[0, r, :] = (-parts).astype(BF16)

        sgs_ref[0, r, :] = _sigmoid(mm(C_GS, C_GA)).astype(BF16)
        sga_ref[0, r, :] = _sigmoid(mm(C_GA, C_END)).astype(BF16)

        qv = lax.dot_general(wt_ref[...], ub, (((1,), (1,)), ((), ())),
                             preferred_element_type=F32)
        qt_ref[0, :, r] = (qv[:ATTN_WIDTH] * (LOG2E * ATTN_HEAD_DIM ** -0.5)).astype(BF16)
        vt_ref[0, :, r] = qv[ATTN_WIDTH:].astype(BF16)
    carry_ref[b] = jnp.broadcast_to(carry, carry_ref.shape[1:])


def _inproj(x, mod4, w_main, w_t, w_f3, b_f3, *, ts):
    nb, s, d = x.shape
    nt = s // ts
    tri = jnp.tril(jnp.ones((INPROJ_SUB, INPROJ_SUB), BF16))
    tok3 = lambda w: pl.BlockSpec((1, ts, w), lambda i, b: (b, i, 0))
    const2 = lambda a: pl.BlockSpec(a.shape, lambda i, b: (0, 0), pipeline_mode=pl.Buffered(1))
    bf = lambda shape: jax.ShapeDtypeStruct(shape, BF16)
    return pl.pallas_call(
        functools.partial(_inproj_kernel, ts=ts),
        grid=(nt, nb),
        in_specs=[tok3(d), _mod_spec(1), _mod_spec(0),
                  const2(w_main), const2(w_t), const2(w_f3), const2(b_f3), const2(tri)],
        out_specs=[pl.BlockSpec((S5_TILES, ts * nb, LANES), lambda i, b: (0, i, 0)),
                   tok3(SSM_WIDTH), tok3(ATTN_WIDTH), tok3(ATTN_WIDTH),
                   tok3(D_MODEL), tok3(D_MODEL),
                   pl.BlockSpec((1, ATTN_WIDTH, ts), lambda i, b: (b, 0, i)),
                   pl.BlockSpec((1, ATTN_WIDTH, ts), lambda i, b: (b, 0, i)),
                   tok3(LANES)],
        out_shape=[jax.ShapeDtypeStruct((S5_TILES, s * nb, LANES), F32),
                   bf((nb, s, SSM_WIDTH)), bf((nb, s, ATTN_WIDTH)), bf((nb, s, ATTN_WIDTH)),
                   bf((nb, s, D_MODEL)), bf((nb, s, D_MODEL)),
                   bf((nb, ATTN_WIDTH, s)), bf((nb, ATTN_WIDTH, s)),
                   bf((nb, s, LANES))],
        scratch_shapes=[pltpu.VMEM((nb, SUBLANES, LANES), F32)],
        compiler_params=_cparams("arbitrary", "arbitrary"),
        name="inproj",
    )(x, mod4, mod4, w_main, w_t, w_f3, b_f3, tri)


def _gelu_tanh(x):
    return 0.5 * x * (1.0 + jnp.tanh(math.sqrt(2.0 / math.pi) * (x + 0.044715 * (x * x * x))))


def _s5_kernel(xs_ref, bmat_ref, cmat_ref, are_ref, aim_ref, d_ref, wglu_ref,
               y_ref, buf_ref, h_ref, ybuf_ref, *, steps, sub):
    @pl.when(pl.program_id(0) == 0)
    def _():
        h_ref[...] = jnp.zeros_like(h_ref)

    srows = sub * SUBLANES

    def expand(c):
        r = slice(c * srows, (c + 1) * srows)
        for j in range(S5_TILES):
            lo = j * S5_TILE_LANES
            buf_ref[r, lo:lo + S5_TILE_LANES] = jnp.dot(
                xs_ref[j, r, :].astype(BF16), bmat_ref[j], preferred_element_type=F32)

    def recur(c, state):
        new_state = []
        for j in range(S5_TILES):
            re = slice(j * S5_TILE_LANES, j * S5_TILE_LANES + S5_HALF)
            im = slice(j * S5_TILE_LANES + S5_HALF, (j + 1) * S5_TILE_LANES)
            ar = jnp.broadcast_to(are_ref[:, j * S5_HALF:(j + 1) * S5_HALF], (SUBLANES, S5_HALF))
            ai = jnp.broadcast_to(aim_ref[:, j * S5_HALF:(j + 1) * S5_HALF], (SUBLANES, S5_HALF))
            hr, hi = state[j]
            for t in range(sub):
                rt = slice(c * srows + t * SUBLANES, c * srows + (t + 1) * SUBLANES)
                nhr = ar * hr - ai * hi + buf_ref[rt, re]
                nhi = ar * hi + ai * hr + buf_ref[rt, im]
                buf_ref[rt, re] = nhr
                buf_ref[rt, im] = nhi
                hr, hi = nhr, nhi
            new_state.append((hr, hi))
        return new_state

    def project(c):
        r = slice(c * srows, (c + 1) * srows)
        ys = []
        for j in range(S5_TILES):
            lo = j * S5_TILE_LANES
            hb = buf_ref[r, lo:lo + S5_TILE_LANES].astype(BF16)
            ys.append(jnp.dot(hb, cmat_ref[j], preferred_element_type=F32)
                      + d_ref[:, j * LANES:(j + 1) * LANES] * xs_ref[j, r, :])
        g = _gelu_tanh(jnp.concatenate(ys, axis=1)).astype(BF16)
        ab = jnp.dot(g, wglu_ref[...], preferred_element_type=F32)
        out = ab[:, :SSM_WIDTH] * _sigmoid(ab[:, SSM_WIDTH:])
        for j in range(S5_TILES):
            ybuf_ref[j] = out[:, j * LANES:(j + 1) * LANES]
        for b in range(BATCH):
            yb = jnp.concatenate([ybuf_ref[j, pl.ds(b, sub, stride=SUBLANES), :]
                                  for j in range(S5_TILES)], axis=1)
            y_ref[b, c * sub:(c + 1) * sub, :] = yb.astype(BF16)

    state = [(h_ref[:, j * S5_TILE_LANES:j * S5_TILE_LANES + S5_HALF],
              h_ref[:, j * S5_TILE_LANES + S5_HALF:(j + 1) * S5_TILE_LANES])
             for j in range(S5_TILES)]
    nsub = steps // sub
    expand(0)
    for c in range(nsub):
        if c + 1 < nsub:
            expand(c + 1)
        state = recur(c, state)
        project(c)
    for j in range(S5_TILES):
        h_ref[:, j * S5_TILE_LANES:j * S5_TILE_LANES + S5_HALF] = state[j][0]
        h_ref[:, j * S5_TILE_LANES + S5_HALF:(j + 1) * S5_TILE_LANES] = state[j][1]


def _s5(xs_t, bmat, cmat, a_re, a_im, d_skip, w_glu, *, steps, sub):
    t = xs_t.shape[1]
    rows = steps * SUBLANES
    const = lambda a: pl.BlockSpec(a.shape, lambda i: (0,) * a.ndim)
    tiles = pl.BlockSpec((S5_TILES, rows, LANES), lambda i: (0, i, 0))
    return pl.pallas_call(
        functools.partial(_s5_kernel, steps=steps, sub=sub),
        grid=(t // rows,),
        in_specs=[tiles,
                  const(bmat), const(cmat), const(a_re), const(a_im), const(d_skip), const(w_glu)],
        out_specs=pl.BlockSpec((BATCH, steps, SSM_WIDTH), lambda i: (0, i, 0)),
        out_shape=jax.ShapeDtypeStruct((BATCH, t // SUBLANES, SSM_WIDTH), BF16),
        scratch_shapes=[pltpu.VMEM((rows, S5_STATE_LANES), F32),
                        pltpu.VMEM((SUBLANES, S5_STATE_LANES), F32),
                        pltpu.VMEM((S5_TILES, sub * SUBLANES, LANES), F32)],
        compiler_params=_cparams("arbitrary"),
        name="s5",
    )(xs_t, bmat, cmat, a_re, a_im, d_skip, w_glu)


def _attn_kernel(qt_ref, k_ref, caug_ref, vt_ref, sza_ref, o_ref,
                 rhs_ref, s_ref, m_ref, l_ref, acc_ref, *, tq):
    qi = pl.program_id(1)
    tk = ATTN_TK
    nd = tq // tk
    hd = ATTN_HEAD_DIM
    row = lax.broadcasted_iota(jnp.int32, (LANES, tq), 0)

    for h in range(ATTN_HEADS):
        j = h // 2
        q2 = qt_ref[0, j * LANES:(j + 1) * LANES, :]
        rhs_ref[h, 0:LANES, :] = jnp.where((row // hd) == (h % 2), q2, jnp.zeros_like(q2))
        rhs_ref[h, LANES:2 * LANES, :] = jnp.where(
            (row == h) | (row == 8 + h) | (row == 16 + h), 1.0, 0.0).astype(BF16)
    m_ref[...] = jnp.full(m_ref.shape, -jnp.inf, F32)
    l_ref[...] = jnp.zeros_like(l_ref)
    acc_ref[...] = jnp.zeros_like(acc_ref)

    def scores(k0, h, diag=None):
        j = h // 2
        lo = 0 if diag is None else diag
        lhs = jnp.concatenate([k_ref[0, pl.ds(k0, tk), j * LANES:(j + 1) * LANES],
                               caug_ref[0, pl.ds(k0, tk), :]], axis=1)
        s_ref[h % ATTN_SLOTS, :, lo:tq] = jnp.dot(lhs, rhs_ref[h, :, lo:tq],
                                                  preferred_element_type=F32)

    def update(k0, h, diag=None):
        lo = 0 if diag is None else diag
        s = s_ref[h % ATTN_SLOTS, :, lo:tq]
        if diag is not None:
            kpos = lax.broadcasted_iota(jnp.int32, s.shape, 0)
            qpos = lax.broadcasted_iota(jnp.int32, s.shape, 1)
            s = jnp.where(kpos <= qpos, s, NEG_BIG)
        hs = slice(h * SUBLANES, h * SUBLANES + 1)
        m = m_ref[hs, lo:tq]
        m_new = jnp.maximum(m, jnp.max(s, axis=0, keepdims=True))
        alpha = jnp.exp2(m - m_new)
        p = jnp.exp2(s - m_new).astype(BF16)
        m_ref[hs, lo:tq] = m_new
        vt1 = jnp.concatenate([vt_ref[0, h * hd:(h + 1) * hd, pl.ds(k0, tk)],
                               jnp.ones((2 * SUBLANES, tk), BF16)], axis=0)
        pv = jnp.dot(vt1, p, preferred_element_type=F32)
        l_ref[hs, lo:tq] = alpha * l_ref[hs, lo:tq] + pv[hd:hd + 1, :]
        acc_ref[h * hd:(h + 1) * hd, lo:tq] = (alpha * acc_ref[h * hd:(h + 1) * hd, lo:tq]
                                               + pv[:hd, :])

    def body(kj, carry):
        k0 = pl.multiple_of(kj * tk, tk)
        k1 = pl.multiple_of(kj * tk + tk, tk)
        for h in range(ATTN_HEADS):
            ha = h + ATTN_LOOKAHEAD
            if ha < ATTN_HEADS:
                scores(k0, ha)
            else:
                scores(k1, ha - ATTN_HEADS)
            update(k0, h)
        return carry

    for h in range(ATTN_LOOKAHEAD):
        scores(0, h)
    lax.fori_loop(0, qi * nd, body, 0)
    for d in range(nd):
        kd = pl.multiple_of(qi * tq + d * tk, tk)
        for h in range(ATTN_HEADS):
            ha = h + ATTN_LOOKAHEAD
            if ha < ATTN_HEADS:
                scores(kd, ha, d * tk)
            elif d + 1 < nd:
                scores(pl.multiple_of(kd + tk, tk), ha - ATTN_HEADS, (d + 1) * tk)
            update(kd, h, d * tk)

    for h in range(ATTN_HEADS):
        acc_ref[h * hd:(h + 1) * hd, :] = (acc_ref[h * hd:(h + 1) * hd, :]
                                           / l_ref[h * SUBLANES:h * SUBLANES + 1, :])
    o = acc_ref[...].T
    o_ref[0] = (o * sza_ref[0].astype(F32)).astype(BF16)


def _attn(qt, k, caug, vt, sza, *, tq):
    nb, s, w = k.shape
    return pl.pallas_call(
        functools.partial(_attn_kernel, tq=tq),
        grid=(nb, s // tq),
        in_specs=[pl.BlockSpec((1, w, tq), lambda b, i: (b, 0, i)),
                  pl.BlockSpec((1, s, w), lambda b, i: (b, 0, 0)),
                  pl.BlockSpec((1, s, LANES), lambda b, i: (b, 0, 0)),
                  pl.BlockSpec((1, w, s), lambda b, i: (b, 0, 0)),
                  pl.BlockSpec((1, tq, w), lambda b, i: (b, i, 0))],
        out_specs=pl.BlockSpec((1, tq, w), lambda b, i: (b, i, 0)),
        out_shape=jax.ShapeDtypeStruct((nb, s, w), BF16),
        scratch_shapes=[pltpu.VMEM((ATTN_HEADS, 2 * LANES, tq), BF16),
                        pltpu.VMEM((ATTN_SLOTS, ATTN_TK, tq), F32),
                        pltpu.VMEM((ATTN_HEADS * SUBLANES, tq), F32),
                        pltpu.VMEM((ATTN_HEADS * SUBLANES, tq), F32),
                        pltpu.VMEM((w, tq), F32)],
        compiler_params=_cparams("arbitrary", "arbitrary"),
        name="attn",
    )(qt, k, caug, vt, sza)


def _merge_kernel(x_ref, ys_ref, szs_ref, oz_ref, sgs_ref, sga_ref, gate_ref,
                  wps_ref, wpa_ref, wout_ref, lng_ref, lnb_ref, o_ref,
                  wps_bf, wpa_bf, wout_bf, *, ts):
    @pl.when((pl.program_id(0) == 0) & (pl.program_id(1) == 0))
    def _():
        wps_bf[...] = wps_ref[...].astype(BF16)
        wpa_bf[...] = wpa_ref[...].astype(BF16)
        wout_bf[...] = wout_ref[...].astype(BF16)

    gate = gate_ref[0, 0]
    for c in range(ts // MERGE_SUB):
        r = slice(c * MERGE_SUB, (c + 1) * MERGE_SUB)
        a = (ys_ref[0, r, :].astype(F32) * szs_ref[0, r, :].astype(F32)).astype(BF16)
        m1 = jnp.dot(a, wps_bf[...], preferred_element_type=F32) * sgs_ref[0, r, :].astype(F32)
        m2 = (jnp.dot(oz_ref[0, r, :], wpa_bf[...], preferred_element_type=F32)
              * sga_ref[0, r, :].astype(F32))
        merged = (m1 + m2).astype(BF16)
        sub = jnp.dot(merged, wout_bf[...], preferred_element_type=F32) * gate
        y = DEEPNORM_ALPHA * x_ref[0, r, :] + sub
        mu = jnp.mean(y, axis=-1, keepdims=True)
        yc = y - mu
        var = jnp.mean(yc * yc, axis=-1, keepdims=True)
        o_ref[0, r, :] = yc * lax.rsqrt(var + LN_EPS) * lng_ref[...] + lnb_ref[...]


def _mod_spec(part):
    return pl.BlockSpec((1, 1, 1, D_MODEL), lambda i, b: (b, part, 0, 0))


def _merge(x, ys_t, szs, oz, sgs, sga, mod4, wps, wpa, wout, ln_g, ln_b, *, ts):
    nb, s, d = x.shape
    tok3 = lambda w: pl.BlockSpec((1, ts, w), lambda i, b: (b, i, 0))
    const2 = lambda a: pl.BlockSpec(a.shape, lambda i, b: (0, 0), pipeline_mode=pl.Buffered(1))
    return pl.pallas_call(
        functools.partial(_merge_kernel, ts=ts),
        grid=(s // ts, nb),
        in_specs=[tok3(d), tok3(SSM_WIDTH),
                  tok3(SSM_WIDTH), tok3(ATTN_WIDTH), tok3(d), tok3(d),
                  _mod_spec(2),
                  const2(wps), const2(wpa), const2(wout), const2(ln_g), const2(ln_b)],
        out_specs=tok3(d),
        out_shape=jax.ShapeDtypeStruct((nb, s, d), x.dtype),
        scratch_shapes=[pltpu.VMEM(wps.shape, BF16), pltpu.VMEM(wpa.shape, BF16),
                        pltpu.VMEM(wout.shape, BF16)],
        compiler_params=_cparams("arbitrary", "arbitrary"),
        name="merge",
    )(x, ys_t, szs, oz, sgs, sga, mod4, wps, wpa, wout, ln_g, ln_b)


def kernel(x, c, w_ada, b_ada, w_in, b_f, lam_re, lam_im, log_dt, ssm_b_re, ssm_b_im,
           ssm_c_re, ssm_c_im, ssm_d, w_glu, w_proj_ssm, w_proj_attn, w_out, ln_g, ln_b):
    nb, s, d = x.shape
    l = 0
    mod4 = _ada(c, w_ada[l], b_ada[l]).reshape(nb, 3, 1, d)

    w_main, w_t, w_f3 = _wprep(jnp.swapaxes(w_in[l], 0, 1))
    bfv = b_f[l].astype(F32)
    b_f3 = jnp.concatenate([bfv, bfv, bfv, jnp.zeros((LANES - 3 * ATTN_HEADS,), F32)])[None, :]

    (xs_t, szs, k, sza, sgs, sga, qt, vt, caug) = _inproj(
        x, mod4, w_main, w_t, w_f3, b_f3, ts=512)

    bmat, cmat, a_re, a_im = _zoh(lam_re[l], lam_im[l], log_dt[l], ssm_b_re[l], ssm_b_im[l],
                                  ssm_c_re[l], ssm_c_im[l])

    ys_t = _s5(xs_t, bmat, cmat, a_re, a_im,
               ssm_d[l].reshape(1, SSM_WIDTH).astype(F32), w_glu[l].astype(BF16), steps=128, sub=32)

    oz = _attn(qt, k, caug, vt, sza, tq=1024)

    return _merge(x, ys_t, szs, oz, sgs, sga, mod4,
                  w_proj_ssm[l], w_proj_attn[l], w_out[l],
                  ln_g[l].reshape(1, d).astype(F32), ln_b[l].astype(F32).reshape(1, d), ts=1024)
```

```python
import functools
import math

import jax
import jax.numpy as jnp
from jax import lax
from jax.experimental import pallas as pl
from jax.experimental.pallas import tpu as pltpu

D_MODEL = 1024
BATCH = 8
SEQ = 2048
SSM_WIDTH = 512
SSM_GROUP = 16
SSM_GROUPS = 32
SSM_STATE = 64
ATTN_HEADS = 8
ATTN_HEAD_DIM = 64
ATTN_WIDTH = 512
LN_EPS = 1e-5
DEEPNORM_ALPHA = 2.0 ** 0.25

F32 = jnp.float32
BF16 = jnp.bfloat16

LANES = 128
SUBLANES = 8
VMEM_LIMIT = 52 * 1024 * 1024

S5_TILES = 4
S5_TILE_GROUPS = SSM_GROUPS // S5_TILES
S5_HALF = S5_TILE_GROUPS * SSM_STATE
S5_TILE_LANES = 2 * S5_HALF
S5_TILE_IN = S5_TILE_GROUPS * SSM_GROUP
S5_STATE_LANES = S5_TILES * S5_TILE_LANES

C_XS, C_ZS, C_K, C_ZA, C_GS, C_GA, C_END = 0, 512, 1024, 1536, 2048, 3072, 4096

NEG_BIG = -1e30
LOG2E = math.log2(math.e)

INPROJ_SUB = 256
MERGE_SUB = 256
ATTN_TK = 256
ATTN_LOOKAHEAD = 3
ATTN_SLOTS = 4
assert ATTN_SLOTS > ATTN_LOOKAHEAD and ATTN_HEADS % ATTN_SLOTS == 0


def _sigmoid(x):
    return 1.0 / (1.0 + jnp.exp(-x))


def _silu(x):
    return x * _sigmoid(x)


def _cparams(*sem):
    return pltpu.CompilerParams(dimension_semantics=sem, vmem_limit_bytes=VMEM_LIMIT)


def _ada_kernel(c_ref, w_ref, b_ref, o_ref):
    ca = _silu(c_ref[...]).astype(BF16)
    o_ref[...] = jnp.dot(ca, w_ref[...].astype(BF16), preferred_element_type=F32) + b_ref[...]


def _ada(c, w_ada, b_ada):
    n = w_ada.shape[1]
    tn = 1024
    return pl.pallas_call(
        _ada_kernel,
        grid=(n // tn,),
        in_specs=[pl.BlockSpec((BATCH, D_MODEL), lambda j: (0, 0)),
                  pl.BlockSpec((D_MODEL, tn), lambda j: (0, j)),
                  pl.BlockSpec((1, tn), lambda j: (0, j))],
        out_specs=pl.BlockSpec((BATCH, tn), lambda j: (0, j)),
        out_shape=jax.ShapeDtypeStruct((BATCH, n), F32),
        compiler_params=_cparams("arbitrary"),
        name="ada",
    )(c, w_ada, b_ada.reshape(1, n))


def _zoh_kernel(lr_ref, li_ref, ldt_ref, br_ref, bi_ref, cre_ref, cim_ref,
                bmat_ref, cmat_ref, are_ref, aim_ref):
    lr = lr_ref[...]
    li = li_ref[...]
    dt = jnp.exp(ldt_ref[...])
    mag = jnp.exp(lr * dt)
    abr = mag * jnp.cos(li * dt)
    abi = mag * jnp.sin(li * dt)
    den = lr * lr + li * li
    cr = ((abr - 1.0) * lr + abi * li) / den
    ci = (abi * lr - (abr - 1.0) * li) / den

    tg, gh, gp = S5_TILE_GROUPS, SSM_GROUP, SSM_STATE
    rep_lanes = (lax.broadcasted_iota(jnp.int32, (gp, tg * gp), 1) % gp
                 == lax.broadcasted_iota(jnp.int32, (gp, tg * gp), 0))
    rep_rows = (lax.broadcasted_iota(jnp.int32, (tg * gp, gp), 0) % gp
                == lax.broadcasted_iota(jnp.int32, (tg * gp, gp), 1))
    rep_lanes_bf = jnp.where(rep_lanes, 1.0, 0.0).astype(BF16)
    rep_rows_bf = jnp.where(rep_rows, 1.0, 0.0).astype(BF16)
    rep_lanes_f32 = jnp.where(rep_lanes, 1.0, 0.0)
    b_diag = (lax.broadcasted_iota(jnp.int32, (tg * gh, tg * gp), 0) // gh
              == lax.broadcasted_iota(jnp.int32, (tg * gh, tg * gp), 1) // gp)
    c_diag = (lax.broadcasted_iota(jnp.int32, (tg * gp, tg * gh), 0) // gp
              == lax.broadcasted_iota(jnp.int32, (tg * gp, tg * gh), 1) // gh)
    a_diag = (lax.broadcasted_iota(jnp.int32, (tg, tg * gp), 0)
              == lax.broadcasted_iota(jnp.int32, (tg, tg * gp), 1) // gp)

    for j in range(S5_TILES):
        bbr, bbi, crs, cis = [], [], [], []
        for g in range(j * tg, (j + 1) * tg):
            crg, cig = cr[g:g + 1, :], ci[g:g + 1, :]
            br, bi = br_ref[g], bi_ref[g]
            bbr.append(crg * br - cig * bi)
            bbi.append(crg * bi + cig * br)
            crs.append(cre_ref[g])
            cis.append(-cim_ref[g])
        halves = []
        for blk in (jnp.concatenate(bbr, axis=0), jnp.concatenate(bbi, axis=0)):
            tiled = jnp.dot(blk.astype(BF16), rep_lanes_bf, preferred_element_type=F32)
            halves.append(jnp.where(b_diag, tiled, 0.0))
        bmat_ref[j] = jnp.concatenate(halves, axis=1).astype(BF16)
        halves = []
        for blk in (jnp.concatenate(crs, axis=0), jnp.concatenate(cis, axis=0)):
            tiled = lax.dot_general(rep_rows_bf, blk.astype(BF16), (((1,), (1,)), ((), ())),
                                    preferred_element_type=F32)
            halves.append(jnp.where(c_diag, tiled, 0.0))
        cmat_ref[j] = jnp.concatenate(halves, axis=0).astype(BF16)
        for a, a_ref in ((abr, are_ref), (abi, aim_ref)):
            tiled = jnp.dot(a[j * tg:(j + 1) * tg, :], rep_lanes_f32, preferred_element_type=F32,
                            precision=lax.Precision.HIGHEST)
            a_ref[:, j * S5_HALF:(j + 1) * S5_HALF] = jnp.sum(
                jnp.where(a_diag, tiled, 0.0), axis=0, keepdims=True)


def _zoh(lam_re, lam_im, log_dt, b_re, b_im, c_re, c_im):
    swap = lambda a: jnp.swapaxes(a, 1, 2)
    return pl.pallas_call(
        _zoh_kernel,
        out_shape=(jax.ShapeDtypeStruct((S5_TILES, S5_TILE_IN, S5_TILE_LANES), BF16),
                   jax.ShapeDtypeStruct((S5_TILES, S5_TILE_LANES, S5_TILE_IN), BF16),
                   jax.ShapeDtypeStruct((1, SSM_GROUPS * SSM_STATE), F32),
                   jax.ShapeDtypeStruct((1, SSM_GROUPS * SSM_STATE), F32)),
        name="zoh",
    )(lam_re, lam_im, log_dt.reshape(SSM_GROUPS, 1), swap(b_re), swap(b_im), c_re, c_im)


O_XS, O_ZS, O_Q, O_K, O_V, O_F, O_ZA, O_GS, O_GA, O_END = (
    0, 512, 1024, 1536, 2048, 2560, 2568, 3080, 4104, 5128)


WPREP_CHUNK = 256


def _wprep_kernel(wt_ref, wm_ref, wqv_ref, wf_ref):
    def put(c_lo, o_lo, n):
        for s in range(0, n, WPREP_CHUNK):
            wm_ref[:, c_lo + s:c_lo + s + WPREP_CHUNK] = (
                wt_ref[o_lo + s:o_lo + s + WPREP_CHUNK, :].T.astype(BF16))

    put(C_XS, O_XS, O_Q - O_XS)
    put(C_K, O_K, O_V - O_K)
    put(C_ZA, O_ZA, O_GS - O_ZA)
    put(C_GS, O_GS, O_END - O_GS)
    wqv_ref[0:ATTN_WIDTH, :] = wt_ref[O_Q:O_K, :].astype(BF16)
    wqv_ref[ATTN_WIDTH:2 * ATTN_WIDTH, :] = wt_ref[O_V:O_F, :].astype(BF16)
    f8 = wt_ref[O_F:O_ZA, :]
    blk = jnp.concatenate([f8, f8, f8, jnp.zeros((LANES - 3 * ATTN_HEADS, f8.shape[1]), F32)], axis=0)
    wf_ref[...] = blk.T.astype(BF16)


def _wprep(wt):
    n, d = wt.shape
    return pl.pallas_call(
        _wprep_kernel,
        out_shape=[jax.ShapeDtypeStruct((d, C_END), BF16),
                   jax.ShapeDtypeStruct((2 * ATTN_WIDTH, d), BF16),
                   jax.ShapeDtypeStruct((d, LANES), BF16)],
        compiler_params=pltpu.CompilerParams(vmem_limit_bytes=VMEM_LIMIT),
        name="wprep",
    )(wt)


def _inproj_kernel(x_ref, scale_ref, shift_ref, wm_ref, wt_ref, wf_ref, bf_ref, tri_ref,
                   xs_ref, szs_ref, k_ref, sza_ref, sgs_ref, sga_ref, qt_ref, vt_ref, caug_ref,
                   carry_ref, *, ts):
    i = pl.program_id(0)
    b = pl.program_id(1)

    @pl.when(i == 0)
    def _():
        carry_ref[b] = jnp.zeros(carry_ref.shape[1:], F32)

    carry = carry_ref[b][0:1, :]
    for c in range(ts // INPROJ_SUB):
        r = slice(c * INPROJ_SUB, (c + 1) * INPROJ_SUB)
        x = x_ref[0, r, :]
        mu = jnp.mean(x, axis=-1, keepdims=True)
        xc = x - mu
        var = jnp.mean(xc * xc, axis=-1, keepdims=True)
        xn = xc * lax.rsqrt(var + LN_EPS)
        u = xn * (1.0 + scale_ref[0, 0]) + shift_ref[0, 0]
        ub = u.astype(BF16)

        def mm(lo, hi, ub=ub):
            return jnp.dot(ub, wm_ref[:, lo:hi], preferred_element_type=F32)

        f = jnp.dot(ub, wf_ref[...], preferred_element_type=F32) + bf_ref[...]
        lf = jnp.minimum(f, 0.0) - jnp.log(1.0 + jnp.exp(-jnp.abs(f)))
        l1 = lf.astype(BF16)
        r1 = lf - l1.astype(F32)
        l2 = r1.astype(BF16)
        l3 = (r1 - l2.astype(F32)).astype(BF16)

        xs = mm(C_XS, C_ZS)
        for j in range(S5_TILES):
            xs_ref[j, pl.ds(b + c * INPROJ_SUB * SUBLANES, INPROJ_SUB, stride=SUBLANES), :] = (
                xs[:, j * LANES:(j + 1) * LANES])
        szs_ref[0, r, :] = _silu(mm(C_ZS, C_K)).astype(BF16)

        tri = tri_ref[...]
        cf = (jnp.dot(tri, l1, preferred_element_type=F32)
              + jnp.dot(tri, l2, preferred_element_type=F32)
              + jnp.dot(tri, l3, preferred_element_type=F32)) + carry
        carry = cf[INPROJ_SUB - 1:INPROJ_SUB, :]

        k_ref[0, r, :] = mm(C_K, C_ZA).astype(BF16)
        sza_ref[0, r, :] = _silu(mm(C_ZA, C_GS)).astype(BF16)

        cf2 = cf * LOG2E
        c1 = cf2.astype(BF16).astype(F32)
        rem = cf2 - c1
        c2 = rem.astype(BF16).astype(F32)
        c3 = rem - c2
        lane = lax.broadcasted_iota(jnp.int32, cf.shape, 1)
        parts = jnp.where(lane < 8, c1, jnp.where(lane < 16, c2, jnp.where(lane < 24, c3, 0.0)))
        caug_ref[0, r, :] = (-parts).astype(BF16)

        sgs_ref[0, r, :] = _sigmoid(mm(C_GS, C_GA)).astype(BF16)
        sga_ref[0, r, :] = _sigmoid(mm(C_GA, C_END)).astype(BF16)

        qv = lax.dot_general(wt_ref[...], ub, (((1,), (1,)), ((), ())),
                             preferred_element_type=F32)
        qt_ref[0, :, r] = (qv[:ATTN_WIDTH] * (LOG2E * ATTN_HEAD_DIM ** -0.5)).astype(BF16)
        vt_ref[0, :, r] = qv[ATTN_WIDTH:].astype(BF16)
    carry_ref[b] = jnp.broadcast_to(carry, carry_ref.shape[1:])


def _inproj(x, mod4, w_main, w_t, w_f3, b_f3, *, ts):
    nb, s, d = x.shape
    nt = s // ts
    tri = jnp.tril(jnp.ones((INPROJ_SUB, INPROJ_SUB), BF16))
    tok3 = lambda w: pl.BlockSpec((1, ts, w), lambda i, b: (b, i, 0))
    const2 = lambda a: pl.BlockSpec(a.shape, lambda i, b: (0, 0), pipeline_mode=pl.Buffered(1))
    bf = lambda shape: jax.ShapeDtypeStruct(shape, BF16)
    return pl.pallas_call(
        functools.partial(_inproj_kernel, ts=ts),
        grid=(nt, nb),
        in_specs=[tok3(d), _mod_spec(1), _mod_spec(0),
                  const2(w_main), const2(w_t), const2(w_f3), const2(b_f3), const2(tri)],
        out_specs=[pl.BlockSpec((S5_TILES, ts * nb, LANES), lambda i, b: (0, i, 0)),
                   tok3(SSM_WIDTH), tok3(ATTN_WIDTH), tok3(ATTN_WIDTH),
                   tok3(D_MODEL), tok3(D_MODEL),
                   pl.BlockSpec((1, ATTN_WIDTH, ts), lambda i, b: (b, 0, i)),
                   pl.BlockSpec((1, ATTN_WIDTH, ts), lambda i, b: (b, 0, i)),
                   tok3(LANES)],
        out_shape=[jax.ShapeDtypeStruct((S5_TILES, s * nb, LANES), F32),
                   bf((nb, s, SSM_WIDTH)), bf((nb, s, ATTN_WIDTH)), bf((nb, s, ATTN_WIDTH)),
                   bf((nb, s, D_MODEL)), bf((nb, s, D_MODEL)),
                   bf((nb, ATTN_WIDTH, s)), bf((nb, ATTN_WIDTH, s)),
                   bf((nb, s, LANES))],
        scratch_shapes=[pltpu.VMEM((nb, SUBLANES, LANES), F32)],
        compiler_params=_cparams("arbitrary", "arbitrary"),
        name="inproj",
    )(x, mod4, mod4, w_main, w_t, w_f3, b_f3, tri)


def _gelu_tanh(x):
    return 0.5 * x * (1.0 + jnp.tanh(math.sqrt(2.0 / math.pi) * (x + 0.044715 * (x * x * x))))


def _s5_kernel(xs_ref, bmat_ref, cmat_ref, are_ref, aim_ref, d_ref, wglu_ref,
               y_ref, buf_ref, h_ref, ybuf_ref, wglu_bf, *, steps, sub):
    @pl.when(pl.program_id(0) == 0)
    def _():
        h_ref[...] = jnp.zeros_like(h_ref)
        wglu_bf[...] = wglu_ref[...].astype(BF16)

    srows = sub * SUBLANES

    def expand(c):
        r = slice(c * srows, (c + 1) * srows)
        for j in range(S5_TILES):
            lo = j * S5_TILE_LANES
            buf_ref[r, lo:lo + S5_TILE_LANES] = jnp.dot(
                xs_ref[j, r, :].astype(BF16), bmat_ref[j], preferred_element_type=F32)

    def recur(c, state):
        new_state = []
        for j in range(S5_TILES):
            re = slice(j * S5_TILE_LANES, j * S5_TILE_LANES + S5_HALF)
            im = slice(j * S5_TILE_LANES + S5_HALF, (j + 1) * S5_TILE_LANES)
            ar = jnp.broadcast_to(are_ref[:, j * S5_HALF:(j + 1) * S5_HALF], (SUBLANES, S5_HALF))
            ai = jnp.broadcast_to(aim_ref[:, j * S5_HALF:(j + 1) * S5_HALF], (SUBLANES, S5_HALF))
            hr, hi = state[j]
            for t in range(sub):
                rt = slice(c * srows + t * SUBLANES, c * srows + (t + 1) * SUBLANES)
                nhr = ar * hr - ai * hi + buf_ref[rt, re]
                nhi = ar * hi + ai * hr + buf_ref[rt, im]
                buf_ref[rt, re] = nhr
                buf_ref[rt, im] = nhi
                hr, hi = nhr, nhi
            new_state.append((hr, hi))
        return new_state

    def project(c):
        r = slice(c * srows, (c + 1) * srows)
        ys = []
        for j in range(S5_TILES):
            lo = j * S5_TILE_LANES
            hb = buf_ref[r, lo:lo + S5_TILE_LANES].astype(BF16)
            ys.append(jnp.dot(hb, cmat_ref[j], preferred_element_type=F32)
                      + d_ref[:, j * LANES:(j + 1) * LANES] * xs_ref[j, r, :])
        g = _gelu_tanh(jnp.concatenate(ys, axis=1)).astype(BF16)
        ab = jnp.dot(g, wglu_bf[...], preferred_element_type=F32)
        out = ab[:, :SSM_WIDTH] * _sigmoid(ab[:, SSM_WIDTH:])
        for j in range(S5_TILES):
            ybuf_ref[j] = out[:, j * LANES:(j + 1) * LANES]
        for b in range(BATCH):
            yb = jnp.concatenate([ybuf_ref[j, pl.ds(b, sub, stride=SUBLANES), :]
                                  for j in range(S5_TILES)], axis=1)
            y_ref[b, c * sub:(c + 1) * sub, :] = yb.astype(BF16)

    state = [(h_ref[:, j * S5_TILE_LANES:j * S5_TILE_LANES + S5_HALF],
              h_ref[:, j * S5_TILE_LANES + S5_HALF:(j + 1) * S5_TILE_LANES])
             for j in range(S5_TILES)]
    nsub = steps // sub
    expand(0)
    for c in range(nsub):
        if c + 1 < nsub:
            expand(c + 1)
        state = recur(c, state)
        project(c)
    for j in range(S5_TILES):
        h_ref[:, j * S5_TILE_LANES:j * S5_TILE_LANES + S5_HALF] = state[j][0]
        h_ref[:, j * S5_TILE_LANES + S5_HALF:(j + 1) * S5_TILE_LANES] = state[j][1]


def _s5(xs_t, bmat, cmat, a_re, a_im, d_skip, w_glu, *, steps, sub):
    t = xs_t.shape[1]
    rows = steps * SUBLANES
    const = lambda a: pl.BlockSpec(a.shape, lambda i: (0,) * a.ndim)
    tiles = pl.BlockSpec((S5_TILES, rows, LANES), lambda i: (0, i, 0))
    return pl.pallas_call(
        functools.partial(_s5_kernel, steps=steps, sub=sub),
        grid=(t // rows,),
        in_specs=[tiles,
                  const(bmat), const(cmat), const(a_re), const(a_im), const(d_skip), const(w_glu)],
        out_specs=pl.BlockSpec((BATCH, steps, SSM_WIDTH), lambda i: (0, i, 0)),
        out_shape=jax.ShapeDtypeStruct((BATCH, t // SUBLANES, SSM_WIDTH), BF16),
        scratch_shapes=[pltpu.VMEM((rows, S5_STATE_LANES), F32),
                        pltpu.VMEM((SUBLANES, S5_STATE_LANES), F32),
                        pltpu.VMEM((S5_TILES, sub * SUBLANES, LANES), F32),
                        pltpu.VMEM(w_glu.shape, BF16)],
        compiler_params=_cparams("arbitrary"),
        name="s5",
    )(xs_t, bmat, cmat, a_re, a_im, d_skip, w_glu)


def _attn_kernel(qt_ref, k_ref, caug_ref, vt_ref, sza_ref, o_ref,
                 rhs_ref, s_ref, m_ref, l_ref, acc_ref, *, tq):
    qi = pl.program_id(1)
    tk = ATTN_TK
    nd = tq // tk
    hd = ATTN_HEAD_DIM
    row = lax.broadcasted_iota(jnp.int32, (LANES, tq), 0)

    @pl.when((pl.program_id(0) == 0) & (qi == 0))
    def _():
        for h in range(ATTN_HEADS):
            rhs_ref[h, LANES:2 * LANES, :] = jnp.where(
                (row == h) | (row == 8 + h) | (row == 16 + h), 1.0, 0.0).astype(BF16)

    for h in range(ATTN_HEADS):
        j = h // 2
        q2 = qt_ref[0, j * LANES:(j + 1) * LANES, :]
        rhs_ref[h, 0:LANES, :] = jnp.where((row // hd) == (h % 2), q2, jnp.zeros_like(q2))
    m_ref[...] = jnp.full(m_ref.shape, -jnp.inf, F32)
    l_ref[...] = jnp.zeros_like(l_ref)
    acc_ref[...] = jnp.zeros_like(acc_ref)

    def scores(k0, h, diag=None):
        j = h // 2
        lo = 0 if diag is None else diag
        lhs = jnp.concatenate([k_ref[0, pl.ds(k0, tk), j * LANES:(j + 1) * LANES],
                               caug_ref[0, pl.ds(k0, tk), :]], axis=1)
        s_ref[h % ATTN_SLOTS, :, lo:tq] = jnp.dot(lhs, rhs_ref[h, :, lo:tq],
                                                  preferred_element_type=F32)

    def update(k0, h, diag=None):
        lo = 0 if diag is None else diag
        s = s_ref[h % ATTN_SLOTS, :, lo:tq]
        if diag is not None:
            kpos = lax.broadcasted_iota(jnp.int32, s.shape, 0)
            qpos = lax.broadcasted_iota(jnp.int32, s.shape, 1)
            s = jnp.where(kpos <= qpos, s, NEG_BIG)
        hs = slice(h * SUBLANES, h * SUBLANES + 1)
        m = m_ref[hs, lo:tq]
        m_new = jnp.maximum(m, jnp.max(s, axis=0, keepdims=True))
        alpha = jnp.exp2(m - m_new)
        p = jnp.exp2(s - m_new).astype(BF16)
        m_ref[hs, lo:tq] = m_new
        vt1 = jnp.concatenate([vt_ref[0, h * hd:(h + 1) * hd, pl.ds(k0, tk)],
                               jnp.ones((2 * SUBLANES, tk), BF16)], axis=0)
        pv = jnp.dot(vt1, p, preferred_element_type=F32)
        l_ref[hs, lo:tq] = alpha * l_ref[hs, lo:tq] + pv[hd:hd + 1, :]
        acc_ref[h * hd:(h + 1) * hd, lo:tq] = (alpha * acc_ref[h * hd:(h + 1) * hd, lo:tq]
                                               + pv[:hd, :])

    def body(kj, carry):
        k0 = pl.multiple_of(kj * tk, tk)
        k1 = pl.multiple_of(kj * tk + tk, tk)
        for h in range(ATTN_HEADS):
            ha = h + ATTN_LOOKAHEAD
            if ha < ATTN_HEADS:
                scores(k0, ha)
            else:
                scores(k1, ha - ATTN_HEADS)
            update(k0, h)
        return carry

    for h in range(ATTN_LOOKAHEAD):
        scores(0, h)
    lax.fori_loop(0, qi * nd, body, 0)
    for d in range(nd):
        kd = pl.multiple_of(qi * tq + d * tk, tk)
        for h in range(ATTN_HEADS):
            ha = h + ATTN_LOOKAHEAD
            if ha < ATTN_HEADS:
                scores(kd, ha, d * tk)
            elif d + 1 < nd:
                scores(pl.multiple_of(kd + tk, tk), ha - ATTN_HEADS, (d + 1) * tk)
            update(kd, h, d * tk)

    for h in range(ATTN_HEADS):
        acc_ref[h * hd:(h + 1) * hd, :] = (acc_ref[h * hd:(h + 1) * hd, :]
                                           / l_ref[h * SUBLANES:h * SUBLANES + 1, :])
    o = acc_ref[...].T
    o_ref[0] = (o * sza_ref[0].astype(F32)).astype(BF16)


def _attn(qt, k, caug, vt, sza, *, tq):
    nb, s, w = k.shape
    return pl.pallas_call(
        functools.partial(_attn_kernel, tq=tq),
        grid=(nb, s // tq),
        in_specs=[pl.BlockSpec((1, w, tq), lambda b, i: (b, 0, i)),
                  pl.BlockSpec((1, s, w), lambda b, i: (b, 0, 0)),
                  pl.BlockSpec((1, s, LANES), lambda b, i: (b, 0, 0)),
                  pl.BlockSpec((1, w, s), lambda b, i: (b, 0, 0)),
                  pl.BlockSpec((1, tq, w), lambda b, i: (b, i, 0))],
        out_specs=pl.BlockSpec((1, tq, w), lambda b, i: (b, i, 0)),
        out_shape=jax.ShapeDtypeStruct((nb, s, w), BF16),
        scratch_shapes=[pltpu.VMEM((ATTN_HEADS, 2 * LANES, tq), BF16),
                        pltpu.VMEM((ATTN_SLOTS, ATTN_TK, tq), F32),
                        pltpu.VMEM((ATTN_HEADS * SUBLANES, tq), F32),
                        pltpu.VMEM((ATTN_HEADS * SUBLANES, tq), F32),
                        pltpu.VMEM((w, tq), F32)],
        compiler_params=_cparams("arbitrary", "arbitrary"),
        name="attn",
    )(qt, k, caug, vt, sza)


def _merge_kernel(x_ref, ys_ref, szs_ref, oz_ref, sgs_ref, sga_ref, gate_ref,
                  wps_ref, wpa_ref, wout_ref, lng_ref, lnb_ref, o_ref,
                  wps_bf, wpa_bf, wout_bf, *, ts):
    @pl.when((pl.program_id(0) == 0) & (pl.program_id(1) == 0))
    def _():
        wps_bf[...] = wps_ref[...].astype(BF16)
        wpa_bf[...] = wpa_ref[...].astype(BF16)
        wout_bf[...] = wout_ref[...].astype(BF16)

    gate = gate_ref[0, 0]
    for c in range(ts // MERGE_SUB):
        r = slice(c * MERGE_SUB, (c + 1) * MERGE_SUB)
        a = (ys_ref[0, r, :].astype(F32) * szs_ref[0, r, :].astype(F32)).astype(BF16)
        m1 = jnp.dot(a, wps_bf[...], preferred_element_type=F32) * sgs_ref[0, r, :].astype(F32)
        m2 = (jnp.dot(oz_ref[0, r, :], wpa_bf[...], preferred_element_type=F32)
              * sga_ref[0, r, :].astype(F32))
        merged = (m1 + m2).astype(BF16)
        sub = jnp.dot(merged, wout_bf[...], preferred_element_type=F32) * gate
        y = DEEPNORM_ALPHA * x_ref[0, r, :] + sub
        mu = jnp.mean(y, axis=-1, keepdims=True)
        yc = y - mu
        var = jnp.mean(yc * yc, axis=-1, keepdims=True)
        o_ref[0, r, :] = yc * lax.rsqrt(var + LN_EPS) * lng_ref[...] + lnb_ref[...]


def _mod_spec(part):
    return pl.BlockSpec((1, 1, 1, D_MODEL), lambda i, b: (b, part, 0, 0))


def _merge(x, ys_t, szs, oz, sgs, sga, mod4, wps, wpa, wout, ln_g, ln_b, *, ts):
    nb, s, d = x.shape
    tok3 = lambda w: pl.BlockSpec((1, ts, w), lambda i, b: (b, i, 0))
    const2 = lambda a: pl.BlockSpec(a.shape, lambda i, b: (0, 0), pipeline_mode=pl.Buffered(1))
    return pl.pallas_call(
        functools.partial(_merge_kernel, ts=ts),
        grid=(s // ts, nb),
        in_specs=[tok3(d), tok3(SSM_WIDTH),
                  tok3(SSM_WIDTH), tok3(ATTN_WIDTH), tok3(d), tok3(d),
                  _mod_spec(2),
                  const2(wps), const2(wpa), const2(wout), const2(ln_g), const2(ln_b)],
        out_specs=tok3(d),
        out_shape=jax.ShapeDtypeStruct((nb, s, d), x.dtype),
        scratch_shapes=[pltpu.VMEM(wps.shape, BF16), pltpu.VMEM(wpa.shape, BF16),
                        pltpu.VMEM(wout.shape, BF16)],
        compiler_params=_cparams("arbitrary", "arbitrary"),
        name="merge",
    )(x, ys_t, szs, oz, sgs, sga, mod4, wps, wpa, wout, ln_g, ln_b)


def kernel(x, c, w_ada, b_ada, w_in, b_f, lam_re, lam_im, log_dt, ssm_b_re, ssm_b_im,
           ssm_c_re, ssm_c_im, ssm_d, w_glu, w_proj_ssm, w_proj_attn, w_out, ln_g, ln_b):
    nb, s, d = x.shape
    l = 0
    mod4 = _ada(c, w_ada[l], b_ada[l]).reshape(nb, 3, 1, d)

    w_main, w_t, w_f3 = _wprep(jnp.swapaxes(w_in[l], 0, 1))
    bfv = b_f[l].astype(F32)
    b_f3 = jnp.concatenate([bfv, bfv, bfv, jnp.zeros((LANES - 3 * ATTN_HEADS,), F32)])[None, :]

    (xs_t, szs, k, sza, sgs, sga, qt, vt, caug) = _inproj(
        x, mod4, w_main, w_t, w_f3, b_f3, ts=512)

    bmat, cmat, a_re, a_im = _zoh(lam_re[l], lam_im[l], log_dt[l], ssm_b_re[l], ssm_b_im[l],
                                  ssm_c_re[l], ssm_c_im[l])

    ys_t = _s5(xs_t, bmat, cmat, a_re, a_im,
               ssm_d[l].reshape(1, SSM_WIDTH).astype(F32), w_glu[l], steps=128, sub=32)

    oz = _attn(qt, k, caug, vt, sza, tq=1024)

    return _merge(x, ys_t, szs, oz, sgs, sga, mod4,
                  w_proj_ssm[l], w_proj_attn[l], w_out[l],
                  ln_g[l].reshape(1, d).astype(F32), ln_b[l].astype(F32).reshape(1, d), ts=1024)
```

```python
import functools
import math

import jax
import jax.numpy as jnp
from jax import lax
from jax.experimental import pallas as pl
from jax.experimental.pallas import tpu as pltpu

D_MODEL = 1024
BATCH = 8
SEQ = 2048
SSM_WIDTH = 512
SSM_GROUP = 16
SSM_GROUPS = 32
SSM_STATE = 64
ATTN_HEADS = 8
ATTN_HEAD_DIM = 64
ATTN_WIDTH = 512
LN_EPS = 1e-5
DEEPNORM_ALPHA = 2.0 ** 0.25

F32 = jnp.float32
BF16 = jnp.bfloat16

LANES = 128
SUBLANES = 8
VMEM_LIMIT = 52 * 1024 * 1024

S5_TILES = 4
S5_TILE_GROUPS = SSM_GROUPS // S5_TILES
S5_HALF = S5_TILE_GROUPS * SSM_STATE
S5_TILE_LANES = 2 * S5_HALF
S5_TILE_IN = S5_TILE_GROUPS * SSM_GROUP
S5_STATE_LANES = S5_TILES * S5_TILE_LANES

C_XS, C_ZS, C_K, C_ZA, C_GS, C_GA, C_END = 0, 512, 1024, 1536, 2048, 3072, 4096

NEG_BIG = -1e30
LOG2E = math.log2(math.e)

INPROJ_SUB = 256
MERGE_SUB = 256
ATTN_TK = 256
ATTN_LOOKAHEAD = 3
ATTN_SLOTS = 4
assert ATTN_SLOTS > ATTN_LOOKAHEAD and ATTN_HEADS % ATTN_SLOTS == 0


def _sigmoid(x):
    return 1.0 / (1.0 + jnp.exp(-x))


def _silu(x):
    return x * _sigmoid(x)


def _cparams(*sem):
    return pltpu.CompilerParams(dimension_semantics=sem, vmem_limit_bytes=VMEM_LIMIT)


def _ada_kernel(c_ref, w_ref, b_ref, o_ref):
    ca = _silu(c_ref[...]).astype(BF16)
    o_ref[...] = jnp.dot(ca, w_ref[...].astype(BF16), preferred_element_type=F32) + b_ref[...]


def _ada(c, w_ada, b_ada):
    n = w_ada.shape[1]
    tn = 1024
    return pl.pallas_call(
        _ada_kernel,
        grid=(n // tn,),
        in_specs=[pl.BlockSpec((BATCH, D_MODEL), lambda j: (0, 0)),
                  pl.BlockSpec((D_MODEL, tn), lambda j: (0, j)),
                  pl.BlockSpec((1, tn), lambda j: (0, j))],
        out_specs=pl.BlockSpec((BATCH, tn), lambda j: (0, j)),
        out_shape=jax.ShapeDtypeStruct((BATCH, n), F32),
        compiler_params=_cparams("arbitrary"),
        name="ada",
    )(c, w_ada, b_ada.reshape(1, n))


def _zoh_kernel(lr_ref, li_ref, ldt_ref, br_ref, bi_ref, cre_ref, cim_ref,
                bmat_ref, cmat_ref, are_ref, aim_ref):
    lr = lr_ref[...]
    li = li_ref[...]
    dt = jnp.exp(ldt_ref[...])
    mag = jnp.exp(lr * dt)
    abr = mag * jnp.cos(li * dt)
    abi = mag * jnp.sin(li * dt)
    den = lr * lr + li * li
    cr = ((abr - 1.0) * lr + abi * li) / den
    ci = (abi * lr - (abr - 1.0) * li) / den

    tg, gh, gp = S5_TILE_GROUPS, SSM_GROUP, SSM_STATE
    rep_lanes = (lax.broadcasted_iota(jnp.int32, (gp, tg * gp), 1) % gp
                 == lax.broadcasted_iota(jnp.int32, (gp, tg * gp), 0))
    rep_rows = (lax.broadcasted_iota(jnp.int32, (tg * gp, gp), 0) % gp
                == lax.broadcasted_iota(jnp.int32, (tg * gp, gp), 1))
    rep_lanes_bf = jnp.where(rep_lanes, 1.0, 0.0).astype(BF16)
    rep_rows_bf = jnp.where(rep_rows, 1.0, 0.0).astype(BF16)
    rep_lanes_f32 = jnp.where(rep_lanes, 1.0, 0.0)
    b_diag = (lax.broadcasted_iota(jnp.int32, (tg * gh, tg * gp), 0) // gh
              == lax.broadcasted_iota(jnp.int32, (tg * gh, tg * gp), 1) // gp)
    c_diag = (lax.broadcasted_iota(jnp.int32, (tg * gp, tg * gh), 0) // gp
              == lax.broadcasted_iota(jnp.int32, (tg * gp, tg * gh), 1) // gh)
    a_diag = (lax.broadcasted_iota(jnp.int32, (tg, tg * gp), 0)
              == lax.broadcasted_iota(jnp.int32, (tg, tg * gp), 1) // gp)

    for j in range(S5_TILES):
        bbr, bbi, crs, cis = [], [], [], []
        for g in range(j * tg, (j + 1) * tg):
            crg, cig = cr[g:g + 1, :], ci[g:g + 1, :]
            br, bi = br_ref[g], bi_ref[g]
            bbr.append(crg * br - cig * bi)
            bbi.append(crg * bi + cig * br)
            crs.append(cre_ref[g])
            cis.append(-cim_ref[g])
        halves = []
        for blk in (jnp.concatenate(bbr, axis=0), jnp.concatenate(bbi, axis=0)):
            tiled = jnp.dot(blk.astype(BF16), rep_lanes_bf, preferred_element_type=F32)
            halves.append(jnp.where(b_diag, tiled, 0.0))
        bmat_ref[j] = jnp.concatenate(halves, axis=1).astype(BF16)
        halves = []
        for blk in (jnp.concatenate(crs, axis=0), jnp.concatenate(cis, axis=0)):
            tiled = lax.dot_general(rep_rows_bf, blk.astype(BF16), (((1,), (1,)), ((), ())),
                                    preferred_element_type=F32)
            halves.append(jnp.where(c_diag, tiled, 0.0))
        cmat_ref[j] = jnp.concatenate(halves, axis=0).astype(BF16)
        for a, a_ref in ((abr, are_ref), (abi, aim_ref)):
            tiled = jnp.dot(a[j * tg:(j + 1) * tg, :], rep_lanes_f32, preferred_element_type=F32,
                            precision=lax.Precision.HIGHEST)
            a_ref[:, j * S5_HALF:(j + 1) * S5_HALF] = jnp.sum(
                jnp.where(a_diag, tiled, 0.0), axis=0, keepdims=True)


def _zoh(lam_re, lam_im, log_dt, b_re, b_im, c_re, c_im):
    swap = lambda a: jnp.swapaxes(a, 1, 2)
    return pl.pallas_call(
        _zoh_kernel,
        out_shape=(jax.ShapeDtypeStruct((S5_TILES, S5_TILE_IN, S5_TILE_LANES), BF16),
                   jax.ShapeDtypeStruct((S5_TILES, S5_TILE_LANES, S5_TILE_IN), BF16),
                   jax.ShapeDtypeStruct((1, SSM_GROUPS * SSM_STATE), F32),
                   jax.ShapeDtypeStruct((1, SSM_GROUPS * SSM_STATE), F32)),
        name="zoh",
    )(lam_re, lam_im, log_dt.reshape(SSM_GROUPS, 1), swap(b_re), swap(b_im), c_re, c_im)


O_XS, O_ZS, O_Q, O_K, O_V, O_F, O_ZA, O_GS, O_GA, O_END = (
    0, 512, 1024, 1536, 2048, 2560, 2568, 3080, 4104, 5128)


WPREP_CHUNK = 256


def _wprep_kernel(wt_ref, wm_ref, wqv_ref, wf_ref):
    def put(c_lo, o_lo, n):
        for s in range(0, n, WPREP_CHUNK):
            wm_ref[:, c_lo + s:c_lo + s + WPREP_CHUNK] = (
                wt_ref[o_lo + s:o_lo + s + WPREP_CHUNK, :].T.astype(BF16))

    put(C_XS, O_XS, O_Q - O_XS)
    put(C_K, O_K, O_V - O_K)
    put(C_ZA, O_ZA, O_GS - O_ZA)
    put(C_GS, O_GS, O_END - O_GS)
    wqv_ref[0:ATTN_WIDTH, :] = wt_ref[O_Q:O_K, :].astype(BF16)
    wqv_ref[ATTN_WIDTH:2 * ATTN_WIDTH, :] = wt_ref[O_V:O_F, :].astype(BF16)
    f8 = wt_ref[O_F:O_ZA, :]
    blk = jnp.concatenate([f8, f8, f8, jnp.zeros((LANES - 3 * ATTN_HEADS, f8.shape[1]), F32)], axis=0)
    wf_ref[...] = blk.T.astype(BF16)


def _wprep(wt):
    n, d = wt.shape
    return pl.pallas_call(
        _wprep_kernel,
        out_shape=[jax.ShapeDtypeStruct((d, C_END), BF16),
                   jax.ShapeDtypeStruct((2 * ATTN_WIDTH, d), BF16),
                   jax.ShapeDtypeStruct((d, LANES), BF16)],
        compiler_params=pltpu.CompilerParams(vmem_limit_bytes=VMEM_LIMIT),
        name="wprep",
    )(wt)


def _inproj_kernel(x_ref, scale_ref, shift_ref, wm_ref, wt_ref, wf_ref, bf_ref, tri_ref,
                   xs_ref, szs_ref, k_ref, sza_ref, sgs_ref, sga_ref, qt_ref, vt_ref, caug_ref,
                   carry_ref, *, ts):
    i = pl.program_id(0)
    b = pl.program_id(1)

    @pl.when(i == 0)
    def _():
        carry_ref[b] = jnp.zeros(carry_ref.shape[1:], F32)

    carry = carry_ref[b][0:1, :]
    for c in range(ts // INPROJ_SUB):
        r = slice(c * INPROJ_SUB, (c + 1) * INPROJ_SUB)
        x = x_ref[0, r, :]
        mu = jnp.mean(x, axis=-1, keepdims=True)
        xc = x - mu
        var = jnp.mean(xc * xc, axis=-1, keepdims=True)
        xn = xc * lax.rsqrt(var + LN_EPS)
        u = xn * (1.0 + scale_ref[0, 0]) + shift_ref[0, 0]
        ub = u.astype(BF16)

        def mm(lo, hi, ub=ub):
            return jnp.dot(ub, wm_ref[:, lo:hi], preferred_element_type=F32)

        f = jnp.dot(ub, wf_ref[...], preferred_element_type=F32) + bf_ref[...]
        lf = jnp.minimum(f, 0.0) - jnp.log(1.0 + jnp.exp(-jnp.abs(f)))
        l1 = lf.astype(BF16)
        r1 = lf - l1.astype(F32)
        l2 = r1.astype(BF16)
        l3 = (r1 - l2.astype(F32)).astype(BF16)

        xs = mm(C_XS, C_ZS)
        for j in range(S5_TILES):
            xs_ref[j, pl.ds(b + c * INPROJ_SUB * SUBLANES, INPROJ_SUB, stride=SUBLANES), :] = (
                xs[:, j * LANES:(j + 1) * LANES])
        szs_ref[0, r, :] = _silu(mm(C_ZS, C_K)).astype(BF16)

        tri = tri_ref[...]
        cf = (jnp.dot(tri, l1, preferred_element_type=F32)
              + jnp.dot(tri, l2, preferred_element_type=F32)
              + jnp.dot(tri, l3, preferred_element_type=F32)) + carry
        carry = cf[INPROJ_SUB - 1:INPROJ_SUB, :]

        k_ref[0, r, :] = mm(C_K, C_ZA).astype(BF16)
        sza_ref[0, r, :] = _silu(mm(C_ZA, C_GS)).astype(BF16)

        cf2 = cf * LOG2E
        c1 = cf2.astype(BF16).astype(F32)
        rem = cf2 - c1
        c2 = rem.astype(BF16).astype(F32)
        c3 = rem - c2
        lane = lax.broadcasted_iota(jnp.int32, cf.shape, 1)
        parts = jnp.where(lane < 8, c1, jnp.where(lane < 16, c2, jnp.where(lane < 24, c3, 0.0)))
        caug_ref[0, r, :] = (-parts).astype(BF16)

        sgs_ref[0, r, :] = _sigmoid(mm(C_GS, C_GA)).astype(BF16)
        sga_ref[0, r, :] = _sigmoid(mm(C_GA, C_END)).astype(BF16)

        qv = lax.dot_general(wt_ref[...], ub, (((1,), (1,)), ((), ())),
                             preferred_element_type=F32)
        qt_ref[0, :, r] = (qv[:ATTN_WIDTH] * (LOG2E * ATTN_HEAD_DIM ** -0.5)).astype(BF16)
        vt_ref[0, :, r] = qv[ATTN_WIDTH:].astype(BF16)
    carry_ref[b] = jnp.broadcast_to(carry, carry_ref.shape[1:])


def _inproj(x, mod4, w_main, w_t, w_f3, b_f3, *, ts):
    nb, s, d = x.shape
    nt = s // ts
    tri = jnp.tril(jnp.ones((INPROJ_SUB, INPROJ_SUB), BF16))
    tok3 = lambda w: pl.BlockSpec((1, ts, w), lambda i, b: (b, i, 0))
    const2 = lambda a: pl.BlockSpec(a.shape, lambda i, b: (0, 0), pipeline_mode=pl.Buffered(1))
    bf = lambda shape: jax.ShapeDtypeStruct(shape, BF16)
    return pl.pallas_call(
        functools.partial(_inproj_kernel, ts=ts),
        grid=(nt, nb),
        in_specs=[tok3(d), _mod_spec(1), _mod_spec(0),
                  const2(w_main), const2(w_t), const2(w_f3), const2(b_f3), const2(tri)],
        out_specs=[pl.BlockSpec((S5_TILES, ts * nb, LANES), lambda i, b: (0, i, 0)),
                   tok3(SSM_WIDTH), tok3(ATTN_WIDTH), tok3(ATTN_WIDTH),
                   tok3(D_MODEL), tok3(D_MODEL),
                   pl.BlockSpec((1, ATTN_WIDTH, ts), lambda i, b: (b, 0, i)),
                   pl.BlockSpec((1, ATTN_WIDTH, ts), lambda i, b: (b, 0, i)),
                   tok3(LANES)],
        out_shape=[jax.ShapeDtypeStruct((S5_TILES, s * nb, LANES), F32),
                   bf((nb, s, SSM_WIDTH)), bf((nb, s, ATTN_WIDTH)), bf((nb, s, ATTN_WIDTH)),
                   bf((nb, s, D_MODEL)), bf((nb, s, D_MODEL)),
                   bf((nb, ATTN_WIDTH, s)), bf((nb, ATTN_WIDTH, s)),
                   bf((nb, s, LANES))],
        scratch_shapes=[pltpu.VMEM((nb, SUBLANES, LANES), F32)],
        compiler_params=_cparams("arbitrary", "arbitrary"),
        name="inproj",
    )(x, mod4, mod4, w_main, w_t, w_f3, b_f3, tri)


def _gelu_tanh(x):
    return 0.5 * x * (1.0 + jnp.tanh(math.sqrt(2.0 / math.pi) * (x + 0.044715 * (x * x * x))))


def _s5_kernel(xs_ref, bmat_ref, cmat_ref, are_ref, aim_ref, d_ref, wglu_ref,
               y_ref, buf_ref, h_ref, ybuf_ref, *, steps, sub):
    @pl.when(pl.program_id(0) == 0)
    def _():
        h_ref[...] = jnp.zeros_like(h_ref)

    srows = sub * SUBLANES

    def expand(c):
        r = slice(c * srows, (c + 1) * srows)
        for j in range(S5_TILES):
            lo = j * S5_TILE_LANES
            buf_ref[r, lo:lo + S5_TILE_LANES] = jnp.dot(
                xs_ref[j, r, :].astype(BF16), bmat_ref[j], preferred_element_type=F32)

    def recur(c, state):
        new_state = []
        for j in range(S5_TILES):
            re = slice(j * S5_TILE_LANES, j * S5_TILE_LANES + S5_HALF)
            im = slice(j * S5_TILE_LANES + S5_HALF, (j + 1) * S5_TILE_LANES)
            ar = jnp.broadcast_to(are_ref[:, j * S5_HALF:(j + 1) * S5_HALF], (SUBLANES, S5_HALF))
            ai = jnp.broadcast_to(aim_ref[:, j * S5_HALF:(j + 1) * S5_HALF], (SUBLANES, S5_HALF))
            hr, hi = state[j]
            for t in range(sub):
                rt = slice(c * srows + t * SUBLANES, c * srows + (t + 1) * SUBLANES)
                nhr = ar * hr - ai * hi + buf_ref[rt, re]
                nhi = ar * hi + ai * hr + buf_ref[rt, im]
                buf_ref[rt, re] = nhr
                buf_ref[rt, im] = nhi
                hr, hi = nhr, nhi
            new_state.append((hr, hi))
        return new_state

    def project(c):
        r = slice(c * srows, (c + 1) * srows)
        ys = []
        for j in range(S5_TILES):
            lo = j * S5_TILE_LANES
            hb = buf_ref[r, lo:lo + S5_TILE_LANES].astype(BF16)
            ys.append(jnp.dot(hb, cmat_ref[j], preferred_element_type=F32)
                      + d_ref[:, j * LANES:(j + 1) * LANES] * xs_ref[j, r, :])
        g = _gelu_tanh(jnp.concatenate(ys, axis=1)).astype(BF16)
        ab = jnp.dot(g, wglu_ref[...], preferred_element_type=F32)
        out = ab[:, :SSM_WIDTH] * _sigmoid(ab[:, SSM_WIDTH:])
        for j in range(S5_TILES):
            ybuf_ref[j] = out[:, j * LANES:(j + 1) * LANES]
        for b in range(BATCH):
            yb = jnp.concatenate([ybuf_ref[j, pl.ds(b, sub, stride=SUBLANES), :]
                                  for j in range(S5_TILES)], axis=1)
            y_ref[b, c * sub:(c + 1) * sub, :] = yb.astype(BF16)

    state = [(h_ref[:, j * S5_TILE_LANES:j * S5_TILE_LANES + S5_HALF],
              h_ref[:, j * S5_TILE_LANES + S5_HALF:(j + 1) * S5_TILE_LANES])
             for j in range(S5_TILES)]
    nsub = steps // sub
    expand(0)
    for c in range(nsub):
        if c + 1 < nsub:
            expand(c + 1)
        state = recur(c, state)
        project(c)
    for j in range(S5_TILES):
        h_ref[:, j * S5_TILE_LANES:j * S5_TILE_LANES + S5_HALF] = state[j][0]
        h_ref[:, j * S5_TILE_LANES + S5_HALF:(j + 1) * S5_TILE_LANES] = state[j][1]


def _s5(xs_t, bmat, cmat, a_re, a_im, d_skip, w_glu, *, steps, sub):
    t = xs_t.shape[1]
    rows = steps * SUBLANES
    const = lambda a: pl.BlockSpec(a.shape, lambda i: (0,) * a.ndim)
    tiles = pl.BlockSpec((S5_TILES, rows, LANES), lambda i: (0, i, 0))
    return pl.pallas_call(
        functools.partial(_s5_kernel, steps=steps, sub=sub),
        grid=(t // rows,),
        in_specs=[tiles,
                  const(bmat), const(cmat), const(a_re), const(a_im), const(d_skip), const(w_glu)],
        out_specs=pl.BlockSpec((BATCH, steps, SSM_WIDTH), lambda i: (0, i, 0)),
        out_shape=jax.ShapeDtypeStruct((BATCH, t // SUBLANES, SSM_WIDTH), BF16),
        scratch_shapes=[pltpu.VMEM((rows, S5_STATE_LANES), F32),
                        pltpu.VMEM((SUBLANES, S5_STATE_LANES), F32),
                        pltpu.VMEM((S5_TILES, sub * SUBLANES, LANES), F32)],
        compiler_params=_cparams("arbitrary"),
        name="s5",
    )(xs_t, bmat, cmat, a_re, a_im, d_skip, w_glu)


def _attn_kernel(qt_ref, k_ref, caug_ref, vt_ref, sza_ref, o_ref,
                 rhs_ref, s_ref, m_ref, l_ref, acc_ref, *, tq):
    qi = pl.program_id(1)
    tk = ATTN_TK
    nd = tq // tk
    hd = ATTN_HEAD_DIM
    row = lax.broadcasted_iota(jnp.int32, (LANES, tq), 0)

    for h in range(ATTN_HEADS):
        j = h // 2
        q2 = qt_ref[0, j * LANES:(j + 1) * LANES, :]
        rhs_ref[h, 0:LANES, :] = jnp.where((row // hd) == (h % 2), q2, jnp.zeros_like(q2))
        rhs_ref[h, LANES:2 * LANES, :] = jnp.where(
            (row == h) | (row == 8 + h) | (row == 16 + h), 1.0, 0.0).astype(BF16)
    m_ref[...] = jnp.full(m_ref.shape, -jnp.inf, F32)
    l_ref[...] = jnp.zeros_like(l_ref)
    acc_ref[...] = jnp.zeros_like(acc_ref)

    def scores(k0, h, diag=None):
        j = h // 2
        lo = 0 if diag is None else diag
        lhs = jnp.concatenate([k_ref[0, pl.ds(k0, tk), j * LANES:(j + 1) * LANES],
                               caug_ref[0, pl.ds(k0, tk), :]], axis=1)
        s_ref[h % ATTN_SLOTS, :, lo:tq] = jnp.dot(lhs, rhs_ref[h, :, lo:tq],
                                                  preferred_element_type=F32)

    def update(k0, h, diag=None):
        lo = 0 if diag is None else diag
        s = s_ref[h % ATTN_SLOTS, :, lo:tq]
        if diag is not None:
            kpos = lax.broadcasted_iota(jnp.int32, s.shape, 0)
            qpos = lax.broadcasted_iota(jnp.int32, s.shape, 1)
            s = jnp.where(kpos <= qpos, s, NEG_BIG)
        hs = slice(h * SUBLANES, h * SUBLANES + 1)
        m = m_ref[hs, lo:tq]
        m_new = jnp.maximum(m, jnp.max(s, axis=0, keepdims=True))
        alpha = jnp.exp2(m - m_new)
        p = jnp.exp2(s - m_new).astype(BF16)
        m_ref[hs, lo:tq] = m_new
        vt1 = jnp.concatenate([vt_ref[0, h * hd:(h + 1) * hd, pl.ds(k0, tk)],
                               jnp.ones((2 * SUBLANES, tk), BF16)], axis=0)
        pv = jnp.dot(vt1, p, preferred_element_type=F32)
        l_ref[hs, lo:tq] = alpha * l_ref[hs, lo:tq] + pv[hd:hd + 1, :]
        acc_ref[h * hd:(h + 1) * hd, lo:tq] = (alpha * acc_ref[h * hd:(h + 1) * hd, lo:tq]
                                               + pv[:hd, :])

    def body(kj, carry):
        k0 = pl.multiple_of(kj * tk, tk)
        k1 = pl.multiple_of(kj * tk + tk, tk)
        for h in range(ATTN_HEADS):
            ha = h + ATTN_LOOKAHEAD
            if ha < ATTN_HEADS:
                scores(k0, ha)
            else:
                scores(k1, ha - ATTN_HEADS)
            update(k0, h)
        return carry

    for h in range(ATTN_LOOKAHEAD):
        scores(0, h)
    lax.fori_loop(0, qi * nd, body, 0)
    for d in range(nd):
        kd = pl.multiple_of(qi * tq + d * tk, tk)
        for h in range(ATTN_HEADS):
            ha = h + ATTN_LOOKAHEAD
            if ha < ATTN_HEADS:
                scores(kd, ha, d * tk)
            elif d + 1 < nd:
                scores(pl.multiple_of(kd + tk, tk), ha - ATTN_HEADS, (d + 1) * tk)
            update(kd, h, d * tk)

    for h in range(ATTN_HEADS):
        acc_ref[h * hd:(h + 1) * hd, :] = (acc_ref[h * hd:(h + 1) * hd, :]
                                           / l_ref[h * SUBLANES:h * SUBLANES + 1, :])
    o = acc_ref[...].T
    o_ref[0] = (o * sza_ref[0].astype(F32)).astype(BF16)


def _attn(qt, k, caug, vt, sza, *, tq):
    nb, s, w = k.shape
    return pl.pallas_call(
        functools.partial(_attn_kernel, tq=tq),
        grid=(nb, s // tq),
        in_specs=[pl.BlockSpec((1, w, tq), lambda b, i: (b, 0, i)),
                  pl.BlockSpec((1, s, w), lambda b, i: (b, 0, 0)),
                  pl.BlockSpec((1, s, LANES), lambda b, i: (b, 0, 0)),
                  pl.BlockSpec((1, w, s), lambda b, i: (b, 0, 0)),
                  pl.BlockSpec((1, tq, w), lambda b, i: (b, i, 0))],
        out_specs=pl.BlockSpec((1, tq, w), lambda b, i: (b, i, 0)),
        out_shape=jax.ShapeDtypeStruct((nb, s, w), BF16),
        scratch_shapes=[pltpu.VMEM((ATTN_HEADS, 2 * LANES, tq), BF16),
                        pltpu.VMEM((ATTN_SLOTS, ATTN_TK, tq), F32),
                        pltpu.VMEM((ATTN_HEADS * SUBLANES, tq), F32),
                        pltpu.VMEM((ATTN_HEADS * SUBLANES, tq), F32),
                        pltpu.VMEM((w, tq), F32)],
        compiler_params=_cparams("arbitrary", "arbitrary"),
        name="attn",
    )(qt, k, caug, vt, sza)


def _merge_kernel(x_ref, ys_ref, szs_ref, oz_ref, sgs_ref, sga_ref, gate_ref,
                  wps_ref, wpa_ref, wout_ref, lng_ref, lnb_ref, o_ref,
                  wps_bf, wpa_bf, wout_bf, *, ts):
    @pl.when((pl.program_id(0) == 0) & (pl.program_id(1) == 0))
    def _():
        wps_bf[...] = wps_ref[...].astype(BF16)
        wpa_bf[...] = wpa_ref[...].astype(BF16)
        wout_bf[...] = wout_ref[...].astype(BF16)

    gate = gate_ref[0, 0]
    for c in range(ts // MERGE_SUB):
        r = slice(c * MERGE_SUB, (c + 1) * MERGE_SUB)
        a = (ys_ref[0, r, :].astype(F32) * szs_ref[0, r, :].astype(F32)).astype(BF16)
        m1 = jnp.dot(a, wps_bf[...], preferred_element_type=F32) * sgs_ref[0, r, :].astype(F32)
        m2 = (jnp.dot(oz_ref[0, r, :], wpa_bf[...], preferred_element_type=F32)
              * sga_ref[0, r, :].astype(F32))
        merged = (m1 + m2).astype(BF16)
        sub = jnp.dot(merged, wout_bf[...], preferred_element_type=F32) * gate
        y = DEEPNORM_ALPHA * x_ref[0, r, :] + sub
        mu = jnp.mean(y, axis=-1, keepdims=True)
        yc = y - mu
        var = jnp.mean(yc * yc, axis=-1, keepdims=True)
        o_ref[0, r, :] = yc * lax.rsqrt(var + LN_EPS) * lng_ref[...] + lnb_ref[...]


def _mod_spec(part):
    return pl.BlockSpec((1, 1, 1, D_MODEL), lambda i, b: (b, part, 0, 0))


def _merge(x, ys_t, szs, oz, sgs, sga, mod4, wps, wpa, wout, ln_g, ln_b, *, ts):
    nb, s, d = x.shape
    tok3 = lambda w: pl.BlockSpec((1, ts, w), lambda i, b: (b, i, 0))
    const2 = lambda a: pl.BlockSpec(a.shape, lambda i, b: (0, 0), pipeline_mode=pl.Buffered(1))
    return pl.pallas_call(
        functools.partial(_merge_kernel, ts=ts),
        grid=(s // ts, nb),
        in_specs=[tok3(d), tok3(SSM_WIDTH),
                  tok3(SSM_WIDTH), tok3(ATTN_WIDTH), tok3(d), tok3(d),
                  _mod_spec(2),
                  const2(wps), const2(wpa), const2(wout), const2(ln_g), const2(ln_b)],
        out_specs=tok3(d),
        out_shape=jax.ShapeDtypeStruct((nb, s, d), x.dtype),
        scratch_shapes=[pltpu.VMEM(wps.shape, BF16), pltpu.VMEM(wpa.shape, BF16),
                        pltpu.VMEM(wout.shape, BF16)],
        compiler_params=_cparams("arbitrary", "arbitrary"),
        name="merge",
    )(x, ys_t, szs, oz, sgs, sga, mod4, wps, wpa, wout, ln_g, ln_b)


def kernel(x, c, w_ada, b_ada, w_in, b_f, lam_re, lam_im, log_dt, ssm_b_re, ssm_b_im,
           ssm_c_re, ssm_c_im, ssm_d, w_glu, w_proj_ssm, w_proj_attn, w_out, ln_g, ln_b):
    nb, s, d = x.shape
    l = 0
    mod4 = _ada(c, w_ada[l], b_ada[l]).reshape(nb, 3, 1, d)

    w_main, w_t, w_f3 = _wprep(jnp.swapaxes(w_in[l], 0, 1))
    bfv = b_f[l].astype(F32)
    b_f3 = jnp.concatenate([bfv, bfv, bfv, jnp.zeros((LANES - 3 * ATTN_HEADS,), F32)])[None, :]

    (xs_t, szs, k, sza, sgs, sga, qt, vt, caug) = _inproj(
        x, mod4, w_main, w_t, w_f3, b_f3, ts=512)

    bmat, cmat, a_re, a_im = _zoh(lam_re[l], lam_im[l], log_dt[l], ssm_b_re[l], ssm_b_im[l],
                                  ssm_c_re[l], ssm_c_im[l])

    ys_t = _s5(xs_t, bmat, cmat, a_re, a_im,
               ssm_d[l].reshape(1, SSM_WIDTH).astype(F32), w_glu[l].astype(BF16), steps=128, sub=32)

    oz = _attn(qt, k, caug, vt, sza, tq=1024)

    return _merge(x, ys_t, szs, oz, sgs, sga, mod4,
                  w_proj_ssm[l], w_proj_attn[l], w_out[l],
                  ln_g[l].reshape(1, d).astype(F32), ln_b[l].astype(F32).reshape(1, d), ts=1024)
```

```python
import functools
import math

import jax
import jax.numpy as jnp
from jax import lax
from jax.experimental import pallas as pl
from jax.experimental.pallas import tpu as pltpu

D_MODEL = 1024
BATCH = 8
SEQ = 2048
SSM_WIDTH = 512
SSM_GROUP = 16
SSM_GROUPS = 32
SSM_STATE = 64
ATTN_HEADS = 8
ATTN_HEAD_DIM = 64
ATTN_WIDTH = 512
LN_EPS = 1e-5
DEEPNORM_ALPHA = 2.0 ** 0.25

F32 = jnp.float32
BF16 = jnp.bfloat16

LANES = 128
SUBLANES = 8
VMEM_LIMIT = 52 * 1024 * 1024

S5_TILES = 4
S5_TILE_GROUPS = SSM_GROUPS // S5_TILES
S5_HALF = S5_TILE_GROUPS * SSM_STATE
S5_TILE_LANES = 2 * S5_HALF
S5_TILE_IN = S5_TILE_GROUPS * SSM_GROUP
S5_STATE_LANES = S5_TILES * S5_TILE_LANES

C_XS, C_ZS, C_K, C_ZA, C_GS, C_GA, C_END = 0, 512, 1024, 1536, 2048, 3072, 4096

NEG_BIG = -1e30
LOG2E = math.log2(math.e)

INPROJ_SUB = 256
MERGE_SUB = 256
ATTN_TK = 256
ATTN_LOOKAHEAD = 3
ATTN_SLOTS = 4
assert ATTN_SLOTS > ATTN_LOOKAHEAD and ATTN_HEADS % ATTN_SLOTS == 0


def _sigmoid(x):
    return 1.0 / (1.0 + jnp.exp(-x))


def _silu(x):
    return x * _sigmoid(x)


def _cparams(*sem):
    return pltpu.CompilerParams(dimension_semantics=sem, vmem_limit_bytes=VMEM_LIMIT)


def _ada_kernel(c_ref, w_ref, b_ref, o_ref):
    ca = _silu(c_ref[...]).astype(BF16)
    o_ref[...] = jnp.dot(ca, w_ref[...].astype(BF16), preferred_element_type=F32) + b_ref[...]


def _ada(c, w_ada, b_ada):
    n = w_ada.shape[1]
    tn = 1024
    return pl.pallas_call(
        _ada_kernel,
        grid=(n // tn,),
        in_specs=[pl.BlockSpec((BATCH, D_MODEL), lambda j: (0, 0)),
                  pl.BlockSpec((D_MODEL, tn), lambda j: (0, j)),
                  pl.BlockSpec((1, tn), lambda j: (0, j))],
        out_specs=pl.BlockSpec((BATCH, tn), lambda j: (0, j)),
        out_shape=jax.ShapeDtypeStruct((BATCH, n), F32),
        compiler_params=_cparams("arbitrary"),
        name="ada",
    )(c, w_ada, b_ada.reshape(1, n))


def _zoh_kernel(lr_ref, li_ref, ldt_ref, br_ref, bi_ref, cre_ref, cim_ref,
                bmat_ref, cmat_ref, are_ref, aim_ref):
    lr = lr_ref[...]
    li = li_ref[...]
    dt = jnp.exp(ldt_ref[...])
    mag = jnp.exp(lr * dt)
    abr = mag * jnp.cos(li * dt)
    abi = mag * jnp.sin(li * dt)
    den = lr * lr + li * li
    cr = ((abr - 1.0) * lr + abi * li) / den
    ci = (abi * lr - (abr - 1.0) * li) / den

    tg, gh, gp = S5_TILE_GROUPS, SSM_GROUP, SSM_STATE
    rep_lanes = (lax.broadcasted_iota(jnp.int32, (gp, tg * gp), 1) % gp
                 == lax.broadcasted_iota(jnp.int32, (gp, tg * gp), 0))
    rep_rows = (lax.broadcasted_iota(jnp.int32, (tg * gp, gp), 0) % gp
                == lax.broadcasted_iota(jnp.int32, (tg * gp, gp), 1))
    rep_lanes_bf = jnp.where(rep_lanes, 1.0, 0.0).astype(BF16)
    rep_rows_bf = jnp.where(rep_rows, 1.0, 0.0).astype(BF16)
    rep_lanes_f32 = jnp.where(rep_lanes, 1.0, 0.0)
    b_diag = (lax.broadcasted_iota(jnp.int32, (tg * gh, tg * gp), 0) // gh
              == lax.broadcasted_iota(jnp.int32, (tg * gh, tg * gp), 1) // gp)
    c_diag = (lax.broadcasted_iota(jnp.int32, (tg * gp, tg * gh), 0) // gp
              == lax.broadcasted_iota(jnp.int32, (tg * gp, tg * gh), 1) // gh)
    a_diag = (lax.broadcasted_iota(jnp.int32, (tg, tg * gp), 0)
              == lax.broadcasted_iota(jnp.int32, (tg, tg * gp), 1) // gp)

    for j in range(S5_TILES):
        bbr, bbi, crs, cis = [], [], [], []
        for g in range(j * tg, (j + 1) * tg):
            crg, cig = cr[g:g + 1, :], ci[g:g + 1, :]
            br, bi = br_ref[g], bi_ref[g]
            bbr.append(crg * br - cig * bi)
            bbi.append(crg * bi + cig * br)
            crs.append(cre_ref[g])
            cis.append(-cim_ref[g])
        halves = []
        for blk in (jnp.concatenate(bbr, axis=0), jnp.concatenate(bbi, axis=0)):
            tiled = jnp.dot(blk.astype(BF16), rep_lanes_bf, preferred_element_type=F32)
            halves.append(jnp.where(b_diag, tiled, 0.0))
        bmat_ref[j] = jnp.concatenate(halves, axis=1).astype(BF16)
        halves = []
        for blk in (jnp.concatenate(crs, axis=0), jnp.concatenate(cis, axis=0)):
            tiled = lax.dot_general(rep_rows_bf, blk.astype(BF16), (((1,), (1,)), ((), ())),
                                    preferred_element_type=F32)
            halves.append(jnp.where(c_diag, tiled, 0.0))
        cmat_ref[j] = jnp.concatenate(halves, axis=0).astype(BF16)
        for a, a_ref in ((abr, are_ref), (abi, aim_ref)):
            tiled = jnp.dot(a[j * tg:(j + 1) * tg, :], rep_lanes_f32, preferred_element_type=F32,
                            precision=lax.Precision.HIGHEST)
            a_ref[:, j * S5_HALF:(j + 1) * S5_HALF] = jnp.sum(
                jnp.where(a_diag, tiled, 0.0), axis=0, keepdims=True)


def _zoh(lam_re, lam_im, log_dt, b_re, b_im, c_re, c_im):
    swap = lambda a: jnp.swapaxes(a, 1, 2)
    return pl.pallas_call(
        _zoh_kernel,
        out_shape=(jax.ShapeDtypeStruct((S5_TILES, S5_TILE_IN, S5_TILE_LANES), BF16),
                   jax.ShapeDtypeStruct((S5_TILES, S5_TILE_LANES, S5_TILE_IN), BF16),
                   jax.ShapeDtypeStruct((1, SSM_GROUPS * SSM_STATE), F32),
                   jax.ShapeDtypeStruct((1, SSM_GROUPS * SSM_STATE), F32)),
        name="zoh",
    )(lam_re, lam_im, log_dt.reshape(SSM_GROUPS, 1), swap(b_re), swap(b_im), c_re, c_im)


O_XS, O_ZS, O_Q, O_K, O_V, O_F, O_ZA, O_GS, O_GA, O_END = (
    0, 512, 1024, 1536, 2048, 2560, 2568, 3080, 4104, 5128)


WPREP_CHUNK = 256


def _wprep_kernel(wt_ref, wm_ref, wqv_ref, wf_ref):
    def put(c_lo, o_lo, n):
        for s in range(0, n, WPREP_CHUNK):
            wm_ref[:, c_lo + s:c_lo + s + WPREP_CHUNK] = (
                wt_ref[o_lo + s:o_lo + s + WPREP_CHUNK, :].T.astype(BF16))

    put(C_XS, O_XS, O_Q - O_XS)
    put(C_K, O_K, O_V - O_K)
    put(C_ZA, O_ZA, O_GS - O_ZA)
    put(C_GS, O_GS, O_END - O_GS)
    wqv_ref[0:ATTN_WIDTH, :] = wt_ref[O_Q:O_K, :].astype(BF16)
    wqv_ref[ATTN_WIDTH:2 * ATTN_WIDTH, :] = wt_ref[O_V:O_F, :].astype(BF16)
    f8 = wt_ref[O_F:O_ZA, :]
    blk = jnp.concatenate([f8, f8, f8, jnp.zeros((LANES - 3 * ATTN_HEADS, f8.shape[1]), F32)], axis=0)
    wf_ref[...] = blk.T.astype(BF16)


def _wprep(wt):
    n, d = wt.shape
    return pl.pallas_call(
        _wprep_kernel,
        out_shape=[jax.ShapeDtypeStruct((d, C_END), BF16),
                   jax.ShapeDtypeStruct((2 * ATTN_WIDTH, d), BF16),
                   jax.ShapeDtypeStruct((d, LANES), BF16)],
        compiler_params=pltpu.CompilerParams(vmem_limit_bytes=VMEM_LIMIT),
        name="wprep",
    )(wt)


def _inproj_kernel(x_ref, scale_ref, shift_ref, wm_ref, wt_ref, wf_ref, bf_ref, tri_ref,
                   xs_ref, szs_ref, k_ref, sza_ref, sgs_ref, sga_ref, qt_ref, vt_ref, caug_ref,
                   carry_ref, *, ts):
    i = pl.program_id(0)
    b = pl.program_id(1)

    @pl.when(i == 0)
    def _():
        carry_ref[b] = jnp.zeros(carry_ref.shape[1:], F32)

    carry = carry_ref[b][0:1, :]
    for c in range(ts // INPROJ_SUB):
        r = slice(c * INPROJ_SUB, (c + 1) * INPROJ_SUB)
        x = x_ref[0, r, :]
        mu = jnp.mean(x, axis=-1, keepdims=True)
        xc = x - mu
        var = jnp.mean(xc * xc, axis=-1, keepdims=True)
        xn = xc * lax.rsqrt(var + LN_EPS)
        u = xn * (1.0 + scale_ref[0, 0]) + shift_ref[0, 0]
        ub = u.astype(BF16)

        def mm(lo, hi, ub=ub):
            return jnp.dot(ub, wm_ref[:, lo:hi], preferred_element_type=F32)

        f = jnp.dot(ub, wf_ref[...], preferred_element_type=F32) + bf_ref[...]
        lf = jnp.minimum(f, 0.0) - jnp.log(1.0 + jnp.exp(-jnp.abs(f)))
        l1 = lf.astype(BF16)
        r1 = lf - l1.astype(F32)
        l2 = r1.astype(BF16)
        l3 = (r1 - l2.astype(F32)).astype(BF16)

        xs = mm(C_XS, C_ZS)
        for j in range(S5_TILES):
            xs_ref[j, pl.ds(b + c * INPROJ_SUB * SUBLANES, INPROJ_SUB, stride=SUBLANES), :] = (
                xs[:, j * LANES:(j + 1) * LANES])
        szs_ref[0, r, :] = _silu(mm(C_ZS, C_K)).astype(BF16)

        tri = tri_ref[...]
        cf = (jnp.dot(tri, l1, preferred_element_type=F32)
              + jnp.dot(tri, l2, preferred_element_type=F32)
              + jnp.dot(tri, l3, preferred_element_type=F32)) + carry
        carry = cf[INPROJ_SUB - 1:INPROJ_SUB, :]

        k_ref[0, r, :] = mm(C_K, C_ZA).astype(BF16)
        sza_ref[0, r, :] = _silu(mm(C_ZA, C_GS)).astype(BF16)

        cf2 = cf * LOG2E
        c1 = cf2.astype(BF16).astype(F32)
        rem = cf2 - c1
        c2 = rem.astype(BF16).astype(F32)
        c3 = rem - c2
        lane = lax.broadcasted_iota(jnp.int32, cf.shape, 1)
        parts = jnp.where(lane < 8, c1, jnp.where(lane < 16, c2, jnp.where(lane < 24, c3, 0.0)))
        caug_ref[0, r, :] = (-parts).astype(BF16)

        sgs_ref[0, r, :] = _sigmoid(mm(C_GS, C_GA)).astype(BF16)
        sga_ref[0, r, :] = _sigmoid(mm(C_GA, C_END)).astype(BF16)

        qv = lax.dot_general(wt_ref[...], ub, (((1,), (1,)), ((), ())),
                             preferred_element_type=F32)
        qt_ref[0, :, r] = (qv[:ATTN_WIDTH] * (LOG2E * ATTN_HEAD_DIM ** -0.5)).astype(BF16)
        vt_ref[0, :, r] = qv[ATTN_WIDTH:].astype(BF16)
    carry_ref[b] = jnp.broadcast_to(carry, carry_ref.shape[1:])


def _inproj(x, mod4, w_main, w_t, w_f3, b_f3, *, ts):
    nb, s, d = x.shape
    nt = s // ts
    tri = jnp.tril(jnp.ones((INPROJ_SUB, INPROJ_SUB), BF16))
    tok3 = lambda w: pl.BlockSpec((1, ts, w), lambda i, b: (b, i, 0))
    const2 = lambda a: pl.BlockSpec(a.shape, lambda i, b: (0, 0), pipeline_mode=pl.Buffered(1))
    bf = lambda shape: jax.ShapeDtypeStruct(shape, BF16)
    return pl.pallas_call(
        functools.partial(_inproj_kernel, ts=ts),
        grid=(nt, nb),
        in_specs=[tok3(d), _mod_spec(1), _mod_spec(0),
                  const2(w_main), const2(w_t), const2(w_f3), const2(b_f3), const2(tri)],
        out_specs=[pl.BlockSpec((S5_TILES, ts * nb, LANES), lambda i, b: (0, i, 0)),
                   tok3(SSM_WIDTH), tok3(ATTN_WIDTH), tok3(ATTN_WIDTH),
                   tok3(D_MODEL), tok3(D_MODEL),
                   pl.BlockSpec((1, ATTN_WIDTH, ts), lambda i, b: (b, 0, i)),
                   pl.BlockSpec((1, ATTN_WIDTH, ts), lambda i, b: (b, 0, i)),
                   tok3(LANES)],
        out_shape=[jax.ShapeDtypeStruct((S5_TILES, s * nb, LANES), F32),
                   bf((nb, s, SSM_WIDTH)), bf((nb, s, ATTN_WIDTH)), bf((nb, s, ATTN_WIDTH)),
                   bf((nb, s, D_MODEL)), bf((nb, s, D_MODEL)),
                   bf((nb, ATTN_WIDTH, s)), bf((nb, ATTN_WIDTH, s)),
                   bf((nb, s, LANES))],
        scratch_shapes=[pltpu.VMEM((nb, SUBLANES, LANES), F32)],
        compiler_params=_cparams("arbitrary", "arbitrary"),
        name="inproj",
    )(x, mod4, mod4, w_main, w_t, w_f3, b_f3, tri)


def _gelu_tanh(x):
    return 0.5 * x * (1.0 + jnp.tanh(math.sqrt(2.0 / math.pi) * (x + 0.044715 * (x * x * x))))


def _s5_kernel(xs_ref, bmat_ref, cmat_ref, are_ref, aim_ref, d_ref, wglu_ref,
               y_ref, buf_ref, h_ref, ybuf_ref, *, steps, sub):
    @pl.when(pl.program_id(0) == 0)
    def _():
        h_ref[...] = jnp.zeros_like(h_ref)

    srows = sub * SUBLANES

    def expand(c):
        r = slice(c * srows, (c + 1) * srows)
        for j in range(S5_TILES):
            lo = j * S5_TILE_LANES
            buf_ref[r, lo:lo + S5_TILE_LANES] = jnp.dot(
                xs_ref[j, r, :].astype(BF16), bmat_ref[j], preferred_element_type=F32)

    def recur(c, state):
        new_state = []
        for j in range(S5_TILES):
            re = slice(j * S5_TILE_LANES, j * S5_TILE_LANES + S5_HALF)
            im = slice(j * S5_TILE_LANES + S5_HALF, (j + 1) * S5_TILE_LANES)
            ar = jnp.broadcast_to(are_ref[:, j * S5_HALF:(j + 1) * S5_HALF], (SUBLANES, S5_HALF))
            ai = jnp.broadcast_to(aim_ref[:, j * S5_HALF:(j + 1) * S5_HALF], (SUBLANES, S5_HALF))
            hr, hi = state[j]
            for t in range(sub):
                rt = slice(c * srows + t * SUBLANES, c * srows + (t + 1) * SUBLANES)
                nhr = ar * hr - ai * hi + buf_ref[rt, re]
                nhi = ar * hi + ai * hr + buf_ref[rt, im]
                buf_ref[rt, re] = nhr
                buf_ref[rt, im] = nhi
                hr, hi = nhr, nhi
            new_state.append((hr, hi))
        return new_state

    def project(c):
        r = slice(c * srows, (c + 1) * srows)
        ys = []
        for j in range(S5_TILES):
            lo = j * S5_TILE_LANES
            hb = buf_ref[r, lo:lo + S5_TILE_LANES].astype(BF16)
            ys.append(jnp.dot(hb, cmat_ref[j], preferred_element_type=F32)
                      + d_ref[:, j * LANES:(j + 1) * LANES] * xs_ref[j, r, :])
        g = _gelu_tanh(jnp.concatenate(ys, axis=1)).astype(BF16)
        ab = jnp.dot(g, wglu_ref[...], preferred_element_type=F32)
        out = ab[:, :SSM_WIDTH] * _sigmoid(ab[:, SSM_WIDTH:])
        for j in range(S5_TILES):
            ybuf_ref[j] = out[:, j * LANES:(j + 1) * LANES]
        for b in range(BATCH):
            yb = jnp.concatenate([ybuf_ref[j, pl.ds(b, sub, stride=SUBLANES), :]
                                  for j in range(S5_TILES)], axis=1)
            y_ref[b, c * sub:(c + 1) * sub, :] = yb.astype(BF16)

    state = [(h_ref[:, j * S5_TILE_LANES:j * S5_TILE_LANES + S5_HALF],
              h_ref[:, j * S5_TILE_LANES + S5_HALF:(j + 1) * S5_TILE_LANES])
             for j in range(S5_TILES)]
    nsub = steps // sub
    expand(0)
    for c in range(nsub):
        if c + 1 < nsub:
            expand(c + 1)
        state = recur(c, state)
        project(c)
    for j in range(S5_TILES):
        h_ref[:, j * S5_TILE_LANES:j * S5_TILE_LANES + S5_HALF] = state[j][0]
        h_ref[:, j * S5_TILE_LANES + S5_HALF:(j + 1) * S5_TILE_LANES] = state[j][1]


def _s5(xs_t, bmat, cmat, a_re, a_im, d_skip, w_glu, *, steps, sub):
    t = xs_t.shape[1]
    rows = steps * SUBLANES
    const = lambda a: pl.BlockSpec(a.shape, lambda i: (0,) * a.ndim)
    tiles = pl.BlockSpec((S5_TILES, rows, LANES), lambda i: (0, i, 0))
    return pl.pallas_call(
        functools.partial(_s5_kernel, steps=steps, sub=sub),
        grid=(t // rows,),
        in_specs=[tiles,
                  const(bmat), const(cmat), const(a_re), const(a_im), const(d_skip), const(w_glu)],
        out_specs=pl.BlockSpec((BATCH, steps, SSM_WIDTH), lambda i: (0, i, 0)),
        out_shape=jax.ShapeDtypeStruct((BATCH, t // SUBLANES, SSM_WIDTH), BF16),
        scratch_shapes=[pltpu.VMEM((rows, S5_STATE_LANES), F32),
                        pltpu.VMEM((SUBLANES, S5_STATE_LANES), F32),
                        pltpu.VMEM((S5_TILES, sub * SUBLANES, LANES), F32)],
        compiler_params=_cparams("arbitrary"),
        name="s5",
    )(xs_t, bmat, cmat, a_re, a_im, d_skip, w_glu)


def _attn_kernel(qt_ref, k_ref, caug_ref, vt_ref, sza_ref, o_ref,
                 rhs_ref, s_ref, m_ref, l_ref, acc_ref, *, tq):
    qi = pl.program_id(1)
    tk = ATTN_TK
    nd = tq // tk
    hd = ATTN_HEAD_DIM
    row = lax.broadcasted_iota(jnp.int32, (LANES, tq), 0)

    for h in range(ATTN_HEADS):
        j = h // 2
        q2 = qt_ref[0, j * LANES:(j + 1) * LANES, :]
        rhs_ref[h, 0:LANES, 0:tq] = jnp.where((row // hd) == (h % 2), q2, jnp.zeros_like(q2))
        rhs_ref[h, LANES:2 * LANES, 0:tq] = jnp.where(
            (row == h) | (row == 8 + h) | (row == 16 + h), 1.0, 0.0).astype(BF16)
    m_ref[...] = jnp.full(m_ref.shape, -jnp.inf, F32)
    l_ref[...] = jnp.zeros_like(l_ref)
    acc_ref[...] = jnp.zeros_like(acc_ref)

    def scores(k0, h, diag=None):
        j = h // 2
        lo = 0 if diag is None else diag
        lhs = jnp.concatenate([k_ref[0, pl.ds(k0, tk), j * LANES:(j + 1) * LANES],
                               caug_ref[0, pl.ds(k0, tk), :]], axis=1)
        s_ref[h % ATTN_SLOTS, :, lo:tq] = jnp.dot(lhs, rhs_ref[h, :, lo:tq],
                                                  preferred_element_type=F32)

    def update(k0, h, diag=None):
        lo = 0 if diag is None else diag
        s = s_ref[h % ATTN_SLOTS, :, lo:tq]
        if diag is not None:
            kpos = lax.broadcasted_iota(jnp.int32, s.shape, 0)
            qpos = lax.broadcasted_iota(jnp.int32, s.shape, 1)
            s = jnp.where(kpos <= qpos, s, NEG_BIG)
        hs = slice(h * SUBLANES, h * SUBLANES + 1)
        m = m_ref[hs, lo:tq]
        m_new = jnp.maximum(m, jnp.max(s, axis=0, keepdims=True))
        alpha = jnp.exp2(m - m_new)
        p = jnp.exp2(s - m_new).astype(BF16)
        m_ref[hs, lo:tq] = m_new
        vt1 = jnp.concatenate([vt_ref[0, h * hd:(h + 1) * hd, pl.ds(k0, tk)],
                               jnp.ones((2 * SUBLANES, tk), BF16)], axis=0)
        pv = jnp.dot(vt1, p, preferred_element_type=F32)
        l_ref[hs, lo:tq] = alpha * l_ref[hs, lo:tq] + pv[hd:hd + 1, :]
        acc_ref[h * hd:(h + 1) * hd, lo:tq] = (alpha * acc_ref[h * hd:(h + 1) * hd, lo:tq]
                                               + pv[:hd, :])

    def body(kj, carry):
        k0 = pl.multiple_of(kj * tk, tk)
        k1 = pl.multiple_of(kj * tk + tk, tk)
        for h in range(ATTN_HEADS):
            ha = h + ATTN_LOOKAHEAD
            if ha < ATTN_HEADS:
                scores(k0, ha)
            else:
                scores(k1, ha - ATTN_HEADS)
            update(k0, h)
        return carry

    for h in range(ATTN_LOOKAHEAD):
        scores(0, h)
    lax.fori_loop(0, qi * nd, body, 0)
    for d in range(nd):
        kd = pl.multiple_of(qi * tq + d * tk, tk)
        for h in range(ATTN_HEADS):
            ha = h + ATTN_LOOKAHEAD
            if ha < ATTN_HEADS:
                scores(kd, ha, d * tk)
            elif d + 1 < nd:
                scores(pl.multiple_of(kd + tk, tk), ha - ATTN_HEADS, (d + 1) * tk)
            update(kd, h, d * tk)

    for h in range(ATTN_HEADS):
        acc_ref[h * hd:(h + 1) * hd, 0:tq] = (acc_ref[h * hd:(h + 1) * hd, 0:tq]
                                              / l_ref[h * SUBLANES:h * SUBLANES + 1, :])
    o = acc_ref[:, 0:tq].T
    o_ref[0] = (o * sza_ref[0].astype(F32)).astype(BF16)


def _attn(qt, k, caug, vt, sza, *, tq):
    nb, s, w = k.shape
    return pl.pallas_call(
        functools.partial(_attn_kernel, tq=tq),
        grid=(nb, s // tq),
        in_specs=[pl.BlockSpec((1, w, tq), lambda b, i: (b, 0, i)),
                  pl.BlockSpec((1, s, w), lambda b, i: (b, 0, 0)),
                  pl.BlockSpec((1, s, LANES), lambda b, i: (b, 0, 0)),
                  pl.BlockSpec((1, w, s), lambda b, i: (b, 0, 0)),
                  pl.BlockSpec((1, tq, w), lambda b, i: (b, i, 0))],
        out_specs=pl.BlockSpec((1, tq, w), lambda b, i: (b, i, 0)),
        out_shape=jax.ShapeDtypeStruct((nb, s, w), BF16),
        scratch_shapes=[pltpu.VMEM((ATTN_HEADS, 2 * LANES, tq + LANES), BF16),
                        pltpu.VMEM((ATTN_SLOTS, ATTN_TK, tq + LANES), F32),
                        pltpu.VMEM((ATTN_HEADS * SUBLANES, tq), F32),
                        pltpu.VMEM((ATTN_HEADS * SUBLANES, tq), F32),
                        pltpu.VMEM((w, tq + LANES), F32)],
        compiler_params=_cparams("arbitrary", "arbitrary"),
        name="attn",
    )(qt, k, caug, vt, sza)


def _merge_kernel(x_ref, ys_ref, szs_ref, oz_ref, sgs_ref, sga_ref, gate_ref,
                  wps_ref, wpa_ref, wout_ref, lng_ref, lnb_ref, o_ref,
                  wps_bf, wpa_bf, wout_bf, *, ts):
    @pl.when((pl.program_id(0) == 0) & (pl.program_id(1) == 0))
    def _():
        wps_bf[...] = wps_ref[...].astype(BF16)
        wpa_bf[...] = wpa_ref[...].astype(BF16)
        wout_bf[...] = wout_ref[...].astype(BF16)

    gate = gate_ref[0, 0]
    for c in range(ts // MERGE_SUB):
        r = slice(c * MERGE_SUB, (c + 1) * MERGE_SUB)
        a = (ys_ref[0, r, :].astype(F32) * szs_ref[0, r, :].astype(F32)).astype(BF16)
        m1 = jnp.dot(a, wps_bf[...], preferred_element_type=F32) * sgs_ref[0, r, :].astype(F32)
        m2 = (jnp.dot(oz_ref[0, r, :], wpa_bf[...], preferred_element_type=F32)
              * sga_ref[0, r, :].astype(F32))
        merged = (m1 + m2).astype(BF16)
        sub = jnp.dot(merged, wout_bf[...], preferred_element_type=F32) * gate
        y = DEEPNORM_ALPHA * x_ref[0, r, :] + sub
        mu = jnp.mean(y, axis=-1, keepdims=True)
        yc = y - mu
        var = jnp.mean(yc * yc, axis=-1, keepdims=True)
        o_ref[0, r, :] = yc * lax.rsqrt(var + LN_EPS) * lng_ref[...] + lnb_ref[...]


def _mod_spec(part):
    return pl.BlockSpec((1, 1, 1, D_MODEL), lambda i, b: (b, part, 0, 0))


def _merge(x, ys_t, szs, oz, sgs, sga, mod4, wps, wpa, wout, ln_g, ln_b, *, ts):
    nb, s, d = x.shape
    tok3 = lambda w: pl.BlockSpec((1, ts, w), lambda i, b: (b, i, 0))
    const2 = lambda a: pl.BlockSpec(a.shape, lambda i, b: (0, 0), pipeline_mode=pl.Buffered(1))
    return pl.pallas_call(
        functools.partial(_merge_kernel, ts=ts),
        grid=(s // ts, nb),
        in_specs=[tok3(d), tok3(SSM_WIDTH),
                  tok3(SSM_WIDTH), tok3(ATTN_WIDTH), tok3(d), tok3(d),
                  _mod_spec(2),
                  const2(wps), const2(wpa), const2(wout), const2(ln_g), const2(ln_b)],
        out_specs=tok3(d),
        out_shape=jax.ShapeDtypeStruct((nb, s, d), x.dtype),
        scratch_shapes=[pltpu.VMEM(wps.shape, BF16), pltpu.VMEM(wpa.shape, BF16),
                        pltpu.VMEM(wout.shape, BF16)],
        compiler_params=_cparams("arbitrary", "arbitrary"),
        name="merge",
    )(x, ys_t, szs, oz, sgs, sga, mod4, wps, wpa, wout, ln_g, ln_b)


def kernel(x, c, w_ada, b_ada, w_in, b_f, lam_re, lam_im, log_dt, ssm_b_re, ssm_b_im,
           ssm_c_re, ssm_c_im, ssm_d, w_glu, w_proj_ssm, w_proj_attn, w_out, ln_g, ln_b):
    nb, s, d = x.shape
    l = 0
    mod4 = _ada(c, w_ada[l], b_ada[l]).reshape(nb, 3, 1, d)

    w_main, w_t, w_f3 = _wprep(jnp.swapaxes(w_in[l], 0, 1))
    bfv = b_f[l].astype(F32)
    b_f3 = jnp.concatenate([bfv, bfv, bfv, jnp.zeros((LANES - 3 * ATTN_HEADS,), F32)])[None, :]

    (xs_t, szs, k, sza, sgs, sga, qt, vt, caug) = _inproj(
        x, mod4, w_main, w_t, w_f3, b_f3, ts=512)

    bmat, cmat, a_re, a_im = _zoh(lam_re[l], lam_im[l], log_dt[l], ssm_b_re[l], ssm_b_im[l],
                                  ssm_c_re[l], ssm_c_im[l])

    ys_t = _s5(xs_t, bmat, cmat, a_re, a_im,
               ssm_d[l].reshape(1, SSM_WIDTH).astype(F32), w_glu[l].astype(BF16), steps=128, sub=32)

    oz = _attn(qt, k, caug, vt, sza, tq=1024)

    return _merge(x, ys_t, szs, oz, sgs, sga, mod4,
                  w_proj_ssm[l], w_proj_attn[l], w_out[l],
                  ln_g[l].reshape(1, d).astype(F32), ln_b[l].astype(F32).reshape(1, d), ts=1024)
```

```python
import functools
import math

import jax
import jax.numpy as jnp
from jax import lax
from jax.experimental import pallas as pl
from jax.experimental.pallas import tpu as pltpu

D_MODEL = 1024
BATCH = 8
SEQ = 2048
SSM_WIDTH = 512
SSM_GROUP = 16
SSM_GROUPS = 32
SSM_STATE = 64
ATTN_HEADS = 8
ATTN_HEAD_DIM = 64
ATTN_WIDTH = 512
LN_EPS = 1e-5
DEEPNORM_ALPHA = 2.0 ** 0.25

F32 = jnp.float32
BF16 = jnp.bfloat16

LANES = 128
SUBLANES = 8
VMEM_LIMIT = 52 * 1024 * 1024

S5_TILES = 4
S5_TILE_GROUPS = SSM_GROUPS // S5_TILES
S5_HALF = S5_TILE_GROUPS * SSM_STATE
S5_TILE_LANES = 2 * S5_HALF
S5_TILE_IN = S5_TILE_GROUPS * SSM_GROUP
S5_STATE_LANES = S5_TILES * S5_TILE_LANES

C_XS, C_ZS, C_K, C_ZA, C_GS, C_GA, C_END = 0, 512, 1024, 1536, 2048, 3072, 4096

NEG_BIG = -1e30
LOG2E = math.log2(math.e)

INPROJ_SUB = 256
MERGE_SUB = 256
ATTN_TK = 256
ATTN_LOOKAHEAD = 3
ATTN_SLOTS = 4
assert ATTN_SLOTS > ATTN_LOOKAHEAD and ATTN_HEADS % ATTN_SLOTS == 0


def _sigmoid(x):
    return 1.0 / (1.0 + jnp.exp(-x))


def _silu(x):
    return x * _sigmoid(x)


def _cparams(*sem):
    return pltpu.CompilerParams(dimension_semantics=sem, vmem_limit_bytes=VMEM_LIMIT)


def _ada_kernel(c_ref, w_ref, b_ref, o_ref):
    ca = _silu(c_ref[...]).astype(BF16)
    o_ref[...] = jnp.dot(ca, w_ref[...].astype(BF16), preferred_element_type=F32) + b_ref[...]


def _ada(c, w_ada, b_ada):
    n = w_ada.shape[1]
    tn = 1024
    return pl.pallas_call(
        _ada_kernel,
        grid=(n // tn,),
        in_specs=[pl.BlockSpec((BATCH, D_MODEL), lambda j: (0, 0)),
                  pl.BlockSpec((D_MODEL, tn), lambda j: (0, j)),
                  pl.BlockSpec((1, tn), lambda j: (0, j))],
        out_specs=pl.BlockSpec((BATCH, tn), lambda j: (0, j)),
        out_shape=jax.ShapeDtypeStruct((BATCH, n), F32),
        compiler_params=_cparams("arbitrary"),
        name="ada",
    )(c, w_ada, b_ada.reshape(1, n))


def _zoh_kernel(lr_ref, li_ref, ldt_ref, br_ref, bi_ref, cre_ref, cim_ref,
                bmat_ref, cmat_ref, are_ref, aim_ref):
    lr = lr_ref[...]
    li = li_ref[...]
    dt = jnp.exp(ldt_ref[...])
    mag = jnp.exp(lr * dt)
    abr = mag * jnp.cos(li * dt)
    abi = mag * jnp.sin(li * dt)
    den = lr * lr + li * li
    cr = ((abr - 1.0) * lr + abi * li) / den
    ci = (abi * lr - (abr - 1.0) * li) / den

    tg, gh, gp = S5_TILE_GROUPS, SSM_GROUP, SSM_STATE
    rep_lanes = (lax.broadcasted_iota(jnp.int32, (gp, tg * gp), 1) % gp
                 == lax.broadcasted_iota(jnp.int32, (gp, tg * gp), 0))
    rep_rows = (lax.broadcasted_iota(jnp.int32, (tg * gp, gp), 0) % gp
                == lax.broadcasted_iota(jnp.int32, (tg * gp, gp), 1))
    rep_lanes_bf = jnp.where(rep_lanes, 1.0, 0.0).astype(BF16)
    rep_rows_bf = jnp.where(rep_rows, 1.0, 0.0).astype(BF16)
    rep_lanes_f32 = jnp.where(rep_lanes, 1.0, 0.0)
    b_diag = (lax.broadcasted_iota(jnp.int32, (tg * gh, tg * gp), 0) // gh
              == lax.broadcasted_iota(jnp.int32, (tg * gh, tg * gp), 1) // gp)
    c_diag = (lax.broadcasted_iota(jnp.int32, (tg * gp, tg * gh), 0) // gp
              == lax.broadcasted_iota(jnp.int32, (tg * gp, tg * gh), 1) // gh)
    a_diag = (lax.broadcasted_iota(jnp.int32, (tg, tg * gp), 0)
              == lax.broadcasted_iota(jnp.int32, (tg, tg * gp), 1) // gp)

    for j in range(S5_TILES):
        bbr, bbi, crs, cis = [], [], [], []
        for g in range(j * tg, (j + 1) * tg):
            crg, cig = cr[g:g + 1, :], ci[g:g + 1, :]
            br, bi = br_ref[g], bi_ref[g]
            bbr.append(crg * br - cig * bi)
            bbi.append(crg * bi + cig * br)
            crs.append(cre_ref[g])
            cis.append(-cim_ref[g])
        halves = []
        for blk in (jnp.concatenate(bbr, axis=0), jnp.concatenate(bbi, axis=0)):
            tiled = jnp.dot(blk.astype(BF16), rep_lanes_bf, preferred_element_type=F32)
            halves.append(jnp.where(b_diag, tiled, 0.0))
        bmat_ref[j] = jnp.concatenate(halves, axis=1).astype(BF16)
        halves = []
        for blk in (jnp.concatenate(crs, axis=0), jnp.concatenate(cis, axis=0)):
            tiled = lax.dot_general(rep_rows_bf, blk.astype(BF16), (((1,), (1,)), ((), ())),
                                    preferred_element_type=F32)
            halves.append(jnp.where(c_diag, tiled, 0.0))
        cmat_ref[j] = jnp.concatenate(halves, axis=0).astype(BF16)
        for a, a_ref in ((abr, are_ref), (abi, aim_ref)):
            tiled = jnp.dot(a[j * tg:(j + 1) * tg, :], rep_lanes_f32, preferred_element_type=F32,
                            precision=lax.Precision.HIGHEST)
            a_ref[:, j * S5_HALF:(j + 1) * S5_HALF] = jnp.sum(
                jnp.where(a_diag, tiled, 0.0), axis=0, keepdims=True)


def _zoh(lam_re, lam_im, log_dt, b_re, b_im, c_re, c_im):
    swap = lambda a: jnp.swapaxes(a, 1, 2)
    return pl.pallas_call(
        _zoh_kernel,
        out_shape=(jax.ShapeDtypeStruct((S5_TILES, S5_TILE_IN, S5_TILE_LANES), BF16),
                   jax.ShapeDtypeStruct((S5_TILES, S5_TILE_LANES, S5_TILE_IN), BF16),
                   jax.ShapeDtypeStruct((1, SSM_GROUPS * SSM_STATE), F32),
                   jax.ShapeDtypeStruct((1, SSM_GROUPS * SSM_STATE), F32)),
        name="zoh",
    )(lam_re, lam_im, log_dt.reshape(SSM_GROUPS, 1), swap(b_re), swap(b_im), c_re, c_im)


O_XS, O_ZS, O_Q, O_K, O_V, O_F, O_ZA, O_GS, O_GA, O_END = (
    0, 512, 1024, 1536, 2048, 2560, 2568, 3080, 4104, 5128)


WPREP_CHUNK = 256


def _wprep_kernel(wt_ref, wm_ref, wqv_ref, wf_ref):
    def put(c_lo, o_lo, n):
        for s in range(0, n, WPREP_CHUNK):
            wm_ref[:, c_lo + s:c_lo + s + WPREP_CHUNK] = (
                wt_ref[o_lo + s:o_lo + s + WPREP_CHUNK, :].T.astype(BF16))

    put(C_XS, O_XS, O_Q - O_XS)
    put(C_K, O_K, O_V - O_K)
    put(C_ZA, O_ZA, O_GS - O_ZA)
    put(C_GS, O_GS, O_END - O_GS)
    wqv_ref[0:ATTN_WIDTH, :] = wt_ref[O_Q:O_K, :].astype(BF16)
    wqv_ref[ATTN_WIDTH:2 * ATTN_WIDTH, :] = wt_ref[O_V:O_F, :].astype(BF16)
    f8 = wt_ref[O_F:O_ZA, :]
    blk = jnp.concatenate([f8, f8, f8, jnp.zeros((LANES - 3 * ATTN_HEADS, f8.shape[1]), F32)], axis=0)
    wf_ref[...] = blk.T.astype(BF16)


def _wprep(wt):
    n, d = wt.shape
    return pl.pallas_call(
        _wprep_kernel,
        out_shape=[jax.ShapeDtypeStruct((d, C_END), BF16),
                   jax.ShapeDtypeStruct((2 * ATTN_WIDTH, d), BF16),
                   jax.ShapeDtypeStruct((d, LANES), BF16)],
        compiler_params=pltpu.CompilerParams(vmem_limit_bytes=VMEM_LIMIT),
        name="wprep",
    )(wt)


def _inproj_kernel(x_ref, scale_ref, shift_ref, wm_ref, wt_ref, wf_ref, bf_ref, tri_ref,
                   xs_ref, szs_ref, k_ref, sza_ref, sgs_ref, sga_ref, qt_ref, vt_ref, caug_ref,
                   carry_ref, *, ts):
    i = pl.program_id(0)
    b = pl.program_id(1)

    @pl.when(i == 0)
    def _():
        carry_ref[b] = jnp.zeros(carry_ref.shape[1:], F32)

    carry = carry_ref[b][0:1, :]
    for c in range(ts // INPROJ_SUB):
        r = slice(c * INPROJ_SUB, (c + 1) * INPROJ_SUB)
        x = x_ref[0, r, :]
        mu = jnp.mean(x, axis=-1, keepdims=True)
        xc = x - mu
        var = jnp.mean(xc * xc, axis=-1, keepdims=True)
        xn = xc * lax.rsqrt(var + LN_EPS)
        u = xn * (1.0 + scale_ref[0, 0]) + shift_ref[0, 0]
        ub = u.astype(BF16)

        def mm(lo, hi, ub=ub):
            return jnp.dot(ub, wm_ref[:, lo:hi], preferred_element_type=F32)

        f = jnp.dot(ub, wf_ref[...], preferred_element_type=F32) + bf_ref[...]
        lf = jnp.minimum(f, 0.0) - jnp.log(1.0 + jnp.exp(-jnp.abs(f)))
        l1 = lf.astype(BF16)
        r1 = lf - l1.astype(F32)
        l2 = r1.astype(BF16)
        l3 = (r1 - l2.astype(F32)).astype(BF16)

        xs = mm(C_XS, C_ZS)
        for j in range(S5_TILES):
            xs_ref[j, pl.ds(b + c * INPROJ_SUB * SUBLANES, INPROJ_SUB, stride=SUBLANES), :] = (
                xs[:, j * LANES:(j + 1) * LANES])
        szs_ref[0, r, :] = _silu(mm(C_ZS, C_K)).astype(BF16)

        tri = tri_ref[...]
        cf = (jnp.dot(tri, l1, preferred_element_type=F32)
              + jnp.dot(tri, l2, preferred_element_type=F32)
              + jnp.dot(tri, l3, preferred_element_type=F32)) + carry
        carry = cf[INPROJ_SUB - 1:INPROJ_SUB, :]

        kk = mm(C_K, C_ZA).astype(BF16)
        for j in range(ATTN_HEADS // 2):
            k_ref[0, j, r, :] = kk[:, j * LANES:(j + 1) * LANES]
        sza_ref[0, r, :] = _silu(mm(C_ZA, C_GS)).astype(BF16)

        cf2 = cf * LOG2E
        c1 = cf2.astype(BF16).astype(F32)
        rem = cf2 - c1
        c2 = rem.astype(BF16).astype(F32)
        c3 = rem - c2
        lane = lax.broadcasted_iota(jnp.int32, cf.shape, 1)
        parts = jnp.where(lane < 8, c1, jnp.where(lane < 16, c2, jnp.where(lane < 24, c3, 0.0)))
        caug_ref[0, r, :] = (-parts).astype(BF16)

        sgs_ref[0, r, :] = _sigmoid(mm(C_GS, C_GA)).astype(BF16)
        sga_ref[0, r, :] = _sigmoid(mm(C_GA, C_END)).astype(BF16)

        qv = lax.dot_general(wt_ref[...], ub, (((1,), (1,)), ((), ())),
                             preferred_element_type=F32)
        qt_ref[0, :, r] = (qv[:ATTN_WIDTH] * (LOG2E * ATTN_HEAD_DIM ** -0.5)).astype(BF16)
        vt_ref[0, :, r] = qv[ATTN_WIDTH:].astype(BF16)
    carry_ref[b] = jnp.broadcast_to(carry, carry_ref.shape[1:])


def _inproj(x, mod4, w_main, w_t, w_f3, b_f3, *, ts):
    nb, s, d = x.shape
    nt = s // ts
    tri = jnp.tril(jnp.ones((INPROJ_SUB, INPROJ_SUB), BF16))
    tok3 = lambda w: pl.BlockSpec((1, ts, w), lambda i, b: (b, i, 0))
    const2 = lambda a: pl.BlockSpec(a.shape, lambda i, b: (0, 0), pipeline_mode=pl.Buffered(1))
    bf = lambda shape: jax.ShapeDtypeStruct(shape, BF16)
    return pl.pallas_call(
        functools.partial(_inproj_kernel, ts=ts),
        grid=(nt, nb),
        in_specs=[tok3(d), _mod_spec(1), _mod_spec(0),
                  const2(w_main), const2(w_t), const2(w_f3), const2(b_f3), const2(tri)],
        out_specs=[pl.BlockSpec((S5_TILES, ts * nb, LANES), lambda i, b: (0, i, 0)),
                   tok3(SSM_WIDTH),
                   pl.BlockSpec((1, ATTN_HEADS // 2, ts, LANES), lambda i, b: (b, 0, i, 0)),
                   tok3(ATTN_WIDTH),
                   tok3(D_MODEL), tok3(D_MODEL),
                   pl.BlockSpec((1, ATTN_WIDTH, ts), lambda i, b: (b, 0, i)),
                   pl.BlockSpec((1, ATTN_WIDTH, ts), lambda i, b: (b, 0, i)),
                   tok3(LANES)],
        out_shape=[jax.ShapeDtypeStruct((S5_TILES, s * nb, LANES), F32),
                   bf((nb, s, SSM_WIDTH)), bf((nb, ATTN_HEADS // 2, s, LANES)), bf((nb, s, ATTN_WIDTH)),
                   bf((nb, s, D_MODEL)), bf((nb, s, D_MODEL)),
                   bf((nb, ATTN_WIDTH, s)), bf((nb, ATTN_WIDTH, s)),
                   bf((nb, s, LANES))],
        scratch_shapes=[pltpu.VMEM((nb, SUBLANES, LANES), F32)],
        compiler_params=_cparams("arbitrary", "arbitrary"),
        name="inproj",
    )(x, mod4, mod4, w_main, w_t, w_f3, b_f3, tri)


def _gelu_tanh(x):
    return 0.5 * x * (1.0 + jnp.tanh(math.sqrt(2.0 / math.pi) * (x + 0.044715 * (x * x * x))))


def _s5_kernel(xs_ref, bmat_ref, cmat_ref, are_ref, aim_ref, d_ref, wglu_ref,
               y_ref, buf_ref, h_ref, ybuf_ref, *, steps, sub):
    @pl.when(pl.program_id(0) == 0)
    def _():
        h_ref[...] = jnp.zeros_like(h_ref)

    srows = sub * SUBLANES

    def expand(c):
        r = slice(c * srows, (c + 1) * srows)
        for j in range(S5_TILES):
            lo = j * S5_TILE_LANES
            buf_ref[r, lo:lo + S5_TILE_LANES] = jnp.dot(
                xs_ref[j, r, :].astype(BF16), bmat_ref[j], preferred_element_type=F32)

    def recur(c, state):
        new_state = []
        for j in range(S5_TILES):
            re = slice(j * S5_TILE_LANES, j * S5_TILE_LANES + S5_HALF)
            im = slice(j * S5_TILE_LANES + S5_HALF, (j + 1) * S5_TILE_LANES)
            ar = jnp.broadcast_to(are_ref[:, j * S5_HALF:(j + 1) * S5_HALF], (SUBLANES, S5_HALF))
            ai = jnp.broadcast_to(aim_ref[:, j * S5_HALF:(j + 1) * S5_HALF], (SUBLANES, S5_HALF))
            hr, hi = state[j]
            for t in range(sub):
                rt = slice(c * srows + t * SUBLANES, c * srows + (t + 1) * SUBLANES)
                nhr = ar * hr - ai * hi + buf_ref[rt, re]
                nhi = ar * hi + ai * hr + buf_ref[rt, im]
                buf_ref[rt, re] = nhr
                buf_ref[rt, im] = nhi
                hr, hi = nhr, nhi
            new_state.append((hr, hi))
        return new_state

    def project(c):
        r = slice(c * srows, (c + 1) * srows)
        ys = []
        for j in range(S5_TILES):
            lo = j * S5_TILE_LANES
            hb = buf_ref[r, lo:lo + S5_TILE_LANES].astype(BF16)
            ys.append(jnp.dot(hb, cmat_ref[j], preferred_element_type=F32)
                      + d_ref[:, j * LANES:(j + 1) * LANES] * xs_ref[j, r, :])
        g = _gelu_tanh(jnp.concatenate(ys, axis=1)).astype(BF16)
        ab = jnp.dot(g, wglu_ref[...], preferred_element_type=F32)
        out = ab[:, :SSM_WIDTH] * _sigmoid(ab[:, SSM_WIDTH:])
        for j in range(S5_TILES):
            ybuf_ref[j] = out[:, j * LANES:(j + 1) * LANES]
        for b in range(BATCH):
            yb = jnp.concatenate([ybuf_ref[j, pl.ds(b, sub, stride=SUBLANES), :]
                                  for j in range(S5_TILES)], axis=1)
            y_ref[b, c * sub:(c + 1) * sub, :] = yb.astype(BF16)

    state = [(h_ref[:, j * S5_TILE_LANES:j * S5_TILE_LANES + S5_HALF],
              h_ref[:, j * S5_TILE_LANES + S5_HALF:(j + 1) * S5_TILE_LANES])
             for j in range(S5_TILES)]
    nsub = steps // sub
    expand(0)
    for c in range(nsub):
        if c + 1 < nsub:
            expand(c + 1)
        state = recur(c, state)
        project(c)
    for j in range(S5_TILES):
        h_ref[:, j * S5_TILE_LANES:j * S5_TILE_LANES + S5_HALF] = state[j][0]
        h_ref[:, j * S5_TILE_LANES + S5_HALF:(j + 1) * S5_TILE_LANES] = state[j][1]


def _s5(xs_t, bmat, cmat, a_re, a_im, d_skip, w_glu, *, steps, sub):
    t = xs_t.shape[1]
    rows = steps * SUBLANES
    const = lambda a: pl.BlockSpec(a.shape, lambda i: (0,) * a.ndim)
    tiles = pl.BlockSpec((S5_TILES, rows, LANES), lambda i: (0, i, 0))
    return pl.pallas_call(
        functools.partial(_s5_kernel, steps=steps, sub=sub),
        grid=(t // rows,),
        in_specs=[tiles,
                  const(bmat), const(cmat), const(a_re), const(a_im), const(d_skip), const(w_glu)],
        out_specs=pl.BlockSpec((BATCH, steps, SSM_WIDTH), lambda i: (0, i, 0)),
        out_shape=jax.ShapeDtypeStruct((BATCH, t // SUBLANES, SSM_WIDTH), BF16),
        scratch_shapes=[pltpu.VMEM((rows, S5_STATE_LANES + LANES), F32),
                        pltpu.VMEM((SUBLANES, S5_STATE_LANES), F32),
                        pltpu.VMEM((S5_TILES, sub * SUBLANES, LANES), F32)],
        compiler_params=_cparams("arbitrary"),
        name="s5",
    )(xs_t, bmat, cmat, a_re, a_im, d_skip, w_glu)


def _attn_kernel(qt_ref, k_ref, caug_ref, vt_ref, sza_ref, o_ref,
                 rhs_ref, s_ref, m_ref, l_ref, acc_ref, *, tq):
    qi = pl.program_id(1)
    tk = ATTN_TK
    nd = tq // tk
    hd = ATTN_HEAD_DIM
    row = lax.broadcasted_iota(jnp.int32, (LANES, tq), 0)

    for h in range(ATTN_HEADS):
        j = h // 2
        q2 = qt_ref[0, j * LANES:(j + 1) * LANES, :]
        rhs_ref[h, 0:LANES, 0:tq] = jnp.where((row // hd) == (h % 2), q2, jnp.zeros_like(q2))
        rhs_ref[h, LANES:2 * LANES, 0:tq] = jnp.where(
            (row == h) | (row == 8 + h) | (row == 16 + h), 1.0, 0.0).astype(BF16)
    m_ref[...] = jnp.full(m_ref.shape, -jnp.inf, F32)
    l_ref[...] = jnp.zeros_like(l_ref)
    acc_ref[...] = jnp.zeros_like(acc_ref)

    def scores(k0, h, diag=None):
        j = h // 2
        lo = 0 if diag is None else diag
        lhs = jnp.concatenate([k_ref[0, j, pl.ds(k0, tk), :],
                               caug_ref[0, pl.ds(k0, tk), :]], axis=1)
        s_ref[h % ATTN_SLOTS, :, lo:tq] = jnp.dot(lhs, rhs_ref[h, :, lo:tq],
                                                  preferred_element_type=F32)

    def update(k0, h, diag=None):
        lo = 0 if diag is None else diag
        s = s_ref[h % ATTN_SLOTS, :, lo:tq]
        if diag is not None:
            kpos = lax.broadcasted_iota(jnp.int32, s.shape, 0)
            qpos = lax.broadcasted_iota(jnp.int32, s.shape, 1)
            s = jnp.where(kpos <= qpos, s, NEG_BIG)
        hs = slice(h * SUBLANES, h * SUBLANES + 1)
        m = m_ref[hs, lo:tq]
        m_new = jnp.maximum(m, jnp.max(s, axis=0, keepdims=True))
        alpha = jnp.exp2(m - m_new)
        p = jnp.exp2(s - m_new).astype(BF16)
        m_ref[hs, lo:tq] = m_new
        vt1 = jnp.concatenate([vt_ref[0, h * hd:(h + 1) * hd, pl.ds(k0, tk)],
                               jnp.ones((2 * SUBLANES, tk), BF16)], axis=0)
        pv = jnp.dot(vt1, p, preferred_element_type=F32)
        l_ref[hs, lo:tq] = alpha * l_ref[hs, lo:tq] + pv[hd:hd + 1, :]
        acc_ref[h * hd:(h + 1) * hd, lo:tq] = (alpha * acc_ref[h * hd:(h + 1) * hd, lo:tq]
                                               + pv[:hd, :])

    def body(kj, carry):
        k0 = pl.multiple_of(kj * tk, tk)
        k1 = pl.multiple_of(kj * tk + tk, tk)
        for h in range(ATTN_HEADS):
            ha = h + ATTN_LOOKAHEAD
            if ha < ATTN_HEADS:
                scores(k0, ha)
            else:
                scores(k1, ha - ATTN_HEADS)
            update(k0, h)
        return carry

    for h in range(ATTN_LOOKAHEAD):
        scores(0, h)
    lax.fori_loop(0, qi * nd, body, 0)
    for d in range(nd):
        kd = pl.multiple_of(qi * tq + d * tk, tk)
        for h in range(ATTN_HEADS):
            ha = h + ATTN_LOOKAHEAD
            if ha < ATTN_HEADS:
                scores(kd, ha, d * tk)
            elif d + 1 < nd:
                scores(pl.multiple_of(kd + tk, tk), ha - ATTN_HEADS, (d + 1) * tk)
            update(kd, h, d * tk)

    for h in range(ATTN_HEADS):
        acc_ref[h * hd:(h + 1) * hd, 0:tq] = (acc_ref[h * hd:(h + 1) * hd, 0:tq]
                                              / l_ref[h * SUBLANES:h * SUBLANES + 1, :])
    o = acc_ref[:, 0:tq].T
    o_ref[0] = (o * sza_ref[0].astype(F32)).astype(BF16)


def _attn(qt, k, caug, vt, sza, *, tq):
    nb, s, w = sza.shape
    return pl.pallas_call(
        functools.partial(_attn_kernel, tq=tq),
        grid=(nb, s // tq),
        in_specs=[pl.BlockSpec((1, w, tq), lambda b, i: (b, 0, i)),
                  pl.BlockSpec((1, ATTN_HEADS // 2, s, LANES), lambda b, i: (b, 0, 0, 0)),
                  pl.BlockSpec((1, s, LANES), lambda b, i: (b, 0, 0)),
                  pl.BlockSpec((1, w, s), lambda b, i: (b, 0, 0)),
                  pl.BlockSpec((1, tq, w), lambda b, i: (b, i, 0))],
        out_specs=pl.BlockSpec((1, tq, w), lambda b, i: (b, i, 0)),
        out_shape=jax.ShapeDtypeStruct((nb, s, w), BF16),
        scratch_shapes=[pltpu.VMEM((ATTN_HEADS, 2 * LANES, tq + LANES), BF16),
                        pltpu.VMEM((ATTN_SLOTS, ATTN_TK, tq + LANES), F32),
                        pltpu.VMEM((ATTN_HEADS * SUBLANES, tq), F32),
                        pltpu.VMEM((ATTN_HEADS * SUBLANES, tq), F32),
                        pltpu.VMEM((w, tq + LANES), F32)],
        compiler_params=_cparams("arbitrary", "arbitrary"),
        name="attn",
    )(qt, k, caug, vt, sza)


def _merge_kernel(x_ref, ys_ref, szs_ref, oz_ref, sgs_ref, sga_ref, gate_ref,
                  wps_ref, wpa_ref, wout_ref, lng_ref, lnb_ref, o_ref,
                  wps_bf, wpa_bf, wout_bf, *, ts):
    @pl.when((pl.program_id(0) == 0) & (pl.program_id(1) == 0))
    def _():
        wps_bf[...] = wps_ref[...].astype(BF16)
        wpa_bf[...] = wpa_ref[...].astype(BF16)
        wout_bf[...] = wout_ref[...].astype(BF16)

    gate = gate_ref[0, 0]
    for c in range(ts // MERGE_SUB):
        r = slice(c * MERGE_SUB, (c + 1) * MERGE_SUB)
        a = (ys_ref[0, r, :].astype(F32) * szs_ref[0, r, :].astype(F32)).astype(BF16)
        m1 = jnp.dot(a, wps_bf[...], preferred_element_type=F32) * sgs_ref[0, r, :].astype(F32)
        m2 = (jnp.dot(oz_ref[0, r, :], wpa_bf[...], preferred_element_type=F32)
              * sga_ref[0, r, :].astype(F32))
        merged = (m1 + m2).astype(BF16)
        sub = jnp.dot(merged, wout_bf[...], preferred_element_type=F32) * gate
        y = DEEPNORM_ALPHA * x_ref[0, r, :] + sub
        mu = jnp.mean(y, axis=-1, keepdims=True)
        yc = y - mu
        var = jnp.mean(yc * yc, axis=-1, keepdims=True)
        o_ref[0, r, :] = yc * lax.rsqrt(var + LN_EPS) * lng_ref[...] + lnb_ref[...]


def _mod_spec(part):
    return pl.BlockSpec((1, 1, 1, D_MODEL), lambda i, b: (b, part, 0, 0))


def _merge(x, ys_t, szs, oz, sgs, sga, mod4, wps, wpa, wout, ln_g, ln_b, *, ts):
    nb, s, d = x.shape
    tok3 = lambda w: pl.BlockSpec((1, ts, w), lambda i, b: (b, i, 0))
    const2 = lambda a: pl.BlockSpec(a.shape, lambda i, b: (0, 0), pipeline_mode=pl.Buffered(1))
    return pl.pallas_call(
        functools.partial(_merge_kernel, ts=ts),
        grid=(s // ts, nb),
        in_specs=[tok3(d), tok3(SSM_WIDTH),
                  tok3(SSM_WIDTH), tok3(ATTN_WIDTH), tok3(d), tok3(d),
                  _mod_spec(2),
                  const2(wps), const2(wpa), const2(wout), const2(ln_g), const2(ln_b)],
        out_specs=tok3(d),
        out_shape=jax.ShapeDtypeStruct((nb, s, d), x.dtype),
        scratch_shapes=[pltpu.VMEM(wps.shape, BF16), pltpu.VMEM(wpa.shape, BF16),
                        pltpu.VMEM(wout.shape, BF16)],
        compiler_params=_cparams("arbitrary", "arbitrary"),
        name="merge",
    )(x, ys_t, szs, oz, sgs, sga, mod4, wps, wpa, wout, ln_g, ln_b)


def kernel(x, c, w_ada, b_ada, w_in, b_f, lam_re, lam_im, log_dt, ssm_b_re, ssm_b_im,
           ssm_c_re, ssm_c_im, ssm_d, w_glu, w_proj_ssm, w_proj_attn, w_out, ln_g, ln_b):
    nb, s, d = x.shape
    l = 0
    mod4 = _ada(c, w_ada[l], b_ada[l]).reshape(nb, 3, 1, d)

    w_main, w_t, w_f3 = _wprep(jnp.swapaxes(w_in[l], 0, 1))
    bfv = b_f[l].astype(F32)
    b_f3 = jnp.concatenate([bfv, bfv, bfv, jnp.zeros((LANES - 3 * ATTN_HEADS,), F32)])[None, :]

    (xs_t, szs, k, sza, sgs, sga, qt, vt, caug) = _inproj(
        x, mod4, w_main, w_t, w_f3, b_f3, ts=512)

    bmat, cmat, a_re, a_im = _zoh(lam_re[l], lam_im[l], log_dt[l], ssm_b_re[l], ssm_b_im[l],
                                  ssm_c_re[l], ssm_c_im[l])

    ys_t = _s5(xs_t, bmat, cmat, a_re, a_im,
               ssm_d[l].reshape(1, SSM_WIDTH).astype(F32), w_glu[l].astype(BF16), steps=128, sub=32)

    oz = _attn(qt, k, caug, vt, sza, tq=1024)

    return _merge(x, ys_t, szs, oz, sgs, sga, mod4,
                  w_proj_ssm[l], w_proj_attn[l], w_out[l],
                  ln_g[l].reshape(1, d).astype(F32), ln_b[l].astype(F32).reshape(1, d), ts=1024)
```

```python
import functools
import math

import jax
import jax.numpy as jnp
from jax import lax
from jax.experimental import pallas as pl
from jax.experimental.pallas import tpu as pltpu

D_MODEL = 1024
BATCH = 8
SEQ = 2048
SSM_WIDTH = 512
SSM_GROUP = 16
SSM_GROUPS = 32
SSM_STATE = 64
ATTN_HEADS = 8
ATTN_HEAD_DIM = 64
ATTN_WIDTH = 512
LN_EPS = 1e-5
DEEPNORM_ALPHA = 2.0 ** 0.25

F32 = jnp.float32
BF16 = jnp.bfloat16

LANES = 128
SUBLANES = 8
VMEM_LIMIT = 52 * 1024 * 1024

S5_TILES = 4
S5_TILE_GROUPS = SSM_GROUPS // S5_TILES
S5_HALF = S5_TILE_GROUPS * SSM_STATE
S5_TILE_LANES = 2 * S5_HALF
S5_TILE_IN = S5_TILE_GROUPS * SSM_GROUP
S5_STATE_LANES = S5_TILES * S5_TILE_LANES

C_XS, C_ZS, C_K, C_ZA, C_GS, C_GA, C_END = 0, 512, 1024, 1536, 2048, 3072, 4096

NEG_BIG = -1e30
LOG2E = math.log2(math.e)

INPROJ_SUB = 256
MERGE_SUB = 256
ATTN_TK = 256
ATTN_LOOKAHEAD = 3
ATTN_SLOTS = 4
assert ATTN_SLOTS > ATTN_LOOKAHEAD and ATTN_HEADS % ATTN_SLOTS == 0


def _sigmoid(x):
    return 1.0 / (1.0 + jnp.exp(-x))


def _silu(x):
    return x * _sigmoid(x)


def _cparams(*sem):
    return pltpu.CompilerParams(dimension_semantics=sem, vmem_limit_bytes=VMEM_LIMIT)


def _ada_kernel(c_ref, w_ref, b_ref, o_ref):
    ca = _silu(c_ref[...]).astype(BF16)
    o_ref[...] = jnp.dot(ca, w_ref[...].astype(BF16), preferred_element_type=F32) + b_ref[...]


def _ada(c, w_ada, b_ada):
    n = w_ada.shape[1]
    tn = 1024
    return pl.pallas_call(
        _ada_kernel,
        grid=(n // tn,),
        in_specs=[pl.BlockSpec((BATCH, D_MODEL), lambda j: (0, 0)),
                  pl.BlockSpec((D_MODEL, tn), lambda j: (0, j)),
                  pl.BlockSpec((1, tn), lambda j: (0, j))],
        out_specs=pl.BlockSpec((BATCH, tn), lambda j: (0, j)),
        out_shape=jax.ShapeDtypeStruct((BATCH, n), F32),
        compiler_params=_cparams("arbitrary"),
        name="ada",
    )(c, w_ada, b_ada.reshape(1, n))


def _zoh_kernel(lr_ref, li_ref, ldt_ref, br_ref, bi_ref, cre_ref, cim_ref,
                bmat_ref, cmat_ref, are_ref, aim_ref):
    lr = lr_ref[...]
    li = li_ref[...]
    dt = jnp.exp(ldt_ref[...])
    mag = jnp.exp(lr * dt)
    abr = mag * jnp.cos(li * dt)
    abi = mag * jnp.sin(li * dt)
    den = lr * lr + li * li
    cr = ((abr - 1.0) * lr + abi * li) / den
    ci = (abi * lr - (abr - 1.0) * li) / den

    tg, gh, gp = S5_TILE_GROUPS, SSM_GROUP, SSM_STATE
    rep_lanes = (lax.broadcasted_iota(jnp.int32, (gp, tg * gp), 1) % gp
                 == lax.broadcasted_iota(jnp.int32, (gp, tg * gp), 0))
    rep_rows = (lax.broadcasted_iota(jnp.int32, (tg * gp, gp), 0) % gp
                == lax.broadcasted_iota(jnp.int32, (tg * gp, gp), 1))
    rep_lanes_bf = jnp.where(rep_lanes, 1.0, 0.0).astype(BF16)
    rep_rows_bf = jnp.where(rep_rows, 1.0, 0.0).astype(BF16)
    rep_lanes_f32 = jnp.where(rep_lanes, 1.0, 0.0)
    b_diag = (lax.broadcasted_iota(jnp.int32, (tg * gh, tg * gp), 0) // gh
              == lax.broadcasted_iota(jnp.int32, (tg * gh, tg * gp), 1) // gp)
    c_diag = (lax.broadcasted_iota(jnp.int32, (tg * gp, tg * gh), 0) // gp
              == lax.broadcasted_iota(jnp.int32, (tg * gp, tg * gh), 1) // gh)
    a_diag = (lax.broadcasted_iota(jnp.int32, (tg, tg * gp), 0)
              == lax.broadcasted_iota(jnp.int32, (tg, tg * gp), 1) // gp)

    for j in range(S5_TILES):
        bbr, bbi, crs, cis = [], [], [], []
        for g in range(j * tg, (j + 1) * tg):
            crg, cig = cr[g:g + 1, :], ci[g:g + 1, :]
            br, bi = br_ref[g], bi_ref[g]
            bbr.append(crg * br - cig * bi)
            bbi.append(crg * bi + cig * br)
            crs.append(cre_ref[g])
            cis.append(-cim_ref[g])
        halves = []
        for blk in (jnp.concatenate(bbr, axis=0), jnp.concatenate(bbi, axis=0)):
            tiled = jnp.dot(blk.astype(BF16), rep_lanes_bf, preferred_element_type=F32)
            halves.append(jnp.where(b_diag, tiled, 0.0))
        bmat_ref[j] = jnp.concatenate(halves, axis=1).astype(BF16)
        halves = []
        for blk in (jnp.concatenate(crs, axis=0), jnp.concatenate(cis, axis=0)):
            tiled = lax.dot_general(rep_rows_bf, blk.astype(BF16), (((1,), (1,)), ((), ())),
                                    preferred_element_type=F32)
            halves.append(jnp.where(c_diag, tiled, 0.0))
        cmat_ref[j] = jnp.concatenate(halves, axis=0).astype(BF16)
        for a, a_ref in ((abr, are_ref), (abi, aim_ref)):
            tiled = jnp.dot(a[j * tg:(j + 1) * tg, :], rep_lanes_f32, preferred_element_type=F32,
                            precision=lax.Precision.HIGHEST)
            a_ref[:, j * S5_HALF:(j + 1) * S5_HALF] = jnp.sum(
                jnp.where(a_diag, tiled, 0.0), axis=0, keepdims=True)


def _zoh(lam_re, lam_im, log_dt, b_re, b_im, c_re, c_im):
    swap = lambda a: jnp.swapaxes(a, 1, 2)
    return pl.pallas_call(
        _zoh_kernel,
        out_shape=(jax.ShapeDtypeStruct((S5_TILES, S5_TILE_IN, S5_TILE_LANES), BF16),
                   jax.ShapeDtypeStruct((S5_TILES, S5_TILE_LANES, S5_TILE_IN), BF16),
                   jax.ShapeDtypeStruct((1, SSM_GROUPS * SSM_STATE), F32),
                   jax.ShapeDtypeStruct((1, SSM_GROUPS * SSM_STATE), F32)),
        name="zoh",
    )(lam_re, lam_im, log_dt.reshape(SSM_GROUPS, 1), swap(b_re), swap(b_im), c_re, c_im)


O_XS, O_ZS, O_Q, O_K, O_V, O_F, O_ZA, O_GS, O_GA, O_END = (
    0, 512, 1024, 1536, 2048, 2560, 2568, 3080, 4104, 5128)


WPREP_CHUNK = 256


def _wprep_kernel(wt_ref, wm_ref, wqv_ref, wf_ref):
    def put(c_lo, o_lo, n):
        for s in range(0, n, WPREP_CHUNK):
            wm_ref[:, c_lo + s:c_lo + s + WPREP_CHUNK] = (
                wt_ref[o_lo + s:o_lo + s + WPREP_CHUNK, :].T.astype(BF16))

    put(C_XS, O_XS, O_Q - O_XS)
    put(C_K, O_K, O_V - O_K)
    put(C_ZA, O_ZA, O_GS - O_ZA)
    put(C_GS, O_GS, O_END - O_GS)
    wm_ref[:, C_END:] = jnp.zeros((wm_ref.shape[0], LANES), BF16)
    wqv_ref[0:ATTN_WIDTH, :] = wt_ref[O_Q:O_K, :].astype(BF16)
    wqv_ref[ATTN_WIDTH:2 * ATTN_WIDTH, :] = wt_ref[O_V:O_F, :].astype(BF16)
    f8 = wt_ref[O_F:O_ZA, :]
    blk = jnp.concatenate([f8, f8, f8, jnp.zeros((LANES - 3 * ATTN_HEADS, f8.shape[1]), F32)], axis=0)
    wf_ref[...] = blk.T.astype(BF16)


def _wprep(wt):
    n, d = wt.shape
    return pl.pallas_call(
        _wprep_kernel,
        out_shape=[jax.ShapeDtypeStruct((d, C_END + LANES), BF16),
                   jax.ShapeDtypeStruct((2 * ATTN_WIDTH, d), BF16),
                   jax.ShapeDtypeStruct((d, LANES), BF16)],
        compiler_params=pltpu.CompilerParams(vmem_limit_bytes=VMEM_LIMIT),
        name="wprep",
    )(wt)


def _inproj_kernel(x_ref, scale_ref, shift_ref, wm_ref, wt_ref, wf_ref, bf_ref, tri_ref,
                   xs_ref, szs_ref, k_ref, sza_ref, sgs_ref, sga_ref, qt_ref, vt_ref, caug_ref,
                   carry_ref, *, ts):
    i = pl.program_id(0)
    b = pl.program_id(1)

    @pl.when(i == 0)
    def _():
        carry_ref[b] = jnp.zeros(carry_ref.shape[1:], F32)

    carry = carry_ref[b][0:1, :]
    for c in range(ts // INPROJ_SUB):
        r = slice(c * INPROJ_SUB, (c + 1) * INPROJ_SUB)
        x = x_ref[0, r, :]
        mu = jnp.mean(x, axis=-1, keepdims=True)
        xc = x - mu
        var = jnp.mean(xc * xc, axis=-1, keepdims=True)
        xn = xc * lax.rsqrt(var + LN_EPS)
        u = xn * (1.0 + scale_ref[0, 0]) + shift_ref[0, 0]
        ub = u.astype(BF16)

        def mm(lo, hi, ub=ub):
            return jnp.dot(ub, wm_ref[:, lo:hi], preferred_element_type=F32)

        f = jnp.dot(ub, wf_ref[...], preferred_element_type=F32) + bf_ref[...]
        lf = jnp.minimum(f, 0.0) - jnp.log(1.0 + jnp.exp(-jnp.abs(f)))
        l1 = lf.astype(BF16)
        r1 = lf - l1.astype(F32)
        l2 = r1.astype(BF16)
        l3 = (r1 - l2.astype(F32)).astype(BF16)

        xs = mm(C_XS, C_ZS)
        for j in range(S5_TILES):
            xs_ref[j, pl.ds(b + c * INPROJ_SUB * SUBLANES, INPROJ_SUB, stride=SUBLANES), :] = (
                xs[:, j * LANES:(j + 1) * LANES])
        szs_ref[0, r, :] = _silu(mm(C_ZS, C_K)).astype(BF16)

        tri = tri_ref[...]
        cf = (jnp.dot(tri, l1, preferred_element_type=F32)
              + jnp.dot(tri, l2, preferred_element_type=F32)
              + jnp.dot(tri, l3, preferred_element_type=F32)) + carry
        carry = cf[INPROJ_SUB - 1:INPROJ_SUB, :]

        k_ref[0, r, :] = mm(C_K, C_ZA).astype(BF16)
        sza_ref[0, r, :] = _silu(mm(C_ZA, C_GS)).astype(BF16)

        cf2 = cf * LOG2E
        c1 = cf2.astype(BF16).astype(F32)
        rem = cf2 - c1
        c2 = rem.astype(BF16).astype(F32)
        c3 = rem - c2
        lane = lax.broadcasted_iota(jnp.int32, cf.shape, 1)
        parts = jnp.where(lane < 8, c1, jnp.where(lane < 16, c2, jnp.where(lane < 24, c3, 0.0)))
        caug_ref[0, r, :] = (-parts).astype(BF16)

        sgs_ref[0, r, :] = _sigmoid(mm(C_GS, C_GA)).astype(BF16)
        sga_ref[0, r, :] = _sigmoid(mm(C_GA, C_END)).astype(BF16)

        qv = lax.dot_general(wt_ref[...], ub, (((1,), (1,)), ((), ())),
                             preferred_element_type=F32)
        qt_ref[0, :, r] = (qv[:ATTN_WIDTH] * (LOG2E * ATTN_HEAD_DIM ** -0.5)).astype(BF16)
        vt_ref[0, :, r] = qv[ATTN_WIDTH:].astype(BF16)
    carry_ref[b] = jnp.broadcast_to(carry, carry_ref.shape[1:])


def _inproj(x, mod4, w_main, w_t, w_f3, b_f3, *, ts):
    nb, s, d = x.shape
    nt = s // ts
    tri = jnp.tril(jnp.ones((INPROJ_SUB, INPROJ_SUB), BF16))
    tok3 = lambda w: pl.BlockSpec((1, ts, w), lambda i, b: (b, i, 0))
    const2 = lambda a: pl.BlockSpec(a.shape, lambda i, b: (0, 0), pipeline_mode=pl.Buffered(1))
    bf = lambda shape: jax.ShapeDtypeStruct(shape, BF16)
    return pl.pallas_call(
        functools.partial(_inproj_kernel, ts=ts),
        grid=(nt, nb),
        in_specs=[tok3(d), _mod_spec(1), _mod_spec(0),
                  const2(w_main), const2(w_t), const2(w_f3), const2(b_f3), const2(tri)],
        out_specs=[pl.BlockSpec((S5_TILES, ts * nb, LANES), lambda i, b: (0, i, 0)),
                   tok3(SSM_WIDTH), tok3(ATTN_WIDTH), tok3(ATTN_WIDTH),
                   tok3(D_MODEL), tok3(D_MODEL),
                   pl.BlockSpec((1, ATTN_WIDTH, ts), lambda i, b: (b, 0, i)),
                   pl.BlockSpec((1, ATTN_WIDTH, ts), lambda i, b: (b, 0, i)),
                   tok3(LANES)],
        out_shape=[jax.ShapeDtypeStruct((S5_TILES, s * nb, LANES), F32),
                   bf((nb, s, SSM_WIDTH)), bf((nb, s, ATTN_WIDTH)), bf((nb, s, ATTN_WIDTH)),
                   bf((nb, s, D_MODEL)), bf((nb, s, D_MODEL)),
                   bf((nb, ATTN_WIDTH, s)), bf((nb, ATTN_WIDTH, s)),
                   bf((nb, s, LANES))],
        scratch_shapes=[pltpu.VMEM((nb, SUBLANES, LANES), F32)],
        compiler_params=_cparams("arbitrary", "arbitrary"),
        name="inproj",
    )(x, mod4, mod4, w_main, w_t, w_f3, b_f3, tri)


def _gelu_tanh(x):
    return 0.5 * x * (1.0 + jnp.tanh(math.sqrt(2.0 / math.pi) * (x + 0.044715 * (x * x * x))))


def _s5_kernel(xs_ref, bmat_ref, cmat_ref, are_ref, aim_ref, d_ref, wglu_ref,
               y_ref, buf_ref, h_ref, ybuf_ref, *, steps, sub):
    @pl.when(pl.program_id(0) == 0)
    def _():
        h_ref[...] = jnp.zeros_like(h_ref)

    srows = sub * SUBLANES

    def expand(c):
        r = slice(c * srows, (c + 1) * srows)
        for j in range(S5_TILES):
            lo = j * S5_TILE_LANES
            buf_ref[r, lo:lo + S5_TILE_LANES] = jnp.dot(
                xs_ref[j, r, :].astype(BF16), bmat_ref[j], preferred_element_type=F32)

    def recur(c, state):
        new_state = []
        for j in range(S5_TILES):
            re = slice(j * S5_TILE_LANES, j * S5_TILE_LANES + S5_HALF)
            im = slice(j * S5_TILE_LANES + S5_HALF, (j + 1) * S5_TILE_LANES)
            ar = jnp.broadcast_to(are_ref[:, j * S5_HALF:(j + 1) * S5_HALF], (SUBLANES, S5_HALF))
            ai = jnp.broadcast_to(aim_ref[:, j * S5_HALF:(j + 1) * S5_HALF], (SUBLANES, S5_HALF))
            hr, hi = state[j]
            for t in range(sub):
                rt = slice(c * srows + t * SUBLANES, c * srows + (t + 1) * SUBLANES)
                nhr = ar * hr - ai * hi + buf_ref[rt, re]
                nhi = ar * hi + ai * hr + buf_ref[rt, im]
                buf_ref[rt, re] = nhr
                buf_ref[rt, im] = nhi
                hr, hi = nhr, nhi
            new_state.append((hr, hi))
        return new_state

    def project(c):
        r = slice(c * srows, (c + 1) * srows)
        ys = []
        for j in range(S5_TILES):
            lo = j * S5_TILE_LANES
            hb = buf_ref[r, lo:lo + S5_TILE_LANES].astype(BF16)
            ys.append(jnp.dot(hb, cmat_ref[j], preferred_element_type=F32)
                      + d_ref[:, j * LANES:(j + 1) * LANES] * xs_ref[j, r, :])
        g = _gelu_tanh(jnp.concatenate(ys, axis=1)).astype(BF16)
        ab = jnp.dot(g, wglu_ref[...], preferred_element_type=F32)
        out = ab[:, :SSM_WIDTH] * _sigmoid(ab[:, SSM_WIDTH:])
        for j in range(S5_TILES):
            ybuf_ref[j] = out[:, j * LANES:(j + 1) * LANES]
        for b in range(BATCH):
            yb = jnp.concatenate([ybuf_ref[j, pl.ds(b, sub, stride=SUBLANES), :]
                                  for j in range(S5_TILES)], axis=1)
            y_ref[b, c * sub:(c + 1) * sub, :] = yb.astype(BF16)

    state = [(h_ref[:, j * S5_TILE_LANES:j * S5_TILE_LANES + S5_HALF],
              h_ref[:, j * S5_TILE_LANES + S5_HALF:(j + 1) * S5_TILE_LANES])
             for j in range(S5_TILES)]
    nsub = steps // sub
    expand(0)
    for c in range(nsub):
        if c + 1 < nsub:
            expand(c + 1)
        state = recur(c, state)
        project(c)
    for j in range(S5_TILES):
        h_ref[:, j * S5_TILE_LANES:j * S5_TILE_LANES + S5_HALF] = state[j][0]
        h_ref[:, j * S5_TILE_LANES + S5_HALF:(j + 1) * S5_TILE_LANES] = state[j][1]


def _s5(xs_t, bmat, cmat, a_re, a_im, d_skip, w_glu, *, steps, sub):
    t = xs_t.shape[1]
    rows = steps * SUBLANES
    const = lambda a: pl.BlockSpec(a.shape, lambda i: (0,) * a.ndim)
    tiles = pl.BlockSpec((S5_TILES, rows, LANES), lambda i: (0, i, 0))
    return pl.pallas_call(
        functools.partial(_s5_kernel, steps=steps, sub=sub),
        grid=(t // rows,),
        in_specs=[tiles,
                  const(bmat), const(cmat), const(a_re), const(a_im), const(d_skip), const(w_glu)],
        out_specs=pl.BlockSpec((BATCH, steps, SSM_WIDTH), lambda i: (0, i, 0)),
        out_shape=jax.ShapeDtypeStruct((BATCH, t // SUBLANES, SSM_WIDTH), BF16),
        scratch_shapes=[pltpu.VMEM((rows, S5_STATE_LANES), F32),
                        pltpu.VMEM((SUBLANES, S5_STATE_LANES), F32),
                        pltpu.VMEM((S5_TILES, sub * SUBLANES, LANES), F32)],
        compiler_params=_cparams("arbitrary"),
        name="s5",
    )(xs_t, bmat, cmat, a_re, a_im, d_skip, w_glu)


def _attn_kernel(qt_ref, k_ref, caug_ref, vt_ref, sza_ref, o_ref,
                 rhs_ref, s_ref, m_ref, l_ref, acc_ref, *, tq):
    qi = pl.program_id(1)
    tk = ATTN_TK
    nd = tq // tk
    hd = ATTN_HEAD_DIM
    row = lax.broadcasted_iota(jnp.int32, (LANES, tq), 0)

    for h in range(ATTN_HEADS):
        j = h // 2
        q2 = qt_ref[0, j * LANES:(j + 1) * LANES, :]
        rhs_ref[h, 0:LANES, 0:tq] = jnp.where((row // hd) == (h % 2), q2, jnp.zeros_like(q2))
        rhs_ref[h, LANES:2 * LANES, 0:tq] = jnp.where(
            (row == h) | (row == 8 + h) | (row == 16 + h), 1.0, 0.0).astype(BF16)
    m_ref[...] = jnp.full(m_ref.shape, -jnp.inf, F32)
    l_ref[...] = jnp.zeros_like(l_ref)
    acc_ref[...] = jnp.zeros_like(acc_ref)

    def scores(k0, h, diag=None):
        j = h // 2
        lo = 0 if diag is None else diag
        lhs = jnp.concatenate([k_ref[0, pl.ds(k0, tk), j * LANES:(j + 1) * LANES],
                               caug_ref[0, pl.ds(k0, tk), :]], axis=1)
        s_ref[h % ATTN_SLOTS, :, lo:tq] = jnp.dot(lhs, rhs_ref[h, :, lo:tq],
                                                  preferred_element_type=F32)

    def update(k0, h, diag=None):
        lo = 0 if diag is None else diag
        s = s_ref[h % ATTN_SLOTS, :, lo:tq]
        if diag is not None:
            kpos = lax.broadcasted_iota(jnp.int32, s.shape, 0)
            qpos = lax.broadcasted_iota(jnp.int32, s.shape, 1)
            s = jnp.where(kpos <= qpos, s, NEG_BIG)
        hs = slice(h * SUBLANES, h * SUBLANES + 1)
        m = m_ref[hs, lo:tq]
        m_new = jnp.maximum(m, jnp.max(s, axis=0, keepdims=True))
        alpha = jnp.exp2(m - m_new)
        p = jnp.exp2(s - m_new).astype(BF16)
        m_ref[hs, lo:tq] = m_new
        vt1 = jnp.concatenate([vt_ref[0, h * hd:(h + 1) * hd, pl.ds(k0, tk)],
                               jnp.ones((2 * SUBLANES, tk), BF16)], axis=0)
        pv = jnp.dot(vt1, p, preferred_element_type=F32)
        l_ref[hs, lo:tq] = alpha * l_ref[hs, lo:tq] + pv[hd:hd + 1, :]
        acc_ref[h * hd:(h + 1) * hd, lo:tq] = (alpha * acc_ref[h * hd:(h + 1) * hd, lo:tq]
                                               + pv[:hd, :])

    def body(kj, carry):
        k0 = pl.multiple_of(kj * tk, tk)
        k1 = pl.multiple_of(kj * tk + tk, tk)
        for h in range(ATTN_HEADS):
            ha = h + ATTN_LOOKAHEAD
            if ha < ATTN_HEADS:
                scores(k0, ha)
            else:
                scores(k1, ha - ATTN_HEADS)
            update(k0, h)
        return carry

    for h in range(ATTN_LOOKAHEAD):
        scores(0, h)
    lax.fori_loop(0, qi * nd, body, 0)
    for d in range(nd):
        kd = pl.multiple_of(qi * tq + d * tk, tk)
        for h in range(ATTN_HEADS):
            ha = h + ATTN_LOOKAHEAD
            if ha < ATTN_HEADS:
                scores(kd, ha, d * tk)
            elif d + 1 < nd:
                scores(pl.multiple_of(kd + tk, tk), ha - ATTN_HEADS, (d + 1) * tk)
            update(kd, h, d * tk)

    for h in range(ATTN_HEADS):
        acc_ref[h * hd:(h + 1) * hd, 0:tq] = (acc_ref[h * hd:(h + 1) * hd, 0:tq]
                                              / l_ref[h * SUBLANES:h * SUBLANES + 1, :])
    o = acc_ref[:, 0:tq].T
    o_ref[0] = (o * sza_ref[0].astype(F32)).astype(BF16)


def _attn(qt, k, caug, vt, sza, *, tq):
    nb, s, w = k.shape
    return pl.pallas_call(
        functools.partial(_attn_kernel, tq=tq),
        grid=(nb, s // tq),
        in_specs=[pl.BlockSpec((1, w, tq), lambda b, i: (b, 0, i)),
                  pl.BlockSpec((1, s, w), lambda b, i: (b, 0, 0)),
                  pl.BlockSpec((1, s, LANES), lambda b, i: (b, 0, 0)),
                  pl.BlockSpec((1, w, s), lambda b, i: (b, 0, 0)),
                  pl.BlockSpec((1, tq, w), lambda b, i: (b, i, 0))],
        out_specs=pl.BlockSpec((1, tq, w), lambda b, i: (b, i, 0)),
        out_shape=jax.ShapeDtypeStruct((nb, s, w), BF16),
        scratch_shapes=[pltpu.VMEM((ATTN_HEADS, 2 * LANES, tq + LANES), BF16),
                        pltpu.VMEM((ATTN_SLOTS, ATTN_TK, tq + LANES), F32),
                        pltpu.VMEM((ATTN_HEADS * SUBLANES, tq), F32),
                        pltpu.VMEM((ATTN_HEADS * SUBLANES, tq), F32),
                        pltpu.VMEM((w, tq + LANES), F32)],
        compiler_params=_cparams("arbitrary", "arbitrary"),
        name="attn",
    )(qt, k, caug, vt, sza)


def _merge_kernel(x_ref, ys_ref, szs_ref, oz_ref, sgs_ref, sga_ref, gate_ref,
                  wps_ref, wpa_ref, wout_ref, lng_ref, lnb_ref, o_ref,
                  wps_bf, wpa_bf, wout_bf, *, ts):
    @pl.when((pl.program_id(0) == 0) & (pl.program_id(1) == 0))
    def _():
        wps_bf[:, 0:D_MODEL] = wps_ref[...].astype(BF16)
        wpa_bf[:, 0:D_MODEL] = wpa_ref[...].astype(BF16)
        wout_bf[:, 0:D_MODEL] = wout_ref[...].astype(BF16)

    gate = gate_ref[0, 0]
    for c in range(ts // MERGE_SUB):
        r = slice(c * MERGE_SUB, (c + 1) * MERGE_SUB)
        a = (ys_ref[0, r, :].astype(F32) * szs_ref[0, r, :].astype(F32)).astype(BF16)
        m1 = (jnp.dot(a, wps_bf[:, 0:D_MODEL], preferred_element_type=F32)
              * sgs_ref[0, r, :].astype(F32))
        m2 = (jnp.dot(oz_ref[0, r, :], wpa_bf[:, 0:D_MODEL], preferred_element_type=F32)
              * sga_ref[0, r, :].astype(F32))
        merged = (m1 + m2).astype(BF16)
        sub = jnp.dot(merged, wout_bf[:, 0:D_MODEL], preferred_element_type=F32) * gate
        y = DEEPNORM_ALPHA * x_ref[0, r, :] + sub
        mu = jnp.mean(y, axis=-1, keepdims=True)
        yc = y - mu
        var = jnp.mean(yc * yc, axis=-1, keepdims=True)
        o_ref[0, r, :] = yc * lax.rsqrt(var + LN_EPS) * lng_ref[...] + lnb_ref[...]


def _mod_spec(part):
    return pl.BlockSpec((1, 1, 1, D_MODEL), lambda i, b: (b, part, 0, 0))


def _merge(x, ys_t, szs, oz, sgs, sga, mod4, wps, wpa, wout, ln_g, ln_b, *, ts):
    nb, s, d = x.shape
    tok3 = lambda w: pl.BlockSpec((1, ts, w), lambda i, b: (b, i, 0))
    const2 = lambda a: pl.BlockSpec(a.shape, lambda i, b: (0, 0), pipeline_mode=pl.Buffered(1))
    return pl.pallas_call(
        functools.partial(_merge_kernel, ts=ts),
        grid=(s // ts, nb),
        in_specs=[tok3(d), tok3(SSM_WIDTH),
                  tok3(SSM_WIDTH), tok3(ATTN_WIDTH), tok3(d), tok3(d),
                  _mod_spec(2),
                  const2(wps), const2(wpa), const2(wout), const2(ln_g), const2(ln_b)],
        out_specs=tok3(d),
        out_shape=jax.ShapeDtypeStruct((nb, s, d), x.dtype),
        scratch_shapes=[pltpu.VMEM((w.shape[0], d + LANES), BF16) for w in (wps, wpa, wout)],
        compiler_params=_cparams("arbitrary", "arbitrary"),
        name="merge",
    )(x, ys_t, szs, oz, sgs, sga, mod4, wps, wpa, wout, ln_g, ln_b)


def kernel(x, c, w_ada, b_ada, w_in, b_f, lam_re, lam_im, log_dt, ssm_b_re, ssm_b_im,
           ssm_c_re, ssm_c_im, ssm_d, w_glu, w_proj_ssm, w_proj_attn, w_out, ln_g, ln_b):
    nb, s, d = x.shape
    l = 0
    mod4 = _ada(c, w_ada[l], b_ada[l]).reshape(nb, 3, 1, d)

    w_main, w_t, w_f3 = _wprep(jnp.swapaxes(w_in[l], 0, 1))
    bfv = b_f[l].astype(F32)
    b_f3 = jnp.concatenate([bfv, bfv, bfv, jnp.zeros((LANES - 3 * ATTN_HEADS,), F32)])[None, :]

    (xs_t, szs, k, sza, sgs, sga, qt, vt, caug) = _inproj(
        x, mod4, w_main, w_t, w_f3, b_f3, ts=512)

    bmat, cmat, a_re, a_im = _zoh(lam_re[l], lam_im[l], log_dt[l], ssm_b_re[l], ssm_b_im[l],
                                  ssm_c_re[l], ssm_c_im[l])

    ys_t = _s5(xs_t, bmat, cmat, a_re, a_im,
               ssm_d[l].reshape(1, SSM_WIDTH).astype(F32), w_glu[l].astype(BF16), steps=128, sub=32)

    oz = _attn(qt, k, caug, vt, sza, tq=1024)

    return _merge(x, ys_t, szs, oz, sgs, sga, mod4,
                  w_proj_ssm[l], w_proj_attn[l], w_out[l],
                  ln_g[l].reshape(1, d).astype(F32), ln_b[l].astype(F32).reshape(1, d), ts=1024)
```

```python
import functools
import math

import jax
import jax.numpy as jnp
from jax import lax
from jax.experimental import pallas as pl
from jax.experimental.pallas import tpu as pltpu

D_MODEL = 1024
BATCH = 8
SEQ = 2048
SSM_WIDTH = 512
SSM_GROUP = 16
SSM_GROUPS = 32
SSM_STATE = 64
ATTN_HEADS = 8
ATTN_HEAD_DIM = 64
ATTN_WIDTH = 512
LN_EPS = 1e-5
DEEPNORM_ALPHA = 2.0 ** 0.25

F32 = jnp.float32
BF16 = jnp.bfloat16

LANES = 128
SUBLANES = 8
VMEM_LIMIT = 52 * 1024 * 1024

S5_TILES = 4
S5_TILE_GROUPS = SSM_GROUPS // S5_TILES
S5_HALF = S5_TILE_GROUPS * SSM_STATE
S5_TILE_LANES = 2 * S5_HALF
S5_TILE_IN = S5_TILE_GROUPS * SSM_GROUP
S5_STATE_LANES = S5_TILES * S5_TILE_LANES

C_XS, C_ZS, C_K, C_ZA, C_GS, C_GA, C_END = 0, 512, 1024, 1536, 2048, 3072, 4096

NEG_BIG = -1e30
LOG2E = math.log2(math.e)

INPROJ_SUB = 256
MERGE_SUB = 256
ATTN_TK = 256
ATTN_LOOKAHEAD = 3
ATTN_SLOTS = 4
assert ATTN_SLOTS > ATTN_LOOKAHEAD and ATTN_HEADS % ATTN_SLOTS == 0


def _sigmoid(x):
    return 1.0 / (1.0 + jnp.exp(-x))


def _silu(x):
    return x * _sigmoid(x)


def _cparams(*sem):
    return pltpu.CompilerParams(dimension_semantics=sem, vmem_limit_bytes=VMEM_LIMIT)


def _ada_kernel(c_ref, w_ref, b_ref, o_ref):
    ca = _silu(c_ref[...]).astype(BF16)
    o_ref[...] = jnp.dot(ca, w_ref[...].astype(BF16), preferred_element_type=F32) + b_ref[...]


def _ada(c, w_ada, b_ada):
    n = w_ada.shape[1]
    tn = 1024
    return pl.pallas_call(
        _ada_kernel,
        grid=(n // tn,),
        in_specs=[pl.BlockSpec((BATCH, D_MODEL), lambda j: (0, 0)),
                  pl.BlockSpec((D_MODEL, tn), lambda j: (0, j)),
                  pl.BlockSpec((1, tn), lambda j: (0, j))],
        out_specs=pl.BlockSpec((BATCH, tn), lambda j: (0, j)),
        out_shape=jax.ShapeDtypeStruct((BATCH, n), F32),
        compiler_params=_cparams("arbitrary"),
        name="ada",
    )(c, w_ada, b_ada.reshape(1, n))


def _zoh_kernel(lr_ref, li_ref, ldt_ref, br_ref, bi_ref, cre_ref, cim_ref,
                bmat_ref, cmat_ref, are_ref, aim_ref):
    lr = lr_ref[...]
    li = li_ref[...]
    dt = jnp.exp(ldt_ref[...])
    mag = jnp.exp(lr * dt)
    abr = mag * jnp.cos(li * dt)
    abi = mag * jnp.sin(li * dt)
    den = lr * lr + li * li
    cr = ((abr - 1.0) * lr + abi * li) / den
    ci = (abi * lr - (abr - 1.0) * li) / den

    tg, gh, gp = S5_TILE_GROUPS, SSM_GROUP, SSM_STATE
    rep_lanes = (lax.broadcasted_iota(jnp.int32, (gp, tg * gp), 1) % gp
                 == lax.broadcasted_iota(jnp.int32, (gp, tg * gp), 0))
    rep_rows = (lax.broadcasted_iota(jnp.int32, (tg * gp, gp), 0) % gp
                == lax.broadcasted_iota(jnp.int32, (tg * gp, gp), 1))
    rep_lanes_bf = jnp.where(rep_lanes, 1.0, 0.0).astype(BF16)
    rep_rows_bf = jnp.where(rep_rows, 1.0, 0.0).astype(BF16)
    rep_lanes_f32 = jnp.where(rep_lanes, 1.0, 0.0)
    b_diag = (lax.broadcasted_iota(jnp.int32, (tg * gh, tg * gp), 0) // gh
              == lax.broadcasted_iota(jnp.int32, (tg * gh, tg * gp), 1) // gp)
    c_diag = (lax.broadcasted_iota(jnp.int32, (tg * gp, tg * gh), 0) // gp
              == lax.broadcasted_iota(jnp.int32, (tg * gp, tg * gh), 1) // gh)
    a_diag = (lax.broadcasted_iota(jnp.int32, (tg, tg * gp), 0)
              == lax.broadcasted_iota(jnp.int32, (tg, tg * gp), 1) // gp)

    for j in range(S5_TILES):
        bbr, bbi, crs, cis = [], [], [], []
        for g in range(j * tg, (j + 1) * tg):
            crg, cig = cr[g:g + 1, :], ci[g:g + 1, :]
            br, bi = br_ref[g], bi_ref[g]
            bbr.append(crg * br - cig * bi)
            bbi.append(crg * bi + cig * br)
            crs.append(cre_ref[g])
            cis.append(-cim_ref[g])
        halves = []
        for blk in (jnp.concatenate(bbr, axis=0), jnp.concatenate(bbi, axis=0)):
            tiled = jnp.dot(blk.astype(BF16), rep_lanes_bf, preferred_element_type=F32)
            halves.append(jnp.where(b_diag, tiled, 0.0))
        halves.append(jnp.zeros((tg * gh, LANES), F32))
        bmat_ref[j] = jnp.concatenate(halves, axis=1).astype(BF16)
        halves = []
        for blk in (jnp.concatenate(crs, axis=0), jnp.concatenate(cis, axis=0)):
            tiled = lax.dot_general(rep_rows_bf, blk.astype(BF16), (((1,), (1,)), ((), ())),
                                    preferred_element_type=F32)
            halves.append(jnp.where(c_diag, tiled, 0.0))
        cmat_ref[j] = jnp.concatenate(halves, axis=0).astype(BF16)
        for a, a_ref in ((abr, are_ref), (abi, aim_ref)):
            tiled = jnp.dot(a[j * tg:(j + 1) * tg, :], rep_lanes_f32, preferred_element_type=F32,
                            precision=lax.Precision.HIGHEST)
            a_ref[:, j * S5_HALF:(j + 1) * S5_HALF] = jnp.sum(
                jnp.where(a_diag, tiled, 0.0), axis=0, keepdims=True)


def _zoh(lam_re, lam_im, log_dt, b_re, b_im, c_re, c_im):
    swap = lambda a: jnp.swapaxes(a, 1, 2)
    return pl.pallas_call(
        _zoh_kernel,
        out_shape=(jax.ShapeDtypeStruct((S5_TILES, S5_TILE_IN, S5_TILE_LANES + LANES), BF16),
                   jax.ShapeDtypeStruct((S5_TILES, S5_TILE_LANES, S5_TILE_IN), BF16),
                   jax.ShapeDtypeStruct((1, SSM_GROUPS * SSM_STATE), F32),
                   jax.ShapeDtypeStruct((1, SSM_GROUPS * SSM_STATE), F32)),
        name="zoh",
    )(lam_re, lam_im, log_dt.reshape(SSM_GROUPS, 1), swap(b_re), swap(b_im), c_re, c_im)


O_XS, O_ZS, O_Q, O_K, O_V, O_F, O_ZA, O_GS, O_GA, O_END = (
    0, 512, 1024, 1536, 2048, 2560, 2568, 3080, 4104, 5128)


WPREP_CHUNK = 256


def _wprep_kernel(wt_ref, wm_ref, wqv_ref, wf_ref):
    def put(c_lo, o_lo, n):
        for s in range(0, n, WPREP_CHUNK):
            wm_ref[:, c_lo + s:c_lo + s + WPREP_CHUNK] = (
                wt_ref[o_lo + s:o_lo + s + WPREP_CHUNK, :].T.astype(BF16))

    put(C_XS, O_XS, O_Q - O_XS)
    put(C_K, O_K, O_V - O_K)
    put(C_ZA, O_ZA, O_GS - O_ZA)
    put(C_GS, O_GS, O_END - O_GS)
    wm_ref[:, C_END:] = jnp.zeros((wm_ref.shape[0], LANES), BF16)
    wqv_ref[0:ATTN_WIDTH, 0:D_MODEL] = wt_ref[O_Q:O_K, :].astype(BF16)
    wqv_ref[ATTN_WIDTH:2 * ATTN_WIDTH, 0:D_MODEL] = wt_ref[O_V:O_F, :].astype(BF16)
    wqv_ref[:, D_MODEL:] = jnp.zeros((wqv_ref.shape[0], LANES), BF16)
    f8 = wt_ref[O_F:O_ZA, :]
    blk = jnp.concatenate([f8, f8, f8, jnp.zeros((LANES - 3 * ATTN_HEADS, f8.shape[1]), F32)], axis=0)
    wf_ref[...] = blk.T.astype(BF16)


def _wprep(wt):
    n, d = wt.shape
    return pl.pallas_call(
        _wprep_kernel,
        out_shape=[jax.ShapeDtypeStruct((d, C_END + LANES), BF16),
                   jax.ShapeDtypeStruct((2 * ATTN_WIDTH, d + LANES), BF16),
                   jax.ShapeDtypeStruct((d, LANES), BF16)],
        compiler_params=pltpu.CompilerParams(vmem_limit_bytes=VMEM_LIMIT),
        name="wprep",
    )(wt)


def _inproj_kernel(x_ref, scale_ref, shift_ref, wm_ref, wt_ref, wf_ref, bf_ref, tri_ref,
                   xs_ref, szs_ref, k_ref, sza_ref, sgs_ref, sga_ref, qt_ref, vt_ref, caug_ref,
                   carry_ref, *, ts):
    i = pl.program_id(0)
    b = pl.program_id(1)

    @pl.when(i == 0)
    def _():
        carry_ref[b] = jnp.zeros(carry_ref.shape[1:], F32)

    carry = carry_ref[b][0:1, :]
    for c in range(ts // INPROJ_SUB):
        r = slice(c * INPROJ_SUB, (c + 1) * INPROJ_SUB)
        x = x_ref[0, r, :]
        mu = jnp.mean(x, axis=-1, keepdims=True)
        xc = x - mu
        var = jnp.mean(xc * xc, axis=-1, keepdims=True)
        xn = xc * lax.rsqrt(var + LN_EPS)
        u = xn * (1.0 + scale_ref[0, 0]) + shift_ref[0, 0]
        ub = u.astype(BF16)

        def mm(lo, hi, ub=ub):
            return jnp.dot(ub, wm_ref[:, lo:hi], preferred_element_type=F32)

        f = jnp.dot(ub, wf_ref[...], preferred_element_type=F32) + bf_ref[...]
        lf = jnp.minimum(f, 0.0) - jnp.log(1.0 + jnp.exp(-jnp.abs(f)))
        l1 = lf.astype(BF16)
        r1 = lf - l1.astype(F32)
        l2 = r1.astype(BF16)
        l3 = (r1 - l2.astype(F32)).astype(BF16)

        xs = mm(C_XS, C_ZS)
        for j in range(S5_TILES):
            xs_ref[j, pl.ds(b + c * INPROJ_SUB * SUBLANES, INPROJ_SUB, stride=SUBLANES), :] = (
                xs[:, j * LANES:(j + 1) * LANES])
        szs_ref[0, r, :] = _silu(mm(C_ZS, C_K)).astype(BF16)

        tri = tri_ref[...]
        cf = (jnp.dot(tri, l1, preferred_element_type=F32)
              + jnp.dot(tri, l2, preferred_element_type=F32)
              + jnp.dot(tri, l3, preferred_element_type=F32)) + carry
        carry = cf[INPROJ_SUB - 1:INPROJ_SUB, :]

        k_ref[0, r, :] = mm(C_K, C_ZA).astype(BF16)
        sza_ref[0, r, :] = _silu(mm(C_ZA, C_GS)).astype(BF16)

        cf2 = cf * LOG2E
        c1 = cf2.astype(BF16).astype(F32)
        rem = cf2 - c1
        c2 = rem.astype(BF16).astype(F32)
        c3 = rem - c2
        lane = lax.broadcasted_iota(jnp.int32, cf.shape, 1)
        parts = jnp.where(lane < 8, c1, jnp.where(lane < 16, c2, jnp.where(lane < 24, c3, 0.0)))
        caug_ref[0, r, :] = (-parts).astype(BF16)

        sgs_ref[0, r, :] = _sigmoid(mm(C_GS, C_GA)).astype(BF16)
        sga_ref[0, r, :] = _sigmoid(mm(C_GA, C_END)).astype(BF16)

        qv = lax.dot_general(wt_ref[:, 0:D_MODEL], ub, (((1,), (1,)), ((), ())),
                             preferred_element_type=F32)
        qt_ref[0, :, r] = (qv[:ATTN_WIDTH] * (LOG2E * ATTN_HEAD_DIM ** -0.5)).astype(BF16)
        vt_ref[0, :, r] = qv[ATTN_WIDTH:].astype(BF16)
    carry_ref[b] = jnp.broadcast_to(carry, carry_ref.shape[1:])


def _inproj(x, mod4, w_main, w_t, w_f3, b_f3, *, ts):
    nb, s, d = x.shape
    nt = s // ts
    tri = jnp.tril(jnp.ones((INPROJ_SUB, INPROJ_SUB), BF16))
    tok3 = lambda w: pl.BlockSpec((1, ts, w), lambda i, b: (b, i, 0))
    const2 = lambda a: pl.BlockSpec(a.shape, lambda i, b: (0, 0), pipeline_mode=pl.Buffered(1))
    bf = lambda shape: jax.ShapeDtypeStruct(shape, BF16)
    return pl.pallas_call(
        functools.partial(_inproj_kernel, ts=ts),
        grid=(nt, nb),
        in_specs=[tok3(d), _mod_spec(1), _mod_spec(0),
                  const2(w_main), const2(w_t), const2(w_f3), const2(b_f3), const2(tri)],
        out_specs=[pl.BlockSpec((S5_TILES, ts * nb, LANES), lambda i, b: (0, i, 0)),
                   tok3(SSM_WIDTH), tok3(ATTN_WIDTH), tok3(ATTN_WIDTH),
                   tok3(D_MODEL), tok3(D_MODEL),
                   pl.BlockSpec((1, ATTN_WIDTH, ts), lambda i, b: (b, 0, i)),
                   pl.BlockSpec((1, ATTN_WIDTH, ts), lambda i, b: (b, 0, i)),
                   tok3(LANES)],
        out_shape=[jax.ShapeDtypeStruct((S5_TILES, s * nb, LANES), F32),
                   bf((nb, s, SSM_WIDTH)), bf((nb, s, ATTN_WIDTH)), bf((nb, s, ATTN_WIDTH)),
                   bf((nb, s, D_MODEL)), bf((nb, s, D_MODEL)),
                   bf((nb, ATTN_WIDTH, s)), bf((nb, ATTN_WIDTH, s)),
                   bf((nb, s, LANES))],
        scratch_shapes=[pltpu.VMEM((nb, SUBLANES, LANES), F32)],
        compiler_params=_cparams("arbitrary", "arbitrary"),
        name="inproj",
    )(x, mod4, mod4, w_main, w_t, w_f3, b_f3, tri)


def _gelu_tanh(x):
    return 0.5 * x * (1.0 + jnp.tanh(math.sqrt(2.0 / math.pi) * (x + 0.044715 * (x * x * x))))


def _s5_kernel(xs_ref, bmat_ref, cmat_ref, are_ref, aim_ref, d_ref, wglu_ref,
               y_ref, buf_ref, h_ref, ybuf_ref, wglu_bf, *, steps, sub):
    @pl.when(pl.program_id(0) == 0)
    def _():
        h_ref[...] = jnp.zeros_like(h_ref)
        wglu_bf[:, 0:2 * SSM_WIDTH] = wglu_ref[...].astype(BF16)

    srows = sub * SUBLANES

    def expand(c):
        r = slice(c * srows, (c + 1) * srows)
        for j in range(S5_TILES):
            lo = j * S5_TILE_LANES
            buf_ref[r, lo:lo + S5_TILE_LANES] = jnp.dot(
                xs_ref[j, r, :].astype(BF16), bmat_ref[j, :, 0:S5_TILE_LANES],
                preferred_element_type=F32)

    def recur(c, state):
        new_state = []
        for j in range(S5_TILES):
            re = slice(j * S5_TILE_LANES, j * S5_TILE_LANES + S5_HALF)
            im = slice(j * S5_TILE_LANES + S5_HALF, (j + 1) * S5_TILE_LANES)
            ar = jnp.broadcast_to(are_ref[:, j * S5_HALF:(j + 1) * S5_HALF], (SUBLANES, S5_HALF))
            ai = jnp.broadcast_to(aim_ref[:, j * S5_HALF:(j + 1) * S5_HALF], (SUBLANES, S5_HALF))
            hr, hi = state[j]
            for t in range(sub):
                rt = slice(c * srows + t * SUBLANES, c * srows + (t + 1) * SUBLANES)
                nhr = ar * hr - ai * hi + buf_ref[rt, re]
                nhi = ar * hi + ai * hr + buf_ref[rt, im]
                buf_ref[rt, re] = nhr
                buf_ref[rt, im] = nhi
                hr, hi = nhr, nhi
            new_state.append((hr, hi))
        return new_state

    def project(c):
        r = slice(c * srows, (c + 1) * srows)
        ys = []
        for j in range(S5_TILES):
            lo = j * S5_TILE_LANES
            hb = buf_ref[r, lo:lo + S5_TILE_LANES].astype(BF16)
            ys.append(jnp.dot(hb, cmat_ref[j], preferred_element_type=F32)
                      + d_ref[:, j * LANES:(j + 1) * LANES] * xs_ref[j, r, :])
        g = _gelu_tanh(jnp.concatenate(ys, axis=1)).astype(BF16)
        ab = jnp.dot(g, wglu_bf[:, 0:2 * SSM_WIDTH], preferred_element_type=F32)
        out = ab[:, :SSM_WIDTH] * _sigmoid(ab[:, SSM_WIDTH:])
        for j in range(S5_TILES):
            ybuf_ref[j] = out[:, j * LANES:(j + 1) * LANES]
        for b in range(BATCH):
            yb = jnp.concatenate([ybuf_ref[j, pl.ds(b, sub, stride=SUBLANES), :]
                                  for j in range(S5_TILES)], axis=1)
            y_ref[b, c * sub:(c + 1) * sub, :] = yb.astype(BF16)

    state = [(h_ref[:, j * S5_TILE_LANES:j * S5_TILE_LANES + S5_HALF],
              h_ref[:, j * S5_TILE_LANES + S5_HALF:(j + 1) * S5_TILE_LANES])
             for j in range(S5_TILES)]
    nsub = steps // sub
    expand(0)
    for c in range(nsub):
        if c + 1 < nsub:
            expand(c + 1)
        state = recur(c, state)
        project(c)
    for j in range(S5_TILES):
        h_ref[:, j * S5_TILE_LANES:j * S5_TILE_LANES + S5_HALF] = state[j][0]
        h_ref[:, j * S5_TILE_LANES + S5_HALF:(j + 1) * S5_TILE_LANES] = state[j][1]


def _s5(xs_t, bmat, cmat, a_re, a_im, d_skip, w_glu, *, steps, sub):
    t = xs_t.shape[1]
    rows = steps * SUBLANES
    const = lambda a: pl.BlockSpec(a.shape, lambda i: (0,) * a.ndim)
    tiles = pl.BlockSpec((S5_TILES, rows, LANES), lambda i: (0, i, 0))
    return pl.pallas_call(
        functools.partial(_s5_kernel, steps=steps, sub=sub),
        grid=(t // rows,),
        in_specs=[tiles,
                  const(bmat), const(cmat), const(a_re), const(a_im), const(d_skip), const(w_glu)],
        out_specs=pl.BlockSpec((BATCH, steps, SSM_WIDTH), lambda i: (0, i, 0)),
        out_shape=jax.ShapeDtypeStruct((BATCH, t // SUBLANES, SSM_WIDTH), BF16),
        scratch_shapes=[pltpu.VMEM((rows, S5_STATE_LANES), F32),
                        pltpu.VMEM((SUBLANES, S5_STATE_LANES), F32),
                        pltpu.VMEM((S5_TILES, sub * SUBLANES, LANES), F32),
                        pltpu.VMEM((w_glu.shape[0], w_glu.shape[1] + LANES), BF16)],
        compiler_params=_cparams("arbitrary"),
        name="s5",
    )(xs_t, bmat, cmat, a_re, a_im, d_skip, w_glu)


def _attn_kernel(qt_ref, k_ref, caug_ref, vt_ref, sza_ref, o_ref,
                 rhs_ref, s_ref, m_ref, l_ref, acc_ref, *, tq):
    qi = pl.program_id(1)
    tk = ATTN_TK
    nd = tq // tk
    hd = ATTN_HEAD_DIM
    row = lax.broadcasted_iota(jnp.int32, (LANES, tq), 0)

    for h in range(ATTN_HEADS):
        j = h // 2
        q2 = qt_ref[0, j * LANES:(j + 1) * LANES, :]
        rhs_ref[h, 0:LANES, 0:tq] = jnp.where((row // hd) == (h % 2), q2, jnp.zeros_like(q2))
        rhs_ref[h, LANES:2 * LANES, 0:tq] = jnp.where(
            (row == h) | (row == 8 + h) | (row == 16 + h), 1.0, 0.0).astype(BF16)
    m_ref[...] = jnp.full(m_ref.shape, -jnp.inf, F32)
    l_ref[...] = jnp.zeros_like(l_ref)
    acc_ref[...] = jnp.zeros_like(acc_ref)

    def scores(k0, h, diag=None):
        j = h // 2
        lo = 0 if diag is None else diag
        lhs = jnp.concatenate([k_ref[0, pl.ds(k0, tk), j * LANES:(j + 1) * LANES],
                               caug_ref[0, pl.ds(k0, tk), :]], axis=1)
        s_ref[h % ATTN_SLOTS, :, lo:tq] = jnp.dot(lhs, rhs_ref[h, :, lo:tq],
                                                  preferred_element_type=F32)

    def update(k0, h, diag=None):
        lo = 0 if diag is None else diag
        s = s_ref[h % ATTN_SLOTS, :, lo:tq]
        if diag is not None:
            kpos = lax.broadcasted_iota(jnp.int32, s.shape, 0)
            qpos = lax.broadcasted_iota(jnp.int32, s.shape, 1)
            s = jnp.where(kpos <= qpos, s, NEG_BIG)
        hs = slice(h * SUBLANES, h * SUBLANES + 1)
        m = m_ref[hs, lo:tq]
        m_new = jnp.maximum(m, jnp.max(s, axis=0, keepdims=True))
        alpha = jnp.exp2(m - m_new)
        p = jnp.exp2(s - m_new).astype(BF16)
        m_ref[hs, lo:tq] = m_new
        vt1 = jnp.concatenate([vt_ref[0, h * hd:(h + 1) * hd, pl.ds(k0, tk)],
                               jnp.ones((2 * SUBLANES, tk), BF16)], axis=0)
        pv = jnp.dot(vt1, p, preferred_element_type=F32)
        l_ref[hs, lo:tq] = alpha * l_ref[hs, lo:tq] + pv[hd:hd + 1, :]
        acc_ref[h * hd:(h + 1) * hd, lo:tq] = (alpha * acc_ref[h * hd:(h + 1) * hd, lo:tq]
                                               + pv[:hd, :])

    def body(kj, carry):
        k0 = pl.multiple_of(kj * tk, tk)
        k1 = pl.multiple_of(kj * tk + tk, tk)
        for h in range(ATTN_HEADS):
            ha = h + ATTN_LOOKAHEAD
            if ha < ATTN_HEADS:
                scores(k0, ha)
            else:
                scores(k1, ha - ATTN_HEADS)
            update(k0, h)
        return carry

    for h in range(ATTN_LOOKAHEAD):
        scores(0, h)
    lax.fori_loop(0, qi * nd, body, 0)
    for d in range(nd):
        kd = pl.multiple_of(qi * tq + d * tk, tk)
        for h in range(ATTN_HEADS):
            ha = h + ATTN_LOOKAHEAD
            if ha < ATTN_HEADS:
                scores(kd, ha, d * tk)
            elif d + 1 < nd:
                scores(pl.multiple_of(kd + tk, tk), ha - ATTN_HEADS, (d + 1) * tk)
            update(kd, h, d * tk)

    for h in range(ATTN_HEADS):
        acc_ref[h * hd:(h + 1) * hd, 0:tq] = (acc_ref[h * hd:(h + 1) * hd, 0:tq]
                                              / l_ref[h * SUBLANES:h * SUBLANES + 1, :])
    o = acc_ref[:, 0:tq].T
    o_ref[0] = (o * sza_ref[0].astype(F32)).astype(BF16)


def _attn(qt, k, caug, vt, sza, *, tq):
    nb, s, w = k.shape
    return pl.pallas_call(
        functools.partial(_attn_kernel, tq=tq),
        grid=(nb, s // tq),
        in_specs=[pl.BlockSpec((1, w, tq), lambda b, i: (b, 0, i)),
                  pl.BlockSpec((1, s, w), lambda b, i: (b, 0, 0)),
                  pl.BlockSpec((1, s, LANES), lambda b, i: (b, 0, 0)),
                  pl.BlockSpec((1, w, s), lambda b, i: (b, 0, 0)),
                  pl.BlockSpec((1, tq, w), lambda b, i: (b, i, 0))],
        out_specs=pl.BlockSpec((1, tq, w), lambda b, i: (b, i, 0)),
        out_shape=jax.ShapeDtypeStruct((nb, s, w), BF16),
        scratch_shapes=[pltpu.VMEM((ATTN_HEADS, 2 * LANES, tq + LANES), BF16),
                        pltpu.VMEM((ATTN_SLOTS, ATTN_TK, tq + LANES), F32),
                        pltpu.VMEM((ATTN_HEADS * SUBLANES, tq), F32),
                        pltpu.VMEM((ATTN_HEADS * SUBLANES, tq), F32),
                        pltpu.VMEM((w, tq + LANES), F32)],
        compiler_params=_cparams("arbitrary", "arbitrary"),
        name="attn",
    )(qt, k, caug, vt, sza)


def _merge_kernel(x_ref, ys_ref, szs_ref, oz_ref, sgs_ref, sga_ref, gate_ref,
                  wps_ref, wpa_ref, wout_ref, lng_ref, lnb_ref, o_ref,
                  wps_bf, wpa_bf, wout_bf, *, ts):
    @pl.when((pl.program_id(0) == 0) & (pl.program_id(1) == 0))
    def _():
        wps_bf[:, 0:D_MODEL] = wps_ref[...].astype(BF16)
        wpa_bf[:, 0:D_MODEL] = wpa_ref[...].astype(BF16)
        wout_bf[:, 0:D_MODEL] = wout_ref[...].astype(BF16)

    gate = gate_ref[0, 0]
    for c in range(ts // MERGE_SUB):
        r = slice(c * MERGE_SUB, (c + 1) * MERGE_SUB)
        a = (ys_ref[0, r, :].astype(F32) * szs_ref[0, r, :].astype(F32)).astype(BF16)
        m1 = (jnp.dot(a, wps_bf[:, 0:D_MODEL], preferred_element_type=F32)
              * sgs_ref[0, r, :].astype(F32))
        m2 = (jnp.dot(oz_ref[0, r, :], wpa_bf[:, 0:D_MODEL], preferred_element_type=F32)
              * sga_ref[0, r, :].astype(F32))
        merged = (m1 + m2).astype(BF16)
        sub = jnp.dot(merged, wout_bf[:, 0:D_MODEL], preferred_element_type=F32) * gate
        y = DEEPNORM_ALPHA * x_ref[0, r, :] + sub
        mu = jnp.mean(y, axis=-1, keepdims=True)
        yc = y - mu
        var = jnp.mean(yc * yc, axis=-1, keepdims=True)
        o_ref[0, r, :] = yc * lax.rsqrt(var + LN_EPS) * lng_ref[...] + lnb_ref[...]


def _mod_spec(part):
    return pl.BlockSpec((1, 1, 1, D_MODEL), lambda i, b: (b, part, 0, 0))


def _merge(x, ys_t, szs, oz, sgs, sga, mod4, wps, wpa, wout, ln_g, ln_b, *, ts):
    nb, s, d = x.shape
    tok3 = lambda w: pl.BlockSpec((1, ts, w), lambda i, b: (b, i, 0))
    const2 = lambda a: pl.BlockSpec(a.shape, lambda i, b: (0, 0), pipeline_mode=pl.Buffered(1))
    return pl.pallas_call(
        functools.partial(_merge_kernel, ts=ts),
        grid=(s // ts, nb),
        in_specs=[tok3(d), tok3(SSM_WIDTH),
                  tok3(SSM_WIDTH), tok3(ATTN_WIDTH), tok3(d), tok3(d),
                  _mod_spec(2),
                  const2(wps), const2(wpa), const2(wout), const2(ln_g), const2(ln_b)],
        out_specs=tok3(d),
        out_shape=jax.ShapeDtypeStruct((nb, s, d), x.dtype),
        scratch_shapes=[pltpu.VMEM((w.shape[0], d + LANES), BF16) for w in (wps, wpa, wout)],
        compiler_params=_cparams("arbitrary", "arbitrary"),
        name="merge",
    )(x, ys_t, szs, oz, sgs, sga, mod4, wps, wpa, wout, ln_g, ln_b)


def kernel(x, c, w_ada, b_ada, w_in, b_f, lam_re, lam_im, log_dt, ssm_b_re, ssm_b_im,
           ssm_c_re, ssm_c_im, ssm_d, w_glu, w_proj_ssm, w_proj_attn, w_out, ln_g, ln_b):
    nb, s, d = x.shape
    l = 0
    mod4 = _ada(c, w_ada[l], b_ada[l]).reshape(nb, 3, 1, d)

    w_main, w_t, w_f3 = _wprep(jnp.swapaxes(w_in[l], 0, 1))
    bfv = b_f[l].astype(F32)
    b_f3 = jnp.concatenate([bfv, bfv, bfv, jnp.zeros((LANES - 3 * ATTN_HEADS,), F32)])[None, :]

    (xs_t, szs, k, sza, sgs, sga, qt, vt, caug) = _inproj(
        x, mod4, w_main, w_t, w_f3, b_f3, ts=512)

    bmat, cmat, a_re, a_im = _zoh(lam_re[l], lam_im[l], log_dt[l], ssm_b_re[l], ssm_b_im[l],
                                  ssm_c_re[l], ssm_c_im[l])

    ys_t = _s5(xs_t, bmat, cmat, a_re, a_im,
               ssm_d[l].reshape(1, SSM_WIDTH).astype(F32), w_glu[l], steps=128, sub=32)

    oz = _attn(qt, k, caug, vt, sza, tq=1024)

    return _merge(x, ys_t, szs, oz, sgs, sga, mod4,
                  w_proj_ssm[l], w_proj_attn[l], w_out[l],
                  ln_g[l].reshape(1, d).astype(F32), ln_b[l].astype(F32).reshape(1, d), ts=1024)
```

```python
import functools
import math

import jax
import jax.numpy as jnp
from jax import lax
from jax.experimental import pallas as pl
from jax.experimental.pallas import tpu as pltpu

D_MODEL = 1024
BATCH = 8
SEQ = 2048
SSM_WIDTH = 512
SSM_GROUP = 16
SSM_GROUPS = 32
SSM_STATE = 64
ATTN_HEADS = 8
ATTN_HEAD_DIM = 64
ATTN_WIDTH = 512
LN_EPS = 1e-5
DEEPNORM_ALPHA = 2.0 ** 0.25

F32 = jnp.float32
BF16 = jnp.bfloat16

LANES = 128
SUBLANES = 8
VMEM_LIMIT = 52 * 1024 * 1024

S5_TILES = 4
S5_TILE_GROUPS = SSM_GROUPS // S5_TILES
S5_HALF = S5_TILE_GROUPS * SSM_STATE
S5_TILE_LANES = 2 * S5_HALF
S5_TILE_IN = S5_TILE_GROUPS * SSM_GROUP
S5_STATE_LANES = S5_TILES * S5_TILE_LANES

C_XS, C_ZS, C_K, C_ZA, C_GS, C_GA, C_END = 0, 512, 1024, 1536, 2048, 3072, 4096

NEG_BIG = -1e30
LOG2E = math.log2(math.e)

INPROJ_SUB = 256
MERGE_SUB = 256
ATTN_TK = 256
ATTN_LOOKAHEAD = 3
ATTN_SLOTS = 8
assert ATTN_SLOTS > ATTN_LOOKAHEAD and ATTN_HEADS % ATTN_SLOTS == 0


def _sigmoid(x):
    return 1.0 / (1.0 + jnp.exp(-x))


def _silu(x):
    return x * _sigmoid(x)


def _cparams(*sem):
    return pltpu.CompilerParams(dimension_semantics=sem, vmem_limit_bytes=VMEM_LIMIT)


def _ada_kernel(c_ref, w_ref, b_ref, o_ref):
    ca = _silu(c_ref[...]).astype(BF16)
    o_ref[...] = jnp.dot(ca, w_ref[...].astype(BF16), preferred_element_type=F32) + b_ref[...]


def _ada(c, w_ada, b_ada):
    n = w_ada.shape[1]
    tn = 1024
    return pl.pallas_call(
        _ada_kernel,
        grid=(n // tn,),
        in_specs=[pl.BlockSpec((BATCH, D_MODEL), lambda j: (0, 0)),
                  pl.BlockSpec((D_MODEL, tn), lambda j: (0, j)),
                  pl.BlockSpec((1, tn), lambda j: (0, j))],
        out_specs=pl.BlockSpec((BATCH, tn), lambda j: (0, j)),
        out_shape=jax.ShapeDtypeStruct((BATCH, n), F32),
        compiler_params=_cparams("arbitrary"),
        name="ada",
    )(c, w_ada, b_ada.reshape(1, n))


def _zoh_kernel(lr_ref, li_ref, ldt_ref, br_ref, bi_ref, cre_ref, cim_ref,
                bmat_ref, cmat_ref, are_ref, aim_ref):
    lr = lr_ref[...]
    li = li_ref[...]
    dt = jnp.exp(ldt_ref[...])
    mag = jnp.exp(lr * dt)
    abr = mag * jnp.cos(li * dt)
    abi = mag * jnp.sin(li * dt)
    den = lr * lr + li * li
    cr = ((abr - 1.0) * lr + abi * li) / den
    ci = (abi * lr - (abr - 1.0) * li) / den

    tg, gh, gp = S5_TILE_GROUPS, SSM_GROUP, SSM_STATE
    rep_lanes = (lax.broadcasted_iota(jnp.int32, (gp, tg * gp), 1) % gp
                 == lax.broadcasted_iota(jnp.int32, (gp, tg * gp), 0))
    rep_rows = (lax.broadcasted_iota(jnp.int32, (tg * gp, gp), 0) % gp
                == lax.broadcasted_iota(jnp.int32, (tg * gp, gp), 1))
    rep_lanes_bf = jnp.where(rep_lanes, 1.0, 0.0).astype(BF16)
    rep_rows_bf = jnp.where(rep_rows, 1.0, 0.0).astype(BF16)
    rep_lanes_f32 = jnp.where(rep_lanes, 1.0, 0.0)
    b_diag = (lax.broadcasted_iota(jnp.int32, (tg * gh, tg * gp), 0) // gh
              == lax.broadcasted_iota(jnp.int32, (tg * gh, tg * gp), 1) // gp)
    c_diag = (lax.broadcasted_iota(jnp.int32, (tg * gp, tg * gh), 0) // gp
              == lax.broadcasted_iota(jnp.int32, (tg * gp, tg * gh), 1) // gh)
    a_diag = (lax.broadcasted_iota(jnp.int32, (tg, tg * gp), 0)
              == lax.broadcasted_iota(jnp.int32, (tg, tg * gp), 1) // gp)

    for j in range(S5_TILES):
        bbr, bbi, crs, cis = [], [], [], []
        for g in range(j * tg, (j + 1) * tg):
            crg, cig = cr[g:g + 1, :], ci[g:g + 1, :]
            br, bi = br_ref[g], bi_ref[g]
            bbr.append(crg * br - cig * bi)
            bbi.append(crg * bi + cig * br)
            crs.append(cre_ref[g])
            cis.append(-cim_ref[g])
        halves = []
        for blk in (jnp.concatenate(bbr, axis=0), jnp.concatenate(bbi, axis=0)):
            tiled = jnp.dot(blk.astype(BF16), rep_lanes_bf, preferred_element_type=F32)
            halves.append(jnp.where(b_diag, tiled, 0.0))
        halves.append(jnp.zeros((tg * gh, LANES), F32))
        bmat_ref[j] = jnp.concatenate(halves, axis=1).astype(BF16)
        halves = []
        for blk in (jnp.concatenate(crs, axis=0), jnp.concatenate(cis, axis=0)):
            tiled = lax.dot_general(rep_rows_bf, blk.astype(BF16), (((1,), (1,)), ((), ())),
                                    preferred_element_type=F32)
            halves.append(jnp.where(c_diag, tiled, 0.0))
        cmat_ref[j] = jnp.concatenate(halves, axis=0).astype(BF16)
        for a, a_ref in ((abr, are_ref), (abi, aim_ref)):
            tiled = jnp.dot(a[j * tg:(j + 1) * tg, :], rep_lanes_f32, preferred_element_type=F32,
                            precision=lax.Precision.HIGHEST)
            a_ref[:, j * S5_HALF:(j + 1) * S5_HALF] = jnp.sum(
                jnp.where(a_diag, tiled, 0.0), axis=0, keepdims=True)


def _zoh(lam_re, lam_im, log_dt, b_re, b_im, c_re, c_im):
    swap = lambda a: jnp.swapaxes(a, 1, 2)
    return pl.pallas_call(
        _zoh_kernel,
        out_shape=(jax.ShapeDtypeStruct((S5_TILES, S5_TILE_IN, S5_TILE_LANES + LANES), BF16),
                   jax.ShapeDtypeStruct((S5_TILES, S5_TILE_LANES, S5_TILE_IN), BF16),
                   jax.ShapeDtypeStruct((1, SSM_GROUPS * SSM_STATE), F32),
                   jax.ShapeDtypeStruct((1, SSM_GROUPS * SSM_STATE), F32)),
        name="zoh",
    )(lam_re, lam_im, log_dt.reshape(SSM_GROUPS, 1), swap(b_re), swap(b_im), c_re, c_im)


O_XS, O_ZS, O_Q, O_K, O_V, O_F, O_ZA, O_GS, O_GA, O_END = (
    0, 512, 1024, 1536, 2048, 2560, 2568, 3080, 4104, 5128)


WPREP_CHUNK = 256


def _wprep_kernel(wt_ref, wm_ref, wqv_ref, wf_ref):
    def put(c_lo, o_lo, n):
        for s in range(0, n, WPREP_CHUNK):
            wm_ref[:, c_lo + s:c_lo + s + WPREP_CHUNK] = (
                wt_ref[o_lo + s:o_lo + s + WPREP_CHUNK, :].T.astype(BF16))

    put(C_XS, O_XS, O_Q - O_XS)
    put(C_K, O_K, O_V - O_K)
    put(C_ZA, O_ZA, O_GS - O_ZA)
    put(C_GS, O_GS, O_END - O_GS)
    wm_ref[:, C_END:] = jnp.zeros((wm_ref.shape[0], LANES), BF16)
    wqv_ref[0:ATTN_WIDTH, 0:D_MODEL] = wt_ref[O_Q:O_K, :].astype(BF16)
    wqv_ref[ATTN_WIDTH:2 * ATTN_WIDTH, 0:D_MODEL] = wt_ref[O_V:O_F, :].astype(BF16)
    wqv_ref[:, D_MODEL:] = jnp.zeros((wqv_ref.shape[0], LANES), BF16)
    f8 = wt_ref[O_F:O_ZA, :]
    blk = jnp.concatenate([f8, f8, f8, jnp.zeros((LANES - 3 * ATTN_HEADS, f8.shape[1]), F32)], axis=0)
    wf_ref[...] = blk.T.astype(BF16)


def _wprep(wt):
    n, d = wt.shape
    return pl.pallas_call(
        _wprep_kernel,
        out_shape=[jax.ShapeDtypeStruct((d, C_END + LANES), BF16),
                   jax.ShapeDtypeStruct((2 * ATTN_WIDTH, d + LANES), BF16),
                   jax.ShapeDtypeStruct((d, LANES), BF16)],
        compiler_params=pltpu.CompilerParams(vmem_limit_bytes=VMEM_LIMIT),
        name="wprep",
    )(wt)


def _inproj_kernel(x_ref, scale_ref, shift_ref, wm_ref, wt_ref, wf_ref, bf_ref, tri_ref,
                   xs_ref, szs_ref, k_ref, sza_ref, sgs_ref, sga_ref, qt_ref, vt_ref, caug_ref,
                   carry_ref, *, ts):
    i = pl.program_id(0)
    b = pl.program_id(1)

    @pl.when(i == 0)
    def _():
        carry_ref[b] = jnp.zeros(carry_ref.shape[1:], F32)

    carry = carry_ref[b][0:1, :]
    for c in range(ts // INPROJ_SUB):
        r = slice(c * INPROJ_SUB, (c + 1) * INPROJ_SUB)
        x = x_ref[0, r, :]
        mu = jnp.mean(x, axis=-1, keepdims=True)
        xc = x - mu
        var = jnp.mean(xc * xc, axis=-1, keepdims=True)
        xn = xc * lax.rsqrt(var + LN_EPS)
        u = xn * (1.0 + scale_ref[0, 0]) + shift_ref[0, 0]
        ub = u.astype(BF16)

        def mm(lo, hi, ub=ub):
            return jnp.dot(ub, wm_ref[:, lo:hi], preferred_element_type=F32)

        f = jnp.dot(ub, wf_ref[...], preferred_element_type=F32) + bf_ref[...]
        lf = jnp.minimum(f, 0.0) - jnp.log(1.0 + jnp.exp(-jnp.abs(f)))
        l1 = lf.astype(BF16)
        r1 = lf - l1.astype(F32)
        l2 = r1.astype(BF16)
        l3 = (r1 - l2.astype(F32)).astype(BF16)

        xs = mm(C_XS, C_ZS)
        for j in range(S5_TILES):
            xs_ref[j, pl.ds(b + c * INPROJ_SUB * SUBLANES, INPROJ_SUB, stride=SUBLANES), :] = (
                xs[:, j * LANES:(j + 1) * LANES])
        szs_ref[0, r, :] = _silu(mm(C_ZS, C_K)).astype(BF16)

        tri = tri_ref[...]
        cf = (jnp.dot(tri, l1, preferred_element_type=F32)
              + jnp.dot(tri, l2, preferred_element_type=F32)
              + jnp.dot(tri, l3, preferred_element_type=F32)) + carry
        carry = cf[INPROJ_SUB - 1:INPROJ_SUB, :]

        k_ref[0, r, :] = mm(C_K, C_ZA).astype(BF16)
        sza_ref[0, r, :] = _silu(mm(C_ZA, C_GS)).astype(BF16)

        cf2 = cf * LOG2E
        c1 = cf2.astype(BF16).astype(F32)
        rem = cf2 - c1
        c2 = rem.astype(BF16).astype(F32)
        c3 = rem - c2
        lane = lax.broadcasted_iota(jnp.int32, cf.shape, 1)
        parts = jnp.where(lane < 8, c1, jnp.where(lane < 16, c2, jnp.where(lane < 24, c3, 0.0)))
        caug_ref[0, r, :] = (-parts).astype(BF16)

        sgs_ref[0, r, :] = _sigmoid(mm(C_GS, C_GA)).astype(BF16)
        sga_ref[0, r, :] = _sigmoid(mm(C_GA, C_END)).astype(BF16)

        qv = lax.dot_general(wt_ref[:, 0:D_MODEL], ub, (((1,), (1,)), ((), ())),
                             preferred_element_type=F32)
        qt_ref[0, :, r] = (qv[:ATTN_WIDTH] * (LOG2E * ATTN_HEAD_DIM ** -0.5)).astype(BF16)
        vt_ref[0, :, r] = qv[ATTN_WIDTH:].astype(BF16)
    carry_ref[b] = jnp.broadcast_to(carry, carry_ref.shape[1:])


def _inproj(x, mod4, w_main, w_t, w_f3, b_f3, *, ts):
    nb, s, d = x.shape
    nt = s // ts
    tri = jnp.tril(jnp.ones((INPROJ_SUB, INPROJ_SUB), BF16))
    tok3 = lambda w: pl.BlockSpec((1, ts, w), lambda i, b: (b, i, 0))
    const2 = lambda a: pl.BlockSpec(a.shape, lambda i, b: (0, 0), pipeline_mode=pl.Buffered(1))
    bf = lambda shape: jax.ShapeDtypeStruct(shape, BF16)
    return pl.pallas_call(
        functools.partial(_inproj_kernel, ts=ts),
        grid=(nt, nb),
        in_specs=[tok3(d), _mod_spec(1), _mod_spec(0),
                  const2(w_main), const2(w_t), const2(w_f3), const2(b_f3), const2(tri)],
        out_specs=[pl.BlockSpec((S5_TILES, ts * nb, LANES), lambda i, b: (0, i, 0)),
                   tok3(SSM_WIDTH), tok3(ATTN_WIDTH), tok3(ATTN_WIDTH),
                   tok3(D_MODEL), tok3(D_MODEL),
                   pl.BlockSpec((1, ATTN_WIDTH, ts), lambda i, b: (b, 0, i)),
                   pl.BlockSpec((1, ATTN_WIDTH, ts), lambda i, b: (b, 0, i)),
                   tok3(LANES)],
        out_shape=[jax.ShapeDtypeStruct((S5_TILES, s * nb, LANES), F32),
                   bf((nb, s, SSM_WIDTH)), bf((nb, s, ATTN_WIDTH)), bf((nb, s, ATTN_WIDTH)),
                   bf((nb, s, D_MODEL)), bf((nb, s, D_MODEL)),
                   bf((nb, ATTN_WIDTH, s)), bf((nb, ATTN_WIDTH, s)),
                   bf((nb, s, LANES))],
        scratch_shapes=[pltpu.VMEM((nb, SUBLANES, LANES), F32)],
        compiler_params=_cparams("arbitrary", "arbitrary"),
        name="inproj",
    )(x, mod4, mod4, w_main, w_t, w_f3, b_f3, tri)


def _gelu_tanh(x):
    return 0.5 * x * (1.0 + jnp.tanh(math.sqrt(2.0 / math.pi) * (x + 0.044715 * (x * x * x))))


def _s5_kernel(xs_ref, bmat_ref, cmat_ref, are_ref, aim_ref, d_ref, wglu_ref,
               y_ref, buf_ref, h_ref, ybuf_ref, wglu_bf, *, steps, sub):
    @pl.when(pl.program_id(0) == 0)
    def _():
        h_ref[...] = jnp.zeros_like(h_ref)
        wglu_bf[:, 0:2 * SSM_WIDTH] = wglu_ref[...].astype(BF16)

    srows = sub * SUBLANES

    def expand(c):
        r = slice(c * srows, (c + 1) * srows)
        for j in range(S5_TILES):
            lo = j * S5_TILE_LANES
            buf_ref[r, lo:lo + S5_TILE_LANES] = jnp.dot(
                xs_ref[j, r, :].astype(BF16), bmat_ref[j, :, 0:S5_TILE_LANES],
                preferred_element_type=F32)

    def recur(c, state):
        new_state = []
        for j in range(S5_TILES):
            re = slice(j * S5_TILE_LANES, j * S5_TILE_LANES + S5_HALF)
            im = slice(j * S5_TILE_LANES + S5_HALF, (j + 1) * S5_TILE_LANES)
            ar = jnp.broadcast_to(are_ref[:, j * S5_HALF:(j + 1) * S5_HALF], (SUBLANES, S5_HALF))
            ai = jnp.broadcast_to(aim_ref[:, j * S5_HALF:(j + 1) * S5_HALF], (SUBLANES, S5_HALF))
            hr, hi = state[j]
            for t in range(sub):
                rt = slice(c * srows + t * SUBLANES, c * srows + (t + 1) * SUBLANES)
                nhr = ar * hr - ai * hi + buf_ref[rt, re]
                nhi = ar * hi + ai * hr + buf_ref[rt, im]
                buf_ref[rt, re] = nhr
                buf_ref[rt, im] = nhi
                hr, hi = nhr, nhi
            new_state.append((hr, hi))
        return new_state

    def project(c):
        r = slice(c * srows, (c + 1) * srows)
        ys = []
        for j in range(S5_TILES):
            lo = j * S5_TILE_LANES
            hb = buf_ref[r, lo:lo + S5_TILE_LANES].astype(BF16)
            ys.append(jnp.dot(hb, cmat_ref[j], preferred_element_type=F32)
                      + d_ref[:, j * LANES:(j + 1) * LANES] * xs_ref[j, r, :])
        g = _gelu_tanh(jnp.concatenate(ys, axis=1)).astype(BF16)
        ab = jnp.dot(g, wglu_bf[:, 0:2 * SSM_WIDTH], preferred_element_type=F32)
        out = ab[:, :SSM_WIDTH] * _sigmoid(ab[:, SSM_WIDTH:])
        for j in range(S5_TILES):
            ybuf_ref[j] = out[:, j * LANES:(j + 1) * LANES]
        for b in range(BATCH):
            yb = jnp.concatenate([ybuf_ref[j, pl.ds(b, sub, stride=SUBLANES), :]
                                  for j in range(S5_TILES)], axis=1)
            y_ref[b, c * sub:(c + 1) * sub, :] = yb.astype(BF16)

    state = [(h_ref[:, j * S5_TILE_LANES:j * S5_TILE_LANES + S5_HALF],
              h_ref[:, j * S5_TILE_LANES + S5_HALF:(j + 1) * S5_TILE_LANES])
             for j in range(S5_TILES)]
    nsub = steps // sub
    expand(0)
    for c in range(nsub):
        if c + 1 < nsub:
            expand(c + 1)
        state = recur(c, state)
        project(c)
    for j in range(S5_TILES):
        h_ref[:, j * S5_TILE_LANES:j * S5_TILE_LANES + S5_HALF] = state[j][0]
        h_ref[:, j * S5_TILE_LANES + S5_HALF:(j + 1) * S5_TILE_LANES] = state[j][1]


def _s5(xs_t, bmat, cmat, a_re, a_im, d_skip, w_glu, *, steps, sub):
    t = xs_t.shape[1]
    rows = steps * SUBLANES
    const = lambda a: pl.BlockSpec(a.shape, lambda i: (0,) * a.ndim)
    tiles = pl.BlockSpec((S5_TILES, rows, LANES), lambda i: (0, i, 0))
    return pl.pallas_call(
        functools.partial(_s5_kernel, steps=steps, sub=sub),
        grid=(t // rows,),
        in_specs=[tiles,
                  const(bmat), const(cmat), const(a_re), const(a_im), const(d_skip), const(w_glu)],
        out_specs=pl.BlockSpec((BATCH, steps, SSM_WIDTH), lambda i: (0, i, 0)),
        out_shape=jax.ShapeDtypeStruct((BATCH, t // SUBLANES, SSM_WIDTH), BF16),
        scratch_shapes=[pltpu.VMEM((rows, S5_STATE_LANES), F32),
                        pltpu.VMEM((SUBLANES, S5_STATE_LANES), F32),
                        pltpu.VMEM((S5_TILES, sub * SUBLANES, LANES), F32),
                        pltpu.VMEM((w_glu.shape[0], w_glu.shape[1] + LANES), BF16)],
        compiler_params=_cparams("arbitrary"),
        name="s5",
    )(xs_t, bmat, cmat, a_re, a_im, d_skip, w_glu)


def _attn_kernel(qt_ref, k_ref, caug_ref, vt_ref, sza_ref, o_ref,
                 rhs_ref, s_ref, m_ref, l_ref, acc_ref, *, tq):
    qi = pl.program_id(1)
    tk = ATTN_TK
    nd = tq // tk
    hd = ATTN_HEAD_DIM
    row = lax.broadcasted_iota(jnp.int32, (LANES, tq), 0)

    for h in range(ATTN_HEADS):
        j = h // 2
        q2 = qt_ref[0, j * LANES:(j + 1) * LANES, :]
        rhs_ref[h, 0:LANES, 0:tq] = jnp.where((row // hd) == (h % 2), q2, jnp.zeros_like(q2))
        rhs_ref[h, LANES:2 * LANES, 0:tq] = jnp.where(
            (row == h) | (row == 8 + h) | (row == 16 + h), 1.0, 0.0).astype(BF16)
    m_ref[...] = jnp.full(m_ref.shape, -jnp.inf, F32)
    l_ref[...] = jnp.zeros_like(l_ref)
    acc_ref[...] = jnp.zeros_like(acc_ref)

    def scores(k0, h, diag=None):
        j = h // 2
        lo = 0 if diag is None else diag
        lhs = jnp.concatenate([k_ref[0, pl.ds(k0, tk), j * LANES:(j + 1) * LANES],
                               caug_ref[0, pl.ds(k0, tk), :]], axis=1)
        s_ref[h % ATTN_SLOTS, :, lo:tq] = jnp.dot(lhs, rhs_ref[h, :, lo:tq],
                                                  preferred_element_type=F32)

    def update(k0, h, diag=None):
        lo = 0 if diag is None else diag
        s = s_ref[h % ATTN_SLOTS, :, lo:tq]
        if diag is not None:
            kpos = lax.broadcasted_iota(jnp.int32, s.shape, 0)
            qpos = lax.broadcasted_iota(jnp.int32, s.shape, 1)
            s = jnp.where(kpos <= qpos, s, NEG_BIG)
        hs = slice(h * SUBLANES, h * SUBLANES + 1)
        m = m_ref[hs, lo:tq]
        m_new = jnp.maximum(m, jnp.max(s, axis=0, keepdims=True))
        alpha = jnp.exp2(m - m_new)
        p = jnp.exp2(s - m_new).astype(BF16)
        m_ref[hs, lo:tq] = m_new
        vt1 = jnp.concatenate([vt_ref[0, h * hd:(h + 1) * hd, pl.ds(k0, tk)],
                               jnp.ones((2 * SUBLANES, tk), BF16)], axis=0)
        pv = jnp.dot(vt1, p, preferred_element_type=F32)
        l_ref[hs, lo:tq] = alpha * l_ref[hs, lo:tq] + pv[hd:hd + 1, :]
        acc_ref[h * hd:(h + 1) * hd, lo:tq] = (alpha * acc_ref[h * hd:(h + 1) * hd, lo:tq]
                                               + pv[:hd, :])

    def body(kj, carry):
        k0 = pl.multiple_of(kj * tk, tk)
        k1 = pl.multiple_of(kj * tk + tk, tk)
        for h in range(ATTN_HEADS):
            ha = h + ATTN_LOOKAHEAD
            if ha < ATTN_HEADS:
                scores(k0, ha)
            else:
                scores(k1, ha - ATTN_HEADS)
            update(k0, h)
        return carry

    for h in range(ATTN_LOOKAHEAD):
        scores(0, h)
    lax.fori_loop(0, qi * nd, body, 0)
    for d in range(nd):
        kd = pl.multiple_of(qi * tq + d * tk, tk)
        for h in range(ATTN_HEADS):
            ha = h + ATTN_LOOKAHEAD
            if ha < ATTN_HEADS:
                scores(kd, ha, d * tk)
            elif d + 1 < nd:
                scores(pl.multiple_of(kd + tk, tk), ha - ATTN_HEADS, (d + 1) * tk)
            update(kd, h, d * tk)

    for h in range(ATTN_HEADS):
        acc_ref[h * hd:(h + 1) * hd, 0:tq] = (acc_ref[h * hd:(h + 1) * hd, 0:tq]
                                              / l_ref[h * SUBLANES:h * SUBLANES + 1, :])
    o = acc_ref[:, 0:tq].T
    o_ref[0] = (o * sza_ref[0].astype(F32)).astype(BF16)


def _attn(qt, k, caug, vt, sza, *, tq):
    nb, s, w = k.shape
    return pl.pallas_call(
        functools.partial(_attn_kernel, tq=tq),
        grid=(nb, s // tq),
        in_specs=[pl.BlockSpec((1, w, tq), lambda b, i: (b, 0, i)),
                  pl.BlockSpec((1, s, w), lambda b, i: (b, 0, 0)),
                  pl.BlockSpec((1, s, LANES), lambda b, i: (b, 0, 0)),
                  pl.BlockSpec((1, w, s), lambda b, i: (b, 0, 0)),
                  pl.BlockSpec((1, tq, w), lambda b, i: (b, i, 0))],
        out_specs=pl.BlockSpec((1, tq, w), lambda b, i: (b, i, 0)),
        out_shape=jax.ShapeDtypeStruct((nb, s, w), BF16),
        scratch_shapes=[pltpu.VMEM((ATTN_HEADS, 2 * LANES, tq + LANES), BF16),
                        pltpu.VMEM((ATTN_SLOTS, ATTN_TK, tq + LANES), F32),
                        pltpu.VMEM((ATTN_HEADS * SUBLANES, tq), F32),
                        pltpu.VMEM((ATTN_HEADS * SUBLANES, tq), F32),
                        pltpu.VMEM((w, tq + LANES), F32)],
        compiler_params=_cparams("arbitrary", "arbitrary"),
        name="attn",
    )(qt, k, caug, vt, sza)


def _merge_kernel(x_ref, ys_ref, szs_ref, oz_ref, sgs_ref, sga_ref, gate_ref,
                  wps_ref, wpa_ref, wout_ref, lng_ref, lnb_ref, o_ref,
                  wps_bf, wpa_bf, wout_bf, *, ts):
    @pl.when((pl.program_id(0) == 0) & (pl.program_id(1) == 0))
    def _():
        wps_bf[:, 0:D_MODEL] = wps_ref[...].astype(BF16)
        wpa_bf[:, 0:D_MODEL] = wpa_ref[...].astype(BF16)
        wout_bf[:, 0:D_MODEL] = wout_ref[...].astype(BF16)

    gate = gate_ref[0, 0]
    for c in range(ts // MERGE_SUB):
        r = slice(c * MERGE_SUB, (c + 1) * MERGE_SUB)
        a = (ys_ref[0, r, :].astype(F32) * szs_ref[0, r, :].astype(F32)).astype(BF16)
        m1 = (jnp.dot(a, wps_bf[:, 0:D_MODEL], preferred_element_type=F32)
              * sgs_ref[0, r, :].astype(F32))
        m2 = (jnp.dot(oz_ref[0, r, :], wpa_bf[:, 0:D_MODEL], preferred_element_type=F32)
              * sga_ref[0, r, :].astype(F32))
        merged = (m1 + m2).astype(BF16)
        sub = jnp.dot(merged, wout_bf[:, 0:D_MODEL], preferred_element_type=F32) * gate
        y = DEEPNORM_ALPHA * x_ref[0, r, :] + sub
        mu = jnp.mean(y, axis=-1, keepdims=True)
        yc = y - mu
        var = jnp.mean(yc * yc, axis=-1, keepdims=True)
        o_ref[0, r, :] = yc * lax.rsqrt(var + LN_EPS) * lng_ref[...] + lnb_ref[...]


def _mod_spec(part):
    return pl.BlockSpec((1, 1, 1, D_MODEL), lambda i, b: (b, part, 0, 0))


def _merge(x, ys_t, szs, oz, sgs, sga, mod4, wps, wpa, wout, ln_g, ln_b, *, ts):
    nb, s, d = x.shape
    tok3 = lambda w: pl.BlockSpec((1, ts, w), lambda i, b: (b, i, 0))
    const2 = lambda a: pl.BlockSpec(a.shape, lambda i, b: (0, 0), pipeline_mode=pl.Buffered(1))
    return pl.pallas_call(
        functools.partial(_merge_kernel, ts=ts),
        grid=(s // ts, nb),
        in_specs=[tok3(d), tok3(SSM_WIDTH),
                  tok3(SSM_WIDTH), tok3(ATTN_WIDTH), tok3(d), tok3(d),
                  _mod_spec(2),
                  const2(wps), const2(wpa), const2(wout), const2(ln_g), const2(ln_b)],
        out_specs=tok3(d),
        out_shape=jax.ShapeDtypeStruct((nb, s, d), x.dtype),
        scratch_shapes=[pltpu.VMEM((w.shape[0], d + LANES), BF16) for w in (wps, wpa, wout)],
        compiler_params=_cparams("arbitrary", "arbitrary"),
        name="merge",
    )(x, ys_t, szs, oz, sgs, sga, mod4, wps, wpa, wout, ln_g, ln_b)


def kernel(x, c, w_ada, b_ada, w_in, b_f, lam_re, lam_im, log_dt, ssm_b_re, ssm_b_im,
           ssm_c_re, ssm_c_im, ssm_d, w_glu, w_proj_ssm, w_proj_attn, w_out, ln_g, ln_b):
    nb, s, d = x.shape
    l = 0
    mod4 = _ada(c, w_ada[l], b_ada[l]).reshape(nb, 3, 1, d)

    w_main, w_t, w_f3 = _wprep(jnp.swapaxes(w_in[l], 0, 1))
    bfv = b_f[l].astype(F32)
    b_f3 = jnp.concatenate([bfv, bfv, bfv, jnp.zeros((LANES - 3 * ATTN_HEADS,), F32)])[None, :]

    (xs_t, szs, k, sza, sgs, sga, qt, vt, caug) = _inproj(
        x, mod4, w_main, w_t, w_f3, b_f3, ts=512)

    bmat, cmat, a_re, a_im = _zoh(lam_re[l], lam_im[l], log_dt[l], ssm_b_re[l], ssm_b_im[l],
                                  ssm_c_re[l], ssm_c_im[l])

    ys_t = _s5(xs_t, bmat, cmat, a_re, a_im,
               ssm_d[l].reshape(1, SSM_WIDTH).astype(F32), w_glu[l], steps=128, sub=32)

    oz = _attn(qt, k, caug, vt, sza, tq=1024)

    return _merge(x, ys_t, szs, oz, sgs, sga, mod4,
                  w_proj_ssm[l], w_proj_attn[l], w_out[l],
                  ln_g[l].reshape(1, d).astype(F32), ln_b[l].astype(F32).reshape(1, d), ts=1024)
```

```python
import functools
import math

import jax
import jax.numpy as jnp
from jax import lax
from jax.experimental import pallas as pl
from jax.experimental.pallas import tpu as pltpu

D_MODEL = 1024
BATCH = 8
SEQ = 2048
SSM_WIDTH = 512
SSM_GROUP = 16
SSM_GROUPS = 32
SSM_STATE = 64
ATTN_HEADS = 8
ATTN_HEAD_DIM = 64
ATTN_WIDTH = 512
LN_EPS = 1e-5
DEEPNORM_ALPHA = 2.0 ** 0.25

F32 = jnp.float32
BF16 = jnp.bfloat16

LANES = 128
SUBLANES = 8
VMEM_LIMIT = 52 * 1024 * 1024
MERGE_VMEM_LIMIT = 58 * 1024 * 1024

S5_TILES = 4
S5_TILE_GROUPS = SSM_GROUPS // S5_TILES
S5_HALF = S5_TILE_GROUPS * SSM_STATE
S5_TILE_LANES = 2 * S5_HALF
S5_TILE_IN = S5_TILE_GROUPS * SSM_GROUP
S5_STATE_LANES = S5_TILES * S5_TILE_LANES

C_XS, C_ZS, C_K, C_ZA, C_GS, C_GA, C_END = 0, 512, 1024, 1536, 2048, 3072, 4096

NEG_BIG = -1e30
LOG2E = math.log2(math.e)

INPROJ_SUB = 256
MERGE_SUB = 256
ATTN_TK = 256
ATTN_LOOKAHEAD = 3
ATTN_SLOTS = 8
assert ATTN_SLOTS > ATTN_LOOKAHEAD and ATTN_HEADS % ATTN_SLOTS == 0


def _sigmoid(x):
    return 1.0 / (1.0 + jnp.exp(-x))


def _silu(x):
    return x * _sigmoid(x)


def _cparams(*sem):
    return pltpu.CompilerParams(dimension_semantics=sem, vmem_limit_bytes=VMEM_LIMIT)


def _ada_kernel(c_ref, w_ref, b_ref, o_ref):
    ca = _silu(c_ref[...]).astype(BF16)
    o_ref[...] = jnp.dot(ca, w_ref[...].astype(BF16), preferred_element_type=F32) + b_ref[...]


def _ada(c, w_ada, b_ada):
    n = w_ada.shape[1]
    tn = 1024
    return pl.pallas_call(
        _ada_kernel,
        grid=(n // tn,),
        in_specs=[pl.BlockSpec((BATCH, D_MODEL), lambda j: (0, 0)),
                  pl.BlockSpec((D_MODEL, tn), lambda j: (0, j)),
                  pl.BlockSpec((1, tn), lambda j: (0, j))],
        out_specs=pl.BlockSpec((BATCH, tn), lambda j: (0, j)),
        out_shape=jax.ShapeDtypeStruct((BATCH, n), F32),
        compiler_params=_cparams("arbitrary"),
        name="ada",
    )(c, w_ada, b_ada.reshape(1, n))


def _zoh_kernel(lr_ref, li_ref, ldt_ref, br_ref, bi_ref, cre_ref, cim_ref,
                bmat_ref, cmat_ref, are_ref, aim_ref):
    lr = lr_ref[...]
    li = li_ref[...]
    dt = jnp.exp(ldt_ref[...])
    mag = jnp.exp(lr * dt)
    abr = mag * jnp.cos(li * dt)
    abi = mag * jnp.sin(li * dt)
    den = lr * lr + li * li
    cr = ((abr - 1.0) * lr + abi * li) / den
    ci = (abi * lr - (abr - 1.0) * li) / den

    tg, gh, gp = S5_TILE_GROUPS, SSM_GROUP, SSM_STATE
    rep_lanes = (lax.broadcasted_iota(jnp.int32, (gp, tg * gp), 1) % gp
                 == lax.broadcasted_iota(jnp.int32, (gp, tg * gp), 0))
    rep_rows = (lax.broadcasted_iota(jnp.int32, (tg * gp, gp), 0) % gp
                == lax.broadcasted_iota(jnp.int32, (tg * gp, gp), 1))
    rep_lanes_bf = jnp.where(rep_lanes, 1.0, 0.0).astype(BF16)
    rep_rows_bf = jnp.where(rep_rows, 1.0, 0.0).astype(BF16)
    rep_lanes_f32 = jnp.where(rep_lanes, 1.0, 0.0)
    b_diag = (lax.broadcasted_iota(jnp.int32, (tg * gh, tg * gp), 0) // gh
              == lax.broadcasted_iota(jnp.int32, (tg * gh, tg * gp), 1) // gp)
    c_diag = (lax.broadcasted_iota(jnp.int32, (tg * gp, tg * gh), 0) // gp
              == lax.broadcasted_iota(jnp.int32, (tg * gp, tg * gh), 1) // gh)
    a_diag = (lax.broadcasted_iota(jnp.int32, (tg, tg * gp), 0)
              == lax.broadcasted_iota(jnp.int32, (tg, tg * gp), 1) // gp)

    for j in range(S5_TILES):
        bbr, bbi, crs, cis = [], [], [], []
        for g in range(j * tg, (j + 1) * tg):
            crg, cig = cr[g:g + 1, :], ci[g:g + 1, :]
            br, bi = br_ref[g], bi_ref[g]
            bbr.append(crg * br - cig * bi)
            bbi.append(crg * bi + cig * br)
            crs.append(cre_ref[g])
            cis.append(-cim_ref[g])
        halves = []
        for blk in (jnp.concatenate(bbr, axis=0), jnp.concatenate(bbi, axis=0)):
            tiled = jnp.dot(blk.astype(BF16), rep_lanes_bf, preferred_element_type=F32)
            halves.append(jnp.where(b_diag, tiled, 0.0))
        halves.append(jnp.zeros((tg * gh, LANES), F32))
        bmat_ref[j] = jnp.concatenate(halves, axis=1).astype(BF16)
        halves = []
        for blk in (jnp.concatenate(crs, axis=0), jnp.concatenate(cis, axis=0)):
            tiled = lax.dot_general(rep_rows_bf, blk.astype(BF16), (((1,), (1,)), ((), ())),
                                    preferred_element_type=F32)
            halves.append(jnp.where(c_diag, tiled, 0.0))
        cmat_ref[j] = jnp.concatenate(halves, axis=0).astype(BF16)
        for a, a_ref in ((abr, are_ref), (abi, aim_ref)):
            tiled = jnp.dot(a[j * tg:(j + 1) * tg, :], rep_lanes_f32, preferred_element_type=F32,
                            precision=lax.Precision.HIGHEST)
            a_ref[:, j * S5_HALF:(j + 1) * S5_HALF] = jnp.sum(
                jnp.where(a_diag, tiled, 0.0), axis=0, keepdims=True)


def _zoh(lam_re, lam_im, log_dt, b_re, b_im, c_re, c_im):
    swap = lambda a: jnp.swapaxes(a, 1, 2)
    return pl.pallas_call(
        _zoh_kernel,
        out_shape=(jax.ShapeDtypeStruct((S5_TILES, S5_TILE_IN, S5_TILE_LANES + LANES), BF16),
                   jax.ShapeDtypeStruct((S5_TILES, S5_TILE_LANES, S5_TILE_IN), BF16),
                   jax.ShapeDtypeStruct((1, SSM_GROUPS * SSM_STATE), F32),
                   jax.ShapeDtypeStruct((1, SSM_GROUPS * SSM_STATE), F32)),
        name="zoh",
    )(lam_re, lam_im, log_dt.reshape(SSM_GROUPS, 1), swap(b_re), swap(b_im), c_re, c_im)


O_XS, O_ZS, O_Q, O_K, O_V, O_F, O_ZA, O_GS, O_GA, O_END = (
    0, 512, 1024, 1536, 2048, 2560, 2568, 3080, 4104, 5128)


WPREP_CHUNK = 256


def _wprep_kernel(wt_ref, wm_ref, wqv_ref, wf_ref):
    def put(c_lo, o_lo, n):
        for s in range(0, n, WPREP_CHUNK):
            wm_ref[:, c_lo + s:c_lo + s + WPREP_CHUNK] = (
                wt_ref[o_lo + s:o_lo + s + WPREP_CHUNK, :].T.astype(BF16))

    put(C_XS, O_XS, O_Q - O_XS)
    put(C_K, O_K, O_V - O_K)
    put(C_ZA, O_ZA, O_GS - O_ZA)
    put(C_GS, O_GS, O_END - O_GS)
    wm_ref[:, C_END:] = jnp.zeros((wm_ref.shape[0], LANES), BF16)
    wqv_ref[0:ATTN_WIDTH, 0:D_MODEL] = wt_ref[O_Q:O_K, :].astype(BF16)
    wqv_ref[ATTN_WIDTH:2 * ATTN_WIDTH, 0:D_MODEL] = wt_ref[O_V:O_F, :].astype(BF16)
    wqv_ref[:, D_MODEL:] = jnp.zeros((wqv_ref.shape[0], LANES), BF16)
    f8 = wt_ref[O_F:O_ZA, :]
    blk = jnp.concatenate([f8, f8, f8, jnp.zeros((LANES - 3 * ATTN_HEADS, f8.shape[1]), F32)], axis=0)
    wf_ref[...] = blk.T.astype(BF16)


def _wprep(wt):
    n, d = wt.shape
    return pl.pallas_call(
        _wprep_kernel,
        out_shape=[jax.ShapeDtypeStruct((d, C_END + LANES), BF16),
                   jax.ShapeDtypeStruct((2 * ATTN_WIDTH, d + LANES), BF16),
                   jax.ShapeDtypeStruct((d, LANES), BF16)],
        compiler_params=pltpu.CompilerParams(vmem_limit_bytes=VMEM_LIMIT),
        name="wprep",
    )(wt)


def _inproj_kernel(x_ref, scale_ref, shift_ref, wm_ref, wt_ref, wf_ref, bf_ref, tri_ref,
                   xs_ref, szs_ref, k_ref, sza_ref, sgs_ref, sga_ref, qt_ref, vt_ref, caug_ref,
                   carry_ref, *, ts):
    i = pl.program_id(0)
    b = pl.program_id(1)

    @pl.when(i == 0)
    def _():
        carry_ref[b] = jnp.zeros(carry_ref.shape[1:], F32)

    carry = carry_ref[b][0:1, :]
    for c in range(ts // INPROJ_SUB):
        r = slice(c * INPROJ_SUB, (c + 1) * INPROJ_SUB)
        x = x_ref[0, r, :]
        mu = jnp.mean(x, axis=-1, keepdims=True)
        xc = x - mu
        var = jnp.mean(xc * xc, axis=-1, keepdims=True)
        xn = xc * lax.rsqrt(var + LN_EPS)
        u = xn * (1.0 + scale_ref[0, 0]) + shift_ref[0, 0]
        ub = u.astype(BF16)

        def mm(lo, hi, ub=ub):
            return jnp.dot(ub, wm_ref[:, lo:hi], preferred_element_type=F32)

        f = jnp.dot(ub, wf_ref[...], preferred_element_type=F32) + bf_ref[...]
        lf = jnp.minimum(f, 0.0) - jnp.log(1.0 + jnp.exp(-jnp.abs(f)))
        l1 = lf.astype(BF16)
        r1 = lf - l1.astype(F32)
        l2 = r1.astype(BF16)
        l3 = (r1 - l2.astype(F32)).astype(BF16)

        xs = mm(C_XS, C_ZS)
        for j in range(S5_TILES):
            xs_ref[j, pl.ds(b + c * INPROJ_SUB * SUBLANES, INPROJ_SUB, stride=SUBLANES), :] = (
                xs[:, j * LANES:(j + 1) * LANES])
        szs_ref[0, r, :] = _silu(mm(C_ZS, C_K)).astype(BF16)

        tri = tri_ref[...]
        cf = (jnp.dot(tri, l1, preferred_element_type=F32)
              + jnp.dot(tri, l2, preferred_element_type=F32)
              + jnp.dot(tri, l3, preferred_element_type=F32)) + carry
        carry = cf[INPROJ_SUB - 1:INPROJ_SUB, :]

        k_ref[0, r, :] = mm(C_K, C_ZA).astype(BF16)
        sza_ref[0, r, :] = _silu(mm(C_ZA, C_GS)).astype(BF16)

        cf2 = cf * LOG2E
        c1 = cf2.astype(BF16).astype(F32)
        rem = cf2 - c1
        c2 = rem.astype(BF16).astype(F32)
        c3 = rem - c2
        lane = lax.broadcasted_iota(jnp.int32, cf.shape, 1)
        parts = jnp.where(lane < 8, c1, jnp.where(lane < 16, c2, jnp.where(lane < 24, c3, 0.0)))
        caug_ref[0, r, :] = (-parts).astype(BF16)

        sgs_ref[0, r, :] = _sigmoid(mm(C_GS, C_GA)).astype(BF16)
        sga_ref[0, r, :] = _sigmoid(mm(C_GA, C_END)).astype(BF16)

        qv = lax.dot_general(wt_ref[:, 0:D_MODEL], ub, (((1,), (1,)), ((), ())),
                             preferred_element_type=F32)
        qt_ref[0, :, r] = (qv[:ATTN_WIDTH] * (LOG2E * ATTN_HEAD_DIM ** -0.5)).astype(BF16)
        vt_ref[0, :, r] = qv[ATTN_WIDTH:].astype(BF16)
    carry_ref[b] = jnp.broadcast_to(carry, carry_ref.shape[1:])


def _inproj(x, mod4, w_main, w_t, w_f3, b_f3, *, ts):
    nb, s, d = x.shape
    nt = s // ts
    tri = jnp.tril(jnp.ones((INPROJ_SUB, INPROJ_SUB), BF16))
    tok3 = lambda w: pl.BlockSpec((1, ts, w), lambda i, b: (b, i, 0))
    const2 = lambda a: pl.BlockSpec(a.shape, lambda i, b: (0, 0), pipeline_mode=pl.Buffered(1))
    bf = lambda shape: jax.ShapeDtypeStruct(shape, BF16)
    return pl.pallas_call(
        functools.partial(_inproj_kernel, ts=ts),
        grid=(nt, nb),
        in_specs=[tok3(d), _mod_spec(1), _mod_spec(0),
                  const2(w_main), const2(w_t), const2(w_f3), const2(b_f3), const2(tri)],
        out_specs=[pl.BlockSpec((S5_TILES, ts * nb, LANES), lambda i, b: (0, i, 0)),
                   tok3(SSM_WIDTH), tok3(ATTN_WIDTH), tok3(ATTN_WIDTH),
                   tok3(D_MODEL), tok3(D_MODEL),
                   pl.BlockSpec((1, ATTN_WIDTH, ts), lambda i, b: (b, 0, i)),
                   pl.BlockSpec((1, ATTN_WIDTH, ts), lambda i, b: (b, 0, i)),
                   tok3(LANES)],
        out_shape=[jax.ShapeDtypeStruct((S5_TILES, s * nb, LANES), F32),
                   bf((nb, s, SSM_WIDTH)), bf((nb, s, ATTN_WIDTH)), bf((nb, s, ATTN_WIDTH)),
                   bf((nb, s, D_MODEL)), bf((nb, s, D_MODEL)),
                   bf((nb, ATTN_WIDTH, s)), bf((nb, ATTN_WIDTH, s)),
                   bf((nb, s, LANES))],
        scratch_shapes=[pltpu.VMEM((nb, SUBLANES, LANES), F32)],
        compiler_params=_cparams("arbitrary", "arbitrary"),
        name="inproj",
    )(x, mod4, mod4, w_main, w_t, w_f3, b_f3, tri)


def _gelu_tanh(x):
    return 0.5 * x * (1.0 + jnp.tanh(math.sqrt(2.0 / math.pi) * (x + 0.044715 * (x * x * x))))


def _s5_kernel(xs_ref, bmat_ref, cmat_ref, are_ref, aim_ref, d_ref, wglu_ref,
               y_ref, buf_ref, h_ref, ybuf_ref, wglu_bf, *, steps, sub):
    @pl.when(pl.program_id(0) == 0)
    def _():
        h_ref[...] = jnp.zeros_like(h_ref)
        wglu_bf[:, 0:2 * SSM_WIDTH] = wglu_ref[...].astype(BF16)

    srows = sub * SUBLANES

    def expand(c):
        r = slice(c * srows, (c + 1) * srows)
        for j in range(S5_TILES):
            lo = j * S5_TILE_LANES
            buf_ref[r, lo:lo + S5_TILE_LANES] = jnp.dot(
                xs_ref[j, r, :].astype(BF16), bmat_ref[j, :, 0:S5_TILE_LANES],
                preferred_element_type=F32)

    def recur(c, state):
        new_state = []
        for j in range(S5_TILES):
            re = slice(j * S5_TILE_LANES, j * S5_TILE_LANES + S5_HALF)
            im = slice(j * S5_TILE_LANES + S5_HALF, (j + 1) * S5_TILE_LANES)
            ar = jnp.broadcast_to(are_ref[:, j * S5_HALF:(j + 1) * S5_HALF], (SUBLANES, S5_HALF))
            ai = jnp.broadcast_to(aim_ref[:, j * S5_HALF:(j + 1) * S5_HALF], (SUBLANES, S5_HALF))
            hr, hi = state[j]
            for t in range(sub):
                rt = slice(c * srows + t * SUBLANES, c * srows + (t + 1) * SUBLANES)
                nhr = ar * hr - ai * hi + buf_ref[rt, re]
                nhi = ar * hi + ai * hr + buf_ref[rt, im]
                buf_ref[rt, re] = nhr
                buf_ref[rt, im] = nhi
                hr, hi = nhr, nhi
            new_state.append((hr, hi))
        return new_state

    def project(c):
        r = slice(c * srows, (c + 1) * srows)
        ys = []
        for j in range(S5_TILES):
            lo = j * S5_TILE_LANES
            hb = buf_ref[r, lo:lo + S5_TILE_LANES].astype(BF16)
            ys.append(jnp.dot(hb, cmat_ref[j], preferred_element_type=F32)
                      + d_ref[:, j * LANES:(j + 1) * LANES] * xs_ref[j, r, :])
        g = _gelu_tanh(jnp.concatenate(ys, axis=1)).astype(BF16)
        ab = jnp.dot(g, wglu_bf[:, 0:2 * SSM_WIDTH], preferred_element_type=F32)
        out = ab[:, :SSM_WIDTH] * _sigmoid(ab[:, SSM_WIDTH:])
        for j in range(S5_TILES):
            ybuf_ref[j] = out[:, j * LANES:(j + 1) * LANES]
        for b in range(BATCH):
            yb = jnp.concatenate([ybuf_ref[j, pl.ds(b, sub, stride=SUBLANES), :]
                                  for j in range(S5_TILES)], axis=1)
            y_ref[b, c * sub:(c + 1) * sub, :] = yb.astype(BF16)

    state = [(h_ref[:, j * S5_TILE_LANES:j * S5_TILE_LANES + S5_HALF],
              h_ref[:, j * S5_TILE_LANES + S5_HALF:(j + 1) * S5_TILE_LANES])
             for j in range(S5_TILES)]
    nsub = steps // sub
    expand(0)
    for c in range(nsub):
        if c + 1 < nsub:
            expand(c + 1)
        state = recur(c, state)
        project(c)
    for j in range(S5_TILES):
        h_ref[:, j * S5_TILE_LANES:j * S5_TILE_LANES + S5_HALF] = state[j][0]
        h_ref[:, j * S5_TILE_LANES + S5_HALF:(j + 1) * S5_TILE_LANES] = state[j][1]


def _s5(xs_t, bmat, cmat, a_re, a_im, d_skip, w_glu, *, steps, sub):
    t = xs_t.shape[1]
    rows = steps * SUBLANES
    const = lambda a: pl.BlockSpec(a.shape, lambda i: (0,) * a.ndim)
    tiles = pl.BlockSpec((S5_TILES, rows, LANES), lambda i: (0, i, 0))
    return pl.pallas_call(
        functools.partial(_s5_kernel, steps=steps, sub=sub),
        grid=(t // rows,),
        in_specs=[tiles,
                  const(bmat), const(cmat), const(a_re), const(a_im), const(d_skip), const(w_glu)],
        out_specs=pl.BlockSpec((BATCH, steps, SSM_WIDTH), lambda i: (0, i, 0)),
        out_shape=jax.ShapeDtypeStruct((BATCH, t // SUBLANES, SSM_WIDTH), BF16),
        scratch_shapes=[pltpu.VMEM((rows, S5_STATE_LANES), F32),
                        pltpu.VMEM((SUBLANES, S5_STATE_LANES), F32),
                        pltpu.VMEM((S5_TILES, sub * SUBLANES, LANES), F32),
                        pltpu.VMEM((w_glu.shape[0], w_glu.shape[1] + LANES), BF16)],
        compiler_params=_cparams("arbitrary"),
        name="s5",
    )(xs_t, bmat, cmat, a_re, a_im, d_skip, w_glu)


def _attn_kernel(qt_ref, k_ref, caug_ref, vt_ref, sza_ref, o_ref,
                 rhs_ref, s_ref, m_ref, l_ref, acc_ref, *, tq):
    qi = pl.program_id(1)
    tk = ATTN_TK
    nd = tq // tk
    hd = ATTN_HEAD_DIM
    row = lax.broadcasted_iota(jnp.int32, (LANES, tq), 0)

    for h in range(ATTN_HEADS):
        j = h // 2
        q2 = qt_ref[0, j * LANES:(j + 1) * LANES, :]
        rhs_ref[h, 0:LANES, 0:tq] = jnp.where((row // hd) == (h % 2), q2, jnp.zeros_like(q2))
        rhs_ref[h, LANES:2 * LANES, 0:tq] = jnp.where(
            (row == h) | (row == 8 + h) | (row == 16 + h), 1.0, 0.0).astype(BF16)
    m_ref[...] = jnp.full(m_ref.shape, -jnp.inf, F32)
    l_ref[...] = jnp.zeros_like(l_ref)
    acc_ref[...] = jnp.zeros_like(acc_ref)

    def scores(k0, h, diag=None):
        j = h // 2
        lo = 0 if diag is None else diag
        lhs = jnp.concatenate([k_ref[0, pl.ds(k0, tk), j * LANES:(j + 1) * LANES],
                               caug_ref[0, pl.ds(k0, tk), :]], axis=1)
        s_ref[h % ATTN_SLOTS, :, lo:tq] = jnp.dot(lhs, rhs_ref[h, :, lo:tq],
                                                  preferred_element_type=F32)

    def update(k0, h, diag=None):
        lo = 0 if diag is None else diag
        s = s_ref[h % ATTN_SLOTS, :, lo:tq]
        if diag is not None:
            kpos = lax.broadcasted_iota(jnp.int32, s.shape, 0)
            qpos = lax.broadcasted_iota(jnp.int32, s.shape, 1)
            s = jnp.where(kpos <= qpos, s, NEG_BIG)
        hs = slice(h * SUBLANES, h * SUBLANES + 1)
        m = m_ref[hs, lo:tq]
        m_new = jnp.maximum(m, jnp.max(s, axis=0, keepdims=True))
        alpha = jnp.exp2(m - m_new)
        p = jnp.exp2(s - m_new).astype(BF16)
        m_ref[hs, lo:tq] = m_new
        vt1 = jnp.concatenate([vt_ref[0, h * hd:(h + 1) * hd, pl.ds(k0, tk)],
                               jnp.ones((2 * SUBLANES, tk), BF16)], axis=0)
        pv = jnp.dot(vt1, p, preferred_element_type=F32)
        l_ref[hs, lo:tq] = alpha * l_ref[hs, lo:tq] + pv[hd:hd + 1, :]
        acc_ref[h * hd:(h + 1) * hd, lo:tq] = (alpha * acc_ref[h * hd:(h + 1) * hd, lo:tq]
                                               + pv[:hd, :])

    def body(kj, carry):
        k0 = pl.multiple_of(kj * tk, tk)
        k1 = pl.multiple_of(kj * tk + tk, tk)
        for h in range(ATTN_HEADS):
            ha = h + ATTN_LOOKAHEAD
            if ha < ATTN_HEADS:
                scores(k0, ha)
            else:
                scores(k1, ha - ATTN_HEADS)
            update(k0, h)
        return carry

    for h in range(ATTN_LOOKAHEAD):
        scores(0, h)
    lax.fori_loop(0, qi * nd, body, 0)
    for d in range(nd):
        kd = pl.multiple_of(qi * tq + d * tk, tk)
        for h in range(ATTN_HEADS):
            ha = h + ATTN_LOOKAHEAD
            if ha < ATTN_HEADS:
                scores(kd, ha, d * tk)
            elif d + 1 < nd:
                scores(pl.multiple_of(kd + tk, tk), ha - ATTN_HEADS, (d + 1) * tk)
            update(kd, h, d * tk)

    for h in range(ATTN_HEADS):
        acc_ref[h * hd:(h + 1) * hd, 0:tq] = (acc_ref[h * hd:(h + 1) * hd, 0:tq]
                                              / l_ref[h * SUBLANES:h * SUBLANES + 1, :])
    o = acc_ref[:, 0:tq].T
    o_ref[0] = (o * sza_ref[0].astype(F32)).astype(BF16)


def _attn(qt, k, caug, vt, sza, *, tq):
    nb, s, w = k.shape
    return pl.pallas_call(
        functools.partial(_attn_kernel, tq=tq),
        grid=(nb, s // tq),
        in_specs=[pl.BlockSpec((1, w, tq), lambda b, i: (b, 0, i)),
                  pl.BlockSpec((1, s, w), lambda b, i: (b, 0, 0)),
                  pl.BlockSpec((1, s, LANES), lambda b, i: (b, 0, 0)),
                  pl.BlockSpec((1, w, s), lambda b, i: (b, 0, 0)),
                  pl.BlockSpec((1, tq, w), lambda b, i: (b, i, 0))],
        out_specs=pl.BlockSpec((1, tq, w), lambda b, i: (b, i, 0)),
        out_shape=jax.ShapeDtypeStruct((nb, s, w), BF16),
        scratch_shapes=[pltpu.VMEM((ATTN_HEADS, 2 * LANES, tq + LANES), BF16),
                        pltpu.VMEM((ATTN_SLOTS, ATTN_TK, tq + LANES), F32),
                        pltpu.VMEM((ATTN_HEADS * SUBLANES, tq), F32),
                        pltpu.VMEM((ATTN_HEADS * SUBLANES, tq), F32),
                        pltpu.VMEM((w, tq + LANES), F32)],
        compiler_params=_cparams("arbitrary", "arbitrary"),
        name="attn",
    )(qt, k, caug, vt, sza)


MERGE_DEEP = 3


def _merge_kernel(x_hbm, ys_hbm, szs_hbm, oz_hbm, sgs_hbm, sga_hbm, mod_hbm,
                  wps_ref, wpa_ref, wout_ref, lng_ref, lnb_ref, o_hbm,
                  wps_bf, wpa_bf, wout_bf, *, ts, nb, nt):
    wps_bf[:, 0:D_MODEL] = wps_ref[...].astype(BF16)
    wpa_bf[:, 0:D_MODEL] = wpa_ref[...].astype(BF16)
    wout_bf[:, 0:D_MODEL] = wout_ref[...].astype(BF16)

    def body(x_ref, ys_ref, szs_ref, oz_ref, sgs_ref, sga_ref, gate_ref, o_ref):
        gate = gate_ref[0, 0]
        for c in range(ts // MERGE_SUB):
            r = slice(c * MERGE_SUB, (c + 1) * MERGE_SUB)
            a = (ys_ref[0, r, :].astype(F32) * szs_ref[0, r, :].astype(F32)).astype(BF16)
            m1 = (jnp.dot(a, wps_bf[:, 0:D_MODEL], preferred_element_type=F32)
                  * sgs_ref[0, r, :].astype(F32))
            m2 = (jnp.dot(oz_ref[0, r, :], wpa_bf[:, 0:D_MODEL], preferred_element_type=F32)
                  * sga_ref[0, r, :].astype(F32))
            merged = (m1 + m2).astype(BF16)
            sub = jnp.dot(merged, wout_bf[:, 0:D_MODEL], preferred_element_type=F32) * gate
            y = DEEPNORM_ALPHA * x_ref[0, r, :] + sub
            mu = jnp.mean(y, axis=-1, keepdims=True)
            yc = y - mu
            var = jnp.mean(yc * yc, axis=-1, keepdims=True)
            o_ref[0, r, :] = yc * lax.rsqrt(var + LN_EPS) * lng_ref[...] + lnb_ref[...]

    tok3 = lambda w, deep=2: pl.BlockSpec((1, ts, w), lambda i, b: (b, i, 0),
                                          pipeline_mode=pl.Buffered(deep))
    pltpu.emit_pipeline(
        body, grid=(nt, nb),
        in_specs=[tok3(D_MODEL, MERGE_DEEP), tok3(SSM_WIDTH), tok3(SSM_WIDTH), tok3(ATTN_WIDTH),
                  tok3(D_MODEL, MERGE_DEEP), tok3(D_MODEL, MERGE_DEEP), _mod_spec(2)],
        out_specs=[tok3(D_MODEL)],
    )(x_hbm, ys_hbm, szs_hbm, oz_hbm, sgs_hbm, sga_hbm, mod_hbm, o_hbm)


def _mod_spec(part):
    return pl.BlockSpec((1, 1, 1, D_MODEL), lambda i, b: (b, part, 0, 0))


def _merge(x, ys_t, szs, oz, sgs, sga, mod4, wps, wpa, wout, ln_g, ln_b, *, ts):
    nb, s, d = x.shape
    hbm = pl.BlockSpec(memory_space=pl.ANY)
    vmem = pl.BlockSpec(memory_space=pltpu.VMEM)
    return pl.pallas_call(
        functools.partial(_merge_kernel, ts=ts, nb=nb, nt=s // ts),
        in_specs=[hbm] * 7 + [vmem] * 5,
        out_specs=hbm,
        out_shape=jax.ShapeDtypeStruct((nb, s, d), x.dtype),
        scratch_shapes=[pltpu.VMEM((w.shape[0], d + LANES), BF16) for w in (wps, wpa, wout)],
        compiler_params=pltpu.CompilerParams(vmem_limit_bytes=MERGE_VMEM_LIMIT),
        name="merge",
    )(x, ys_t, szs, oz, sgs, sga, mod4, wps, wpa, wout, ln_g, ln_b)


def kernel(x, c, w_ada, b_ada, w_in, b_f, lam_re, lam_im, log_dt, ssm_b_re, ssm_b_im,
           ssm_c_re, ssm_c_im, ssm_d, w_glu, w_proj_ssm, w_proj_attn, w_out, ln_g, ln_b):
    nb, s, d = x.shape
    l = 0
    mod4 = _ada(c, w_ada[l], b_ada[l]).reshape(nb, 3, 1, d)

    w_main, w_t, w_f3 = _wprep(jnp.swapaxes(w_in[l], 0, 1))
    bfv = b_f[l].astype(F32)
    b_f3 = jnp.concatenate([bfv, bfv, bfv, jnp.zeros((LANES - 3 * ATTN_HEADS,), F32)])[None, :]

    (xs_t, szs, k, sza, sgs, sga, qt, vt, caug) = _inproj(
        x, mod4, w_main, w_t, w_f3, b_f3, ts=512)

    bmat, cmat, a_re, a_im = _zoh(lam_re[l], lam_im[l], log_dt[l], ssm_b_re[l], ssm_b_im[l],
                                  ssm_c_re[l], ssm_c_im[l])

    ys_t = _s5(xs_t, bmat, cmat, a_re, a_im,
               ssm_d[l].reshape(1, SSM_WIDTH).astype(F32), w_glu[l], steps=128, sub=32)

    oz = _attn(qt, k, caug, vt, sza, tq=1024)

    return _merge(x, ys_t, szs, oz, sgs, sga, mod4,
                  w_proj_ssm[l], w_proj_attn[l], w_out[l],
                  ln_g[l].reshape(1, d).astype(F32), ln_b[l].astype(F32).reshape(1, d), ts=1024)
```

```python
import functools
import math

import jax
import jax.numpy as jnp
from jax import lax
from jax.experimental import pallas as pl
from jax.experimental.pallas import tpu as pltpu

D_MODEL = 1024
BATCH = 8
SEQ = 2048
SSM_WIDTH = 512
SSM_GROUP = 16
SSM_GROUPS = 32
SSM_STATE = 64
ATTN_HEADS = 8
ATTN_HEAD_DIM = 64
ATTN_WIDTH = 512
LN_EPS = 1e-5
DEEPNORM_ALPHA = 2.0 ** 0.25

F32 = jnp.float32
BF16 = jnp.bfloat16

LANES = 128
SUBLANES = 8
VMEM_LIMIT = 52 * 1024 * 1024

S5_TILES = 4
S5_TILE_GROUPS = SSM_GROUPS // S5_TILES
S5_HALF = S5_TILE_GROUPS * SSM_STATE
S5_TILE_LANES = 2 * S5_HALF
S5_TILE_IN = S5_TILE_GROUPS * SSM_GROUP
S5_STATE_LANES = S5_TILES * S5_TILE_LANES

C_XS, C_ZS, C_K, C_ZA, C_GS, C_GA, C_END = 0, 512, 1024, 1536, 2048, 3072, 4096

NEG_BIG = -1e30
LOG2E = math.log2(math.e)

INPROJ_SUB = 256
MERGE_SUB = 256
ATTN_TK = 256
ATTN_LOOKAHEAD = 3
ATTN_SLOTS = 8
assert ATTN_SLOTS > ATTN_LOOKAHEAD and ATTN_HEADS % ATTN_SLOTS == 0


def _sigmoid(x):
    return 1.0 / (1.0 + jnp.exp(-x))


def _silu(x):
    return x * _sigmoid(x)


def _cparams(*sem):
    return pltpu.CompilerParams(dimension_semantics=sem, vmem_limit_bytes=VMEM_LIMIT)


def _ada_kernel(c_ref, w_ref, b_ref, o_ref):
    ca = _silu(c_ref[...]).astype(BF16)
    o_ref[...] = jnp.dot(ca, w_ref[...].astype(BF16), preferred_element_type=F32) + b_ref[...]


def _ada(c, w_ada, b_ada):
    n = w_ada.shape[1]
    tn = 1024
    return pl.pallas_call(
        _ada_kernel,
        grid=(n // tn,),
        in_specs=[pl.BlockSpec((BATCH, D_MODEL), lambda j: (0, 0)),
                  pl.BlockSpec((D_MODEL, tn), lambda j: (0, j)),
                  pl.BlockSpec((1, tn), lambda j: (0, j))],
        out_specs=pl.BlockSpec((BATCH, tn), lambda j: (0, j)),
        out_shape=jax.ShapeDtypeStruct((BATCH, n), F32),
        compiler_params=_cparams("arbitrary"),
        name="ada",
    )(c, w_ada, b_ada.reshape(1, n))


def _zoh_kernel(lr_ref, li_ref, ldt_ref, br_ref, bi_ref, cre_ref, cim_ref,
                bmat_ref, cmat_ref, are_ref, aim_ref):
    lr = lr_ref[...]
    li = li_ref[...]
    dt = jnp.exp(ldt_ref[...])
    mag = jnp.exp(lr * dt)
    abr = mag * jnp.cos(li * dt)
    abi = mag * jnp.sin(li * dt)
    den = lr * lr + li * li
    cr = ((abr - 1.0) * lr + abi * li) / den
    ci = (abi * lr - (abr - 1.0) * li) / den

    tg, gh, gp = S5_TILE_GROUPS, SSM_GROUP, SSM_STATE
    rep_lanes = (lax.broadcasted_iota(jnp.int32, (gp, tg * gp), 1) % gp
                 == lax.broadcasted_iota(jnp.int32, (gp, tg * gp), 0))
    rep_rows = (lax.broadcasted_iota(jnp.int32, (tg * gp, gp), 0) % gp
                == lax.broadcasted_iota(jnp.int32, (tg * gp, gp), 1))
    rep_lanes_bf = jnp.where(rep_lanes, 1.0, 0.0).astype(BF16)
    rep_rows_bf = jnp.where(rep_rows, 1.0, 0.0).astype(BF16)
    rep_lanes_f32 = jnp.where(rep_lanes, 1.0, 0.0)
    b_diag = (lax.broadcasted_iota(jnp.int32, (tg * gh, tg * gp), 0) // gh
              == lax.broadcasted_iota(jnp.int32, (tg * gh, tg * gp), 1) // gp)
    c_diag = (lax.broadcasted_iota(jnp.int32, (tg * gp, tg * gh), 0) // gp
              == lax.broadcasted_iota(jnp.int32, (tg * gp, tg * gh), 1) // gh)
    a_diag = (lax.broadcasted_iota(jnp.int32, (tg, tg * gp), 0)
              == lax.broadcasted_iota(jnp.int32, (tg, tg * gp), 1) // gp)

    for j in range(S5_TILES):
        bbr, bbi, crs, cis = [], [], [], []
        for g in range(j * tg, (j + 1) * tg):
            crg, cig = cr[g:g + 1, :], ci[g:g + 1, :]
            br, bi = br_ref[g], bi_ref[g]
            bbr.append(crg * br - cig * bi)
            bbi.append(crg * bi + cig * br)
            crs.append(cre_ref[g])
            cis.append(-cim_ref[g])
        halves = []
        for blk in (jnp.concatenate(bbr, axis=0), jnp.concatenate(bbi, axis=0)):
            tiled = jnp.dot(blk.astype(BF16), rep_lanes_bf, preferred_element_type=F32)
            halves.append(jnp.where(b_diag, tiled, 0.0))
        halves.append(jnp.zeros((tg * gh, LANES), F32))
        bmat_ref[j] = jnp.concatenate(halves, axis=1).astype(BF16)
        halves = []
        for blk in (jnp.concatenate(crs, axis=0), jnp.concatenate(cis, axis=0)):
            tiled = lax.dot_general(rep_rows_bf, blk.astype(BF16), (((1,), (1,)), ((), ())),
                                    preferred_element_type=F32)
            halves.append(jnp.where(c_diag, tiled, 0.0))
        cmat_ref[j] = jnp.concatenate(halves, axis=0).astype(BF16)
        for a, a_ref in ((abr, are_ref), (abi, aim_ref)):
            tiled = jnp.dot(a[j * tg:(j + 1) * tg, :], rep_lanes_f32, preferred_element_type=F32,
                            precision=lax.Precision.HIGHEST)
            a_ref[:, j * S5_HALF:(j + 1) * S5_HALF] = jnp.sum(
                jnp.where(a_diag, tiled, 0.0), axis=0, keepdims=True)


def _zoh(lam_re, lam_im, log_dt, b_re, b_im, c_re, c_im):
    swap = lambda a: jnp.swapaxes(a, 1, 2)
    return pl.pallas_call(
        _zoh_kernel,
        out_shape=(jax.ShapeDtypeStruct((S5_TILES, S5_TILE_IN, S5_TILE_LANES + LANES), BF16),
                   jax.ShapeDtypeStruct((S5_TILES, S5_TILE_LANES, S5_TILE_IN), BF16),
                   jax.ShapeDtypeStruct((1, SSM_GROUPS * SSM_STATE), F32),
                   jax.ShapeDtypeStruct((1, SSM_GROUPS * SSM_STATE), F32)),
        name="zoh",
    )(lam_re, lam_im, log_dt.reshape(SSM_GROUPS, 1), swap(b_re), swap(b_im), c_re, c_im)


O_XS, O_ZS, O_Q, O_K, O_V, O_F, O_ZA, O_GS, O_GA, O_END = (
    0, 512, 1024, 1536, 2048, 2560, 2568, 3080, 4104, 5128)


WPREP_CHUNK = 256


def _wprep_chunks():
    chunks = []
    for c_lo, o_lo, n in ((C_XS, O_XS, O_Q - O_XS), (C_K, O_K, O_V - O_K),
                          (C_ZA, O_ZA, O_GS - O_ZA), (C_GS, O_GS, O_END - O_GS)):
        chunks += [(o_lo + s, WPREP_CHUNK, "main", c_lo + s) for s in range(0, n, WPREP_CHUNK)]
    for d_lo, o_lo in ((0, O_Q), (ATTN_WIDTH, O_V)):
        chunks += [(o_lo + s, WPREP_CHUNK, "qv", d_lo + s) for s in range(0, ATTN_WIDTH, WPREP_CHUNK)]
    chunks.append((O_F, O_ZA - O_F, "forget", 0))
    return chunks


def _wprep_kernel(wt_hbm, wm_ref, wqv_ref, wf_ref, buf_ref, sem_ref):
    chunks = _wprep_chunks()

    def copy(n):
        row, rows, _, _ = chunks[n]
        return pltpu.make_async_copy(wt_hbm.at[pl.ds(row, rows), :],
                                     buf_ref.at[n % 2, pl.ds(0, rows), :], sem_ref.at[n % 2])

    copy(0).start()
    wm_ref[:, C_END:] = jnp.zeros((wm_ref.shape[0], LANES), BF16)
    wqv_ref[:, D_MODEL:] = jnp.zeros((wqv_ref.shape[0], LANES), BF16)
    for n, (_, rows, kind, dst) in enumerate(chunks):
        if n + 1 < len(chunks):
            copy(n + 1).start()
        copy(n).wait()
        blk = buf_ref[n % 2, 0:rows, :]
        if kind == "main":
            wm_ref[:, dst:dst + rows] = blk.T.astype(BF16)
        elif kind == "qv":
            wqv_ref[dst:dst + rows, 0:D_MODEL] = blk.astype(BF16)
        else:
            rep = jnp.concatenate(
                [blk, blk, blk, jnp.zeros((LANES - 3 * ATTN_HEADS, blk.shape[1]), F32)], axis=0)
            wf_ref[...] = rep.T.astype(BF16)


def _wprep(wt):
    n, d = wt.shape
    return pl.pallas_call(
        _wprep_kernel,
        in_specs=[pl.BlockSpec(memory_space=pl.ANY)],
        out_shape=[jax.ShapeDtypeStruct((d, C_END + LANES), BF16),
                   jax.ShapeDtypeStruct((2 * ATTN_WIDTH, d + LANES), BF16),
                   jax.ShapeDtypeStruct((d, LANES), BF16)],
        scratch_shapes=[pltpu.VMEM((2, WPREP_CHUNK, d), F32), pltpu.SemaphoreType.DMA((2,))],
        compiler_params=pltpu.CompilerParams(vmem_limit_bytes=VMEM_LIMIT),
        name="wprep",
    )(wt)


def _inproj_kernel(x_ref, scale_ref, shift_ref, wm_ref, wt_ref, wf_ref, bf_ref, tri_ref,
                   xs_ref, szs_ref, k_ref, sza_ref, sgs_ref, sga_ref, qt_ref, vt_ref, caug_ref,
                   carry_ref, *, ts):
    i = pl.program_id(0)
    b = pl.program_id(1)

    @pl.when(i == 0)
    def _():
        carry_ref[b] = jnp.zeros(carry_ref.shape[1:], F32)

    carry = carry_ref[b][0:1, :]
    for c in range(ts // INPROJ_SUB):
        r = slice(c * INPROJ_SUB, (c + 1) * INPROJ_SUB)
        x = x_ref[0, r, :]
        mu = jnp.mean(x, axis=-1, keepdims=True)
        xc = x - mu
        var = jnp.mean(xc * xc, axis=-1, keepdims=True)
        xn = xc * lax.rsqrt(var + LN_EPS)
        u = xn * (1.0 + scale_ref[0, 0]) + shift_ref[0, 0]
        ub = u.astype(BF16)

        def mm(lo, hi, ub=ub):
            return jnp.dot(ub, wm_ref[:, lo:hi], preferred_element_type=F32)

        f = jnp.dot(ub, wf_ref[...], preferred_element_type=F32) + bf_ref[...]
        lf = jnp.minimum(f, 0.0) - jnp.log(1.0 + jnp.exp(-jnp.abs(f)))
        l1 = lf.astype(BF16)
        r1 = lf - l1.astype(F32)
        l2 = r1.astype(BF16)
        l3 = (r1 - l2.astype(F32)).astype(BF16)

        xs = mm(C_XS, C_ZS)
        for j in range(S5_TILES):
            xs_ref[j, pl.ds(b + c * INPROJ_SUB * SUBLANES, INPROJ_SUB, stride=SUBLANES), :] = (
                xs[:, j * LANES:(j + 1) * LANES])
        szs_ref[0, r, :] = _silu(mm(C_ZS, C_K)).astype(BF16)

        tri = tri_ref[...]
        cf = (jnp.dot(tri, l1, preferred_element_type=F32)
              + jnp.dot(tri, l2, preferred_element_type=F32)
              + jnp.dot(tri, l3, preferred_element_type=F32)) + carry
        carry = cf[INPROJ_SUB - 1:INPROJ_SUB, :]

        k_ref[0, r, :] = mm(C_K, C_ZA).astype(BF16)
        sza_ref[0, r, :] = _silu(mm(C_ZA, C_GS)).astype(BF16)

        cf2 = cf * LOG2E
        c1 = cf2.astype(BF16).astype(F32)
        rem = cf2 - c1
        c2 = rem.astype(BF16).astype(F32)
        c3 = rem - c2
        lane = lax.broadcasted_iota(jnp.int32, cf.shape, 1)
        parts = jnp.where(lane < 8, c1, jnp.where(lane < 16, c2, jnp.where(lane < 24, c3, 0.0)))
        caug_ref[0, r, :] = (-parts).astype(BF16)

        sgs_ref[0, r, :] = _sigmoid(mm(C_GS, C_GA)).astype(BF16)
        sga_ref[0, r, :] = _sigmoid(mm(C_GA, C_END)).astype(BF16)

        qv = lax.dot_general(wt_ref[:, 0:D_MODEL], ub, (((1,), (1,)), ((), ())),
                             preferred_element_type=F32)
        qt_ref[0, :, r] = (qv[:ATTN_WIDTH] * (LOG2E * ATTN_HEAD_DIM ** -0.5)).astype(BF16)
        vt_ref[0, :, r] = qv[ATTN_WIDTH:].astype(BF16)
    carry_ref[b] = jnp.broadcast_to(carry, carry_ref.shape[1:])


def _inproj(x, mod4, w_main, w_t, w_f3, b_f3, *, ts):
    nb, s, d = x.shape
    nt = s // ts
    tri = jnp.tril(jnp.ones((INPROJ_SUB, INPROJ_SUB), BF16))
    tok3 = lambda w: pl.BlockSpec((1, ts, w), lambda i, b: (b, i, 0))
    const2 = lambda a: pl.BlockSpec(a.shape, lambda i, b: (0, 0), pipeline_mode=pl.Buffered(1))
    bf = lambda shape: jax.ShapeDtypeStruct(shape, BF16)
    return pl.pallas_call(
        functools.partial(_inproj_kernel, ts=ts),
        grid=(nt, nb),
        in_specs=[tok3(d), _mod_spec(1), _mod_spec(0),
                  const2(w_main), const2(w_t), const2(w_f3), const2(b_f3), const2(tri)],
        out_specs=[pl.BlockSpec((S5_TILES, ts * nb, LANES), lambda i, b: (0, i, 0)),
                   tok3(SSM_WIDTH), tok3(ATTN_WIDTH), tok3(ATTN_WIDTH),
                   tok3(D_MODEL), tok3(D_MODEL),
                   pl.BlockSpec((1, ATTN_WIDTH, ts), lambda i, b: (b, 0, i)),
                   pl.BlockSpec((1, ATTN_WIDTH, ts), lambda i, b: (b, 0, i)),
                   tok3(LANES)],
        out_shape=[jax.ShapeDtypeStruct((S5_TILES, s * nb, LANES), F32),
                   bf((nb, s, SSM_WIDTH)), bf((nb, s, ATTN_WIDTH)), bf((nb, s, ATTN_WIDTH)),
                   bf((nb, s, D_MODEL)), bf((nb, s, D_MODEL)),
                   bf((nb, ATTN_WIDTH, s)), bf((nb, ATTN_WIDTH, s)),
                   bf((nb, s, LANES))],
        scratch_shapes=[pltpu.VMEM((nb, SUBLANES, LANES), F32)],
        compiler_params=_cparams("arbitrary", "arbitrary"),
        name="inproj",
    )(x, mod4, mod4, w_main, w_t, w_f3, b_f3, tri)


def _gelu_tanh(x):
    return 0.5 * x * (1.0 + jnp.tanh(math.sqrt(2.0 / math.pi) * (x + 0.044715 * (x * x * x))))


def _s5_kernel(xs_ref, bmat_ref, cmat_ref, are_ref, aim_ref, d_ref, wglu_ref,
               y_ref, buf_ref, h_ref, ybuf_ref, wglu_bf, *, steps, sub):
    @pl.when(pl.program_id(0) == 0)
    def _():
        h_ref[...] = jnp.zeros_like(h_ref)
        wglu_bf[:, 0:2 * SSM_WIDTH] = wglu_ref[...].astype(BF16)

    srows = sub * SUBLANES

    def expand(c):
        r = slice(c * srows, (c + 1) * srows)
        for j in range(S5_TILES):
            lo = j * S5_TILE_LANES
            buf_ref[r, lo:lo + S5_TILE_LANES] = jnp.dot(
                xs_ref[j, r, :].astype(BF16), bmat_ref[j, :, 0:S5_TILE_LANES],
                preferred_element_type=F32)

    def recur(c, state):
        new_state = []
        for j in range(S5_TILES):
            re = slice(j * S5_TILE_LANES, j * S5_TILE_LANES + S5_HALF)
            im = slice(j * S5_TILE_LANES + S5_HALF, (j + 1) * S5_TILE_LANES)
            ar = jnp.broadcast_to(are_ref[:, j * S5_HALF:(j + 1) * S5_HALF], (SUBLANES, S5_HALF))
            ai = jnp.broadcast_to(aim_ref[:, j * S5_HALF:(j + 1) * S5_HALF], (SUBLANES, S5_HALF))
            hr, hi = state[j]
            for t in range(sub):
                rt = slice(c * srows + t * SUBLANES, c * srows + (t + 1) * SUBLANES)
                nhr = ar * hr - ai * hi + buf_ref[rt, re]
                nhi = ar * hi + ai * hr + buf_ref[rt, im]
                buf_ref[rt, re] = nhr
                buf_ref[rt, im] = nhi
                hr, hi = nhr, nhi
            new_state.append((hr, hi))
        return new_state

    def project(c):
        r = slice(c * srows, (c + 1) * srows)
        ys = []
        for j in range(S5_TILES):
            lo = j * S5_TILE_LANES
            hb = buf_ref[r, lo:lo + S5_TILE_LANES].astype(BF16)
            ys.append(jnp.dot(hb, cmat_ref[j], preferred_element_type=F32)
                      + d_ref[:, j * LANES:(j + 1) * LANES] * xs_ref[j, r, :])
        g = _gelu_tanh(jnp.concatenate(ys, axis=1)).astype(BF16)
        ab = jnp.dot(g, wglu_bf[:, 0:2 * SSM_WIDTH], preferred_element_type=F32)
        out = ab[:, :SSM_WIDTH] * _sigmoid(ab[:, SSM_WIDTH:])
        for j in range(S5_TILES):
            ybuf_ref[j] = out[:, j * LANES:(j + 1) * LANES]
        for b in range(BATCH):
            yb = jnp.concatenate([ybuf_ref[j, pl.ds(b, sub, stride=SUBLANES), :]
                                  for j in range(S5_TILES)], axis=1)
            y_ref[b, c * sub:(c + 1) * sub, :] = yb.astype(BF16)

    state = [(h_ref[:, j * S5_TILE_LANES:j * S5_TILE_LANES + S5_HALF],
              h_ref[:, j * S5_TILE_LANES + S5_HALF:(j + 1) * S5_TILE_LANES])
             for j in range(S5_TILES)]
    nsub = steps // sub
    expand(0)
    for c in range(nsub):
        if c + 1 < nsub:
            expand(c + 1)
        state = recur(c, state)
        project(c)
    for j in range(S5_TILES):
        h_ref[:, j * S5_TILE_LANES:j * S5_TILE_LANES + S5_HALF] = state[j][0]
        h_ref[:, j * S5_TILE_LANES + S5_HALF:(j + 1) * S5_TILE_LANES] = state[j][1]


def _s5(xs_t, bmat, cmat, a_re, a_im, d_skip, w_glu, *, steps, sub):
    t = xs_t.shape[1]
    rows = steps * SUBLANES
    const = lambda a: pl.BlockSpec(a.shape, lambda i: (0,) * a.ndim)
    tiles = pl.BlockSpec((S5_TILES, rows, LANES), lambda i: (0, i, 0))
    return pl.pallas_call(
        functools.partial(_s5_kernel, steps=steps, sub=sub),
        grid=(t // rows,),
        in_specs=[tiles,
                  const(bmat), const(cmat), const(a_re), const(a_im), const(d_skip), const(w_glu)],
        out_specs=pl.BlockSpec((BATCH, steps, SSM_WIDTH), lambda i: (0, i, 0)),
        out_shape=jax.ShapeDtypeStruct((BATCH, t // SUBLANES, SSM_WIDTH), BF16),
        scratch_shapes=[pltpu.VMEM((rows, S5_STATE_LANES), F32),
                        pltpu.VMEM((SUBLANES, S5_STATE_LANES), F32),
                        pltpu.VMEM((S5_TILES, sub * SUBLANES, LANES), F32),
                        pltpu.VMEM((w_glu.shape[0], w_glu.shape[1] + LANES), BF16)],
        compiler_params=_cparams("arbitrary"),
        name="s5",
    )(xs_t, bmat, cmat, a_re, a_im, d_skip, w_glu)


def _attn_kernel(qt_ref, k_ref, caug_ref, vt_ref, sza_ref, o_ref,
                 rhs_ref, s_ref, m_ref, l_ref, acc_ref, *, tq):
    qi = pl.program_id(1)
    tk = ATTN_TK
    nd = tq // tk
    hd = ATTN_HEAD_DIM
    row = lax.broadcasted_iota(jnp.int32, (LANES, tq), 0)

    for h in range(ATTN_HEADS):
        j = h // 2
        q2 = qt_ref[0, j * LANES:(j + 1) * LANES, :]
        rhs_ref[h, 0:LANES, 0:tq] = jnp.where((row // hd) == (h % 2), q2, jnp.zeros_like(q2))
        rhs_ref[h, LANES:2 * LANES, 0:tq] = jnp.where(
            (row == h) | (row == 8 + h) | (row == 16 + h), 1.0, 0.0).astype(BF16)
    m_ref[...] = jnp.full(m_ref.shape, -jnp.inf, F32)
    l_ref[...] = jnp.zeros_like(l_ref)
    acc_ref[...] = jnp.zeros_like(acc_ref)

    def scores(k0, h, diag=None):
        j = h // 2
        lo = 0 if diag is None else diag
        lhs = jnp.concatenate([k_ref[0, pl.ds(k0, tk), j * LANES:(j + 1) * LANES],
                               caug_ref[0, pl.ds(k0, tk), :]], axis=1)
        s_ref[h % ATTN_SLOTS, :, lo:tq] = jnp.dot(lhs, rhs_ref[h, :, lo:tq],
                                                  preferred_element_type=F32)

    def update(k0, h, diag=None):
        lo = 0 if diag is None else diag
        s = s_ref[h % ATTN_SLOTS, :, lo:tq]
        if diag is not None:
            kpos = lax.broadcasted_iota(jnp.int32, s.shape, 0)
            qpos = lax.broadcasted_iota(jnp.int32, s.shape, 1)
            s = jnp.where(kpos <= qpos, s, NEG_BIG)
        hs = slice(h * SUBLANES, h * SUBLANES + 1)
        m = m_ref[hs, lo:tq]
        m_new = jnp.maximum(m, jnp.max(s, axis=0, keepdims=True))
        alpha = jnp.exp2(m - m_new)
        p = jnp.exp2(s - m_new).astype(BF16)
        m_ref[hs, lo:tq] = m_new
        vt1 = jnp.concatenate([vt_ref[0, h * hd:(h + 1) * hd, pl.ds(k0, tk)],
                               jnp.ones((2 * SUBLANES, tk), BF16)], axis=0)
        pv = jnp.dot(vt1, p, preferred_element_type=F32)
        l_ref[hs, lo:tq] = alpha * l_ref[hs, lo:tq] + pv[hd:hd + 1, :]
        acc_ref[h * hd:(h + 1) * hd, lo:tq] = (alpha * acc_ref[h * hd:(h + 1) * hd, lo:tq]
                                               + pv[:hd, :])

    def body(kj, carry):
        k0 = pl.multiple_of(kj * tk, tk)
        k1 = pl.multiple_of(kj * tk + tk, tk)
        for h in range(ATTN_HEADS):
            ha = h + ATTN_LOOKAHEAD
            if ha < ATTN_HEADS:
                scores(k0, ha)
            else:
                scores(k1, ha - ATTN_HEADS)
            update(k0, h)
        return carry

    for h in range(ATTN_LOOKAHEAD):
        scores(0, h)
    lax.fori_loop(0, qi * nd, body, 0)
    for d in range(nd):
        kd = pl.multiple_of(qi * tq + d * tk, tk)
        for h in range(ATTN_HEADS):
            ha = h + ATTN_LOOKAHEAD
            if ha < ATTN_HEADS:
                scores(kd, ha, d * tk)
            elif d + 1 < nd:
                scores(pl.multiple_of(kd + tk, tk), ha - ATTN_HEADS, (d + 1) * tk)
            update(kd, h, d * tk)

    for h in range(ATTN_HEADS):
        acc_ref[h * hd:(h + 1) * hd, 0:tq] = (acc_ref[h * hd:(h + 1) * hd, 0:tq]
                                              / l_ref[h * SUBLANES:h * SUBLANES + 1, :])
    o = acc_ref[:, 0:tq].T
    o_ref[0] = (o * sza_ref[0].astype(F32)).astype(BF16)


def _attn(qt, k, caug, vt, sza, *, tq):
    nb, s, w = k.shape
    return pl.pallas_call(
        functools.partial(_attn_kernel, tq=tq),
        grid=(nb, s // tq),
        in_specs=[pl.BlockSpec((1, w, tq), lambda b, i: (b, 0, i)),
                  pl.BlockSpec((1, s, w), lambda b, i: (b, 0, 0)),
                  pl.BlockSpec((1, s, LANES), lambda b, i: (b, 0, 0)),
                  pl.BlockSpec((1, w, s), lambda b, i: (b, 0, 0)),
                  pl.BlockSpec((1, tq, w), lambda b, i: (b, i, 0))],
        out_specs=pl.BlockSpec((1, tq, w), lambda b, i: (b, i, 0)),
        out_shape=jax.ShapeDtypeStruct((nb, s, w), BF16),
        scratch_shapes=[pltpu.VMEM((ATTN_HEADS, 2 * LANES, tq + LANES), BF16),
                        pltpu.VMEM((ATTN_SLOTS, ATTN_TK, tq + LANES), F32),
                        pltpu.VMEM((ATTN_HEADS * SUBLANES, tq), F32),
                        pltpu.VMEM((ATTN_HEADS * SUBLANES, tq), F32),
                        pltpu.VMEM((w, tq + LANES), F32)],
        compiler_params=_cparams("arbitrary", "arbitrary"),
        name="attn",
    )(qt, k, caug, vt, sza)


def _merge_kernel(x_ref, ys_ref, szs_ref, oz_ref, sgs_ref, sga_ref, gate_ref,
                  wps_ref, wpa_ref, wout_ref, lng_ref, lnb_ref, o_ref,
                  wps_bf, wpa_bf, wout_bf, *, ts):
    @pl.when((pl.program_id(0) == 0) & (pl.program_id(1) == 0))
    def _():
        wps_bf[:, 0:D_MODEL] = wps_ref[...].astype(BF16)
        wpa_bf[:, 0:D_MODEL] = wpa_ref[...].astype(BF16)
        wout_bf[:, 0:D_MODEL] = wout_ref[...].astype(BF16)

    gate = gate_ref[0, 0]
    for c in range(ts // MERGE_SUB):
        r = slice(c * MERGE_SUB, (c + 1) * MERGE_SUB)
        a = (ys_ref[0, r, :].astype(F32) * szs_ref[0, r, :].astype(F32)).astype(BF16)
        m1 = (jnp.dot(a, wps_bf[:, 0:D_MODEL], preferred_element_type=F32)
              * sgs_ref[0, r, :].astype(F32))
        m2 = (jnp.dot(oz_ref[0, r, :], wpa_bf[:, 0:D_MODEL], preferred_element_type=F32)
              * sga_ref[0, r, :].astype(F32))
        merged = (m1 + m2).astype(BF16)
        sub = jnp.dot(merged, wout_bf[:, 0:D_MODEL], preferred_element_type=F32) * gate
        y = DEEPNORM_ALPHA * x_ref[0, r, :] + sub
        mu = jnp.mean(y, axis=-1, keepdims=True)
        yc = y - mu
        var = jnp.mean(yc * yc, axis=-1, keepdims=True)
        o_ref[0, r, :] = yc * lax.rsqrt(var + LN_EPS) * lng_ref[...] + lnb_ref[...]


def _mod_spec(part):
    return pl.BlockSpec((1, 1, 1, D_MODEL), lambda i, b: (b, part, 0, 0))


def _merge(x, ys_t, szs, oz, sgs, sga, mod4, wps, wpa, wout, ln_g, ln_b, *, ts):
    nb, s, d = x.shape
    tok3 = lambda w: pl.BlockSpec((1, ts, w), lambda i, b: (b, i, 0))
    const2 = lambda a: pl.BlockSpec(a.shape, lambda i, b: (0, 0), pipeline_mode=pl.Buffered(1))
    return pl.pallas_call(
        functools.partial(_merge_kernel, ts=ts),
        grid=(s // ts, nb),
        in_specs=[tok3(d), tok3(SSM_WIDTH),
                  tok3(SSM_WIDTH), tok3(ATTN_WIDTH), tok3(d), tok3(d),
                  _mod_spec(2),
                  const2(wps), const2(wpa), const2(wout), const2(ln_g), const2(ln_b)],
        out_specs=tok3(d),
        out_shape=jax.ShapeDtypeStruct((nb, s, d), x.dtype),
        scratch_shapes=[pltpu.VMEM((w.shape[0], d + LANES), BF16) for w in (wps, wpa, wout)],
        compiler_params=_cparams("arbitrary", "arbitrary"),
        name="merge",
    )(x, ys_t, szs, oz, sgs, sga, mod4, wps, wpa, wout, ln_g, ln_b)


def kernel(x, c, w_ada, b_ada, w_in, b_f, lam_re, lam_im, log_dt, ssm_b_re, ssm_b_im,
           ssm_c_re, ssm_c_im, ssm_d, w_glu, w_proj_ssm, w_proj_attn, w_out, ln_g, ln_b):
    nb, s, d = x.shape
    l = 0
    mod4 = _ada(c, w_ada[l], b_ada[l]).reshape(nb, 3, 1, d)

    w_main, w_t, w_f3 = _wprep(jnp.swapaxes(w_in[l], 0, 1))
    bfv = b_f[l].astype(F32)
    b_f3 = jnp.concatenate([bfv, bfv, bfv, jnp.zeros((LANES - 3 * ATTN_HEADS,), F32)])[None, :]

    (xs_t, szs, k, sza, sgs, sga, qt, vt, caug) = _inproj(
        x, mod4, w_main, w_t, w_f3, b_f3, ts=512)

    bmat, cmat, a_re, a_im = _zoh(lam_re[l], lam_im[l], log_dt[l], ssm_b_re[l], ssm_b_im[l],
                                  ssm_c_re[l], ssm_c_im[l])

    ys_t = _s5(xs_t, bmat, cmat, a_re, a_im,
               ssm_d[l].reshape(1, SSM_WIDTH).astype(F32), w_glu[l], steps=128, sub=32)

    oz = _attn(qt, k, caug, vt, sza, tq=1024)

    return _merge(x, ys_t, szs, oz, sgs, sga, mod4,
                  w_proj_ssm[l], w_proj_attn[l], w_out[l],
                  ln_g[l].reshape(1, d).astype(F32), ln_b[l].astype(F32).reshape(1, d), ts=1024)
```

```python
import functools
import math

import jax
import jax.numpy as jnp
from jax import lax
from jax.experimental import pallas as pl
from jax.experimental.pallas import tpu as pltpu

D_MODEL = 1024
BATCH = 8
SEQ = 2048
SSM_WIDTH = 512
SSM_GROUP = 16
SSM_GROUPS = 32
SSM_STATE = 64
ATTN_HEADS = 8
ATTN_HEAD_DIM = 64
ATTN_WIDTH = 512
LN_EPS = 1e-5
DEEPNORM_ALPHA = 2.0 ** 0.25

F32 = jnp.float32
BF16 = jnp.bfloat16

LANES = 128
SUBLANES = 8
VMEM_LIMIT = 52 * 1024 * 1024

S5_TILES = 4
S5_TILE_GROUPS = SSM_GROUPS // S5_TILES
S5_HALF = S5_TILE_GROUPS * SSM_STATE
S5_TILE_LANES = 2 * S5_HALF
S5_TILE_IN = S5_TILE_GROUPS * SSM_GROUP
S5_STATE_LANES = S5_TILES * S5_TILE_LANES

C_XS, C_ZS, C_K, C_ZA, C_GS, C_GA, C_END = 0, 512, 1024, 1536, 2048, 3072, 4096

NEG_BIG = -1e30
LOG2E = math.log2(math.e)

INPROJ_SUB = 256
MERGE_SUB = 256
ATTN_TK = 256
ATTN_LOOKAHEAD = 3
ATTN_SLOTS = 8
assert ATTN_SLOTS > ATTN_LOOKAHEAD and ATTN_HEADS % ATTN_SLOTS == 0


def _sigmoid(x):
    return 1.0 / (1.0 + jnp.exp(-x))


def _silu(x):
    return x * _sigmoid(x)


def _cparams(*sem):
    return pltpu.CompilerParams(dimension_semantics=sem, vmem_limit_bytes=VMEM_LIMIT)


def _ada_kernel(c_ref, w_ref, b_ref, o_ref):
    ca = _silu(c_ref[...]).astype(BF16)
    o_ref[...] = jnp.dot(ca, w_ref[...].astype(BF16), preferred_element_type=F32) + b_ref[...]


def _ada(c, w_ada, b_ada):
    n = w_ada.shape[1]
    tn = 1024
    return pl.pallas_call(
        _ada_kernel,
        grid=(n // tn,),
        in_specs=[pl.BlockSpec((BATCH, D_MODEL), lambda j: (0, 0)),
                  pl.BlockSpec((D_MODEL, tn), lambda j: (0, j)),
                  pl.BlockSpec((1, tn), lambda j: (0, j))],
        out_specs=pl.BlockSpec((BATCH, tn), lambda j: (0, j)),
        out_shape=jax.ShapeDtypeStruct((BATCH, n), F32),
        compiler_params=_cparams("arbitrary"),
        name="ada",
    )(c, w_ada, b_ada.reshape(1, n))


def _zoh_kernel(lr_ref, li_ref, ldt_ref, br_ref, bi_ref, cre_ref, cim_ref,
                bmat_ref, cmat_ref, are_ref, aim_ref):
    lr = lr_ref[...]
    li = li_ref[...]
    dt = jnp.exp(ldt_ref[...])
    mag = jnp.exp(lr * dt)
    abr = mag * jnp.cos(li * dt)
    abi = mag * jnp.sin(li * dt)
    den = lr * lr + li * li
    cr = ((abr - 1.0) * lr + abi * li) / den
    ci = (abi * lr - (abr - 1.0) * li) / den

    tg, gh, gp = S5_TILE_GROUPS, SSM_GROUP, SSM_STATE
    rep_lanes = (lax.broadcasted_iota(jnp.int32, (gp, tg * gp), 1) % gp
                 == lax.broadcasted_iota(jnp.int32, (gp, tg * gp), 0))
    rep_rows = (lax.broadcasted_iota(jnp.int32, (tg * gp, gp), 0) % gp
                == lax.broadcasted_iota(jnp.int32, (tg * gp, gp), 1))
    rep_lanes_bf = jnp.where(rep_lanes, 1.0, 0.0).astype(BF16)
    rep_rows_bf = jnp.where(rep_rows, 1.0, 0.0).astype(BF16)
    rep_lanes_f32 = jnp.where(rep_lanes, 1.0, 0.0)
    b_diag = (lax.broadcasted_iota(jnp.int32, (tg * gh, tg * gp), 0) // gh
              == lax.broadcasted_iota(jnp.int32, (tg * gh, tg * gp), 1) // gp)
    c_diag = (lax.broadcasted_iota(jnp.int32, (tg * gp, tg * gh), 0) // gp
              == lax.broadcasted_iota(jnp.int32, (tg * gp, tg * gh), 1) // gh)
    a_diag = (lax.broadcasted_iota(jnp.int32, (tg, tg * gp), 0)
              == lax.broadcasted_iota(jnp.int32, (tg, tg * gp), 1) // gp)

    for j in range(S5_TILES):
        bbr, bbi, crs, cis = [], [], [], []
        for g in range(j * tg, (j + 1) * tg):
            crg, cig = cr[g:g + 1, :], ci[g:g + 1, :]
            br, bi = br_ref[g], bi_ref[g]
            bbr.append(crg * br - cig * bi)
            bbi.append(crg * bi + cig * br)
            crs.append(cre_ref[g])
            cis.append(-cim_ref[g])
        halves = []
        for blk in (jnp.concatenate(bbr, axis=0), jnp.concatenate(bbi, axis=0)):
            tiled = jnp.dot(blk.astype(BF16), rep_lanes_bf, preferred_element_type=F32)
            halves.append(jnp.where(b_diag, tiled, 0.0))
        halves.append(jnp.zeros((tg * gh, LANES), F32))
        bmat_ref[j] = jnp.concatenate(halves, axis=1).astype(BF16)
        halves = []
        for blk in (jnp.concatenate(crs, axis=0), jnp.concatenate(cis, axis=0)):
            tiled = lax.dot_general(rep_rows_bf, blk.astype(BF16), (((1,), (1,)), ((), ())),
                                    preferred_element_type=F32)
            halves.append(jnp.where(c_diag, tiled, 0.0))
        cmat_ref[j] = jnp.concatenate(halves, axis=0).astype(BF16)
        for a, a_ref in ((abr, are_ref), (abi, aim_ref)):
            tiled = jnp.dot(a[j * tg:(j + 1) * tg, :], rep_lanes_f32, preferred_element_type=F32,
                            precision=lax.Precision.HIGHEST)
            a_ref[:, j * S5_HALF:(j + 1) * S5_HALF] = jnp.sum(
                jnp.where(a_diag, tiled, 0.0), axis=0, keepdims=True)


def _zoh(lam_re, lam_im, log_dt, b_re, b_im, c_re, c_im):
    swap = lambda a: jnp.swapaxes(a, 1, 2)
    return pl.pallas_call(
        _zoh_kernel,
        out_shape=(jax.ShapeDtypeStruct((S5_TILES, S5_TILE_IN, S5_TILE_LANES + LANES), BF16),
                   jax.ShapeDtypeStruct((S5_TILES, S5_TILE_LANES, S5_TILE_IN), BF16),
                   jax.ShapeDtypeStruct((1, SSM_GROUPS * SSM_STATE), F32),
                   jax.ShapeDtypeStruct((1, SSM_GROUPS * SSM_STATE), F32)),
        name="zoh",
    )(lam_re, lam_im, log_dt.reshape(SSM_GROUPS, 1), swap(b_re), swap(b_im), c_re, c_im)


O_XS, O_ZS, O_Q, O_K, O_V, O_F, O_ZA, O_GS, O_GA, O_END = (
    0, 512, 1024, 1536, 2048, 2560, 2568, 3080, 4104, 5128)


WPREP_CHUNK = 256


def _wprep_kernel(wt_ref, c_ref, wada_ref, bada_ref, wm_ref, wqv_ref, wf_ref, mod_ref):
    _ada_kernel(c_ref, wada_ref, bada_ref, mod_ref)
    def put(c_lo, o_lo, n):
        for s in range(0, n, WPREP_CHUNK):
            wm_ref[:, c_lo + s:c_lo + s + WPREP_CHUNK] = (
                wt_ref[o_lo + s:o_lo + s + WPREP_CHUNK, :].T.astype(BF16))

    put(C_XS, O_XS, O_Q - O_XS)
    put(C_K, O_K, O_V - O_K)
    put(C_ZA, O_ZA, O_GS - O_ZA)
    put(C_GS, O_GS, O_END - O_GS)
    wm_ref[:, C_END:] = jnp.zeros((wm_ref.shape[0], LANES), BF16)
    wqv_ref[0:ATTN_WIDTH, 0:D_MODEL] = wt_ref[O_Q:O_K, :].astype(BF16)
    wqv_ref[ATTN_WIDTH:2 * ATTN_WIDTH, 0:D_MODEL] = wt_ref[O_V:O_F, :].astype(BF16)
    wqv_ref[:, D_MODEL:] = jnp.zeros((wqv_ref.shape[0], LANES), BF16)
    f8 = wt_ref[O_F:O_ZA, :]
    blk = jnp.concatenate([f8, f8, f8, jnp.zeros((LANES - 3 * ATTN_HEADS, f8.shape[1]), F32)], axis=0)
    wf_ref[...] = blk.T.astype(BF16)


def _wprep(wt, c, w_ada, b_ada):
    n, d = wt.shape
    return pl.pallas_call(
        _wprep_kernel,
        out_shape=[jax.ShapeDtypeStruct((d, C_END + LANES), BF16),
                   jax.ShapeDtypeStruct((2 * ATTN_WIDTH, d + LANES), BF16),
                   jax.ShapeDtypeStruct((d, LANES), BF16),
                   jax.ShapeDtypeStruct((c.shape[0], w_ada.shape[1]), F32)],
        compiler_params=pltpu.CompilerParams(vmem_limit_bytes=VMEM_LIMIT),
        name="wprep",
    )(wt, c, w_ada, b_ada.reshape(1, -1))


def _inproj_kernel(x_ref, scale_ref, shift_ref, wm_ref, wt_ref, wf_ref, bf_ref, tri_ref,
                   xs_ref, szs_ref, k_ref, sza_ref, sgs_ref, sga_ref, qt_ref, vt_ref, caug_ref,
                   carry_ref, *, ts):
    i = pl.program_id(0)
    b = pl.program_id(1)

    @pl.when(i == 0)
    def _():
        carry_ref[b] = jnp.zeros(carry_ref.shape[1:], F32)

    carry = carry_ref[b][0:1, :]
    for c in range(ts // INPROJ_SUB):
        r = slice(c * INPROJ_SUB, (c + 1) * INPROJ_SUB)
        x = x_ref[0, r, :]
        mu = jnp.mean(x, axis=-1, keepdims=True)
        xc = x - mu
        var = jnp.mean(xc * xc, axis=-1, keepdims=True)
        xn = xc * lax.rsqrt(var + LN_EPS)
        u = xn * (1.0 + scale_ref[0, 0]) + shift_ref[0, 0]
        ub = u.astype(BF16)

        def mm(lo, hi, ub=ub):
            return jnp.dot(ub, wm_ref[:, lo:hi], preferred_element_type=F32)

        f = jnp.dot(ub, wf_ref[...], preferred_element_type=F32) + bf_ref[...]
        lf = jnp.minimum(f, 0.0) - jnp.log(1.0 + jnp.exp(-jnp.abs(f)))
        l1 = lf.astype(BF16)
        r1 = lf - l1.astype(F32)
        l2 = r1.astype(BF16)
        l3 = (r1 - l2.astype(F32)).astype(BF16)

        xs = mm(C_XS, C_ZS)
        for j in range(S5_TILES):
            xs_ref[j, pl.ds(b + c * INPROJ_SUB * SUBLANES, INPROJ_SUB, stride=SUBLANES), :] = (
                xs[:, j * LANES:(j + 1) * LANES])
        szs_ref[0, r, :] = _silu(mm(C_ZS, C_K)).astype(BF16)

        tri = tri_ref[...]
        cf = (jnp.dot(tri, l1, preferred_element_type=F32)
              + jnp.dot(tri, l2, preferred_element_type=F32)
              + jnp.dot(tri, l3, preferred_element_type=F32)) + carry
        carry = cf[INPROJ_SUB - 1:INPROJ_SUB, :]

        k_ref[0, r, :] = mm(C_K, C_ZA).astype(BF16)
        sza_ref[0, r, :] = _silu(mm(C_ZA, C_GS)).astype(BF16)

        cf2 = cf * LOG2E
        c1 = cf2.astype(BF16).astype(F32)
        rem = cf2 - c1
        c2 = rem.astype(BF16).astype(F32)
        c3 = rem - c2
        lane = lax.broadcasted_iota(jnp.int32, cf.shape, 1)
        parts = jnp.where(lane < 8, c1, jnp.where(lane < 16, c2, jnp.where(lane < 24, c3, 0.0)))
        caug_ref[0, r, :] = (-parts).astype(BF16)

        sgs_ref[0, r, :] = _sigmoid(mm(C_GS, C_GA)).astype(BF16)
        sga_ref[0, r, :] = _sigmoid(mm(C_GA, C_END)).astype(BF16)

        qv = lax.dot_general(wt_ref[:, 0:D_MODEL], ub, (((1,), (1,)), ((), ())),
                             preferred_element_type=F32)
        qt_ref[0, :, r] = (qv[:ATTN_WIDTH] * (LOG2E * ATTN_HEAD_DIM ** -0.5)).astype(BF16)
        vt_ref[0, :, r] = qv[ATTN_WIDTH:].astype(BF16)
    carry_ref[b] = jnp.broadcast_to(carry, carry_ref.shape[1:])


def _inproj(x, mod4, w_main, w_t, w_f3, b_f3, *, ts):
    nb, s, d = x.shape
    nt = s // ts
    tri = jnp.tril(jnp.ones((INPROJ_SUB, INPROJ_SUB), BF16))
    tok3 = lambda w: pl.BlockSpec((1, ts, w), lambda i, b: (b, i, 0))
    const2 = lambda a: pl.BlockSpec(a.shape, lambda i, b: (0, 0), pipeline_mode=pl.Buffered(1))
    bf = lambda shape: jax.ShapeDtypeStruct(shape, BF16)
    return pl.pallas_call(
        functools.partial(_inproj_kernel, ts=ts),
        grid=(nt, nb),
        in_specs=[tok3(d), _mod_spec(1), _mod_spec(0),
                  const2(w_main), const2(w_t), const2(w_f3), const2(b_f3), const2(tri)],
        out_specs=[pl.BlockSpec((S5_TILES, ts * nb, LANES), lambda i, b: (0, i, 0)),
                   tok3(SSM_WIDTH), tok3(ATTN_WIDTH), tok3(ATTN_WIDTH),
                   tok3(D_MODEL), tok3(D_MODEL),
                   pl.BlockSpec((1, ATTN_WIDTH, ts), lambda i, b: (b, 0, i)),
                   pl.BlockSpec((1, ATTN_WIDTH, ts), lambda i, b: (b, 0, i)),
                   tok3(LANES)],
        out_shape=[jax.ShapeDtypeStruct((S5_TILES, s * nb, LANES), F32),
                   bf((nb, s, SSM_WIDTH)), bf((nb, s, ATTN_WIDTH)), bf((nb, s, ATTN_WIDTH)),
                   bf((nb, s, D_MODEL)), bf((nb, s, D_MODEL)),
                   bf((nb, ATTN_WIDTH, s)), bf((nb, ATTN_WIDTH, s)),
                   bf((nb, s, LANES))],
        scratch_shapes=[pltpu.VMEM((nb, SUBLANES, LANES), F32)],
        compiler_params=_cparams("arbitrary", "arbitrary"),
        name="inproj",
    )(x, mod4, mod4, w_main, w_t, w_f3, b_f3, tri)


def _gelu_tanh(x):
    return 0.5 * x * (1.0 + jnp.tanh(math.sqrt(2.0 / math.pi) * (x + 0.044715 * (x * x * x))))


def _s5_kernel(xs_ref, bmat_ref, cmat_ref, are_ref, aim_ref, d_ref, wglu_ref,
               y_ref, buf_ref, h_ref, ybuf_ref, wglu_bf, *, steps, sub):
    @pl.when(pl.program_id(0) == 0)
    def _():
        h_ref[...] = jnp.zeros_like(h_ref)
        wglu_bf[:, 0:2 * SSM_WIDTH] = wglu_ref[...].astype(BF16)

    srows = sub * SUBLANES

    def expand(c):
        r = slice(c * srows, (c + 1) * srows)
        for j in range(S5_TILES):
            lo = j * S5_TILE_LANES
            buf_ref[r, lo:lo + S5_TILE_LANES] = jnp.dot(
                xs_ref[j, r, :].astype(BF16), bmat_ref[j, :, 0:S5_TILE_LANES],
                preferred_element_type=F32)

    def recur(c, state):
        new_state = []
        for j in range(S5_TILES):
            re = slice(j * S5_TILE_LANES, j * S5_TILE_LANES + S5_HALF)
            im = slice(j * S5_TILE_LANES + S5_HALF, (j + 1) * S5_TILE_LANES)
            ar = jnp.broadcast_to(are_ref[:, j * S5_HALF:(j + 1) * S5_HALF], (SUBLANES, S5_HALF))
            ai = jnp.broadcast_to(aim_ref[:, j * S5_HALF:(j + 1) * S5_HALF], (SUBLANES, S5_HALF))
            hr, hi = state[j]
            for t in range(sub):
                rt = slice(c * srows + t * SUBLANES, c * srows + (t + 1) * SUBLANES)
                nhr = ar * hr - ai * hi + buf_ref[rt, re]
                nhi = ar * hi + ai * hr + buf_ref[rt, im]
                buf_ref[rt, re] = nhr
                buf_ref[rt, im] = nhi
                hr, hi = nhr, nhi
            new_state.append((hr, hi))
        return new_state

    def project(c):
        r = slice(c * srows, (c + 1) * srows)
        ys = []
        for j in range(S5_TILES):
            lo = j * S5_TILE_LANES
            hb = buf_ref[r, lo:lo + S5_TILE_LANES].astype(BF16)
            ys.append(jnp.dot(hb, cmat_ref[j], preferred_element_type=F32)
                      + d_ref[:, j * LANES:(j + 1) * LANES] * xs_ref[j, r, :])
        g = _gelu_tanh(jnp.concatenate(ys, axis=1)).astype(BF16)
        ab = jnp.dot(g, wglu_bf[:, 0:2 * SSM_WIDTH], preferred_element_type=F32)
        out = ab[:, :SSM_WIDTH] * _sigmoid(ab[:, SSM_WIDTH:])
        for j in range(S5_TILES):
            ybuf_ref[j] = out[:, j * LANES:(j + 1) * LANES]
        for b in range(BATCH):
            yb = jnp.concatenate([ybuf_ref[j, pl.ds(b, sub, stride=SUBLANES), :]
                                  for j in range(S5_TILES)], axis=1)
            y_ref[b, c * sub:(c + 1) * sub, :] = yb.astype(BF16)

    state = [(h_ref[:, j * S5_TILE_LANES:j * S5_TILE_LANES + S5_HALF],
              h_ref[:, j * S5_TILE_LANES + S5_HALF:(j + 1) * S5_TILE_LANES])
             for j in range(S5_TILES)]
    nsub = steps // sub
    expand(0)
    for c in range(nsub):
        if c + 1 < nsub:
            expand(c + 1)
        state = recur(c, state)
        project(c)
    for j in range(S5_TILES):
        h_ref[:, j * S5_TILE_LANES:j * S5_TILE_LANES + S5_HALF] = state[j][0]
        h_ref[:, j * S5_TILE_LANES + S5_HALF:(j + 1) * S5_TILE_LANES] = state[j][1]


def _s5(xs_t, bmat, cmat, a_re, a_im, d_skip, w_glu, *, steps, sub):
    t = xs_t.shape[1]
    rows = steps * SUBLANES
    const = lambda a: pl.BlockSpec(a.shape, lambda i: (0,) * a.ndim)
    tiles = pl.BlockSpec((S5_TILES, rows, LANES), lambda i: (0, i, 0))
    return pl.pallas_call(
        functools.partial(_s5_kernel, steps=steps, sub=sub),
        grid=(t // rows,),
        in_specs=[tiles,
                  const(bmat), const(cmat), const(a_re), const(a_im), const(d_skip), const(w_glu)],
        out_specs=pl.BlockSpec((BATCH, steps, SSM_WIDTH), lambda i: (0, i, 0)),
        out_shape=jax.ShapeDtypeStruct((BATCH, t // SUBLANES, SSM_WIDTH), BF16),
        scratch_shapes=[pltpu.VMEM((rows, S5_STATE_LANES), F32),
                        pltpu.VMEM((SUBLANES, S5_STATE_LANES), F32),
                        pltpu.VMEM((S5_TILES, sub * SUBLANES, LANES), F32),
                        pltpu.VMEM((w_glu.shape[0], w_glu.shape[1] + LANES), BF16)],
        compiler_params=_cparams("arbitrary"),
        name="s5",
    )(xs_t, bmat, cmat, a_re, a_im, d_skip, w_glu)


def _attn_kernel(qt_ref, k_ref, caug_ref, vt_ref, sza_ref, o_ref,
                 rhs_ref, s_ref, m_ref, l_ref, acc_ref, *, tq):
    qi = pl.program_id(1)
    tk = ATTN_TK
    nd = tq // tk
    hd = ATTN_HEAD_DIM
    row = lax.broadcasted_iota(jnp.int32, (LANES, tq), 0)

    for h in range(ATTN_HEADS):
        j = h // 2
        q2 = qt_ref[0, j * LANES:(j + 1) * LANES, :]
        rhs_ref[h, 0:LANES, 0:tq] = jnp.where((row // hd) == (h % 2), q2, jnp.zeros_like(q2))
        rhs_ref[h, LANES:2 * LANES, 0:tq] = jnp.where(
            (row == h) | (row == 8 + h) | (row == 16 + h), 1.0, 0.0).astype(BF16)
    m_ref[...] = jnp.full(m_ref.shape, -jnp.inf, F32)
    l_ref[...] = jnp.zeros_like(l_ref)
    acc_ref[...] = jnp.zeros_like(acc_ref)

    def scores(k0, h, diag=None):
        j = h // 2
        lo = 0 if diag is None else diag
        lhs = jnp.concatenate([k_ref[0, pl.ds(k0, tk), j * LANES:(j + 1) * LANES],
                               caug_ref[0, pl.ds(k0, tk), :]], axis=1)
        s_ref[h % ATTN_SLOTS, :, lo:tq] = jnp.dot(lhs, rhs_ref[h, :, lo:tq],
                                                  preferred_element_type=F32)

    def update(k0, h, diag=None):
        lo = 0 if diag is None else diag
        s = s_ref[h % ATTN_SLOTS, :, lo:tq]
        if diag is not None:
            kpos = lax.broadcasted_iota(jnp.int32, s.shape, 0)
            qpos = lax.broadcasted_iota(jnp.int32, s.shape, 1)
            s = jnp.where(kpos <= qpos, s, NEG_BIG)
        hs = slice(h * SUBLANES, h * SUBLANES + 1)
        m = m_ref[hs, lo:tq]
        m_new = jnp.maximum(m, jnp.max(s, axis=0, keepdims=True))
        alpha = jnp.exp2(m - m_new)
        p = jnp.exp2(s - m_new).astype(BF16)
        m_ref[hs, lo:tq] = m_new
        vt1 = jnp.concatenate([vt_ref[0, h * hd:(h + 1) * hd, pl.ds(k0, tk)],
                               jnp.ones((2 * SUBLANES, tk), BF16)], axis=0)
        pv = jnp.dot(vt1, p, preferred_element_type=F32)
        l_ref[hs, lo:tq] = alpha * l_ref[hs, lo:tq] + pv[hd:hd + 1, :]
        acc_ref[h * hd:(h + 1) * hd, lo:tq] = (alpha * acc_ref[h * hd:(h + 1) * hd, lo:tq]
                                               + pv[:hd, :])

    def body(kj, carry):
        k0 = pl.multiple_of(kj * tk, tk)
        k1 = pl.multiple_of(kj * tk + tk, tk)
        for h in range(ATTN_HEADS):
            ha = h + ATTN_LOOKAHEAD
            if ha < ATTN_HEADS:
                scores(k0, ha)
            else:
                scores(k1, ha - ATTN_HEADS)
            update(k0, h)
        return carry

    for h in range(ATTN_LOOKAHEAD):
        scores(0, h)
    lax.fori_loop(0, qi * nd, body, 0)
    for d in range(nd):
        kd = pl.multiple_of(qi * tq + d * tk, tk)
        for h in range(ATTN_HEADS):
            ha = h + ATTN_LOOKAHEAD
            if ha < ATTN_HEADS:
                scores(kd, ha, d * tk)
            elif d + 1 < nd:
                scores(pl.multiple_of(kd + tk, tk), ha - ATTN_HEADS, (d + 1) * tk)
            update(kd, h, d * tk)

    for h in range(ATTN_HEADS):
        acc_ref[h * hd:(h + 1) * hd, 0:tq] = (acc_ref[h * hd:(h + 1) * hd, 0:tq]
                                              / l_ref[h * SUBLANES:h * SUBLANES + 1, :])
    o = acc_ref[:, 0:tq].T
    o_ref[0] = (o * sza_ref[0].astype(F32)).astype(BF16)


def _attn(qt, k, caug, vt, sza, *, tq):
    nb, s, w = k.shape
    return pl.pallas_call(
        functools.partial(_attn_kernel, tq=tq),
        grid=(nb, s // tq),
        in_specs=[pl.BlockSpec((1, w, tq), lambda b, i: (b, 0, i)),
                  pl.BlockSpec((1, s, w), lambda b, i: (b, 0, 0)),
                  pl.BlockSpec((1, s, LANES), lambda b, i: (b, 0, 0)),
                  pl.BlockSpec((1, w, s), lambda b, i: (b, 0, 0)),
                  pl.BlockSpec((1, tq, w), lambda b, i: (b, i, 0))],
        out_specs=pl.BlockSpec((1, tq, w), lambda b, i: (b, i, 0)),
        out_shape=jax.ShapeDtypeStruct((nb, s, w), BF16),
        scratch_shapes=[pltpu.VMEM((ATTN_HEADS, 2 * LANES, tq + LANES), BF16),
                        pltpu.VMEM((ATTN_SLOTS, ATTN_TK, tq + LANES), F32),
                        pltpu.VMEM((ATTN_HEADS * SUBLANES, tq), F32),
                        pltpu.VMEM((ATTN_HEADS * SUBLANES, tq), F32),
                        pltpu.VMEM((w, tq + LANES), F32)],
        compiler_params=_cparams("arbitrary", "arbitrary"),
        name="attn",
    )(qt, k, caug, vt, sza)


def _merge_kernel(x_ref, ys_ref, szs_ref, oz_ref, sgs_ref, sga_ref, gate_ref,
                  wps_ref, wpa_ref, wout_ref, lng_ref, lnb_ref, o_ref,
                  wps_bf, wpa_bf, wout_bf, *, ts):
    @pl.when((pl.program_id(0) == 0) & (pl.program_id(1) == 0))
    def _():
        wps_bf[:, 0:D_MODEL] = wps_ref[...].astype(BF16)
        wpa_bf[:, 0:D_MODEL] = wpa_ref[...].astype(BF16)
        wout_bf[:, 0:D_MODEL] = wout_ref[...].astype(BF16)

    gate = gate_ref[0, 0]
    for c in range(ts // MERGE_SUB):
        r = slice(c * MERGE_SUB, (c + 1) * MERGE_SUB)
        a = (ys_ref[0, r, :].astype(F32) * szs_ref[0, r, :].astype(F32)).astype(BF16)
        m1 = (jnp.dot(a, wps_bf[:, 0:D_MODEL], preferred_element_type=F32)
              * sgs_ref[0, r, :].astype(F32))
        m2 = (jnp.dot(oz_ref[0, r, :], wpa_bf[:, 0:D_MODEL], preferred_element_type=F32)
              * sga_ref[0, r, :].astype(F32))
        merged = (m1 + m2).astype(BF16)
        sub = jnp.dot(merged, wout_bf[:, 0:D_MODEL], preferred_element_type=F32) * gate
        y = DEEPNORM_ALPHA * x_ref[0, r, :] + sub
        mu = jnp.mean(y, axis=-1, keepdims=True)
        yc = y - mu
        var = jnp.mean(yc * yc, axis=-1, keepdims=True)
        o_ref[0, r, :] = yc * lax.rsqrt(var + LN_EPS) * lng_ref[...] + lnb_ref[...]


def _mod_spec(part):
    return pl.BlockSpec((1, 1, 1, D_MODEL), lambda i, b: (b, part, 0, 0))


def _merge(x, ys_t, szs, oz, sgs, sga, mod4, wps, wpa, wout, ln_g, ln_b, *, ts):
    nb, s, d = x.shape
    tok3 = lambda w: pl.BlockSpec((1, ts, w), lambda i, b: (b, i, 0))
    const2 = lambda a: pl.BlockSpec(a.shape, lambda i, b: (0, 0), pipeline_mode=pl.Buffered(1))
    return pl.pallas_call(
        functools.partial(_merge_kernel, ts=ts),
        grid=(s // ts, nb),
        in_specs=[tok3(d), tok3(SSM_WIDTH),
                  tok3(SSM_WIDTH), tok3(ATTN_WIDTH), tok3(d), tok3(d),
                  _mod_spec(2),
                  const2(wps), const2(wpa), const2(wout), const2(ln_g), const2(ln_b)],
        out_specs=tok3(d),
        out_shape=jax.ShapeDtypeStruct((nb, s, d), x.dtype),
        scratch_shapes=[pltpu.VMEM((w.shape[0], d + LANES), BF16) for w in (wps, wpa, wout)],
        compiler_params=_cparams("arbitrary", "arbitrary"),
        name="merge",
    )(x, ys_t, szs, oz, sgs, sga, mod4, wps, wpa, wout, ln_g, ln_b)


def kernel(x, c, w_ada, b_ada, w_in, b_f, lam_re, lam_im, log_dt, ssm_b_re, ssm_b_im,
           ssm_c_re, ssm_c_im, ssm_d, w_glu, w_proj_ssm, w_proj_attn, w_out, ln_g, ln_b):
    nb, s, d = x.shape
    l = 0
    w_main, w_t, w_f3, mod = _wprep(jnp.swapaxes(w_in[l], 0, 1), c, w_ada[l], b_ada[l])
    mod4 = mod.reshape(nb, 3, 1, d)
    bfv = b_f[l].astype(F32)
    b_f3 = jnp.concatenate([bfv, bfv, bfv, jnp.zeros((LANES - 3 * ATTN_HEADS,), F32)])[None, :]

    (xs_t, szs, k, sza, sgs, sga, qt, vt, caug) = _inproj(
        x, mod4, w_main, w_t, w_f3, b_f3, ts=512)

    bmat, cmat, a_re, a_im = _zoh(lam_re[l], lam_im[l], log_dt[l], ssm_b_re[l], ssm_b_im[l],
                                  ssm_c_re[l], ssm_c_im[l])

    ys_t = _s5(xs_t, bmat, cmat, a_re, a_im,
               ssm_d[l].reshape(1, SSM_WIDTH).astype(F32), w_glu[l], steps=128, sub=32)

    oz = _attn(qt, k, caug, vt, sza, tq=1024)

    return _merge(x, ys_t, szs, oz, sgs, sga, mod4,
                  w_proj_ssm[l], w_proj_attn[l], w_out[l],
                  ln_g[l].reshape(1, d).astype(F32), ln_b[l].astype(F32).reshape(1, d), ts=1024)
```
